```python
import math
import jax
import jax.numpy as jnp
from jax import lax
import numpy as np

D_MODEL = 2048
BATCH = 4
SEQ = 2048
DEPTH = 4
DEC_BATCH = 8
DEC_SEQ = 1
PAST_LEN = 16384
PAGE_SIZE = 128

N_MIXERS = 3
N_A = (DEPTH + 2) // N_MIXERS
N_B = (DEPTH + 1) // N_MIXERS
N_C = DEPTH // N_MIXERS
FF_DIM = -(-(8 * D_MODEL) // (3 * 256)) * 256
EPS = 1e-6
NEG = -1e30
BIG = 1e30

MLSTM_HEADS = 8
MLSTM_DV = D_MODEL // MLSTM_HEADS
MLSTM_DK = MLSTM_DV // 2
MLSTM_CHUNK = 64
FORGET_BIAS = 3.0
A_IN = 2 * MLSTM_HEADS * MLSTM_DK + 2 * MLSTM_HEADS * MLSTM_DV + 2 * MLSTM_HEADS

CONV_W = 3

NSA_HD = 128
NSA_HEADS = D_MODEL // NSA_HD
NSA_KV = 4
NSA_HPG = NSA_HEADS // NSA_KV
CMP_BLOCK = 32
CMP_STRIDE = 16
CMP_RATIO = CMP_BLOCK // CMP_STRIDE
CMP_HIDDEN = 128
SEL_BLOCK = 64
N_SELECT = 16
WINDOW = 512
SEL_QBLOCK = 16
WIN_QBLOCK = 128
C_IN = NSA_HEADS * NSA_HD + 6 * NSA_KV * NSA_HD + 3 * NSA_HEADS

kernel_name = 'hybrid_mlstm_shortconv_nsa_step'


def split_cols(x, sizes):
    return jnp.split(x, [int(c) for c in np.cumsum(sizes)[:-1]], axis=-1)


def rmsnorm(x, w):
    xf = x.astype(jnp.float32)
    xf = xf * lax.rsqrt(jnp.mean(xf * xf, axis=-1, keepdims=True) + EPS)
    return xf.astype(x.dtype) * w


def masked_softmax(s, mask):
    p = jax.nn.softmax(jnp.where(mask, s.astype(jnp.float32), NEG), axis=-1)
    return jnp.where(mask, p, 0.0)


def swiglu_block(x, nw, w_in, w_out):
    g, u = jnp.split(rmsnorm(x, nw[2]) @ w_in, 2, axis=-1)
    return x + rmsnorm((jax.nn.silu(g) * u) @ w_out, nw[3])


def mlstm_mixer(xn, C0, n0, m0, w_in, b_gates, head_norm, w_out):
    bsz, T, _ = xn.shape
    H, DK, DV = MLSTM_HEADS, MLSTM_DK, MLSTM_DV
    f32 = jnp.float32
    q, k, v, o, ig, fg = split_cols(xn @ w_in, [H * DK, H * DK, H * DV, H * DV, H, H])
    q = q.reshape(bsz, T, H, DK).astype(f32)
    k = k.reshape(bsz, T, H, DK).astype(f32) * (DK ** -0.5)
    v = v.reshape(bsz, T, H, DV).astype(f32)
    ig = (ig + b_gates[:H]).astype(f32)
    lf = jax.nn.log_sigmoid((fg + b_gates[H:]).astype(f32))
    L = math.gcd(T, MLSTM_CHUNK)
    nc = T // L

    def to_chunks(a):
        a = a.reshape(bsz, nc, L, H, *a.shape[3:])
        return jnp.moveaxis(a, (1, 3), (0, 2))

    causal = jnp.tril(jnp.ones((L, L), dtype=bool))

    def step(carry, inp):
        C, n, m = carry
        qc, kc, vc, ic, lfc = inp
        bcum = jnp.cumsum(lfc, axis=-1)
        log_d = jnp.where(causal, bcum[..., :, None] - bcum[..., None, :] + ic[..., None, :], NEG)
        log_inter = bcum + m[..., None]
        m_t = jnp.maximum(log_inter, jnp.max(log_d, axis=-1))
        d = jnp.exp(log_d - m_t[..., None])
        inter = jnp.exp(log_inter - m_t)
        s = jnp.einsum('bhtk,bhsk->bhts', qc, kc) * d
        num = jnp.einsum('bhts,bhsv->bhtv', s, vc) + inter[..., None] * jnp.einsum('bhtk,bhkv->bhtv', qc, C)
        den = jnp.sum(s, axis=-1) + inter * jnp.einsum('bhtk,bhk->bht', qc, n)
        h = num / jnp.maximum(jnp.abs(den), jnp.exp(-m_t))[..., None]
        m_new = m_t[..., -1]
        w_s = jnp.exp(bcum[..., -1:] - bcum + ic - m_new[..., None])
        decay = jnp.exp(log_inter[..., -1] - m_new)
        C_new = decay[..., None, None] * C + jnp.einsum('bhs,bhsk,bhsv->bhkv', w_s, kc, vc)
        n_new = decay[..., None] * n + jnp.einsum('bhs,bhsk->bhk', w_s, kc)
        return (C_new, n_new, m_new), h

    carry0 = (C0.astype(f32), n0.astype(f32), m0.astype(f32))
    (C, n, m), h = lax.scan(step, carry0, tuple(to_chunks(a) for a in (q, k, v, ig, lf)))
    h = jnp.moveaxis(h, (0, 2), (1, 3)).reshape(bsz, T, H, DV)
    h = h * lax.rsqrt(jnp.mean(h * h, axis=-1, keepdims=True) + EPS)
    h = h.reshape(bsz, T, H * DV).astype(xn.dtype) * head_norm
    return (h * jax.nn.sigmoid(o)) @ w_out, C, n, m


def shortconv_mixer(xn, buf, w_in, conv_w, w_out):
    T = xn.shape[1]
    bg, cg, u = jnp.split(xn @ w_in, 3, axis=-1)
    z = cg * u
    zc = jnp.concatenate([buf.astype(z.dtype), z], axis=1)
    y = conv_w[0] * zc[:, 0:T]
    for j in range(1, CONV_W):
        y = y + conv_w[j] * zc[:, j:j + T]
    return (bg * y) @ w_out, zc[:, -(CONV_W - 1):]


def nsa_compress(kv, pos_emb, w1, b1, w2, b2):
    bsz, tk, G, hd = kv.shape
    nch = tk // CMP_STRIDE
    nbc = nch - CMP_RATIO + 1
    ch = kv[:, :nch * CMP_STRIDE].reshape(bsz, nch, CMP_STRIDE, G, hd)
    pos = pos_emb.reshape(CMP_RATIO, CMP_STRIDE, hd)
    h = b1
    for r in range(CMP_RATIO):
        h = h + jnp.einsum('bcpgd,pdh->bcgh', ch[:, r:r + nbc] + pos[r][:, None, :], w1[r])
    return jax.nn.gelu(h) @ w2 + b2


def nsa_mixer(xn, p0, k_cmp_past, v_cmp_past, k_slc_past, v_slc_past, k_win_past, v_win_past,
              w_in, b_gate, cmp_pos, cmp_w1, cmp_b1, cmp_w2, cmp_b2, w_out):
    bsz, T, _ = xn.shape
    H, G, HPG, hd = NSA_HEADS, NSA_KV, NSA_HPG, NSA_HD
    q, kc, vc, ks, vs, kw, vw, gate = split_cols(xn @ w_in, [H * hd] + [G * hd] * 6 + [3 * H])
    q = q.reshape(bsz, T, G, HPG, hd) * (hd ** -0.5)
    kc, vc, ks, vs, kw, vw = [a.reshape(bsz, T, G, hd) for a in (kc, vc, ks, vs, kw, vw)]
    gate = jax.nn.sigmoid((gate + b_gate).astype(jnp.float32)).reshape(bsz, T, G, HPG, 3)
    q_pos = p0 + jnp.arange(T)

    ck = nsa_compress(jnp.concatenate([k_cmp_past, kc], 1), cmp_pos[0], cmp_w1[0], cmp_b1[0], cmp_w2[0], cmp_b2[0])
    cv = nsa_compress(jnp.concatenate([v_cmp_past, vc], 1), cmp_pos[1], cmp_w1[1], cmp_b1[1], cmp_w2[1], cmp_b2[1])
    nbc = ck.shape[1]
    blk_end = jnp.arange(nbc) * CMP_STRIDE + (CMP_BLOCK - 1)
    p_cmp = masked_softmax(jnp.einsum('btgqd,bngd->bgqtn', q, ck), blk_end[None, :] <= q_pos[:, None])
    o_cmp = jnp.einsum('bgqtn,bngd->btgqd', p_cmp.astype(cv.dtype), cv)

    k_all = jnp.concatenate([k_slc_past, ks], 1)
    v_all = jnp.concatenate([v_slc_past, vs], 1)
    tk = k_all.shape[1]
    nbs = -(-tk // SEL_BLOCK)
    c_start = np.arange(nbc) * CMP_STRIDE
    s_start = np.arange(nbs) * SEL_BLOCK
    overlap = np.clip(np.minimum(c_start[:, None] + CMP_BLOCK, s_start[None, :] + SEL_BLOCK)
                      - np.maximum(c_start[:, None], s_start[None, :]), 0, None) / CMP_STRIDE
    imp = jnp.einsum('bgqtn,nj->bgtj', p_cmp, jnp.asarray(overlap, jnp.float32))
    blk = jnp.arange(nbs)[None, :]
    t_blk = (q_pos // SEL_BLOCK)[:, None]
    forced = (blk == 0) | (blk == t_blk) | (blk == t_blk - 1)
    visible = blk * SEL_BLOCK <= q_pos[:, None]
    score = jnp.where(visible, jnp.where(forced, BIG, imp), NEG)
    n_sel = min(N_SELECT, nbs)
    _, sel_idx = lax.top_k(score, n_sel)
    pad = nbs * SEL_BLOCK - tk

    def to_blocks(a):
        a = jnp.pad(a, ((0, 0), (0, pad), (0, 0), (0, 0)))
        return a.reshape(bsz, nbs, SEL_BLOCK, G, hd).transpose(0, 3, 1, 2, 4)

    kb, vb = to_blocks(k_all), to_blocks(v_all)
    gather = jax.vmap(jax.vmap(lambda a, i: a[i]))

    def sel_block(args):
        qq, ii, pp = args
        qb = pp.shape[0]
        kk = gather(kb, ii).reshape(bsz, G, qb, n_sel * SEL_BLOCK, hd)
        vv = gather(vb, ii).reshape(bsz, G, qb, n_sel * SEL_BLOCK, hd)
        kpos = (ii[..., None] * SEL_BLOCK + jnp.arange(SEL_BLOCK)).reshape(bsz, G, 1, qb, n_sel * SEL_BLOCK)
        p = masked_softmax(jnp.einsum('btgqd,bgtkd->bgqtk', qq, kk), kpos <= pp[:, None])
        return jnp.einsum('bgqtk,bgtkd->btgqd', p.astype(vv.dtype), vv)

    qb = math.gcd(T, SEL_QBLOCK)
    nqb = T // qb
    o_slc = lax.map(sel_block, (jnp.moveaxis(q.reshape(bsz, nqb, qb, G, HPG, hd), 1, 0),
                                jnp.moveaxis(sel_idx.reshape(bsz, G, nqb, qb, n_sel), 2, 0),
                                q_pos.reshape(nqb, qb)))
    o_slc = jnp.moveaxis(o_slc, 0, 1).reshape(bsz, T, G, HPG, hd)

    wb = k_win_past.shape[1]
    kw_all = jnp.concatenate([k_win_past, kw], 1)
    vw_all = jnp.concatenate([v_win_past, vw], 1)
    npad = WINDOW + wb + T
    kw_pad = jnp.pad(kw_all, ((0, 0), (WINDOW, 0), (0, 0), (0, 0)))
    vw_pad = jnp.pad(vw_all, ((0, 0), (WINDOW, 0), (0, 0), (0, 0)))
    kpos_pad = (p0 - wb - WINDOW) + jnp.arange(npad)
    kvalid = jnp.arange(npad) >= WINDOW
    qw = math.gcd(T, WIN_QBLOCK)
    nqw = T // qw

    def win_block(args):
        jb, qq, pp = args
        start = jb * qw + wb
        kk = lax.dynamic_slice_in_dim(kw_pad, start, qw + WINDOW, axis=1)
        vv = lax.dynamic_slice_in_dim(vw_pad, start, qw + WINDOW, axis=1)
        kp = lax.dynamic_slice_in_dim(kpos_pad, start, qw + WINDOW)
        ok = lax.dynamic_slice_in_dim(kvalid, start, qw + WINDOW)
        mask = ok[None, :] & (kp[None, :] <= pp[:, None]) & (kp[None, :] >= pp[:, None] - WINDOW)
        p = masked_softmax(jnp.einsum('btgqd,bkgd->bgqtk', qq, kk), mask)
        return jnp.einsum('bgqtk,bkgd->btgqd', p.astype(vv.dtype), vv)

    o_win = lax.map(win_block, (jnp.arange(nqw),
                                jnp.moveaxis(q.reshape(bsz, nqw, qw, G, HPG, hd), 1, 0),
                                q_pos.reshape(nqw, qw)))
    o_win = jnp.moveaxis(o_win, 0, 1).reshape(bsz, T, G, HPG, hd)

    o = gate[..., 0:1] * o_cmp + gate[..., 1:2] * o_slc + gate[..., 2:3] * o_win
    out = o.reshape(bsz, T, H * hd).astype(xn.dtype) @ w_out
    keep = min(WINDOW, wb + T)
    return out, kc, vc, ks, vs, kw_all[:, -keep:], vw_all[:, -keep:]


def gather_pages(pool, table):
    return pool[table].reshape(table.shape[0], table.shape[1] * pool.shape[1], pool.shape[2], pool.shape[3])


def setup_inputs(seed: int = 0) -> dict:
    key = jax.random.key(seed)
    ks = iter(jax.random.split(key, 48))
    f32 = jnp.float32

    def nrm(shape, scale=1.0):
        return scale * jax.random.normal(next(ks), shape, f32)

    n_pages = PAST_LEN // PAGE_SIZE
    n_pool = (DEC_BATCH * n_pages * 5) // 4
    wb = min(WINDOW, PAST_LEN)
    H, G, hd = NSA_HEADS, NSA_KV, NSA_HD
    page_table = jax.random.permutation(next(ks), n_pool)[:DEC_BATCH * n_pages]
    page_table = page_table.reshape(DEC_BATCH, n_pages).astype(jnp.int32)
    return {
        'x_prompt': nrm((BATCH, SEQ, D_MODEL)),
        'x_sample': nrm((DEC_BATCH, DEC_SEQ, D_MODEL)),
        'state_a_C': nrm((N_A, DEC_BATCH, MLSTM_HEADS, MLSTM_DK, MLSTM_DV)),
        'state_a_n': nrm((N_A, DEC_BATCH, MLSTM_HEADS, MLSTM_DK)),
        'state_a_m': nrm((N_A, DEC_BATCH, MLSTM_HEADS)),
        'state_b_conv': nrm((N_B, DEC_BATCH, CONV_W - 1, D_MODEL)),
        'cache_c_k_cmp': nrm((N_C, n_pool, PAGE_SIZE, G, hd)),
        'cache_c_v_cmp': nrm((N_C, n_pool, PAGE_SIZE, G, hd)),
        'cache_c_k_slc': nrm((N_C, n_pool, PAGE_SIZE, G, hd)),
        'cache_c_v_slc': nrm((N_C, n_pool, PAGE_SIZE, G, hd)),
        'cache_c_k_win': nrm((N_C, DEC_BATCH, wb, G, hd)),
        'cache_c_v_win': nrm((N_C, DEC_BATCH, wb, G, hd)),
        'page_table': page_table,
        'norm_w': 1.0 + nrm((DEPTH, 4, D_MODEL), 0.05),
        'ffn_w_in': nrm((DEPTH, D_MODEL, 2 * FF_DIM), D_MODEL ** -0.5),
        'ffn_w_out': nrm((DEPTH, FF_DIM, D_MODEL), FF_DIM ** -0.5),
        'a_w_in': nrm((N_A, D_MODEL, A_IN), D_MODEL ** -0.5),
        'a_b_gates': jnp.concatenate([nrm((N_A, MLSTM_HEADS), 0.1),
                                      FORGET_BIAS + nrm((N_A, MLSTM_HEADS), 0.1)], axis=-1),
        'a_head_norm': 1.0 + nrm((N_A, MLSTM_HEADS * MLSTM_DV), 0.05),
        'a_w_out': nrm((N_A, MLSTM_HEADS * MLSTM_DV, D_MODEL), (MLSTM_HEADS * MLSTM_DV) ** -0.5),
        'b_w_in': nrm((N_B, D_MODEL, 3 * D_MODEL), D_MODEL ** -0.5),
        'b_conv_w': nrm((N_B, CONV_W, D_MODEL), CONV_W ** -0.5),
        'b_w_out': nrm((N_B, D_MODEL, D_MODEL), D_MODEL ** -0.5),
        'c_w_in': nrm((N_C, D_MODEL, C_IN), D_MODEL ** -0.5),
        'c_b_gate': nrm((N_C, 3 * H), 0.1),
        'c_cmp_pos': nrm((N_C, 2, CMP_BLOCK, hd), 0.1),
        'c_cmp_w1': nrm((N_C, 2, CMP_RATIO, CMP_STRIDE, hd, CMP_HIDDEN), (CMP_BLOCK * hd) ** -0.5),
        'c_cmp_b1': nrm((N_C, 2, CMP_HIDDEN), 0.01),
        'c_cmp_w2': nrm((N_C, 2, CMP_HIDDEN, hd), CMP_HIDDEN ** -0.5),
        'c_cmp_b2': nrm((N_C, 2, hd), 0.01),
        'c_w_out': nrm((N_C, H * hd, D_MODEL), (H * hd) ** -0.5),
    }


def reference(x_prompt, x_sample, state_a_C, state_a_n, state_a_m, state_b_conv,
              cache_c_k_cmp, cache_c_v_cmp, cache_c_k_slc, cache_c_v_slc,
              cache_c_k_win, cache_c_v_win, page_table,
              norm_w, ffn_w_in, ffn_w_out,
              a_w_in, a_b_gates, a_head_norm, a_w_out,
              b_w_in, b_conv_w, b_w_out,
              c_w_in, c_b_gate, c_cmp_pos, c_cmp_w1, c_cmp_b1, c_cmp_w2, c_cmp_b2, c_w_out):
    f32 = jnp.float32
    past_len = page_table.shape[1] * cache_c_k_cmp.shape[2]
    xp, xs = x_prompt, x_sample
    bp = xp.shape[0]
    G, hd = NSA_KV, NSA_HD
    pa, sa, pb, sb, pc, sc = [], [], [], [], [], []
    for i in range(DEPTH):
        kind, j = i % N_MIXERS, i // N_MIXERS
        nw = norm_w[i]
        hp_in, hs_in = rmsnorm(xp, nw[0]), rmsnorm(xs, nw[0])
        if kind == 0:
            wts = (a_w_in[j], a_b_gates[j], a_head_norm[j], a_w_out[j])
            hp, *st_p = mlstm_mixer(hp_in, jnp.zeros((bp, MLSTM_HEADS, MLSTM_DK, MLSTM_DV), f32),
                                    jnp.zeros((bp, MLSTM_HEADS, MLSTM_DK), f32),
                                    jnp.zeros((bp, MLSTM_HEADS), f32), *wts)
            hs, *st_s = mlstm_mixer(hs_in, state_a_C[j], state_a_n[j], state_a_m[j], *wts)
            pa.append(st_p)
            sa.append(st_s)
        elif kind == 1:
            hp, st_p = shortconv_mixer(hp_in, jnp.zeros((bp, CONV_W - 1, D_MODEL), xp.dtype),
                                       b_w_in[j], b_conv_w[j], b_w_out[j])
            hs, st_s = shortconv_mixer(hs_in, state_b_conv[j], b_w_in[j], b_conv_w[j], b_w_out[j])
            pb.append(st_p)
            sb.append(st_s)
        else:
            wts = (c_w_in[j], c_b_gate[j], c_cmp_pos[j], c_cmp_w1[j], c_cmp_b1[j],
                   c_cmp_w2[j], c_cmp_b2[j], c_w_out[j])
            empty = jnp.zeros((bp, 0, G, hd), xp.dtype)
            hp, *st_p = nsa_mixer(hp_in, 0, empty, empty, empty, empty, empty, empty, *wts)
            hs, *st_s = nsa_mixer(hs_in, past_len,
                                  gather_pages(cache_c_k_cmp[j], page_table),
                                  gather_pages(cache_c_v_cmp[j], page_table),
                                  gather_pages(cache_c_k_slc[j], page_table),
                                  gather_pages(cache_c_v_slc[j], page_table),
                                  cache_c_k_win[j], cache_c_v_win[j], *wts)
            pc.append(st_p)
            sc.append(st_s)
        xp = swiglu_block(xp + rmsnorm(hp, nw[1]), nw, ffn_w_in[i], ffn_w_out[i])
        xs = swiglu_block(xs + rmsnorm(hs, nw[1]), nw, ffn_w_in[i], ffn_w_out[i])

    p_a_C, p_a_n, p_a_m = [jnp.stack(a) for a in zip(*pa)]
    s_a_C, s_a_n, s_a_m = [jnp.stack(a) for a in zip(*sa)]
    p_b_conv = jnp.stack(pb)
    s_b_conv = jnp.stack(sb)
    p_c_k_cmp, p_c_v_cmp, p_c_k_slc, p_c_v_slc, p_c_k_win, p_c_v_win = [jnp.stack(a) for a in zip(*pc)]
    s_c_k_cmp, s_c_v_cmp, s_c_k_slc, s_c_v_slc, s_c_k_win, s_c_v_win = [jnp.stack(a) for a in zip(*sc)]
    return (xp, xs,
            p_a_C, p_a_n, p_a_m, s_a_C, s_a_n, s_a_m,
            p_b_conv, s_b_conv,
            p_c_k_cmp, p_c_v_cmp, p_c_k_slc, p_c_v_slc, p_c_k_win, p_c_v_win,
            s_c_k_cmp, s_c_v_cmp, s_c_k_slc, s_c_v_slc, s_c_k_win, s_c_v_win)
```

```python
import functools
import math

import jax
import jax.numpy as jnp
from jax import lax
from jax.experimental import pallas as pl
from jax.experimental.pallas import tpu as pltpu

F32 = jnp.float32
BF16 = jnp.bfloat16

EPS = 1e-6
NEG = -1e30
BIG = 1e30

LANES = 128
SUBLANES = 8
VMEM_LIMIT = 56 * 1024 * 1024

MLSTM_HEADS = 8
MLSTM_CHUNK = 256
FORGET_BIAS_COL = MLSTM_HEADS
CONV_W = 3
NSA_HD = 128
NSA_KV = 4
NSA_HPG = 4
CMP_BLOCK = 32
CMP_STRIDE = 16
SEL_BLOCK = 64
N_SELECT = 16
WINDOW = 512
ATT_TILE = 256
PAGES_PER_STEP = 8


def _cparams(*sem):
    return pltpu.CompilerParams(dimension_semantics=sem, vmem_limit_bytes=VMEM_LIMIT)


def _dot(a, b):
    return jnp.dot(a, b, preferred_element_type=F32)


def _dot_nt(a, b):
    return lax.dot_general(a, b, (((1,), (1,)), ((), ())), preferred_element_type=F32)


def _dot_tn(a, b):
    return lax.dot_general(a, b, (((0,), (0,)), ((), ())), preferred_element_type=F32)


def _split_bf16(x):
    hi = x.astype(BF16)
    lo = (x - hi.astype(F32)).astype(BF16)
    return hi, lo


def _iota(shape, dim):
    return lax.broadcasted_iota(jnp.int32, shape, dim)


def _sigmoid(x):
    return 1.0 / (1.0 + jnp.exp(-x))


def _rms(x):
    return x * lax.rsqrt(jnp.mean(x * x, axis=-1, keepdims=True) + EPS)


def _row_tile(m, want):
    t = min(m, want)
    assert m % t == 0, (m, t)
    return t


def _norm_kernel(x_ref, w_ref, o_ref):
    o_ref[...] = (_rms(x_ref[...]) * w_ref[...]).astype(BF16)


def norm_cast(x, w):
    m, d = x.shape
    tm = _row_tile(m, 512)
    return pl.pallas_call(
        _norm_kernel,
        grid=(m // tm,),
        in_specs=[pl.BlockSpec((tm, d), lambda i: (i, 0)), pl.BlockSpec((1, d), lambda i: (0, 0))],
        out_specs=pl.BlockSpec((tm, d), lambda i: (i, 0)),
        out_shape=jax.ShapeDtypeStruct((m, d), BF16),
        compiler_params=_cparams("parallel"),
    )(x, w.reshape(1, d))


def _proj_kernel(x_ref, *refs, n_groups, epilogue):
    w_refs, out_refs = refs[:n_groups], refs[n_groups:]
    x = x_ref[...]
    epilogue([_dot(x, w[...]) for w in w_refs], out_refs)


def _proj(x, w, *, col0, tn, n_tiles, group_stride, n_groups, epilogue, outs, tm_want=1024):
    m, k = x.shape
    tm = _row_tile(m, tm_want)
    assert col0 % tn == 0
    b0 = col0 // tn
    w_specs = [pl.BlockSpec((k, tn), functools.partial(lambda j, i, off: (0, off + j), off=b0 + g * group_stride))
               for g in range(n_groups)]
    return pl.pallas_call(
        functools.partial(_proj_kernel, n_groups=n_groups, epilogue=epilogue),
        grid=(n_tiles, m // tm),
        in_specs=[pl.BlockSpec((tm, k), lambda j, i: (i, 0))] + w_specs,
        out_specs=[spec for _, spec in outs],
        out_shape=[shape for shape, _ in outs],
        compiler_params=_cparams("parallel", "parallel"),
    )(x, *([w] * n_groups))


def _ep_plain(accs, outs):
    outs[0][...] = accs[0]


def proj_plain(x, w, col0, n_cols, tn):
    m = x.shape[0]
    tm = _row_tile(m, 1024)
    out = (jax.ShapeDtypeStruct((m, n_cols), F32), pl.BlockSpec((tm, tn), lambda j, i: (i, j)))
    return _proj(x, w, col0=col0, tn=tn, n_tiles=n_cols // tn, group_stride=0, n_groups=1,
                 epilogue=_ep_plain, outs=[out])[0]


def _ep_swiglu(accs, outs):
    g, u = accs
    outs[0][...] = (g * _sigmoid(g) * u).astype(BF16)


def proj_swiglu(x, w, ff):
    m = x.shape[0]
    tn = 512
    tm = _row_tile(m, 1024)
    out = (jax.ShapeDtypeStruct((m, ff), BF16), pl.BlockSpec((tm, tn), lambda j, i: (i, j)))
    return _proj(x, w, col0=0, tn=tn, n_tiles=ff // tn, group_stride=ff // tn, n_groups=2,
                 epilogue=_ep_swiglu, outs=[out])[0]


def _ep_conv_in(accs, outs):
    bg, cg, u = accs
    outs[0][...] = bg
    outs[1][...] = cg * u


def proj_conv_in(x, w, d):
    m = x.shape[0]
    tn = 512
    tm = _row_tile(m, 1024)
    out = (jax.ShapeDtypeStruct((m, d), F32), pl.BlockSpec((tm, tn), lambda j, i: (i, j)))
    return _proj(x, w, col0=0, tn=tn, n_tiles=d // tn, group_stride=d // tn, n_groups=3,
                 epilogue=_ep_conv_in, outs=[out, out])


def _ep_heads(accs, outs, *, with_tok):
    acc = accs[0]
    hm = outs[0]
    for c in range(hm.shape[0]):
        hm[c] = acc[:, c * LANES:(c + 1) * LANES]
    if with_tok:
        outs[1][...] = acc


def proj_heads(x, w, col0, n_cols, with_tok):
    m = x.shape[0]
    tn = 1024
    tm = _row_tile(m, 1024)
    hpt = tn // LANES
    outs = [(jax.ShapeDtypeStruct((n_cols // LANES, m, LANES), F32),
             pl.BlockSpec((hpt, tm, LANES), lambda j, i: (j, i, 0)))]
    if with_tok:
        outs.append((jax.ShapeDtypeStruct((m, n_cols), F32), pl.BlockSpec((tm, tn), lambda j, i: (i, j))))
    return _proj(x, w, col0=col0, tn=tn, n_tiles=n_cols // tn, group_stride=0, n_groups=1,
                 epilogue=functools.partial(_ep_heads, with_tok=with_tok), outs=outs)


def _out_kernel(h_ref, w_ref, r_ref, nwa_ref, nwb_ref, *refs, nk, emit_next):
    if emit_next:
        x_ref, xn_ref = refs[0], refs[1]
        acc_ref = refs[2] if nk > 1 else None
    else:
        x_ref, xn_ref = refs[0], None
        acc_ref = refs[1] if nk > 1 else None

    def finalize(y):
        x_new = r_ref[...] + _rms(y) * nwa_ref[...]
        x_ref[...] = x_new
        if emit_next:
            xn_ref[...] = (_rms(x_new) * nwb_ref[...]).astype(BF16)

    part = _dot(h_ref[...], w_ref[...])
    if nk == 1:
        finalize(part)
        return
    kk = pl.program_id(1)

    @pl.when(kk == 0)
    def _():
        acc_ref[...] = part

    @pl.when(kk > 0)
    def _():
        acc_ref[...] += part

    @pl.when(kk == nk - 1)
    def _():
        finalize(acc_ref[...])


def proj_out(h, w, resid, nw_post, nw_next, emit_next=True):
    m, k = h.shape
    d = w.shape[1]
    tm = _row_tile(m, 512)
    tk = k if k <= 2048 else 1408
    assert k % tk == 0
    nk = k // tk
    row = lambda i, kk: (i, 0)
    out_shape = [jax.ShapeDtypeStruct((m, d), F32)]
    out_specs = [pl.BlockSpec((tm, d), row)]
    if emit_next:
        out_shape.append(jax.ShapeDtypeStruct((m, d), BF16))
        out_specs.append(pl.BlockSpec((tm, d), row))
    res = pl.pallas_call(
        functools.partial(_out_kernel, nk=nk, emit_next=emit_next),
        grid=(m // tm, nk),
        in_specs=[pl.BlockSpec((tm, tk), lambda i, kk: (i, kk)),
                  pl.BlockSpec((tk, d), lambda i, kk: (kk, 0)),
                  pl.BlockSpec((tm, d), row),
                  pl.BlockSpec((1, d), lambda i, kk: (0, 0)),
                  pl.BlockSpec((1, d), lambda i, kk: (0, 0))],
        out_specs=out_specs,
        out_shape=out_shape,
        scratch_shapes=[pltpu.VMEM((tm, d), F32)] if nk > 1 else [],
        compiler_params=_cparams("parallel", "arbitrary"),
    )(h, w, resid, nw_post.reshape(1, d), nw_next.reshape(1, d))
    return (res[0], res[1]) if emit_next else (res[0], None)


def _log_sigmoid(x):
    return jnp.minimum(x, 0.0) - jnp.log1p(jnp.exp(-jnp.abs(x)))


def _mlstm_kernel(q_ref, k_ref, v_ref, o_ref, g_ref, bg_ref, hn_ref, hg_ref, c_ref, n_ref, m_ref, *, L, H, DK, DV):
    @pl.when(pl.program_id(1) == 0)
    def _():
        c_ref[...] = jnp.zeros_like(c_ref)
        n_ref[...] = jnp.zeros_like(n_ref)
        m_ref[...] = jnp.zeros_like(m_ref)

    gpre = g_ref[...] + bg_ref[...]
    lf = _log_sigmoid(gpre)
    rows = _iota((L, L), 0)
    cols = _iota((L, L), 1)
    causal = rows >= cols
    tril = jnp.where(causal, 1.0, 0.0).astype(BF16)
    lf_hi, lf_lo = _split_bf16(lf)
    bcum = _dot(tril, lf_hi) + _dot(tril, lf_lo)
    g_t = gpre.T
    b_t = bcum.T
    for h in range(H):
        f = FORGET_BIAS_COL + h
        a_col = bcum[:, f:f + 1]
        i_col = gpre[:, h:h + 1]
        b_row = b_t[f:f + 1, :]
        i_row = g_t[h:h + 1, :]
        m_prev = m_ref[0, h:h + 1, 0:1]
        log_d = jnp.where(causal, a_col - b_row + i_row, NEG)
        log_inter = a_col + m_prev
        m_t = jnp.maximum(log_inter, jnp.max(log_d, axis=-1, keepdims=True))
        d = jnp.exp(log_d - m_t)
        inter = jnp.exp(log_inter - m_t)
        qh = q_ref[:, h * DK:(h + 1) * DK]
        kh = k_ref[:, h * DK:(h + 1) * DK] * (DK ** -0.5)
        vb = v_ref[:, h * DV:(h + 1) * DV].astype(BF16)
        qb = qh.astype(BF16)
        s = _dot_nt(qb, kh.astype(BF16)) * d
        c_old = c_ref[0, h]
        n_old = n_ref[0, h:h + 1, :]
        num = _dot(s.astype(BF16), vb) + inter * _dot(qb, c_old.astype(BF16))
        den = jnp.sum(s, axis=-1, keepdims=True) + inter * jnp.sum(qh * n_old, axis=-1, keepdims=True)
        hh = _rms(num / jnp.maximum(jnp.abs(den), jnp.exp(-m_t)))
        gate = _sigmoid(o_ref[:, h * DV:(h + 1) * DV])
        hg_ref[:, h * DV:(h + 1) * DV] = (hh * hn_ref[:, h * DV:(h + 1) * DV] * gate).astype(BF16)
        m_new = m_t[L - 1:L, :]
        w_col = jnp.exp(a_col[L - 1:L, :] - a_col + i_col - m_new)
        decay = jnp.exp(log_inter[L - 1:L, :] - m_new)
        kw = kh * w_col
        c_ref[0, h] = decay * c_old + _dot_tn(kw.astype(BF16), vb)
        n_ref[0, h:h + 1, :] = decay * n_old + jnp.sum(kw, axis=0, keepdims=True)
        m_ref[0, h:h + 1, :] = jnp.broadcast_to(m_new, (1, LANES))


def mlstm_prompt(qkvo, gates, b_gates, head_norm, bsz, T):
    H = MLSTM_HEADS
    m = qkvo.shape[0]
    hdv = qkvo.shape[1] // 3
    hdk = hdv // 2
    DK, DV = hdk // H, hdv // H
    L = math.gcd(T, MLSTM_CHUNK)
    nc = T // L
    row = lambda b, c: (b * nc + c, 0)
    state = lambda b, c: (b, 0, 0)
    return pl.pallas_call(
        functools.partial(_mlstm_kernel, L=L, H=H, DK=DK, DV=DV),
        grid=(bsz, nc),
        in_specs=[pl.BlockSpec((L, hdk), row),
                  pl.BlockSpec((L, hdk), lambda b, c: (b * nc + c, 1)),
                  pl.BlockSpec((L, hdv), lambda b, c: (b * nc + c, 1)),
                  pl.BlockSpec((L, hdv), lambda b, c: (b * nc + c, 2)),
                  pl.BlockSpec((L, LANES), row),
                  pl.BlockSpec((1, LANES), lambda b, c: (0, 0)),
                  pl.BlockSpec((1, hdv), lambda b, c: (0, 0))],
        out_specs=[pl.BlockSpec((L, hdv), row),
                   pl.BlockSpec((1, H, DK, DV), lambda b, c: (b, 0, 0, 0)),
                   pl.BlockSpec((1, H, DK), state),
                   pl.BlockSpec((1, H, LANES), state)],
        out_shape=[jax.ShapeDtypeStruct((m, hdv), BF16),
                   jax.ShapeDtypeStruct((bsz, H, DK, DV), F32),
                   jax.ShapeDtypeStruct((bsz, H, DK), F32),
                   jax.ShapeDtypeStruct((bsz, H, LANES), F32)],
        compiler_params=_cparams("parallel", "arbitrary"),
    )(qkvo, qkvo, qkvo, qkvo, gates, _pad_lanes(b_gates.reshape(1, -1)), head_norm.reshape(1, hdv))


def _pad_lanes(x, width=LANES):
    return jnp.pad(x, ((0, 0), (0, width - x.shape[-1])))


def _to_col(row, n):
    eye = _iota((n, n), 0) == _iota((n, n), 1)
    return jnp.sum(jnp.where(eye, jnp.broadcast_to(row, (n, n)), 0.0), axis=-1, keepdims=True)


def _mlstm_step_kernel(x_ref, g_ref, bg_ref, hn_ref, c0_ref, n0_ref, m0_ref, hg_ref, c_ref, n_ref, m_ref,
                       *, H, DK, DV):
    x = x_ref[0]
    gpre = g_ref[0] + bg_ref[...]
    lf = _log_sigmoid(gpre)
    hdk = H * DK
    hdv = H * DV
    for h in range(H):
        f = FORGET_BIAS_COL + h
        ig = gpre[:, h:h + 1]
        m_prev = m0_ref[0, h:h + 1, :]
        log_inter = lf[:, f:f + 1] + m_prev
        m_t = jnp.maximum(log_inter, ig)
        d = jnp.exp(ig - m_t)
        inter = jnp.exp(log_inter - m_t)
        q = x[:, h * DK:(h + 1) * DK]
        k = x[:, hdk + h * DK:hdk + (h + 1) * DK] * (DK ** -0.5)
        v = x[:, 2 * hdk + h * DV:2 * hdk + (h + 1) * DV]
        og = x[:, 2 * hdk + hdv + h * DV:2 * hdk + hdv + (h + 1) * DV]
        c_old = c0_ref[0, h]
        n_old = n0_ref[0, h:h + 1, :]
        s = jnp.sum(q * k, axis=-1, keepdims=True) * d
        q_col = _to_col(q, DK)
        k_col = _to_col(k, DK)
        num = s * v + inter * jnp.sum(q_col * c_old, axis=0, keepdims=True)
        den = s + inter * jnp.sum(q * n_old, axis=-1, keepdims=True)
        hh = _rms(num / jnp.maximum(jnp.abs(den), jnp.exp(-m_t)))
        hg_ref[0, :, h * DV:(h + 1) * DV] = (hh * hn_ref[:, h * DV:(h + 1) * DV] * _sigmoid(og)).astype(BF16)
        c_ref[0, h] = inter * c_old + d * (k_col * v)
        n_ref[0, h:h + 1, :] = inter * n_old + d * k
        m_ref[0, h:h + 1, :] = m_t


def mlstm_step(qkvo, gates, b_gates, head_norm, c0, n0, m0):
    bsz, H, DK, DV = c0.shape
    wid = qkvo.shape[1]
    hdv = H * DV
    one = lambda b: (b, 0, 0)
    hg, c1, n1, m1 = pl.pallas_call(
        functools.partial(_mlstm_step_kernel, H=H, DK=DK, DV=DV),
        grid=(bsz,),
        in_specs=[pl.BlockSpec((1, 1, wid), one),
                  pl.BlockSpec((1, 1, LANES), one),
                  pl.BlockSpec((1, LANES), lambda b: (0, 0)),
                  pl.BlockSpec((1, hdv), lambda b: (0, 0)),
                  pl.BlockSpec((1, H, DK, DV), lambda b: (b, 0, 0, 0)),
                  pl.BlockSpec((1, H, DK), one),
                  pl.BlockSpec((1, H, 1), one)],
        out_specs=[pl.BlockSpec((1, 1, hdv), one),
                   pl.BlockSpec((1, H, DK, DV), lambda b: (b, 0, 0, 0)),
                   pl.BlockSpec((1, H, DK), one),
                   pl.BlockSpec((1, H, 1), one)],
        out_shape=[jax.ShapeDtypeStruct((bsz, 1, hdv), BF16),
                   jax.ShapeDtypeStruct((bsz, H, DK, DV), F32),
                   jax.ShapeDtypeStruct((bsz, H, DK), F32),
                   jax.ShapeDtypeStruct((bsz, H, 1), F32)],
        compiler_params=_cparams("parallel"),
    )(qkvo.reshape(bsz, 1, wid), gates.reshape(bsz, 1, LANES), _pad_lanes(b_gates.reshape(1, -1)),
      head_norm.reshape(1, hdv), c0, n0, m0.reshape(bsz, H, 1))
    return hg.reshape(bsz, hdv), c1, n1, m1.reshape(bsz, H)


def _conv_kernel(bg_ref, z_ref, zp_ref, cw_ref, o_ref, *, tm, T):
    i = pl.program_id(0)
    z = z_ref[...]
    zc = jnp.concatenate([zp_ref[...], z], axis=0)
    tpos = (i * tm + _iota((tm, 1), 0)) % T
    y = cw_ref[CONV_W - 1:CONV_W, :] * z
    for back in range(1, CONV_W):
        zb = zc[SUBLANES - back:SUBLANES - back + tm, :]
        y = y + cw_ref[CONV_W - 1 - back:CONV_W - back, :] * jnp.where(tpos >= back, zb, 0.0)
    o_ref[...] = (bg_ref[...] * y).astype(BF16)


def conv_gate(bg, z, conv_w, T):
    m, d = z.shape
    tm = _row_tile(T, 256)
    per = tm // SUBLANES
    return pl.pallas_call(
        functools.partial(_conv_kernel, tm=tm, T=T),
        grid=(m // tm,),
        in_specs=[pl.BlockSpec((tm, d), lambda i: (i, 0)),
                  pl.BlockSpec((tm, d), lambda i: (i, 0)),
                  pl.BlockSpec((SUBLANES, d), lambda i: (jnp.maximum(i * per - 1, 0), 0)),
                  pl.BlockSpec((CONV_W, d), lambda i: (0, 0))],
        out_specs=pl.BlockSpec((tm, d), lambda i: (i, 0)),
        out_shape=jax.ShapeDtypeStruct((m, d), BF16),
        compiler_params=_cparams("parallel"),
    )(bg, z, z, conv_w)


def _conv_step_kernel(bg_ref, z_ref, buf_ref, cw_ref, o_ref):
    y = cw_ref[CONV_W - 1:CONV_W, :] * z_ref[...]
    for j in range(CONV_W - 1):
        y = y + cw_ref[j:j + 1, :] * buf_ref[j]
    o_ref[...] = bg_ref[...] * y


def conv_gate_step(bg, z, buf, conv_w):
    bsz, d = z.shape
    full = lambda: (0, 0)
    return pl.pallas_call(
        _conv_step_kernel,
        in_specs=[pl.BlockSpec((bsz, d), full), pl.BlockSpec((bsz, d), full),
                  pl.BlockSpec((CONV_W - 1, bsz, d), lambda: (0, 0, 0)), pl.BlockSpec((CONV_W, d), full)],
        out_specs=pl.BlockSpec((bsz, d), full),
        out_shape=jax.ShapeDtypeStruct((bsz, d), F32),
    )(bg, z, buf, conv_w)


def _gelu(x):
    return 0.5 * x * (1.0 + jnp.tanh(math.sqrt(2.0 / math.pi) * (x + 0.044715 * x * x * x)))


def _cmp_finish(a, posw, b1, w2, b2):
    hid = a.shape[1] // 2
    a1 = a[:, hid:]
    a1_next = jnp.concatenate([a1[1:], a1[:1]], axis=0)
    pre = a[:, :hid] + a1_next + b1 + posw[0:1, :hid] + posw[1:2, hid:]
    return _dot(_gelu(pre).astype(BF16), w2) + b2


def _cmp_prompt_kernel(k_ref, v_ref, w1_ref, pos_ref, b1_ref, w2_ref, b2_ref, ck_ref, cv_ref, *, nch):
    for idx, (x_ref, o_ref) in enumerate(((k_ref, ck_ref), (v_ref, cv_ref))):
        w1 = w1_ref[idx]
        x2d = x_ref.at[0]
        lhs = jnp.concatenate([x2d[pl.ds(p, nch, stride=CMP_STRIDE), :] for p in range(CMP_STRIDE)], axis=1)
        a = _dot(lhs.astype(BF16), w1)
        posw = _dot(pos_ref[idx], w1)
        o_ref[0, 0] = _cmp_finish(a, posw, b1_ref[idx], w2_ref[idx], b2_ref[idx])


def _cmp_weights(cmp_pos, cmp_w1, cmp_b1, cmp_w2, cmp_b2):
    two, ratio, stride, hd, hid = cmp_w1.shape
    w1 = cmp_w1.transpose(0, 2, 3, 1, 4).reshape(two, stride * hd, ratio * hid).astype(BF16)
    pos = cmp_pos.reshape(two, ratio, stride * hd)
    pos = jnp.pad(pos, ((0, 0), (0, SUBLANES - ratio), (0, 0))).astype(BF16)
    return w1, pos, cmp_b1.reshape(two, 1, hid), cmp_w2.astype(BF16), cmp_b2.reshape(two, 1, hd)


def cmp_prompt(hm, k_slab, v_slab, bsz, T, cw):
    w1, pos, b1, w2, b2 = cw
    nch = T // CMP_STRIDE
    G = NSA_KV
    out = jax.ShapeDtypeStruct((bsz, G, nch, NSA_HD), F32)
    ospec = pl.BlockSpec((1, 1, nch, NSA_HD), lambda b, g: (b, g, 0, 0))
    whole = lambda a: pl.BlockSpec(a.shape, lambda b, g: (0,) * a.ndim)
    return pl.pallas_call(
        functools.partial(_cmp_prompt_kernel, nch=nch),
        grid=(bsz, G),
        in_specs=[pl.BlockSpec((1, T, NSA_HD), lambda b, g: (k_slab + g, b, 0)),
                  pl.BlockSpec((1, T, NSA_HD), lambda b, g: (v_slab + g, b, 0)),
                  whole(w1), whole(pos), whole(b1), whole(w2), whole(b2)],
        out_specs=[ospec, ospec],
        out_shape=[out, out],
        compiler_params=_cparams("parallel", "parallel"),
    )(hm, hm, w1, pos, b1, w2, b2)


def _overlap_t(nbs_pad, nbc_pad):
    j = _iota((nbs_pad, nbc_pad), 0)
    n = _iota((nbs_pad, nbc_pad), 1)
    lo = jnp.maximum(n * CMP_STRIDE, j * SEL_BLOCK)
    hi = jnp.minimum(n * CMP_STRIDE + CMP_BLOCK, j * SEL_BLOCK + SEL_BLOCK)
    return (jnp.maximum(hi - lo, 0) // CMP_STRIDE).astype(F32)


def _rank_select(score, n_sel, axis):
    n = score.shape[axis]
    idx = _iota(score.shape, axis)
    cnt = jnp.zeros(score.shape, F32)
    for jp in range(n):
        other = lax.slice_in_dim(score, jp, jp + 1, axis=axis)
        tie = jnp.where(idx > jp, 1.0, 0.0)
        cnt = cnt + jnp.where(other > score, 1.0, jnp.where(other == score, tie, 0.0))
    return jnp.where(cnt < n_sel, 1.0, 0.0)


def _online_update(carry, s, mask, v_b):
    m_run, l_run, acc = carry
    sm = jnp.where(mask > 0.5, s, NEG)
    m_new = jnp.maximum(m_run, jnp.max(sm, axis=-1, keepdims=True))
    alpha = jnp.exp(m_run - m_new)
    p = jnp.exp(sm - m_new) * mask
    l_new = alpha * l_run + jnp.sum(p, axis=-1, keepdims=True)
    return m_new, l_new, alpha * acc + _dot(p.astype(BF16), v_b)


def _nsa_attn_kernel(q_ref, ck_ref, cv_ref, ks_ref, vs_ref, kw_ref, vw_ref, g_ref, bg_ref, o_ref,
                     *, tq, nbc, nbs, nbs_pad):
    qi = pl.program_id(2)
    t0 = qi * tq
    hpg = q_ref.shape[0]
    rows = hpg * tq
    nbc_pad = ck_ref.shape[2]
    qb = (jnp.concatenate([q_ref[h] for h in range(hpg)], axis=0) * (NSA_HD ** -0.5)).astype(BF16)

    r = _iota((rows, nbc_pad), 0)
    n = _iota((rows, nbc_pad), 1)
    t_row = t0 + (r % tq)
    cmask = jnp.where(n < nbc, jnp.where(n * CMP_STRIDE + (CMP_BLOCK - 1) <= t_row, 1.0, 0.0), 0.0)
    sc = jnp.where(cmask > 0.5, _dot_nt(qb, ck_ref[0, 0].astype(BF16)), NEG)
    pe = jnp.exp(sc - jnp.max(sc, axis=-1, keepdims=True))
    p_cmp = pe / jnp.sum(pe, axis=-1, keepdims=True) * cmask
    o_cmp = _dot(p_cmp.astype(BF16), cv_ref[0, 0].astype(BF16))

    p_sum = p_cmp[0:tq]
    for h in range(1, hpg):
        p_sum = p_sum + p_cmp[h * tq:(h + 1) * tq]
    ov_t = _overlap_t(nbs_pad, nbc_pad).astype(BF16)
    p_hi, p_lo = _split_bf16(p_sum)
    imp_t = _dot_nt(ov_t, p_hi) + _dot_nt(ov_t, p_lo)
    j = _iota((nbs_pad, tq), 0)
    t = t0 + _iota((nbs_pad, tq), 1)
    t_blk = t // SEL_BLOCK
    forced = jnp.where(j == 0, 1.0, jnp.where(j == t_blk, 1.0, jnp.where(j == t_blk - 1, 1.0, 0.0)))
    visible = jnp.where(j < nbs, jnp.where(j * SEL_BLOCK <= t, 1.0, 0.0), 0.0)
    score = jnp.where(visible > 0.5, jnp.where(forced > 0.5, BIG, imp_t), NEG)
    sel_t = _rank_select(score, min(N_SELECT, nbs), axis=0).astype(BF16)

    tk = tq
    qpos = t0 + _iota((tq, tk), 0)
    kofs = _iota((tq, tk), 1)
    init = (jnp.full((rows, 1), NEG, F32), jnp.zeros((rows, 1), F32), jnp.zeros((rows, NSA_HD), F32))

    def slc_body(kt, carry):
        k0 = pl.multiple_of(kt * tk, tk)
        k_b = ks_ref[0, pl.ds(k0, tk), :].astype(BF16)
        v_b = vs_ref[0, pl.ds(k0, tk), :].astype(BF16)
        blk = (k0 + _iota((nbs_pad, tk), 1)) // SEL_BLOCK
        expand = jnp.where(blk == _iota((nbs_pad, tk), 0), 1.0, 0.0).astype(BF16)
        mask = jnp.where(k0 + kofs <= qpos, _dot_tn(sel_t, expand), 0.0)
        mask = jnp.concatenate([mask] * hpg, axis=0)
        return _online_update(carry, _dot_nt(qb, k_b), mask, v_b)

    m_s, l_s, acc_s = lax.fori_loop(0, qi + 1, slc_body, init)

    def win_body(kt, carry):
        k0 = pl.multiple_of(kt * tk, tk)
        k_b = kw_ref[0, pl.ds(k0, tk), :].astype(BF16)
        v_b = vw_ref[0, pl.ds(k0, tk), :].astype(BF16)
        kpos = k0 + kofs
        mask = jnp.where(kpos <= qpos, jnp.where(kpos >= qpos - WINDOW, 1.0, 0.0), 0.0)
        mask = jnp.concatenate([mask] * hpg, axis=0)
        return _online_update(carry, _dot_nt(qb, k_b), mask, v_b)

    first = jnp.maximum(qi - (WINDOW + tk - 1) // tk, 0)
    m_w, l_w, acc_w = lax.fori_loop(first, qi + 1, win_body, init)

    o_slc = acc_s / l_s
    o_win = acc_w / l_w
    gate = _sigmoid(g_ref[...] + bg_ref[...])
    for h in range(hpg):
        sl = slice(h * tq, (h + 1) * tq)
        o = (gate[:, 3 * h:3 * h + 1] * o_cmp[sl] + gate[:, 3 * h + 1:3 * h + 2] * o_slc[sl]
             + gate[:, 3 * h + 2:3 * h + 3] * o_win[sl])
        o_ref[:, h * NSA_HD:(h + 1) * NSA_HD] = o.astype(BF16)


def nsa_attn_prompt(hm, ck, cv, gates, b_gate, bsz, T):
    G, HPG = NSA_KV, NSA_HPG
    tq = math.gcd(T, ATT_TILE)
    nq = T // tq
    nch = T // CMP_STRIDE
    nbc = nch - CMP_BLOCK // CMP_STRIDE + 1
    nbs = -(-T // SEL_BLOCK)
    nbs_pad = -(-nbs // SUBLANES) * SUBLANES
    nheads = G * HPG
    slab = lambda s: pl.BlockSpec((1, T, NSA_HD), functools.partial(lambda b, g, i, s: (s + g, b, 0), s=s))
    cspec = pl.BlockSpec((1, 1, nch, NSA_HD), lambda b, g, i: (b, g, 0, 0))
    return pl.pallas_call(
        functools.partial(_nsa_attn_kernel, tq=tq, nbc=nbc, nbs=nbs, nbs_pad=nbs_pad),
        grid=(bsz, G, nq),
        in_specs=[pl.BlockSpec((HPG, tq, NSA_HD), lambda b, g, i: (g, b * nq + i, 0)),
                  cspec, cspec,
                  slab(nheads + 2 * G), slab(nheads + 3 * G), slab(nheads + 4 * G), slab(nheads + 5 * G),
                  pl.BlockSpec((tq, LANES), lambda b, g, i: (b * nq + i, g)),
                  pl.BlockSpec((1, LANES), lambda b, g, i: (0, g))],
        out_specs=pl.BlockSpec((tq, HPG * NSA_HD), lambda b, g, i: (b * nq + i, g)),
        out_shape=jax.ShapeDtypeStruct((bsz * T, nheads * NSA_HD), BF16),
        compiler_params=_cparams("parallel", "parallel", "arbitrary"),
    )(hm, ck, cv, hm, hm, hm, hm, gates, b_gate)


def _page_cmp_kernel(tbl_ref, *refs, P):
    k_pages, v_pages = refs[:P], refs[P:2 * P]
    w1_ref, ak_ref, av_ref = refs[2 * P:]
    rows = _iota((LANES, LANES), 0)
    cols = _iota((LANES, LANES), 1)
    cpp = LANES // CMP_STRIDE
    perm = jnp.where(cols == (rows % cpp) * CMP_STRIDE + rows // cpp, 1.0, 0.0).astype(BF16)
    for idx, (pages, a_ref) in enumerate(((k_pages, ak_ref), (v_pages, av_ref))):
        xp = [_dot(perm, pg[0].astype(BF16)) for pg in pages]
        blocks = []
        for g in range(NSA_KV):
            for x in xp:
                blocks.append(jnp.concatenate(
                    [x[p * cpp:(p + 1) * cpp, g * NSA_HD:(g + 1) * NSA_HD] for p in range(CMP_STRIDE)], axis=1))
        a = _dot(jnp.concatenate(blocks, axis=0).astype(BF16), w1_ref[idx])
        per_g = P * cpp
        for g in range(NSA_KV):
            a_ref[0, g] = a[g * per_g:(g + 1) * per_g]


def page_cmp(pool_k, pool_v, table, w1):
    n_pool, page, G, hd = pool_k.shape
    assert page == LANES and G == NSA_KV and hd == NSA_HD
    bsz, n_pages = table.shape
    P = math.gcd(n_pages, PAGES_PER_STEP)
    cpp = page // CMP_STRIDE
    hid2 = w1.shape[2]
    pk = pool_k.reshape(n_pool, page, G * hd)
    pv = pool_v.reshape(n_pool, page, G * hd)
    pspec = lambda i: pl.BlockSpec((1, page, G * hd),
                                   functools.partial(lambda b, s, tbl, i: (tbl[b * n_pages + s * P + i], 0, 0), i=i))
    out = jax.ShapeDtypeStruct((bsz, G, n_pages * cpp, hid2), F32)
    ospec = pl.BlockSpec((1, G, P * cpp, hid2), lambda b, s, tbl: (b, 0, s, 0))
    grid_spec = pltpu.PrefetchScalarGridSpec(
        num_scalar_prefetch=1,
        grid=(bsz, n_pages // P),
        in_specs=[pspec(i) for i in range(P)] * 2 + [pl.BlockSpec(w1.shape, lambda b, s, tbl: (0, 0, 0))],
        out_specs=[ospec, ospec],
    )
    return pl.pallas_call(
        functools.partial(_page_cmp_kernel, P=P),
        grid_spec=grid_spec,
        out_shape=[out, out],
        compiler_params=_cparams("parallel", "parallel"),
    )(table.reshape(-1), *([pk] * P), *([pv] * P), w1)


def _dec_cmp_kernel(ak_ref, av_ref, q_ref, pos_ref, w1_ref, b1_ref, w2_ref, b2_ref, o_ref, sel_ref,
                    *, nbc, nbs, nbs_pad, p0):
    nch = ak_ref.shape[2]
    for g in range(NSA_KV):
        cks = []
        for idx, a_ref in enumerate((ak_ref, av_ref)):
            posw = _dot(pos_ref[idx], w1_ref[idx])
            cks.append(_cmp_finish(a_ref[0, g], posw, b1_ref[idx], w2_ref[idx], b2_ref[idx]))
        ck, cv = cks
        qb = (q_ref[0, g] * (NSA_HD ** -0.5)).astype(BF16)
        n = _iota((SUBLANES, nch), 1)
        cmask = jnp.where(n < nbc, jnp.where(n * CMP_STRIDE + (CMP_BLOCK - 1) <= p0, 1.0, 0.0), 0.0)
        sc = jnp.where(cmask > 0.5, _dot_nt(qb, ck.astype(BF16)), NEG)
        pe = jnp.exp(sc - jnp.max(sc, axis=-1, keepdims=True))
        p_cmp = pe / jnp.sum(pe, axis=-1, keepdims=True) * cmask
        o_ref[0, g] = _dot(p_cmp.astype(BF16), cv.astype(BF16))

        head = jnp.where(_iota((SUBLANES, nch), 0) < NSA_HPG, p_cmp, 0.0)
        p_sum = jnp.broadcast_to(jnp.sum(head, axis=0, keepdims=True), (SUBLANES, nch))
        ov_t = _overlap_t(nbs_pad, nch).astype(BF16)
        p_hi, p_lo = _split_bf16(p_sum)
        imp = _dot_nt(p_hi, ov_t) + _dot_nt(p_lo, ov_t)
        j = _iota((SUBLANES, nbs_pad), 1)
        t_blk = p0 // SEL_BLOCK
        forced = jnp.where(j == 0, 1.0, jnp.where(j == t_blk, 1.0, jnp.where(j == t_blk - 1, 1.0, 0.0)))
        visible = jnp.where(j < nbs, jnp.where(j * SEL_BLOCK <= p0, 1.0, 0.0), 0.0)
        score = jnp.where(visible > 0.5, jnp.where(forced > 0.5, BIG, imp), NEG)
        s_row = jnp.broadcast_to(score[0:1], (nbs_pad, nbs_pad))
        s_col = jnp.concatenate([jnp.broadcast_to(score[0:1], (LANES, nbs_pad)).T] * (nbs_pad // LANES), axis=1)
        jr = _iota((nbs_pad, nbs_pad), 0)
        jc = _iota((nbs_pad, nbs_pad), 1)
        tie = jnp.where(jc < jr, 1.0, 0.0)
        beats = jnp.where(s_row > s_col, 1.0, jnp.where(s_row == s_col, tie, 0.0))
        rank = jnp.sum(beats, axis=-1, keepdims=True)
        slot = _iota((nbs_pad, LANES), 1).astype(F32)
        blk = _iota((nbs_pad, LANES), 0).astype(F32)
        picked = jnp.sum(jnp.where(rank == slot, blk, 0.0), axis=0, keepdims=True)
        sel_ref[0, g:g + 1, :] = picked.astype(jnp.int32)


def dec_cmp(ak, av, q, cw, p0):
    w1, pos, b1, w2, b2 = cw
    bsz, G, nch, hid2 = ak.shape
    nbc = nch - CMP_BLOCK // CMP_STRIDE + 1
    nbs = -(-(p0 + 1) // SEL_BLOCK)
    nbs_pad = -(-nbs // LANES) * LANES
    whole = lambda a: pl.BlockSpec(a.shape, lambda b: (0,) * a.ndim)
    aspec = pl.BlockSpec((1, G, nch, hid2), lambda b: (b, 0, 0, 0))
    return pl.pallas_call(
        functools.partial(_dec_cmp_kernel, nbc=nbc, nbs=nbs, nbs_pad=nbs_pad, p0=p0),
        grid=(bsz,),
        in_specs=[aspec, aspec, pl.BlockSpec((1, G, SUBLANES, NSA_HD), lambda b: (b, 0, 0, 0)),
                  whole(pos), whole(w1), whole(b1), whole(w2), whole(b2)],
        out_specs=[pl.BlockSpec((1, G, SUBLANES, NSA_HD), lambda b: (b, 0, 0, 0)),
                   pl.BlockSpec((1, G, LANES), lambda b: (b, 0, 0))],
        out_shape=[jax.ShapeDtypeStruct((bsz, G, SUBLANES, NSA_HD), F32),
                   jax.ShapeDtypeStruct((bsz, G, LANES), jnp.int32)],
        compiler_params=_cparams("parallel"),
    )(ak, av, q, pos, w1, b1, w2, b2)


def _dec_attn_kernel(sel_ref, tbl_ref, ks_ref, vs_ref, q_ref, new_ref, kw_ref, vw_ref, oc_ref, g_ref, bg_ref,
                     o_ref, m_ref, l_ref, acc_ref, *, n_cache_blocks, n_sel):
    b, g, s = pl.program_id(0), pl.program_id(1), pl.program_id(2)
    q = q_ref[0, 0] * (NSA_HD ** -0.5)
    qb = q.astype(BF16)
    new = new_ref[0, 0]

    @pl.when(s == 0)
    def _():
        m_ref[...] = jnp.broadcast_to(jnp.sum(q * new[0:1], axis=-1, keepdims=True), m_ref.shape)
        l_ref[...] = jnp.ones_like(l_ref)
        acc_ref[...] = jnp.broadcast_to(new[1:2], acc_ref.shape)

    blk = sel_ref[(b * NSA_KV + g) * n_sel + s]
    valid = jnp.where(blk < n_cache_blocks, 1.0, 0.0)
    sc = _dot_nt(qb, ks_ref[0, 0].astype(BF16))
    mask = jnp.full(sc.shape, valid, F32)
    m_new, l_new, acc_new = _online_update((m_ref[:, 0:1], l_ref[:, 0:1], acc_ref[...]), sc, mask,
                                           vs_ref[0, 0].astype(BF16))
    m_ref[...] = jnp.broadcast_to(m_new, m_ref.shape)
    l_ref[...] = jnp.broadcast_to(l_new, l_ref.shape)
    acc_ref[...] = acc_new

    @pl.when(s == n_sel - 1)
    def _():
        o_slc = acc_ref[...] / l_ref[:, 0:1]
        sw = _dot_nt(qb, kw_ref[0].astype(BF16))
        sn = jnp.sum(q * new[2:3], axis=-1, keepdims=True)
        mx = jnp.maximum(jnp.max(sw, axis=-1, keepdims=True), sn)
        pw = jnp.exp(sw - mx)
        pn = jnp.exp(sn - mx)
        o_win = (_dot(pw.astype(BF16), vw_ref[0].astype(BF16)) + pn * new[3:4]) / (
            jnp.sum(pw, axis=-1, keepdims=True) + pn)
        gate = jnp.broadcast_to(_sigmoid(g_ref[0, 0] + bg_ref[...]), (SUBLANES, LANES))
        head = _iota((SUBLANES, LANES), 0)
        lane = _iota((SUBLANES, LANES), 1)
        o = jnp.zeros((SUBLANES, NSA_HD), F32)
        for br, ob in enumerate((oc_ref[0, 0], o_slc, o_win)):
            o = o + jnp.sum(jnp.where(lane == 3 * head + br, gate, 0.0), axis=-1, keepdims=True) * ob
        o_ref[0, 0] = o


def dec_attn(sel, table, pool_ks, pool_vs, q, new, kw_past, vw_past, o_cmp, gates, b_gate):
    n_pool, page, G, hd = pool_ks.shape
    bsz, n_pages = table.shape
    halves = page // SEL_BLOCK
    wb = kw_past.shape[1]
    assert wb <= WINDOW
    n_sel = sel.shape[-1]
    pk = pool_ks.reshape(n_pool, halves, SEL_BLOCK, G * hd)
    pv = pool_vs.reshape(n_pool, halves, SEL_BLOCK, G * hd)
    n_cache_blocks = n_pages * halves

    def page_map(b, g, s, sel_ref, tbl_ref):
        blk = jnp.minimum(sel_ref[(b * G + g) * n_sel + s], n_cache_blocks - 1)
        return (tbl_ref[b * n_pages + blk // halves], blk % halves, 0, g)

    bg_map = lambda b, g, s, sel_ref, tbl_ref: (b, g, 0, 0)
    small = pl.BlockSpec((1, 1, SUBLANES, hd), bg_map)
    wspec = pl.BlockSpec((1, wb, hd), lambda b, g, s, sel_ref, tbl_ref: (b, 0, g))
    grid_spec = pltpu.PrefetchScalarGridSpec(
        num_scalar_prefetch=2,
        grid=(bsz, G, n_sel),
        in_specs=[pl.BlockSpec((1, 1, SEL_BLOCK, hd), page_map), pl.BlockSpec((1, 1, SEL_BLOCK, hd), page_map),
                  small, small, wspec, wspec, small,
                  pl.BlockSpec((1, 1, 1, LANES), bg_map),
                  pl.BlockSpec((1, LANES), lambda b, g, s, sel_ref, tbl_ref: (0, g))],
        out_specs=small,
        scratch_shapes=[pltpu.VMEM((SUBLANES, LANES), F32)] * 3,
    )
    return pl.pallas_call(
        functools.partial(_dec_attn_kernel, n_cache_blocks=n_cache_blocks, n_sel=n_sel),
        grid_spec=grid_spec,
        out_shape=jax.ShapeDtypeStruct((bsz, G, SUBLANES, hd), F32),
        compiler_params=_cparams("parallel", "parallel", "arbitrary"),
    )(sel.reshape(-1), table.reshape(-1), pk, pv, q, new,
      kw_past.reshape(bsz, wb, G * hd), vw_past.reshape(bsz, wb, G * hd), o_cmp,
      gates.reshape(bsz, G, 1, LANES), b_gate)


def _ffn(x, xn, nw, w_in, w_out, nw_next, emit_next):
    ff = w_out.shape[0]
    act = proj_swiglu(xn, w_in, ff)
    return proj_out(act, w_out, x, nw[3], nw_next, emit_next)


def _gate_weights(w, col0, n):
    return _pad_lanes(w[:, col0:col0 + n]).astype(BF16)


def _nsa_gate_weights(w, b_gate, col0):
    per = 3 * NSA_HPG
    k = w.shape[0]
    wg = w[:, col0:col0 + NSA_KV * per].reshape(k, NSA_KV, per)
    wg = jnp.pad(wg, ((0, 0), (0, 0), (0, LANES - per))).reshape(k, NSA_KV * LANES).astype(BF16)
    bg = jnp.pad(b_gate.reshape(NSA_KV, per), ((0, 0), (0, LANES - per))).reshape(1, NSA_KV * LANES)
    return wg, bg


def kernel(x_prompt, x_sample, state_a_C, state_a_n, state_a_m, state_b_conv, cache_c_k_cmp, cache_c_v_cmp, cache_c_k_slc, cache_c_v_slc, cache_c_k_win, cache_c_v_win, page_table, norm_w, ffn_w_in, ffn_w_out, a_w_in, a_b_gates, a_head_norm, a_w_out, b_w_in, b_conv_w, b_w_out, c_w_in, c_b_gate, c_cmp_pos, c_cmp_w1, c_cmp_b1, c_cmp_w2, c_cmp_b2, c_w_out):
    bp, T, D = x_prompt.shape
    bs = x_sample.shape[0]
    assert x_sample.shape[1] == 1
    depth = norm_w.shape[0]
    G, hd = NSA_KV, NSA_HD
    past_len = page_table.shape[1] * cache_c_k_cmp.shape[2]
    xp = x_prompt.reshape(bp * T, D)
    xs = x_sample.reshape(bs, D)
    hp_in = norm_cast(xp, norm_w[0, 0])
    hs_in = norm_cast(xs, norm_w[0, 0])
    pa, sa, pb, sb, pc, sc = [], [], [], [], [], []
    for i in range(depth):
        kind, j = i % 3, i // 3
        nw = norm_w[i]
        if kind == 0:
            w_in = a_w_in[j].astype(BF16)
            hdv = a_head_norm.shape[1]
            n_main = 3 * hdv
            wg = _gate_weights(a_w_in[j], n_main, 2 * MLSTM_HEADS)
            qkvo_p = proj_plain(hp_in, w_in, 0, n_main, 1024)
            qkvo_s = proj_plain(hs_in, w_in, 0, n_main, 1024)
            g_p = proj_plain(hp_in, wg, 0, LANES, LANES)
            g_s = proj_plain(hs_in, wg, 0, LANES, LANES)
            hp, c_p, n_p, m_p = mlstm_prompt(qkvo_p, g_p, a_b_gates[j], a_head_norm[j], bp, T)
            hs, c_s, n_s, m_s = mlstm_step(qkvo_s, g_s, a_b_gates[j], a_head_norm[j],
                                           state_a_C[j], state_a_n[j], state_a_m[j])
            pa.append((c_p, n_p, m_p[:, :, 0]))
            sa.append((c_s, n_s, m_s))
            w_out = a_w_out[j].astype(BF16)
        elif kind == 1:
            w_in = b_w_in[j].astype(BF16)
            bg_p, z_p = proj_conv_in(hp_in, w_in, D)
            bg_s, z_s = proj_conv_in(hs_in, w_in, D)
            hp = conv_gate(bg_p, z_p, b_conv_w[j], T)
            buf = state_b_conv[j]
            hs = conv_gate_step(bg_s, z_s, buf.transpose(1, 0, 2), b_conv_w[j]).astype(BF16)
            pb.append(z_p.reshape(bp, T, D)[:, T - (CONV_W - 1):])
            sb.append(jnp.concatenate([buf, z_s[:, None]], axis=1)[:, -(CONV_W - 1):])
            w_out = b_w_out[j].astype(BF16)
        else:
            w_in = c_w_in[j].astype(BF16)
            nq_cols = NSA_HPG * G * hd
            nkv_cols = 6 * G * hd
            wg, bgate = _nsa_gate_weights(c_w_in[j], c_b_gate[j], nq_cols + nkv_cols)
            cw = _cmp_weights(c_cmp_pos[j], c_cmp_w1[j], c_cmp_b1[j], c_cmp_w2[j], c_cmp_b2[j])
            hm_p, tok_p = proj_heads(hp_in, w_in, 0, nq_cols + nkv_cols, True)
            gt_p = proj_plain(hp_in, wg, 0, G * LANES, G * LANES)
            nheads = G * NSA_HPG
            ck, cv = cmp_prompt(hm_p, nheads, nheads + G, bp, T, cw)
            hp = nsa_attn_prompt(hm_p, ck, cv, gt_p, bgate, bp, T)
            kv_p = tok_p[:, nq_cols:].reshape(bp, T, 6, G, hd)
            keep = min(WINDOW, T)
            pc.append(tuple(kv_p[:, :, a] for a in range(4)) + tuple(kv_p[:, T - keep:, a] for a in (4, 5)))
            tok_s = proj_plain(hs_in, w_in, 0, nq_cols + nkv_cols, 1024)
            gt_s = proj_plain(hs_in, wg, 0, G * LANES, G * LANES)
            q_s = jnp.pad(tok_s[:, :nq_cols].reshape(bs, G, NSA_HPG, hd),
                          ((0, 0), (0, 0), (0, SUBLANES - NSA_HPG), (0, 0)))
            kv_s = tok_s[:, nq_cols:].reshape(bs, 6, G, hd)
            new = jnp.pad(kv_s[:, 2:6].transpose(0, 2, 1, 3), ((0, 0), (0, 0), (0, SUBLANES - 4), (0, 0)))
            ak, av = page_cmp(cache_c_k_cmp[j], cache_c_v_cmp[j], page_table, cw[0])
            o_cmp, ranked = dec_cmp(ak, av, q_s, cw, past_len)
            n_sel = min(N_SELECT, -(-(past_len + 1) // SEL_BLOCK))
            o_s = dec_attn(ranked[:, :, :n_sel], page_table, cache_c_k_slc[j], cache_c_v_slc[j], q_s, new,
                           cache_c_k_win[j], cache_c_v_win[j], o_cmp, gt_s, bgate)
            hs = o_s[:, :, :NSA_HPG].reshape(bs, nheads * hd).astype(BF16)
            keep_s = min(WINDOW, cache_c_k_win.shape[2] + 1)
            win = [jnp.concatenate([c[j], kv_s[:, a][:, None]], axis=1)[:, -keep_s:]
                   for c, a in ((cache_c_k_win, 4), (cache_c_v_win, 5))]
            sc.append(tuple(kv_s[:, a][:, None] for a in range(4)) + tuple(win))
            w_out = c_w_out[j].astype(BF16)
        last = i == depth - 1
        nw_next = norm_w[i + 1, 0] if not last else nw[0]
        f_in, f_out = ffn_w_in[i].astype(BF16), ffn_w_out[i].astype(BF16)
        xp, hp_mid = proj_out(hp, w_out, xp, nw[1], nw[2])
        xs, hs_mid = proj_out(hs, w_out, xs, nw[1], nw[2])
        xp, hp_in = _ffn(xp, hp_mid, nw, f_in, f_out, nw_next, not last)
        xs, hs_in = _ffn(xs, hs_mid, nw, f_in, f_out, nw_next, not last)

    stack = lambda items: [jnp.stack(a) for a in zip(*items)]
    p_a, s_a = stack(pa), stack(sa)
    p_c, s_c = stack(pc), stack(sc)
    return (xp.reshape(bp, T, D), xs.reshape(bs, 1, D), *p_a, *s_a, jnp.stack(pb), jnp.stack(sb), *p_c, *s_c)
```

```python
import functools
import math

import jax
import jax.numpy as jnp
from jax import lax
from jax.experimental import pallas as pl
from jax.experimental.pallas import tpu as pltpu

F32 = jnp.float32
BF16 = jnp.bfloat16

EPS = 1e-6
NEG = -1e30
BIG = 1e30

LANES = 128
SUBLANES = 8
VMEM_LIMIT = 56 * 1024 * 1024

MLSTM_HEADS = 8
MLSTM_CHUNK = 256
FORGET_BIAS_COL = MLSTM_HEADS
CONV_W = 3
NSA_HD = 128
NSA_KV = 4
NSA_HPG = 4
CMP_BLOCK = 32
CMP_STRIDE = 16
SEL_BLOCK = 64
N_SELECT = 16
WINDOW = 512
ATT_TILE = 256
PAGES_PER_STEP = 8


def _cparams(*sem):
    return pltpu.CompilerParams(dimension_semantics=sem, vmem_limit_bytes=VMEM_LIMIT)


def _dot(a, b):
    return jnp.dot(a, b, preferred_element_type=F32)


def _dot_nt(a, b):
    return lax.dot_general(a, b, (((1,), (1,)), ((), ())), preferred_element_type=F32)


def _dot_tn(a, b):
    return lax.dot_general(a, b, (((0,), (0,)), ((), ())), preferred_element_type=F32)


def _split_bf16(x):
    hi = x.astype(BF16)
    lo = (x - hi.astype(F32)).astype(BF16)
    return hi, lo


def _iota(shape, dim):
    return lax.broadcasted_iota(jnp.int32, shape, dim)


def _sigmoid(x):
    return 1.0 / (1.0 + jnp.exp(-x))


def _rms(x):
    return x * lax.rsqrt(jnp.mean(x * x, axis=-1, keepdims=True) + EPS)


def _row_tile(m, want):
    t = min(m, want)
    assert m % t == 0, (m, t)
    return t


def _norm_kernel(x_ref, w_ref, o_ref):
    o_ref[...] = (_rms(x_ref[...]) * w_ref[...]).astype(BF16)


def norm_cast(x, w):
    m, d = x.shape
    tm = _row_tile(m, 512)
    return pl.pallas_call(
        _norm_kernel,
        grid=(m // tm,),
        in_specs=[pl.BlockSpec((tm, d), lambda i: (i, 0)), pl.BlockSpec((1, d), lambda i: (0, 0))],
        out_specs=pl.BlockSpec((tm, d), lambda i: (i, 0)),
        out_shape=jax.ShapeDtypeStruct((m, d), BF16),
        compiler_params=_cparams("parallel"),
    )(x, w.reshape(1, d))


def _proj_kernel(x_ref, xs_ref, *refs, n_groups, n_out, epilogue):
    w_refs = refs[:n_groups]
    out_refs = refs[n_groups:n_groups + n_out]
    sample_out_refs = refs[n_groups + n_out:n_groups + 2 * n_out]
    wb_ref = refs[n_groups + 2 * n_out]

    @pl.when(pl.program_id(1) == 0)
    def _():
        for g in range(n_groups):
            wb_ref[g] = w_refs[g][...].astype(BF16)
        xs = xs_ref[...]
        epilogue([_dot(xs, wb_ref[g]) for g in range(n_groups)], sample_out_refs)

    x = x_ref[...]
    epilogue([_dot(x, wb_ref[g]) for g in range(n_groups)], out_refs)


def _proj(x, xs, w, *, col0, tn, n_tiles, group_stride, n_groups, epilogue, outs, tm_want=1024):
    w, layer = w
    m, k = x.shape
    s = xs.shape[0]
    tm = _row_tile(m, tm_want)
    assert col0 % tn == 0
    b0 = col0 // tn
    w_specs = [pl.BlockSpec((None, k, tn),
                            functools.partial(lambda j, i, off: (layer, 0, off + j), off=b0 + g * group_stride))
               for g in range(n_groups)]
    n_out = len(outs)
    res = pl.pallas_call(
        functools.partial(_proj_kernel, n_groups=n_groups, n_out=n_out, epilogue=epilogue),
        grid=(n_tiles, m // tm),
        in_specs=[pl.BlockSpec((tm, k), lambda j, i: (i, 0)), pl.BlockSpec((s, k), lambda j, i: (0, 0))] + w_specs,
        out_specs=[pl.BlockSpec((tm, tn), lambda j, i: (i, j))] * n_out
        + [pl.BlockSpec((s, tn), lambda j, i: (0, j))] * n_out,
        out_shape=[jax.ShapeDtypeStruct((m, n), dt) for n, dt in outs]
        + [jax.ShapeDtypeStruct((s, n), dt) for n, dt in outs],
        scratch_shapes=[pltpu.VMEM((n_groups, k, tn), BF16)],
        compiler_params=_cparams("arbitrary", "arbitrary"),
    )(x, xs, *([w] * n_groups))
    return res[:n_out], res[n_out:]


def _ep_plain(accs, outs):
    for acc, o in zip(accs, outs):
        o[...] = acc


def proj_plain(x, xs, w, col0, n_cols, tn):
    p, s = _proj(x, xs, w, col0=col0, tn=tn, n_tiles=n_cols // tn, group_stride=0, n_groups=1,
                 epilogue=_ep_plain, outs=[(n_cols, F32)])
    return p[0], s[0]


def proj_groups(x, xs, w, col0, n_groups, group_cols, tn):
    return _proj(x, xs, w, col0=col0, tn=tn, n_tiles=group_cols // tn, group_stride=group_cols // tn,
                 n_groups=n_groups, epilogue=_ep_plain, outs=[(group_cols, F32)] * n_groups, tm_want=512)


def _ep_swiglu(accs, outs):
    g, u = accs
    outs[0][...] = (g * _sigmoid(g) * u).astype(BF16)


def proj_swiglu(x, xs, w, ff):
    tn = 512
    p, s = _proj(x, xs, w, col0=0, tn=tn, n_tiles=ff // tn, group_stride=ff // tn, n_groups=2,
                 epilogue=_ep_swiglu, outs=[(ff, BF16)])
    return p[0], s[0]


def _ep_conv_in(accs, outs):
    bg, cg, u = accs
    outs[0][...] = bg
    outs[1][...] = cg * u


def proj_conv_in(x, xs, w, d):
    tn = 512
    return _proj(x, xs, w, col0=0, tn=tn, n_tiles=d // tn, group_stride=d // tn, n_groups=3,
                 epilogue=_ep_conv_in, outs=[(d, F32), (d, F32)])


def _out_kernel(h_ref, w_ref, r_ref, nwa_ref, nwb_ref, *refs, nk, emit_next):
    if emit_next:
        x_ref, xn_ref = refs[0], refs[1]
        acc_ref = refs[2] if nk > 1 else None
    else:
        x_ref, xn_ref = refs[0], None
        acc_ref = refs[1] if nk > 1 else None

    def finalize(y):
        x_new = r_ref[...] + _rms(y) * nwa_ref[...]
        x_ref[...] = x_new
        if emit_next:
            xn_ref[...] = (_rms(x_new) * nwb_ref[...]).astype(BF16)

    part = _dot(h_ref[...], w_ref[...])
    if nk == 1:
        finalize(part)
        return
    kk = pl.program_id(1)

    @pl.when(kk == 0)
    def _():
        acc_ref[...] = part

    @pl.when(kk > 0)
    def _():
        acc_ref[...] += part

    @pl.when(kk == nk - 1)
    def _():
        finalize(acc_ref[...])


def proj_out(h, w, resid, nw_post, nw_next, emit_next=True):
    m, k = h.shape
    d = w.shape[1]
    tm = _row_tile(m, 512)
    tk = k if k <= 2048 else 1408
    assert k % tk == 0
    nk = k // tk
    row = lambda i, kk: (i, 0)
    out_shape = [jax.ShapeDtypeStruct((m, d), F32)]
    out_specs = [pl.BlockSpec((tm, d), row)]
    if emit_next:
        out_shape.append(jax.ShapeDtypeStruct((m, d), BF16))
        out_specs.append(pl.BlockSpec((tm, d), row))
    res = pl.pallas_call(
        functools.partial(_out_kernel, nk=nk, emit_next=emit_next),
        grid=(m // tm, nk),
        in_specs=[pl.BlockSpec((tm, tk), lambda i, kk: (i, kk)),
                  pl.BlockSpec((tk, d), lambda i, kk: (kk, 0)),
                  pl.BlockSpec((tm, d), row),
                  pl.BlockSpec((1, d), lambda i, kk: (0, 0)),
                  pl.BlockSpec((1, d), lambda i, kk: (0, 0))],
        out_specs=out_specs,
        out_shape=out_shape,
        scratch_shapes=[pltpu.VMEM((tm, d), F32)] if nk > 1 else [],
        compiler_params=_cparams("parallel", "arbitrary"),
    )(h, w, resid, nw_post.reshape(1, d), nw_next.reshape(1, d))
    return (res[0], res[1]) if emit_next else (res[0], None)


def _log_sigmoid(x):
    return jnp.minimum(x, 0.0) - jnp.log1p(jnp.exp(-jnp.abs(x)))


def _mlstm_kernel(q_ref, k_ref, v_ref, o_ref, g_ref, bg_ref, hn_ref, hg_ref, c_ref, n_ref, m_ref, *, L, H, DK, DV):
    @pl.when(pl.program_id(1) == 0)
    def _():
        c_ref[...] = jnp.zeros_like(c_ref)
        n_ref[...] = jnp.zeros_like(n_ref)
        m_ref[...] = jnp.zeros_like(m_ref)

    gpre = g_ref[...] + bg_ref[...]
    lf = _log_sigmoid(gpre)
    rows = _iota((L, L), 0)
    cols = _iota((L, L), 1)
    causal = rows >= cols
    tril = jnp.where(causal, 1.0, 0.0).astype(BF16)
    lf_hi, lf_lo = _split_bf16(lf)
    bcum = _dot(tril, lf_hi) + _dot(tril, lf_lo)
    g_t = gpre.T
    b_t = bcum.T
    for h in range(H):
        f = FORGET_BIAS_COL + h
        a_col = bcum[:, f:f + 1]
        i_col = gpre[:, h:h + 1]
        b_row = b_t[f:f + 1, :]
        i_row = g_t[h:h + 1, :]
        m_prev = m_ref[0, h:h + 1, 0:1]
        log_d = jnp.where(causal, a_col - b_row + i_row, NEG)
        log_inter = a_col + m_prev
        m_t = jnp.maximum(log_inter, jnp.max(log_d, axis=-1, keepdims=True))
        d = jnp.exp(log_d - m_t)
        inter = jnp.exp(log_inter - m_t)
        qh = q_ref[:, h * DK:(h + 1) * DK]
        kh = k_ref[:, h * DK:(h + 1) * DK] * (DK ** -0.5)
        vb = v_ref[:, h * DV:(h + 1) * DV].astype(BF16)
        qb = qh.astype(BF16)
        s = _dot_nt(qb, kh.astype(BF16)) * d
        c_old = c_ref[0, h]
        n_old = n_ref[0, h:h + 1, :]
        num = _dot(s.astype(BF16), vb) + inter * _dot(qb, c_old.astype(BF16))
        den = jnp.sum(s, axis=-1, keepdims=True) + inter * jnp.sum(qh * n_old, axis=-1, keepdims=True)
        hh = _rms(num / jnp.maximum(jnp.abs(den), jnp.exp(-m_t)))
        gate = _sigmoid(o_ref[:, h * DV:(h + 1) * DV])
        hg_ref[:, h * DV:(h + 1) * DV] = (hh * hn_ref[:, h * DV:(h + 1) * DV] * gate).astype(BF16)
        m_new = m_t[L - 1:L, :]
        w_col = jnp.exp(a_col[L - 1:L, :] - a_col + i_col - m_new)
        decay = jnp.exp(log_inter[L - 1:L, :] - m_new)
        kw = kh * w_col
        c_ref[0, h] = decay * c_old + _dot_tn(kw.astype(BF16), vb)
        n_ref[0, h:h + 1, :] = decay * n_old + jnp.sum(kw, axis=0, keepdims=True)
        m_ref[0, h:h + 1, :] = jnp.broadcast_to(m_new, (1, LANES))


def mlstm_prompt(qkvo, gates, b_gates, head_norm, bsz, T):
    H = MLSTM_HEADS
    m = qkvo.shape[0]
    hdv = qkvo.shape[1] // 3
    hdk = hdv // 2
    DK, DV = hdk // H, hdv // H
    L = math.gcd(T, MLSTM_CHUNK)
    nc = T // L
    row = lambda b, c: (b * nc + c, 0)
    state = lambda b, c: (b, 0, 0)
    return pl.pallas_call(
        functools.partial(_mlstm_kernel, L=L, H=H, DK=DK, DV=DV),
        grid=(bsz, nc),
        in_specs=[pl.BlockSpec((L, hdk), row),
                  pl.BlockSpec((L, hdk), lambda b, c: (b * nc + c, 1)),
                  pl.BlockSpec((L, hdv), lambda b, c: (b * nc + c, 1)),
                  pl.BlockSpec((L, hdv), lambda b, c: (b * nc + c, 2)),
                  pl.BlockSpec((L, LANES), row),
                  pl.BlockSpec((1, LANES), lambda b, c: (0, 0)),
                  pl.BlockSpec((1, hdv), lambda b, c: (0, 0))],
        out_specs=[pl.BlockSpec((L, hdv), row),
                   pl.BlockSpec((1, H, DK, DV), lambda b, c: (b, 0, 0, 0)),
                   pl.BlockSpec((1, H, DK), state),
                   pl.BlockSpec((1, H, LANES), state)],
        out_shape=[jax.ShapeDtypeStruct((m, hdv), BF16),
                   jax.ShapeDtypeStruct((bsz, H, DK, DV), F32),
                   jax.ShapeDtypeStruct((bsz, H, DK), F32),
                   jax.ShapeDtypeStruct((bsz, H, LANES), F32)],
        compiler_params=_cparams("parallel", "arbitrary"),
    )(qkvo, qkvo, qkvo, qkvo, gates, _pad_lanes(b_gates.reshape(1, -1)), head_norm.reshape(1, hdv))


def _pad_lanes(x, width=LANES):
    return jnp.pad(x, ((0, 0), (0, width - x.shape[-1])))


def _to_col(row, n):
    eye = _iota((n, n), 0) == _iota((n, n), 1)
    return jnp.sum(jnp.where(eye, jnp.broadcast_to(row, (n, n)), 0.0), axis=-1, keepdims=True)


def _mlstm_step_kernel(x_ref, g_ref, bg_ref, hn_ref, c0_ref, n0_ref, m0_ref, hg_ref, c_ref, n_ref, m_ref,
                       *, H, DK, DV):
    x = x_ref[0]
    gpre = g_ref[0] + bg_ref[...]
    lf = _log_sigmoid(gpre)
    hdk = H * DK
    hdv = H * DV
    for h in range(H):
        f = FORGET_BIAS_COL + h
        ig = gpre[:, h:h + 1]
        m_prev = m0_ref[0, h:h + 1, :]
        log_inter = lf[:, f:f + 1] + m_prev
        m_t = jnp.maximum(log_inter, ig)
        d = jnp.exp(ig - m_t)
        inter = jnp.exp(log_inter - m_t)
        q = x[:, h * DK:(h + 1) * DK]
        k = x[:, hdk + h * DK:hdk + (h + 1) * DK] * (DK ** -0.5)
        v = x[:, 2 * hdk + h * DV:2 * hdk + (h + 1) * DV]
        og = x[:, 2 * hdk + hdv + h * DV:2 * hdk + hdv + (h + 1) * DV]
        c_old = c0_ref[0, h]
        n_old = n0_ref[0, h:h + 1, :]
        s = jnp.sum(q * k, axis=-1, keepdims=True) * d
        q_col = _to_col(q, DK)
        k_col = _to_col(k, DK)
        num = s * v + inter * jnp.sum(q_col * c_old, axis=0, keepdims=True)
        den = s + inter * jnp.sum(q * n_old, axis=-1, keepdims=True)
        hh = _rms(num / jnp.maximum(jnp.abs(den), jnp.exp(-m_t)))
        hg_ref[0, :, h * DV:(h + 1) * DV] = (hh * hn_ref[:, h * DV:(h + 1) * DV] * _sigmoid(og)).astype(BF16)
        c_ref[0, h] = inter * c_old + d * (k_col * v)
        n_ref[0, h:h + 1, :] = inter * n_old + d * k
        m_ref[0, h:h + 1, :] = m_t


def mlstm_step(qkvo, gates, b_gates, head_norm, c0, n0, m0):
    bsz, H, DK, DV = c0.shape
    wid = qkvo.shape[1]
    hdv = H * DV
    one = lambda b: (b, 0, 0)
    hg, c1, n1, m1 = pl.pallas_call(
        functools.partial(_mlstm_step_kernel, H=H, DK=DK, DV=DV),
        grid=(bsz,),
        in_specs=[pl.BlockSpec((1, 1, wid), one),
                  pl.BlockSpec((1, 1, LANES), one),
                  pl.BlockSpec((1, LANES), lambda b: (0, 0)),
                  pl.BlockSpec((1, hdv), lambda b: (0, 0)),
                  pl.BlockSpec((1, H, DK, DV), lambda b: (b, 0, 0, 0)),
                  pl.BlockSpec((1, H, DK), one),
                  pl.BlockSpec((1, H, 1), one)],
        out_specs=[pl.BlockSpec((1, 1, hdv), one),
                   pl.BlockSpec((1, H, DK, DV), lambda b: (b, 0, 0, 0)),
                   pl.BlockSpec((1, H, DK), one),
                   pl.BlockSpec((1, H, 1), one)],
        out_shape=[jax.ShapeDtypeStruct((bsz, 1, hdv), BF16),
                   jax.ShapeDtypeStruct((bsz, H, DK, DV), F32),
                   jax.ShapeDtypeStruct((bsz, H, DK), F32),
                   jax.ShapeDtypeStruct((bsz, H, 1), F32)],
        compiler_params=_cparams("parallel"),
    )(qkvo.reshape(bsz, 1, wid), gates.reshape(bsz, 1, LANES), _pad_lanes(b_gates.reshape(1, -1)),
      head_norm.reshape(1, hdv), c0, n0, m0.reshape(bsz, H, 1))
    return hg.reshape(bsz, hdv), c1, n1, m1.reshape(bsz, H)


def _conv_kernel(bg_ref, z_ref, zp_ref, cw_ref, o_ref, *, tm, T):
    i = pl.program_id(0)
    z = z_ref[...]
    zc = jnp.concatenate([zp_ref[...], z], axis=0)
    tpos = (i * tm + _iota((tm, 1), 0)) % T
    y = cw_ref[CONV_W - 1:CONV_W, :] * z
    for back in range(1, CONV_W):
        zb = zc[SUBLANES - back:SUBLANES - back + tm, :]
        y = y + cw_ref[CONV_W - 1 - back:CONV_W - back, :] * jnp.where(tpos >= back, zb, 0.0)
    o_ref[...] = (bg_ref[...] * y).astype(BF16)


def conv_gate(bg, z, conv_w, T):
    m, d = z.shape
    tm = _row_tile(T, 256)
    per = tm // SUBLANES
    return pl.pallas_call(
        functools.partial(_conv_kernel, tm=tm, T=T),
        grid=(m // tm,),
        in_specs=[pl.BlockSpec((tm, d), lambda i: (i, 0)),
                  pl.BlockSpec((tm, d), lambda i: (i, 0)),
                  pl.BlockSpec((SUBLANES, d), lambda i: (jnp.maximum(i * per - 1, 0), 0)),
                  pl.BlockSpec((CONV_W, d), lambda i: (0, 0))],
        out_specs=pl.BlockSpec((tm, d), lambda i: (i, 0)),
        out_shape=jax.ShapeDtypeStruct((m, d), BF16),
        compiler_params=_cparams("parallel"),
    )(bg, z, z, conv_w)


def _conv_step_kernel(bg_ref, z_ref, buf_ref, cw_ref, o_ref):
    y = cw_ref[CONV_W - 1:CONV_W, :] * z_ref[...]
    for j in range(CONV_W - 1):
        y = y + cw_ref[j:j + 1, :] * buf_ref[j]
    o_ref[...] = bg_ref[...] * y


def conv_gate_step(bg, z, buf, conv_w):
    bsz, d = z.shape
    full = lambda: (0, 0)
    return pl.pallas_call(
        _conv_step_kernel,
        in_specs=[pl.BlockSpec((bsz, d), full), pl.BlockSpec((bsz, d), full),
                  pl.BlockSpec((CONV_W - 1, bsz, d), lambda: (0, 0, 0)), pl.BlockSpec((CONV_W, d), full)],
        out_specs=pl.BlockSpec((bsz, d), full),
        out_shape=jax.ShapeDtypeStruct((bsz, d), F32),
    )(bg, z, buf, conv_w)


def _gelu(x):
    return 0.5 * x * (1.0 + jnp.tanh(math.sqrt(2.0 / math.pi) * (x + 0.044715 * x * x * x)))


def _cmp_finish(a, posw, b1, w2, b2):
    hid = a.shape[1] // 2
    a1 = a[:, hid:]
    a1_next = jnp.concatenate([a1[1:], a1[:1]], axis=0)
    pre = a[:, :hid] + a1_next + b1 + posw[0:1, :hid] + posw[1:2, hid:]
    return _dot(_gelu(pre).astype(BF16), w2) + b2


def _cmp_prompt_kernel(k_ref, v_ref, w1_ref, pos_ref, b1_ref, w2_ref, b2_ref, ck_ref, cv_ref, *, nch):
    for idx, (x_ref, o_ref) in enumerate(((k_ref, ck_ref), (v_ref, cv_ref))):
        w1 = w1_ref[idx]
        lhs = jnp.concatenate([x_ref[pl.ds(p, nch, stride=CMP_STRIDE), :] for p in range(CMP_STRIDE)], axis=1)
        a = _dot(lhs.astype(BF16), w1)
        posw = _dot(pos_ref[idx], w1)
        o_ref[0, 0] = _cmp_finish(a, posw, b1_ref[idx], w2_ref[idx], b2_ref[idx])


def _cmp_weights(cmp_pos, cmp_w1, cmp_b1, cmp_w2, cmp_b2):
    two, ratio, stride, hd, hid = cmp_w1.shape
    w1 = cmp_w1.transpose(0, 2, 3, 1, 4).reshape(two, stride * hd, ratio * hid).astype(BF16)
    pos = cmp_pos.reshape(two, ratio, stride * hd)
    pos = jnp.pad(pos, ((0, 0), (0, SUBLANES - ratio), (0, 0))).astype(BF16)
    return w1, pos, cmp_b1.reshape(two, 1, hid), cmp_w2.astype(BF16), cmp_b2.reshape(two, 1, hd)


def cmp_prompt(kc, vc, bsz, T, cw):
    w1, pos, b1, w2, b2 = cw
    nch = T // CMP_STRIDE
    G = NSA_KV
    out = jax.ShapeDtypeStruct((bsz, G, nch, NSA_HD), F32)
    ospec = pl.BlockSpec((1, 1, nch, NSA_HD), lambda b, g: (b, g, 0, 0))
    whole = lambda a: pl.BlockSpec(a.shape, lambda b, g: (0,) * a.ndim)
    seq = pl.BlockSpec((T, NSA_HD), lambda b, g: (b, g))
    return pl.pallas_call(
        functools.partial(_cmp_prompt_kernel, nch=nch),
        grid=(bsz, G),
        in_specs=[seq, seq, whole(w1), whole(pos), whole(b1), whole(w2), whole(b2)],
        out_specs=[ospec, ospec],
        out_shape=[out, out],
        compiler_params=_cparams("parallel", "parallel"),
    )(kc, vc, w1, pos, b1, w2, b2)


def _overlap_t(nbs_pad, nbc_pad):
    j = _iota((nbs_pad, nbc_pad), 0)
    n = _iota((nbs_pad, nbc_pad), 1)
    lo = jnp.maximum(n * CMP_STRIDE, j * SEL_BLOCK)
    hi = jnp.minimum(n * CMP_STRIDE + CMP_BLOCK, j * SEL_BLOCK + SEL_BLOCK)
    return (jnp.maximum(hi - lo, 0) // CMP_STRIDE).astype(F32)


def _rank_select(score, n_sel, axis):
    n = score.shape[axis]
    idx = _iota(score.shape, axis)
    cnt = jnp.zeros(score.shape, F32)
    for jp in range(n):
        other = lax.slice_in_dim(score, jp, jp + 1, axis=axis)
        tie = jnp.where(idx > jp, 1.0, 0.0)
        cnt = cnt + jnp.where(other > score, 1.0, jnp.where(other == score, tie, 0.0))
    return jnp.where(cnt < n_sel, 1.0, 0.0)


def _attn_first(state, s, v_b):
    m_ref, l_ref, acc_ref = state
    m = jnp.max(s, axis=-1, keepdims=True)
    p = jnp.exp(s - m)
    m_ref[...] = m
    l_ref[...] = jnp.sum(p, axis=-1, keepdims=True)
    acc_ref[...] = _dot(p.astype(BF16), v_b)


def _attn_more(state, s, v_b):
    m_ref, l_ref, acc_ref = state
    m_old = m_ref[...]
    m_new = jnp.maximum(m_old, jnp.max(s, axis=-1, keepdims=True))
    alpha = jnp.exp(m_old - m_new)
    p = jnp.exp(s - m_new)
    m_ref[...] = m_new
    l_ref[...] = alpha * l_ref[...] + jnp.sum(p, axis=-1, keepdims=True)
    acc_ref[...] = alpha * acc_ref[...] + _dot(p.astype(BF16), v_b)


def _nsa_attn_kernel(q_ref, ck_ref, cv_ref, ks_ref, vs_ref, kw_ref, vw_ref, g_ref, bg_ref, o_ref,
                     ms_ref, ls_ref, as_ref, mw_ref, lw_ref, aw_ref, *, tq, hpg, nbc, nbs, nbs_pad):
    qi = pl.program_id(2)
    t0 = qi * tq
    rows = hpg * tq
    nbc_pad = ck_ref.shape[2]
    qb = (jnp.concatenate([q_ref[:, h * NSA_HD:(h + 1) * NSA_HD] for h in range(hpg)], axis=0)
          * (NSA_HD ** -0.5)).astype(BF16)

    n = _iota((rows, nbc_pad), 1)
    t_row = t0 + jnp.concatenate([_iota((tq, nbc_pad), 0)] * hpg, axis=0)
    cmask = jnp.where(n < nbc, jnp.where(n * CMP_STRIDE + (CMP_BLOCK - 1) <= t_row, 1.0, 0.0), 0.0)
    sc = jnp.where(cmask > 0.5, _dot_nt(qb, ck_ref[0, 0].astype(BF16)), NEG)
    pe = jnp.exp(sc - jnp.max(sc, axis=-1, keepdims=True))
    p_cmp = pe / jnp.sum(pe, axis=-1, keepdims=True) * cmask
    o_cmp = _dot(p_cmp.astype(BF16), cv_ref[0, 0].astype(BF16))

    p_sum = p_cmp[0:tq]
    for h in range(1, hpg):
        p_sum = p_sum + p_cmp[h * tq:(h + 1) * tq]
    ov_t = _overlap_t(nbs_pad, nbc_pad).astype(BF16)
    p_hi, p_lo = _split_bf16(p_sum)
    imp_t = _dot_nt(ov_t, p_hi) + _dot_nt(ov_t, p_lo)
    j = _iota((nbs_pad, tq), 0)
    t = t0 + _iota((nbs_pad, tq), 1)
    t_blk = t // SEL_BLOCK
    forced = jnp.where(j == 0, 1.0, jnp.where(j == t_blk, 1.0, jnp.where(j == t_blk - 1, 1.0, 0.0)))
    visible = jnp.where(j < nbs, jnp.where(j * SEL_BLOCK <= t, 1.0, 0.0), 0.0)
    score = jnp.where(visible > 0.5, jnp.where(forced > 0.5, BIG, imp_t), NEG)
    sel_t = _rank_select(score, min(N_SELECT, nbs), axis=0).astype(BF16)

    tk = tq
    r_ofs = _iota((tq, tk), 0)
    k_ofs = _iota((tq, tk), 1)
    causal = k_ofs <= r_ofs
    per_head = lambda bias: jnp.concatenate([bias] * hpg, axis=0)

    def tile(ref, kt):
        return ref[pl.ds(pl.multiple_of(kt * tk, tk), tk), :].astype(BF16)

    def slc_bias(kt):
        blk = (kt * tk + _iota((nbs_pad, tk), 1)) // SEL_BLOCK
        expand = jnp.where(blk == _iota((nbs_pad, tk), 0), 1.0, 0.0).astype(BF16)
        return (_dot_tn(sel_t, expand) - 1.0) * BIG

    slc = (ms_ref, ls_ref, as_ref)
    _attn_first(slc, _dot_nt(qb, tile(ks_ref, qi)) + per_head(jnp.where(causal, slc_bias(qi), NEG)),
                tile(vs_ref, qi))

    def slc_body(kt, _):
        _attn_more(slc, _dot_nt(qb, tile(ks_ref, kt)) + per_head(slc_bias(kt)), tile(vs_ref, kt))
        return 0

    lax.fori_loop(0, qi, slc_body, 0)

    n_back = WINDOW // tk
    win = (mw_ref, lw_ref, aw_ref)
    _attn_first(win, _dot_nt(qb, tile(kw_ref, qi)) + per_head(jnp.where(causal, 0.0, NEG)), tile(vw_ref, qi))
    for back in range(1, n_back + 1):
        @pl.when(qi >= back)
        def _():
            s = _dot_nt(qb, tile(kw_ref, qi - back))
            if back == n_back:
                s = s + per_head(jnp.where(k_ofs >= r_ofs, 0.0, NEG))
            _attn_more(win, s, tile(vw_ref, qi - back))

    o_slc = as_ref[...] / ls_ref[...]
    o_win = aw_ref[...] / lw_ref[...]
    gate = _sigmoid(g_ref[...] + bg_ref[...])
    for h in range(hpg):
        sl = slice(h * tq, (h + 1) * tq)
        o = (gate[:, 3 * h:3 * h + 1] * o_cmp[sl] + gate[:, 3 * h + 1:3 * h + 2] * o_slc[sl]
             + gate[:, 3 * h + 2:3 * h + 3] * o_win[sl])
        o_ref[:, h * NSA_HD:(h + 1) * NSA_HD] = o.astype(BF16)


def nsa_attn_prompt(q, ck, cv, ks, vs, kw, vw, gates, b_gate, bsz, T):
    G, HPG = NSA_KV, NSA_HPG
    tq = math.gcd(T, ATT_TILE)
    assert WINDOW % tq == 0
    nq = T // tq
    nch = T // CMP_STRIDE
    nbc = nch - CMP_BLOCK // CMP_STRIDE + 1
    nbs = -(-T // SEL_BLOCK)
    nbs_pad = -(-nbs // SUBLANES) * SUBLANES
    rows = HPG * tq
    seq = pl.BlockSpec((T, NSA_HD), lambda b, g, i: (b, g))
    cspec = pl.BlockSpec((1, 1, nch, NSA_HD), lambda b, g, i: (b, g, 0, 0))
    qspec = pl.BlockSpec((tq, HPG * NSA_HD), lambda b, g, i: (b * nq + i, g))
    return pl.pallas_call(
        functools.partial(_nsa_attn_kernel, tq=tq, hpg=HPG, nbc=nbc, nbs=nbs, nbs_pad=nbs_pad),
        grid=(bsz, G, nq),
        in_specs=[qspec, cspec, cspec, seq, seq, seq, seq,
                  pl.BlockSpec((tq, LANES), lambda b, g, i: (b * nq + i, g)),
                  pl.BlockSpec((1, LANES), lambda b, g, i: (0, g))],
        out_specs=qspec,
        out_shape=jax.ShapeDtypeStruct((bsz * T, G * HPG * NSA_HD), BF16),
        scratch_shapes=[pltpu.VMEM((rows, 1), F32), pltpu.VMEM((rows, 1), F32), pltpu.VMEM((rows, NSA_HD), F32)] * 2,
        compiler_params=_cparams("parallel", "parallel", "arbitrary"),
    )(q, ck, cv, ks, vs, kw, vw, gates, b_gate)


def _page_cmp_kernel(tbl_ref, *refs, P):
    k_pages, v_pages = refs[:P], refs[P:2 * P]
    w1_ref, ak_ref, av_ref = refs[2 * P:]
    rows = _iota((LANES, LANES), 0)
    cols = _iota((LANES, LANES), 1)
    cpp = LANES // CMP_STRIDE
    perm = jnp.where(cols == (rows % cpp) * CMP_STRIDE + rows // cpp, 1.0, 0.0).astype(BF16)
    for idx, (pages, a_ref) in enumerate(((k_pages, ak_ref), (v_pages, av_ref))):
        xp = [_dot(perm, pg[0].astype(BF16)) for pg in pages]
        blocks = []
        for g in range(NSA_KV):
            for x in xp:
                blocks.append(jnp.concatenate(
                    [x[p * cpp:(p + 1) * cpp, g * NSA_HD:(g + 1) * NSA_HD] for p in range(CMP_STRIDE)], axis=1))
        a = _dot(jnp.concatenate(blocks, axis=0).astype(BF16), w1_ref[idx])
        per_g = P * cpp
        for g in range(NSA_KV):
            a_ref[0, g] = a[g * per_g:(g + 1) * per_g]


def page_cmp(pool_k, pool_v, table, w1):
    n_pool, page, G, hd = pool_k.shape
    assert page == LANES and G == NSA_KV and hd == NSA_HD
    bsz, n_pages = table.shape
    P = math.gcd(n_pages, PAGES_PER_STEP)
    cpp = page // CMP_STRIDE
    hid2 = w1.shape[2]
    pk = pool_k.reshape(n_pool, page, G * hd)
    pv = pool_v.reshape(n_pool, page, G * hd)
    pspec = lambda i: pl.BlockSpec((1, page, G * hd),
                                   functools.partial(lambda b, s, tbl, i: (tbl[b * n_pages + s * P + i], 0, 0), i=i))
    out = jax.ShapeDtypeStruct((bsz, G, n_pages * cpp, hid2), F32)
    ospec = pl.BlockSpec((1, G, P * cpp, hid2), lambda b, s, tbl: (b, 0, s, 0))
    grid_spec = pltpu.PrefetchScalarGridSpec(
        num_scalar_prefetch=1,
        grid=(bsz, n_pages // P),
        in_specs=[pspec(i) for i in range(P)] * 2 + [pl.BlockSpec(w1.shape, lambda b, s, tbl: (0, 0, 0))],
        out_specs=[ospec, ospec],
    )
    return pl.pallas_call(
        functools.partial(_page_cmp_kernel, P=P),
        grid_spec=grid_spec,
        out_shape=[out, out],
        compiler_params=_cparams("parallel", "parallel"),
    )(table.reshape(-1), *([pk] * P), *([pv] * P), w1)


def _dec_cmp_kernel(ak_ref, av_ref, q_ref, pos_ref, w1_ref, b1_ref, w2_ref, b2_ref, o_ref, sel_ref,
                    *, nbc, nbs, nbs_pad, p0):
    nch = ak_ref.shape[2]
    for g in range(NSA_KV):
        cks = []
        for idx, a_ref in enumerate((ak_ref, av_ref)):
            posw = _dot(pos_ref[idx], w1_ref[idx])
            cks.append(_cmp_finish(a_ref[0, g], posw, b1_ref[idx], w2_ref[idx], b2_ref[idx]))
        ck, cv = cks
        qb = (q_ref[0, g] * (NSA_HD ** -0.5)).astype(BF16)
        n = _iota((SUBLANES, nch), 1)
        cmask = jnp.where(n < nbc, jnp.where(n * CMP_STRIDE + (CMP_BLOCK - 1) <= p0, 1.0, 0.0), 0.0)
        sc = jnp.where(cmask > 0.5, _dot_nt(qb, ck.astype(BF16)), NEG)
        pe = jnp.exp(sc - jnp.max(sc, axis=-1, keepdims=True))
        p_cmp = pe / jnp.sum(pe, axis=-1, keepdims=True) * cmask
        o_ref[0, g] = _dot(p_cmp.astype(BF16), cv.astype(BF16))

        head = jnp.where(_iota((SUBLANES, nch), 0) < NSA_HPG, p_cmp, 0.0)
        p_sum = jnp.broadcast_to(jnp.sum(head, axis=0, keepdims=True), (SUBLANES, nch))
        ov_t = _overlap_t(nbs_pad, nch).astype(BF16)
        p_hi, p_lo = _split_bf16(p_sum)
        imp = _dot_nt(p_hi, ov_t) + _dot_nt(p_lo, ov_t)
        j = _iota((SUBLANES, nbs_pad), 1)
        t_blk = p0 // SEL_BLOCK
        forced = jnp.where(j == 0, 1.0, jnp.where(j == t_blk, 1.0, jnp.where(j == t_blk - 1, 1.0, 0.0)))
        visible = jnp.where(j < nbs, jnp.where(j * SEL_BLOCK <= p0, 1.0, 0.0), 0.0)
        score = jnp.where(visible > 0.5, jnp.where(forced > 0.5, BIG, imp), NEG)
        s_row = jnp.broadcast_to(score[0:1], (nbs_pad, nbs_pad))
        s_col = jnp.concatenate([jnp.broadcast_to(score[0:1], (LANES, nbs_pad)).T] * (nbs_pad // LANES), axis=1)
        jr = _iota((nbs_pad, nbs_pad), 0)
        jc = _iota((nbs_pad, nbs_pad), 1)
        tie = jnp.where(jc < jr, 1.0, 0.0)
        beats = jnp.where(s_row > s_col, 1.0, jnp.where(s_row == s_col, tie, 0.0))
        rank = jnp.sum(beats, axis=-1, keepdims=True)
        slot = _iota((nbs_pad, LANES), 1).astype(F32)
        blk = _iota((nbs_pad, LANES), 0).astype(F32)
        picked = jnp.sum(jnp.where(rank == slot, blk, 0.0), axis=0, keepdims=True)
        sel_ref[0, g:g + 1, :] = picked.astype(jnp.int32)


def dec_cmp(ak, av, q, cw, p0):
    w1, pos, b1, w2, b2 = cw
    bsz, G, nch, hid2 = ak.shape
    nbc = nch - CMP_BLOCK // CMP_STRIDE + 1
    nbs = -(-(p0 + 1) // SEL_BLOCK)
    nbs_pad = -(-nbs // LANES) * LANES
    whole = lambda a: pl.BlockSpec(a.shape, lambda b: (0,) * a.ndim)
    aspec = pl.BlockSpec((1, G, nch, hid2), lambda b: (b, 0, 0, 0))
    return pl.pallas_call(
        functools.partial(_dec_cmp_kernel, nbc=nbc, nbs=nbs, nbs_pad=nbs_pad, p0=p0),
        grid=(bsz,),
        in_specs=[aspec, aspec, pl.BlockSpec((1, G, SUBLANES, NSA_HD), lambda b: (b, 0, 0, 0)),
                  whole(pos), whole(w1), whole(b1), whole(w2), whole(b2)],
        out_specs=[pl.BlockSpec((1, G, SUBLANES, NSA_HD), lambda b: (b, 0, 0, 0)),
                   pl.BlockSpec((1, G, LANES), lambda b: (b, 0, 0))],
        out_shape=[jax.ShapeDtypeStruct((bsz, G, SUBLANES, NSA_HD), F32),
                   jax.ShapeDtypeStruct((bsz, G, LANES), jnp.int32)],
        compiler_params=_cparams("parallel"),
    )(ak, av, q, pos, w1, b1, w2, b2)


def _softmax_with_new_key(qb, k_b, v_b, bias, s_new, v_new):
    s = _dot_nt(qb, k_b)
    if bias is not None:
        s = s + bias
    mx = jnp.maximum(jnp.max(s, axis=-1, keepdims=True), s_new)
    p = jnp.exp(s - mx)
    p_new = jnp.exp(s_new - mx)
    return (_dot(p.astype(BF16), v_b) + p_new * v_new) / (jnp.sum(p, axis=-1, keepdims=True) + p_new)


def _dec_attn_kernel(sel_ref, tbl_ref, *refs, n_cache_blocks, n_sel):
    ks_refs, vs_refs = refs[:n_sel], refs[n_sel:2 * n_sel]
    q_ref, new_ref, kw_ref, vw_ref, oc_ref, g_ref, bg_ref, o_ref = refs[2 * n_sel:]
    b, g = pl.program_id(0), pl.program_id(1)
    q = q_ref[0, 0] * (NSA_HD ** -0.5)
    qb = q.astype(BF16)
    new = new_ref[0, 0]

    k_all = jnp.concatenate([r[0, 0] for r in ks_refs], axis=0).astype(BF16)
    v_all = jnp.concatenate([r[0, 0] for r in vs_refs], axis=0).astype(BF16)
    slot = _iota((SUBLANES, n_sel * SEL_BLOCK), 1) // SEL_BLOCK
    bias = jnp.zeros((SUBLANES, n_sel * SEL_BLOCK), F32)
    for s in range(n_sel):
        blk = sel_ref[(b * NSA_KV + g) * n_sel + s]
        bias = jnp.where(slot == s, jnp.where(blk < n_cache_blocks, 0.0, NEG), bias)
    o_slc = _softmax_with_new_key(qb, k_all, v_all, bias, jnp.sum(q * new[0:1], axis=-1, keepdims=True), new[1:2])

    o_win = _softmax_with_new_key(qb, kw_ref[0].astype(BF16), vw_ref[0].astype(BF16), None,
                                  jnp.sum(q * new[2:3], axis=-1, keepdims=True), new[3:4])

    gate = jnp.broadcast_to(_sigmoid(g_ref[0, 0] + bg_ref[...]), (SUBLANES, LANES))
    head = _iota((SUBLANES, LANES), 0)
    lane = _iota((SUBLANES, LANES), 1)
    o = jnp.zeros((SUBLANES, NSA_HD), F32)
    for br, ob in enumerate((oc_ref[0, 0], o_slc, o_win)):
        o = o + jnp.sum(jnp.where(lane == 3 * head + br, gate, 0.0), axis=-1, keepdims=True) * ob
    o_ref[0, 0] = o


def dec_attn(sel, table, pool_ks, pool_vs, q, new, kw_past, vw_past, o_cmp, gates, b_gate):
    n_pool, page, G, hd = pool_ks.shape
    bsz, n_pages = table.shape
    halves = page // SEL_BLOCK
    wb = kw_past.shape[1]
    assert wb <= WINDOW
    n_sel = sel.shape[-1]
    pk = pool_ks.reshape(n_pool, halves, SEL_BLOCK, G * hd)
    pv = pool_vs.reshape(n_pool, halves, SEL_BLOCK, G * hd)
    n_cache_blocks = n_pages * halves

    def page_map(b, g, sel_ref, tbl_ref, s):
        blk = jnp.minimum(sel_ref[(b * G + g) * n_sel + s], n_cache_blocks - 1)
        return (tbl_ref[b * n_pages + blk // halves], blk % halves, 0, g)

    bg_map = lambda b, g, sel_ref, tbl_ref: (b, g, 0, 0)
    small = pl.BlockSpec((1, 1, SUBLANES, hd), bg_map)
    wspec = pl.BlockSpec((1, wb, hd), lambda b, g, sel_ref, tbl_ref: (b, 0, g))
    blocks = [pl.BlockSpec((1, 1, SEL_BLOCK, hd), functools.partial(page_map, s=s)) for s in range(n_sel)]
    grid_spec = pltpu.PrefetchScalarGridSpec(
        num_scalar_prefetch=2,
        grid=(bsz, G),
        in_specs=blocks + blocks + [small, small, wspec, wspec, small,
                                    pl.BlockSpec((1, 1, 1, LANES), bg_map),
                                    pl.BlockSpec((1, LANES), lambda b, g, sel_ref, tbl_ref: (0, g))],
        out_specs=small,
    )
    return pl.pallas_call(
        functools.partial(_dec_attn_kernel, n_cache_blocks=n_cache_blocks, n_sel=n_sel),
        grid_spec=grid_spec,
        out_shape=jax.ShapeDtypeStruct((bsz, G, SUBLANES, hd), F32),
        compiler_params=_cparams("parallel", "parallel"),
    )(sel.reshape(-1), table.reshape(-1), *([pk] * n_sel), *([pv] * n_sel), q, new,
      kw_past.reshape(bsz, wb, G * hd), vw_past.reshape(bsz, wb, G * hd), o_cmp,
      gates.reshape(bsz, G, 1, LANES), b_gate)


def _gate_weights(w, col0, n):
    return _pad_lanes(w[:, col0:col0 + n])[None], 0


def _nsa_gate_weights(w, b_gate, col0):
    per = 3 * NSA_HPG
    k = w.shape[0]
    wg = w[:, col0:col0 + NSA_KV * per].reshape(k, NSA_KV, per)
    wg = jnp.pad(wg, ((0, 0), (0, 0), (0, LANES - per))).reshape(1, k, NSA_KV * LANES)
    bg = jnp.pad(b_gate.reshape(NSA_KV, per), ((0, 0), (0, LANES - per))).reshape(1, NSA_KV * LANES)
    return (wg, 0), bg


def kernel(x_prompt, x_sample, state_a_C, state_a_n, state_a_m, state_b_conv, cache_c_k_cmp, cache_c_v_cmp, cache_c_k_slc, cache_c_v_slc, cache_c_k_win, cache_c_v_win, page_table, norm_w, ffn_w_in, ffn_w_out, a_w_in, a_b_gates, a_head_norm, a_w_out, b_w_in, b_conv_w, b_w_out, c_w_in, c_b_gate, c_cmp_pos, c_cmp_w1, c_cmp_b1, c_cmp_w2, c_cmp_b2, c_w_out):
    bp, T, D = x_prompt.shape
    bs = x_sample.shape[0]
    assert x_sample.shape[1] == 1
    depth = norm_w.shape[0]
    G, hd = NSA_KV, NSA_HD
    past_len = page_table.shape[1] * cache_c_k_cmp.shape[2]
    xp = x_prompt.reshape(bp * T, D)
    xs = x_sample.reshape(bs, D)
    hp_in = norm_cast(xp, norm_w[0, 0])
    hs_in = norm_cast(xs, norm_w[0, 0])
    pa, sa, pb, sb, pc, sc = [], [], [], [], [], []
    for i in range(depth):
        kind, j = i % 3, i // 3
        nw = norm_w[i]
        if kind == 0:
            hdv = a_head_norm.shape[1]
            n_main = 3 * hdv
            wg = _gate_weights(a_w_in[j], n_main, 2 * MLSTM_HEADS)
            qkvo_p, qkvo_s = proj_plain(hp_in, hs_in, (a_w_in, j), 0, n_main, 1024)
            g_p, g_s = proj_plain(hp_in, hs_in, wg, 0, LANES, LANES)
            hp, c_p, n_p, m_p = mlstm_prompt(qkvo_p, g_p, a_b_gates[j], a_head_norm[j], bp, T)
            hs, c_s, n_s, m_s = mlstm_step(qkvo_s, g_s, a_b_gates[j], a_head_norm[j],
                                           state_a_C[j], state_a_n[j], state_a_m[j])
            pa.append((c_p, n_p, m_p[:, :, 0]))
            sa.append((c_s, n_s, m_s))
            w_out = a_w_out[j].astype(BF16)
        elif kind == 1:
            (bg_p, z_p), (bg_s, z_s) = proj_conv_in(hp_in, hs_in, (b_w_in, j), D)
            hp = conv_gate(bg_p, z_p, b_conv_w[j], T)
            buf = state_b_conv[j]
            hs = conv_gate_step(bg_s, z_s, buf.transpose(1, 0, 2), b_conv_w[j]).astype(BF16)
            pb.append(z_p.reshape(bp, T, D)[:, T - (CONV_W - 1):])
            sb.append(jnp.concatenate([buf, z_s[:, None]], axis=1)[:, -(CONV_W - 1):])
            w_out = b_w_out[j].astype(BF16)
        else:
            nq_cols = NSA_HPG * G * hd
            nheads = G * NSA_HPG
            wg, bgate = _nsa_gate_weights(c_w_in[j], c_b_gate[j], nq_cols + 6 * G * hd)
            cw = _cmp_weights(c_cmp_pos[j], c_cmp_w1[j], c_cmp_b1[j], c_cmp_w2[j], c_cmp_b2[j])
            q_p, q_s = proj_plain(hp_in, hs_in, (c_w_in, j), 0, nq_cols, 1024)
            kv_p, kv_s = proj_groups(hp_in, hs_in, (c_w_in, j), nq_cols, 6, G * hd, 256)
            gt_p, gt_s = proj_plain(hp_in, hs_in, wg, 0, G * LANES, G * LANES)
            ck, cv = cmp_prompt(kv_p[0], kv_p[1], bp, T, cw)
            hp = nsa_attn_prompt(q_p, ck, cv, kv_p[2], kv_p[3], kv_p[4], kv_p[5], gt_p, bgate, bp, T)
            kv_p = [a.reshape(bp, T, G, hd) for a in kv_p]
            keep = min(WINDOW, T)
            pc.append(tuple(kv_p[:4]) + tuple(a[:, T - keep:] for a in kv_p[4:]))
            q_s = jnp.pad(q_s.reshape(bs, G, NSA_HPG, hd), ((0, 0), (0, 0), (0, SUBLANES - NSA_HPG), (0, 0)))
            kv_s = jnp.stack([a.reshape(bs, G, hd) for a in kv_s], axis=1)
            new = jnp.pad(kv_s[:, 2:6].transpose(0, 2, 1, 3), ((0, 0), (0, 0), (0, SUBLANES - 4), (0, 0)))
            ak, av = page_cmp(cache_c_k_cmp[j], cache_c_v_cmp[j], page_table, cw[0])
            o_cmp, ranked = dec_cmp(ak, av, q_s, cw, past_len)
            n_sel = min(N_SELECT, -(-(past_len + 1) // SEL_BLOCK))
            o_s = dec_attn(ranked[:, :, :n_sel], page_table, cache_c_k_slc[j], cache_c_v_slc[j], q_s, new,
                           cache_c_k_win[j], cache_c_v_win[j], o_cmp, gt_s, bgate)
            hs = o_s[:, :, :NSA_HPG].reshape(bs, nheads * hd).astype(BF16)
            keep_s = min(WINDOW, cache_c_k_win.shape[2] + 1)
            win = [jnp.concatenate([c[j], kv_s[:, a][:, None]], axis=1)[:, -keep_s:]
                   for c, a in ((cache_c_k_win, 4), (cache_c_v_win, 5))]
            sc.append(tuple(kv_s[:, a][:, None] for a in range(4)) + tuple(win))
            w_out = c_w_out[j].astype(BF16)
        last = i == depth - 1
        nw_next = norm_w[i + 1, 0] if not last else nw[0]
        f_out = ffn_w_out[i].astype(BF16)
        xp, hp_mid = proj_out(hp, w_out, xp, nw[1], nw[2])
        xs, hs_mid = proj_out(hs, w_out, xs, nw[1], nw[2])
        act_p, act_s = proj_swiglu(hp_mid, hs_mid, (ffn_w_in, i), f_out.shape[0])
        xp, hp_in = proj_out(act_p, f_out, xp, nw[3], nw_next, not last)
        xs, hs_in = proj_out(act_s, f_out, xs, nw[3], nw_next, not last)

    stack = lambda items: [jnp.stack(a) for a in zip(*items)]
    p_a, s_a = stack(pa), stack(sa)
    p_c, s_c = stack(pc), stack(sc)
    return (xp.reshape(bp, T, D), xs.reshape(bs, 1, D), *p_a, *s_a, jnp.stack(pb), jnp.stack(sb), *p_c, *s_c)
```

```python
import functools
import math

import jax
import jax.numpy as jnp
from jax import lax
from jax.experimental import pallas as pl
from jax.experimental.pallas import tpu as pltpu

F32 = jnp.float32
BF16 = jnp.bfloat16

EPS = 1e-6
NEG = -1e30
BIG = 1e30

LANES = 128
SUBLANES = 8
VMEM_LIMIT = 56 * 1024 * 1024

MLSTM_HEADS = 8
MLSTM_CHUNK = 256
FORGET_BIAS_COL = MLSTM_HEADS
CONV_W = 3
NSA_HD = 128
NSA_KV = 4
NSA_HPG = 4
CMP_BLOCK = 32
CMP_STRIDE = 16
SEL_BLOCK = 64
N_SELECT = 16
WINDOW = 512
ATT_TILE = 256
PAGES_PER_STEP = 8


def _cparams(*sem):
    return pltpu.CompilerParams(dimension_semantics=sem, vmem_limit_bytes=VMEM_LIMIT)


def _dot(a, b):
    return jnp.dot(a, b, preferred_element_type=F32)


def _dot_nt(a, b):
    return lax.dot_general(a, b, (((1,), (1,)), ((), ())), preferred_element_type=F32)


def _dot_tn(a, b):
    return lax.dot_general(a, b, (((0,), (0,)), ((), ())), preferred_element_type=F32)


def _split_bf16(x):
    hi = x.astype(BF16)
    lo = (x - hi.astype(F32)).astype(BF16)
    return hi, lo


def _iota(shape, dim):
    return lax.broadcasted_iota(jnp.int32, shape, dim)


def _sigmoid(x):
    return 1.0 / (1.0 + jnp.exp(-x))


def _rms(x):
    return x * lax.rsqrt(jnp.mean(x * x, axis=-1, keepdims=True) + EPS)


def _row_tile(m, want):
    t = min(m, want)
    assert m % t == 0, (m, t)
    return t


def _norm_kernel(x_ref, w_ref, o_ref):
    o_ref[...] = (_rms(x_ref[...]) * w_ref[...]).astype(BF16)


def norm_cast(x, w):
    m, d = x.shape
    tm = _row_tile(m, 512)
    return pl.pallas_call(
        _norm_kernel,
        grid=(m // tm,),
        in_specs=[pl.BlockSpec((tm, d), lambda i: (i, 0)), pl.BlockSpec((1, d), lambda i: (0, 0))],
        out_specs=pl.BlockSpec((tm, d), lambda i: (i, 0)),
        out_shape=jax.ShapeDtypeStruct((m, d), BF16),
        compiler_params=_cparams("parallel"),
    )(x, w.reshape(1, d))


def _proj_kernel(x_ref, xs_ref, *refs, n_groups, n_out, epilogue):
    w_refs = refs[:n_groups]
    out_refs = refs[n_groups:n_groups + n_out]
    sample_out_refs = refs[n_groups + n_out:n_groups + 2 * n_out]
    wb_ref = refs[n_groups + 2 * n_out]

    @pl.when(pl.program_id(1) == 0)
    def _():
        for g in range(n_groups):
            wb_ref[g] = w_refs[g][...].astype(BF16)
        xs = xs_ref[...]
        epilogue([_dot(xs, wb_ref[g]) for g in range(n_groups)], sample_out_refs)

    x = x_ref[...]
    epilogue([_dot(x, wb_ref[g]) for g in range(n_groups)], out_refs)


def _proj(x, xs, w, *, col0, tn, n_tiles, group_stride, n_groups, epilogue, outs, tm_want=1024):
    w, layer = w
    m, k = x.shape
    s = xs.shape[0]
    tm = _row_tile(m, tm_want)
    assert col0 % tn == 0
    b0 = col0 // tn
    w_specs = [pl.BlockSpec((None, k, tn),
                            functools.partial(lambda j, i, off: (layer, 0, off + j), off=b0 + g * group_stride))
               for g in range(n_groups)]
    n_out = len(outs)
    res = pl.pallas_call(
        functools.partial(_proj_kernel, n_groups=n_groups, n_out=n_out, epilogue=epilogue),
        grid=(n_tiles, m // tm),
        in_specs=[pl.BlockSpec((tm, k), lambda j, i: (i, 0)), pl.BlockSpec((s, k), lambda j, i: (0, 0))] + w_specs,
        out_specs=[pl.BlockSpec((tm, tn), lambda j, i: (i, j))] * n_out
        + [pl.BlockSpec((s, tn), lambda j, i: (0, j))] * n_out,
        out_shape=[jax.ShapeDtypeStruct((m, n), dt) for n, dt in outs]
        + [jax.ShapeDtypeStruct((s, n), dt) for n, dt in outs],
        scratch_shapes=[pltpu.VMEM((n_groups, k, tn), BF16)],
        compiler_params=_cparams("arbitrary", "arbitrary"),
    )(x, xs, *([w] * n_groups))
    return res[:n_out], res[n_out:]


def _ep_plain(accs, outs):
    for acc, o in zip(accs, outs):
        o[...] = acc


def proj_plain(x, xs, w, col0, n_cols, tn):
    p, s = _proj(x, xs, w, col0=col0, tn=tn, n_tiles=n_cols // tn, group_stride=0, n_groups=1,
                 epilogue=_ep_plain, outs=[(n_cols, F32)])
    return p[0], s[0]


def proj_groups(x, xs, w, col0, n_groups, group_cols, tn):
    return _proj(x, xs, w, col0=col0, tn=tn, n_tiles=group_cols // tn, group_stride=group_cols // tn,
                 n_groups=n_groups, epilogue=_ep_plain, outs=[(group_cols, F32)] * n_groups, tm_want=512)


def _ep_swiglu(accs, outs):
    g, u = accs
    outs[0][...] = (g * _sigmoid(g) * u).astype(BF16)


def proj_swiglu(x, xs, w, ff):
    tn = 512
    p, s = _proj(x, xs, w, col0=0, tn=tn, n_tiles=ff // tn, group_stride=ff // tn, n_groups=2,
                 epilogue=_ep_swiglu, outs=[(ff, BF16)])
    return p[0], s[0]


def _ep_conv_in(accs, outs):
    bg, cg, u = accs
    outs[0][...] = bg
    outs[1][...] = cg * u


def proj_conv_in(x, xs, w, d):
    tn = 512
    return _proj(x, xs, w, col0=0, tn=tn, n_tiles=d // tn, group_stride=d // tn, n_groups=3,
                 epilogue=_ep_conv_in, outs=[(d, F32), (d, F32)])


def _out_kernel(h_ref, w_ref, r_ref, nwa_ref, nwb_ref, *refs, nk, emit_next):
    if emit_next:
        x_ref, xn_ref = refs[0], refs[1]
        acc_ref = refs[2] if nk > 1 else None
    else:
        x_ref, xn_ref = refs[0], None
        acc_ref = refs[1] if nk > 1 else None

    def finalize(y):
        x_new = r_ref[...] + _rms(y) * nwa_ref[...]
        x_ref[...] = x_new
        if emit_next:
            xn_ref[...] = (_rms(x_new) * nwb_ref[...]).astype(BF16)

    part = _dot(h_ref[...], w_ref[...])
    if nk == 1:
        finalize(part)
        return
    kk = pl.program_id(1)

    @pl.when(kk == 0)
    def _():
        acc_ref[...] = part

    @pl.when(kk > 0)
    def _():
        acc_ref[...] += part

    @pl.when(kk == nk - 1)
    def _():
        finalize(acc_ref[...])


def proj_out(h, w, resid, nw_post, nw_next, emit_next=True):
    m, k = h.shape
    d = w.shape[1]
    tm = _row_tile(m, 512)
    tk = k if k <= 2048 else 1408
    assert k % tk == 0
    nk = k // tk
    row = lambda i, kk: (i, 0)
    out_shape = [jax.ShapeDtypeStruct((m, d), F32)]
    out_specs = [pl.BlockSpec((tm, d), row)]
    if emit_next:
        out_shape.append(jax.ShapeDtypeStruct((m, d), BF16))
        out_specs.append(pl.BlockSpec((tm, d), row))
    res = pl.pallas_call(
        functools.partial(_out_kernel, nk=nk, emit_next=emit_next),
        grid=(m // tm, nk),
        in_specs=[pl.BlockSpec((tm, tk), lambda i, kk: (i, kk)),
                  pl.BlockSpec((tk, d), lambda i, kk: (kk, 0)),
                  pl.BlockSpec((tm, d), row),
                  pl.BlockSpec((1, d), lambda i, kk: (0, 0)),
                  pl.BlockSpec((1, d), lambda i, kk: (0, 0))],
        out_specs=out_specs,
        out_shape=out_shape,
        scratch_shapes=[pltpu.VMEM((tm, d), F32)] if nk > 1 else [],
        compiler_params=_cparams("parallel", "arbitrary"),
    )(h, w, resid, nw_post.reshape(1, d), nw_next.reshape(1, d))
    return (res[0], res[1]) if emit_next else (res[0], None)


def _log_sigmoid(x):
    return jnp.minimum(x, 0.0) - jnp.log1p(jnp.exp(-jnp.abs(x)))


def _mlstm_kernel(q_ref, k_ref, v_ref, o_ref, g_ref, bg_ref, hn_ref, hg_ref, c_ref, n_ref, m_ref, *, L, H, DK, DV):
    @pl.when(pl.program_id(1) == 0)
    def _():
        c_ref[...] = jnp.zeros_like(c_ref)
        n_ref[...] = jnp.zeros_like(n_ref)
        m_ref[...] = jnp.zeros_like(m_ref)

    gpre = g_ref[...] + bg_ref[...]
    lf = _log_sigmoid(gpre)
    rows = _iota((L, L), 0)
    cols = _iota((L, L), 1)
    causal = rows >= cols
    tril = jnp.where(causal, 1.0, 0.0).astype(BF16)
    lf_hi, lf_lo = _split_bf16(lf)
    bcum = _dot(tril, lf_hi) + _dot(tril, lf_lo)
    g_t = gpre.T
    b_t = bcum.T
    for h in range(H):
        f = FORGET_BIAS_COL + h
        a_col = bcum[:, f:f + 1]
        i_col = gpre[:, h:h + 1]
        b_row = b_t[f:f + 1, :]
        i_row = g_t[h:h + 1, :]
        m_prev = m_ref[0, h:h + 1, 0:1]
        log_d = jnp.where(causal, a_col - b_row + i_row, NEG)
        log_inter = a_col + m_prev
        m_t = jnp.maximum(log_inter, jnp.max(log_d, axis=-1, keepdims=True))
        d = jnp.exp(log_d - m_t)
        inter = jnp.exp(log_inter - m_t)
        qh = q_ref[:, h * DK:(h + 1) * DK]
        kh = k_ref[:, h * DK:(h + 1) * DK] * (DK ** -0.5)
        vb = v_ref[:, h * DV:(h + 1) * DV].astype(BF16)
        qb = qh.astype(BF16)
        s = _dot_nt(qb, kh.astype(BF16)) * d
        c_old = c_ref[0, h]
        n_old = n_ref[0, h:h + 1, :]
        num = _dot(s.astype(BF16), vb) + inter * _dot(qb, c_old.astype(BF16))
        den = jnp.sum(s, axis=-1, keepdims=True) + inter * jnp.sum(qh * n_old, axis=-1, keepdims=True)
        hh = _rms(num / jnp.maximum(jnp.abs(den), jnp.exp(-m_t)))
        gate = _sigmoid(o_ref[:, h * DV:(h + 1) * DV])
        hg_ref[:, h * DV:(h + 1) * DV] = (hh * hn_ref[:, h * DV:(h + 1) * DV] * gate).astype(BF16)
        m_new = m_t[L - 1:L, :]
        w_col = jnp.exp(a_col[L - 1:L, :] - a_col + i_col - m_new)
        decay = jnp.exp(log_inter[L - 1:L, :] - m_new)
        kw = kh * w_col
        c_ref[0, h] = decay * c_old + _dot_tn(kw.astype(BF16), vb)
        n_ref[0, h:h + 1, :] = decay * n_old + jnp.sum(kw, axis=0, keepdims=True)
        m_ref[0, h:h + 1, :] = jnp.broadcast_to(m_new, (1, LANES))


def mlstm_prompt(qkvo, gates, b_gates, head_norm, bsz, T):
    H = MLSTM_HEADS
    m = qkvo.shape[0]
    hdv = qkvo.shape[1] // 3
    hdk = hdv // 2
    DK, DV = hdk // H, hdv // H
    L = math.gcd(T, MLSTM_CHUNK)
    nc = T // L
    row = lambda b, c: (b * nc + c, 0)
    state = lambda b, c: (b, 0, 0)
    return pl.pallas_call(
        functools.partial(_mlstm_kernel, L=L, H=H, DK=DK, DV=DV),
        grid=(bsz, nc),
        in_specs=[pl.BlockSpec((L, hdk), row),
                  pl.BlockSpec((L, hdk), lambda b, c: (b * nc + c, 1)),
                  pl.BlockSpec((L, hdv), lambda b, c: (b * nc + c, 1)),
                  pl.BlockSpec((L, hdv), lambda b, c: (b * nc + c, 2)),
                  pl.BlockSpec((L, LANES), row),
                  pl.BlockSpec((1, LANES), lambda b, c: (0, 0)),
                  pl.BlockSpec((1, hdv), lambda b, c: (0, 0))],
        out_specs=[pl.BlockSpec((L, hdv), row),
                   pl.BlockSpec((1, H, DK, DV), lambda b, c: (b, 0, 0, 0)),
                   pl.BlockSpec((1, H, DK), state),
                   pl.BlockSpec((1, H, LANES), state)],
        out_shape=[jax.ShapeDtypeStruct((m, hdv), BF16),
                   jax.ShapeDtypeStruct((bsz, H, DK, DV), F32),
                   jax.ShapeDtypeStruct((bsz, H, DK), F32),
                   jax.ShapeDtypeStruct((bsz, H, LANES), F32)],
        compiler_params=_cparams("parallel", "arbitrary"),
    )(qkvo, qkvo, qkvo, qkvo, gates, _pad_lanes(b_gates.reshape(1, -1)), head_norm.reshape(1, hdv))


def _pad_lanes(x, width=LANES):
    return jnp.pad(x, ((0, 0), (0, width - x.shape[-1])))


def _to_col(row, n):
    eye = _iota((n, n), 0) == _iota((n, n), 1)
    return jnp.sum(jnp.where(eye, jnp.broadcast_to(row, (n, n)), 0.0), axis=-1, keepdims=True)


def _mlstm_step_kernel(x_ref, g_ref, bg_ref, hn_ref, c0_ref, n0_ref, m0_ref, hg_ref, c_ref, n_ref, m_ref,
                       *, H, DK, DV):
    x = x_ref[0]
    gpre = g_ref[0] + bg_ref[...]
    lf = _log_sigmoid(gpre)
    hdk = H * DK
    hdv = H * DV
    for h in range(H):
        f = FORGET_BIAS_COL + h
        ig = gpre[:, h:h + 1]
        m_prev = m0_ref[0, h:h + 1, :]
        log_inter = lf[:, f:f + 1] + m_prev
        m_t = jnp.maximum(log_inter, ig)
        d = jnp.exp(ig - m_t)
        inter = jnp.exp(log_inter - m_t)
        q = x[:, h * DK:(h + 1) * DK]
        k = x[:, hdk + h * DK:hdk + (h + 1) * DK] * (DK ** -0.5)
        v = x[:, 2 * hdk + h * DV:2 * hdk + (h + 1) * DV]
        og = x[:, 2 * hdk + hdv + h * DV:2 * hdk + hdv + (h + 1) * DV]
        c_old = c0_ref[0, h]
        n_old = n0_ref[0, h:h + 1, :]
        s = jnp.sum(q * k, axis=-1, keepdims=True) * d
        q_col = _to_col(q, DK)
        k_col = _to_col(k, DK)
        num = s * v + inter * jnp.sum(q_col * c_old, axis=0, keepdims=True)
        den = s + inter * jnp.sum(q * n_old, axis=-1, keepdims=True)
        hh = _rms(num / jnp.maximum(jnp.abs(den), jnp.exp(-m_t)))
        hg_ref[0, :, h * DV:(h + 1) * DV] = (hh * hn_ref[:, h * DV:(h + 1) * DV] * _sigmoid(og)).astype(BF16)
        c_ref[0, h] = inter * c_old + d * (k_col * v)
        n_ref[0, h:h + 1, :] = inter * n_old + d * k
        m_ref[0, h:h + 1, :] = m_t


def mlstm_step(qkvo, gates, b_gates, head_norm, c0, n0, m0):
    bsz, H, DK, DV = c0.shape
    wid = qkvo.shape[1]
    hdv = H * DV
    one = lambda b: (b, 0, 0)
    hg, c1, n1, m1 = pl.pallas_call(
        functools.partial(_mlstm_step_kernel, H=H, DK=DK, DV=DV),
        grid=(bsz,),
        in_specs=[pl.BlockSpec((1, 1, wid), one),
                  pl.BlockSpec((1, 1, LANES), one),
                  pl.BlockSpec((1, LANES), lambda b: (0, 0)),
                  pl.BlockSpec((1, hdv), lambda b: (0, 0)),
                  pl.BlockSpec((1, H, DK, DV), lambda b: (b, 0, 0, 0)),
                  pl.BlockSpec((1, H, DK), one),
                  pl.BlockSpec((1, H, 1), one)],
        out_specs=[pl.BlockSpec((1, 1, hdv), one),
                   pl.BlockSpec((1, H, DK, DV), lambda b: (b, 0, 0, 0)),
                   pl.BlockSpec((1, H, DK), one),
                   pl.BlockSpec((1, H, 1), one)],
        out_shape=[jax.ShapeDtypeStruct((bsz, 1, hdv), BF16),
                   jax.ShapeDtypeStruct((bsz, H, DK, DV), F32),
                   jax.ShapeDtypeStruct((bsz, H, DK), F32),
                   jax.ShapeDtypeStruct((bsz, H, 1), F32)],
        compiler_params=_cparams("parallel"),
    )(qkvo.reshape(bsz, 1, wid), gates.reshape(bsz, 1, LANES), _pad_lanes(b_gates.reshape(1, -1)),
      head_norm.reshape(1, hdv), c0, n0, m0.reshape(bsz, H, 1))
    return hg.reshape(bsz, hdv), c1, n1, m1.reshape(bsz, H)


def _conv_kernel(bg_ref, z_ref, zp_ref, cw_ref, o_ref, *, tm, T):
    i = pl.program_id(0)
    z = z_ref[...]
    zc = jnp.concatenate([zp_ref[...], z], axis=0)
    tpos = (i * tm + _iota((tm, 1), 0)) % T
    y = cw_ref[CONV_W - 1:CONV_W, :] * z
    for back in range(1, CONV_W):
        zb = zc[SUBLANES - back:SUBLANES - back + tm, :]
        y = y + cw_ref[CONV_W - 1 - back:CONV_W - back, :] * jnp.where(tpos >= back, zb, 0.0)
    o_ref[...] = (bg_ref[...] * y).astype(BF16)


def conv_gate(bg, z, conv_w, T):
    m, d = z.shape
    tm = _row_tile(T, 256)
    per = tm // SUBLANES
    return pl.pallas_call(
        functools.partial(_conv_kernel, tm=tm, T=T),
        grid=(m // tm,),
        in_specs=[pl.BlockSpec((tm, d), lambda i: (i, 0)),
                  pl.BlockSpec((tm, d), lambda i: (i, 0)),
                  pl.BlockSpec((SUBLANES, d), lambda i: (jnp.maximum(i * per - 1, 0), 0)),
                  pl.BlockSpec((CONV_W, d), lambda i: (0, 0))],
        out_specs=pl.BlockSpec((tm, d), lambda i: (i, 0)),
        out_shape=jax.ShapeDtypeStruct((m, d), BF16),
        compiler_params=_cparams("parallel"),
    )(bg, z, z, conv_w)


def _conv_step_kernel(bg_ref, z_ref, buf_ref, cw_ref, o_ref):
    y = cw_ref[CONV_W - 1:CONV_W, :] * z_ref[...]
    for j in range(CONV_W - 1):
        y = y + cw_ref[j:j + 1, :] * buf_ref[j]
    o_ref[...] = bg_ref[...] * y


def conv_gate_step(bg, z, buf, conv_w):
    bsz, d = z.shape
    full = lambda: (0, 0)
    return pl.pallas_call(
        _conv_step_kernel,
        in_specs=[pl.BlockSpec((bsz, d), full), pl.BlockSpec((bsz, d), full),
                  pl.BlockSpec((CONV_W - 1, bsz, d), lambda: (0, 0, 0)), pl.BlockSpec((CONV_W, d), full)],
        out_specs=pl.BlockSpec((bsz, d), full),
        out_shape=jax.ShapeDtypeStruct((bsz, d), F32),
    )(bg, z, buf, conv_w)


def _gelu(x):
    return 0.5 * x * (1.0 + jnp.tanh(math.sqrt(2.0 / math.pi) * (x + 0.044715 * x * x * x)))


def _cmp_finish(a, posw, b1, w2, b2):
    hid = a.shape[1] // 2
    a1 = a[:, hid:]
    a1_next = jnp.concatenate([a1[1:], a1[:1]], axis=0)
    pre = a[:, :hid] + a1_next + b1 + posw[0:1, :hid] + posw[1:2, hid:]
    return _dot(_gelu(pre).astype(BF16), w2) + b2


def _cmp_prompt_kernel(k_ref, v_ref, w1_ref, pos_ref, b1_ref, w2_ref, b2_ref, ck_ref, cv_ref, *, nch):
    for idx, (x_ref, o_ref) in enumerate(((k_ref, ck_ref), (v_ref, cv_ref))):
        w1 = w1_ref[idx]
        lhs = jnp.concatenate([x_ref[pl.ds(p, nch, stride=CMP_STRIDE), :] for p in range(CMP_STRIDE)], axis=1)
        a = _dot(lhs.astype(BF16), w1)
        posw = _dot(pos_ref[idx], w1)
        o_ref[0, 0] = _cmp_finish(a, posw, b1_ref[idx], w2_ref[idx], b2_ref[idx])


def _cmp_weights(cmp_pos, cmp_w1, cmp_b1, cmp_w2, cmp_b2):
    two, ratio, stride, hd, hid = cmp_w1.shape
    w1 = cmp_w1.transpose(0, 2, 3, 1, 4).reshape(two, stride * hd, ratio * hid).astype(BF16)
    pos = cmp_pos.reshape(two, ratio, stride * hd)
    pos = jnp.pad(pos, ((0, 0), (0, SUBLANES - ratio), (0, 0))).astype(BF16)
    return w1, pos, cmp_b1.reshape(two, 1, hid), cmp_w2.astype(BF16), cmp_b2.reshape(two, 1, hd)


def cmp_prompt(kc, vc, bsz, T, cw):
    w1, pos, b1, w2, b2 = cw
    nch = T // CMP_STRIDE
    G = NSA_KV
    out = jax.ShapeDtypeStruct((bsz, G, nch, NSA_HD), F32)
    ospec = pl.BlockSpec((1, 1, nch, NSA_HD), lambda b, g: (b, g, 0, 0))
    whole = lambda a: pl.BlockSpec(a.shape, lambda b, g: (0,) * a.ndim)
    seq = pl.BlockSpec((T, NSA_HD), lambda b, g: (b, g))
    return pl.pallas_call(
        functools.partial(_cmp_prompt_kernel, nch=nch),
        grid=(bsz, G),
        in_specs=[seq, seq, whole(w1), whole(pos), whole(b1), whole(w2), whole(b2)],
        out_specs=[ospec, ospec],
        out_shape=[out, out],
        compiler_params=_cparams("parallel", "parallel"),
    )(kc, vc, w1, pos, b1, w2, b2)


def _overlap_t(nbs_pad, nbc_pad):
    j = _iota((nbs_pad, nbc_pad), 0)
    n = _iota((nbs_pad, nbc_pad), 1)
    lo = jnp.maximum(n * CMP_STRIDE, j * SEL_BLOCK)
    hi = jnp.minimum(n * CMP_STRIDE + CMP_BLOCK, j * SEL_BLOCK + SEL_BLOCK)
    return (jnp.maximum(hi - lo, 0) // CMP_STRIDE).astype(F32)


def _rank_select(score, n_sel, axis):
    n = score.shape[axis]
    idx = _iota(score.shape, axis)
    cnt = jnp.zeros(score.shape, F32)
    for jp in range(n):
        other = lax.slice_in_dim(score, jp, jp + 1, axis=axis)
        tie = jnp.where(idx > jp, 1.0, 0.0)
        cnt = cnt + jnp.where(other > score, 1.0, jnp.where(other == score, tie, 0.0))
    return jnp.where(cnt < n_sel, 1.0, 0.0)


def _attn_first(state, s, vt_b):
    m_ref, l_ref, acc_ref = state
    m = jnp.max(s, axis=0, keepdims=True)
    p = jnp.exp(s - m)
    m_ref[...] = m
    l_ref[...] = jnp.sum(p, axis=0, keepdims=True)
    acc_ref[...] = _dot(vt_b, p.astype(BF16))


def _attn_more(state, s, vt_b):
    m_ref, l_ref, acc_ref = state
    m_old = m_ref[...]
    m_new = jnp.maximum(m_old, jnp.max(s, axis=0, keepdims=True))
    alpha = jnp.exp(m_old - m_new)
    p = jnp.exp(s - m_new)
    m_ref[...] = m_new
    l_ref[...] = alpha * l_ref[...] + jnp.sum(p, axis=0, keepdims=True)
    acc_ref[...] = alpha * acc_ref[...] + _dot(vt_b, p.astype(BF16))


def _nsa_attn_kernel(q_ref, ck_ref, cv_ref, ks_ref, vs_ref, kw_ref, vw_ref, g_ref, bg_ref, o_ref,
                     ksb_ref, vst_ref, kwb_ref, vwt_ref, ms_ref, ls_ref, as_ref, mw_ref, lw_ref, aw_ref,
                     *, tq, hpg, nbc, nbs, nbs_pad):
    qi = pl.program_id(2)
    t0 = qi * tq
    cols = hpg * tq
    tk = tq
    n_kt = ks_ref.shape[0] // tk
    nbc_pad = ck_ref.shape[2]

    @pl.when(qi == 0)
    def _():
        ksb_ref[...] = ks_ref[...].astype(BF16)
        kwb_ref[...] = kw_ref[...].astype(BF16)
        for kt in range(n_kt):
            vst_ref[kt] = vs_ref[kt * tk:(kt + 1) * tk, :].T.astype(BF16)
            vwt_ref[kt] = vw_ref[kt * tk:(kt + 1) * tk, :].T.astype(BF16)

    qb = (jnp.concatenate([q_ref[:, h * NSA_HD:(h + 1) * NSA_HD] for h in range(hpg)], axis=0)
          * (NSA_HD ** -0.5)).astype(BF16)

    n = _iota((nbc_pad, cols), 0)
    t_col = t0 + jnp.concatenate([_iota((nbc_pad, tq), 1)] * hpg, axis=1)
    cmask = jnp.where(n < nbc, jnp.where(n * CMP_STRIDE + (CMP_BLOCK - 1) <= t_col, 1.0, 0.0), 0.0)
    sc = jnp.where(cmask > 0.5, _dot_nt(ck_ref[0, 0].astype(BF16), qb), NEG)
    pe = jnp.exp(sc - jnp.max(sc, axis=0, keepdims=True))
    p_cmp = pe / jnp.sum(pe, axis=0, keepdims=True) * cmask
    o_cmp = _dot(cv_ref[0, 0].T.astype(BF16), p_cmp.astype(BF16))

    p_sum = p_cmp[:, 0:tq]
    for h in range(1, hpg):
        p_sum = p_sum + p_cmp[:, h * tq:(h + 1) * tq]
    ov_t = _overlap_t(nbs_pad, nbc_pad).astype(BF16)
    p_hi, p_lo = _split_bf16(p_sum)
    imp_t = _dot(ov_t, p_hi) + _dot(ov_t, p_lo)
    j = _iota((nbs_pad, tq), 0)
    t = t0 + _iota((nbs_pad, tq), 1)
    t_blk = t // SEL_BLOCK
    forced = jnp.where(j == 0, 1.0, jnp.where(j == t_blk, 1.0, jnp.where(j == t_blk - 1, 1.0, 0.0)))
    visible = jnp.where(j < nbs, jnp.where(j * SEL_BLOCK <= t, 1.0, 0.0), 0.0)
    score = jnp.where(visible > 0.5, jnp.where(forced > 0.5, BIG, imp_t), NEG)
    sel_t = _rank_select(score, min(N_SELECT, nbs), axis=0).astype(BF16)

    k_ofs = _iota((tk, tq), 0)
    q_ofs = _iota((tk, tq), 1)
    causal = k_ofs <= q_ofs
    per_head = lambda bias: jnp.concatenate([bias] * hpg, axis=1)

    def scores(kb_ref, kt):
        k_b = kb_ref[pl.ds(pl.multiple_of(kt * tk, tk), tk), :]
        return _dot_nt(k_b, qb)

    def slc_bias(kt):
        blk = (kt * tk + _iota((tk, nbs_pad), 0)) // SEL_BLOCK
        expand = jnp.where(blk == _iota((tk, nbs_pad), 1), 1.0, 0.0).astype(BF16)
        return (_dot(expand, sel_t) - 1.0) * BIG

    slc = (ms_ref, ls_ref, as_ref)
    _attn_first(slc, scores(ksb_ref, qi) + per_head(jnp.where(causal, slc_bias(qi), NEG)), vst_ref[qi])

    def slc_body(kt, _):
        _attn_more(slc, scores(ksb_ref, kt) + per_head(slc_bias(kt)), vst_ref[kt])
        return 0

    lax.fori_loop(0, qi, slc_body, 0)

    n_back = WINDOW // tk
    win = (mw_ref, lw_ref, aw_ref)
    _attn_first(win, scores(kwb_ref, qi) + per_head(jnp.where(causal, 0.0, NEG)), vwt_ref[qi])
    for back in range(1, n_back + 1):
        @pl.when(qi >= back)
        def _():
            s = scores(kwb_ref, qi - back)
            if back == n_back:
                s = s + per_head(jnp.where(k_ofs >= q_ofs, 0.0, NEG))
            _attn_more(win, s, vwt_ref[qi - back])

    o_slc = as_ref[...] / ls_ref[...]
    o_win = aw_ref[...] / lw_ref[...]
    gate_t = _sigmoid(g_ref[...] + bg_ref[...]).T
    for h in range(hpg):
        sl = slice(h * tq, (h + 1) * tq)
        o_t = (gate_t[3 * h:3 * h + 1, :] * o_cmp[:, sl] + gate_t[3 * h + 1:3 * h + 2, :] * o_slc[:, sl]
               + gate_t[3 * h + 2:3 * h + 3, :] * o_win[:, sl])
        o_ref[:, h * NSA_HD:(h + 1) * NSA_HD] = o_t.T.astype(BF16)


def nsa_attn_prompt(q, ck, cv, ks, vs, kw, vw, gates, b_gate, bsz, T):
    G, HPG = NSA_KV, NSA_HPG
    tq = math.gcd(T, ATT_TILE)
    assert WINDOW % tq == 0
    nq = T // tq
    nch = T // CMP_STRIDE
    nbc = nch - CMP_BLOCK // CMP_STRIDE + 1
    nbs = -(-T // SEL_BLOCK)
    nbs_pad = -(-nbs // SUBLANES) * SUBLANES
    cols = HPG * tq
    kv_scratch = [pltpu.VMEM((T, NSA_HD), BF16), pltpu.VMEM((nq, NSA_HD, tq), BF16)] * 2
    stats = [pltpu.VMEM((1, cols), F32), pltpu.VMEM((1, cols), F32), pltpu.VMEM((NSA_HD, cols), F32)] * 2
    seq = pl.BlockSpec((T, NSA_HD), lambda b, g, i: (b, g))
    cspec = pl.BlockSpec((1, 1, nch, NSA_HD), lambda b, g, i: (b, g, 0, 0))
    qspec = pl.BlockSpec((tq, HPG * NSA_HD), lambda b, g, i: (b * nq + i, g))
    return pl.pallas_call(
        functools.partial(_nsa_attn_kernel, tq=tq, hpg=HPG, nbc=nbc, nbs=nbs, nbs_pad=nbs_pad),
        grid=(bsz, G, nq),
        in_specs=[qspec, cspec, cspec, seq, seq, seq, seq,
                  pl.BlockSpec((tq, LANES), lambda b, g, i: (b * nq + i, g)),
                  pl.BlockSpec((1, LANES), lambda b, g, i: (0, g))],
        out_specs=qspec,
        out_shape=jax.ShapeDtypeStruct((bsz * T, G * HPG * NSA_HD), BF16),
        scratch_shapes=kv_scratch + stats,
        compiler_params=_cparams("arbitrary", "arbitrary", "arbitrary"),
    )(q, ck, cv, ks, vs, kw, vw, gates, b_gate)


def _page_cmp_kernel(tbl_ref, *refs, P):
    k_pages, v_pages = refs[:P], refs[P:2 * P]
    w1_ref, ak_ref, av_ref = refs[2 * P:]
    rows = _iota((LANES, LANES), 0)
    cols = _iota((LANES, LANES), 1)
    cpp = LANES // CMP_STRIDE
    perm = jnp.where(cols == (rows % cpp) * CMP_STRIDE + rows // cpp, 1.0, 0.0).astype(BF16)
    for idx, (pages, a_ref) in enumerate(((k_pages, ak_ref), (v_pages, av_ref))):
        blocks = []
        for g in range(NSA_KV):
            for pg in pages:
                x = _dot(perm, pg[0, pl.ds(g, LANES, stride=NSA_KV), :].astype(BF16))
                blocks.append(jnp.concatenate([x[p * cpp:(p + 1) * cpp, :] for p in range(CMP_STRIDE)], axis=1))
        a = _dot(jnp.concatenate(blocks, axis=0).astype(BF16), w1_ref[idx])
        per_g = P * cpp
        for g in range(NSA_KV):
            a_ref[0, g] = a[g * per_g:(g + 1) * per_g]


def page_cmp(pool_k, pool_v, table, w1):
    n_pool, page, G, hd = pool_k.shape
    assert page == LANES and G == NSA_KV and hd == NSA_HD
    bsz, n_pages = table.shape
    P = math.gcd(n_pages, PAGES_PER_STEP)
    cpp = page // CMP_STRIDE
    hid2 = w1.shape[2]
    pk = pool_k.reshape(n_pool, page * G, hd)
    pv = pool_v.reshape(n_pool, page * G, hd)
    pspec = lambda i: pl.BlockSpec((1, page * G, hd),
                                   functools.partial(lambda b, s, tbl, i: (tbl[b * n_pages + s * P + i], 0, 0), i=i))
    out = jax.ShapeDtypeStruct((bsz, G, n_pages * cpp, hid2), F32)
    ospec = pl.BlockSpec((1, G, P * cpp, hid2), lambda b, s, tbl: (b, 0, s, 0))
    grid_spec = pltpu.PrefetchScalarGridSpec(
        num_scalar_prefetch=1,
        grid=(bsz, n_pages // P),
        in_specs=[pspec(i) for i in range(P)] * 2 + [pl.BlockSpec(w1.shape, lambda b, s, tbl: (0, 0, 0))],
        out_specs=[ospec, ospec],
    )
    return pl.pallas_call(
        functools.partial(_page_cmp_kernel, P=P),
        grid_spec=grid_spec,
        out_shape=[out, out],
        compiler_params=_cparams("parallel", "parallel"),
    )(table.reshape(-1), *([pk] * P), *([pv] * P), w1)


def _dec_cmp_kernel(ak_ref, av_ref, q_ref, pos_ref, w1_ref, b1_ref, w2_ref, b2_ref, o_ref, sel_ref,
                    *, nbc, nbs, nbs_pad, p0):
    nch = ak_ref.shape[2]
    for g in range(NSA_KV):
        cks = []
        for idx, a_ref in enumerate((ak_ref, av_ref)):
            posw = _dot(pos_ref[idx], w1_ref[idx])
            cks.append(_cmp_finish(a_ref[0, g], posw, b1_ref[idx], w2_ref[idx], b2_ref[idx]))
        ck, cv = cks
        qb = (q_ref[0, g] * (NSA_HD ** -0.5)).astype(BF16)
        n = _iota((SUBLANES, nch), 1)
        cmask = jnp.where(n < nbc, jnp.where(n * CMP_STRIDE + (CMP_BLOCK - 1) <= p0, 1.0, 0.0), 0.0)
        sc = jnp.where(cmask > 0.5, _dot_nt(qb, ck.astype(BF16)), NEG)
        pe = jnp.exp(sc - jnp.max(sc, axis=-1, keepdims=True))
        p_cmp = pe / jnp.sum(pe, axis=-1, keepdims=True) * cmask
        o_ref[0, g] = _dot(p_cmp.astype(BF16), cv.astype(BF16))

        head = jnp.where(_iota((SUBLANES, nch), 0) < NSA_HPG, p_cmp, 0.0)
        p_sum = jnp.broadcast_to(jnp.sum(head, axis=0, keepdims=True), (SUBLANES, nch))
        ov_t = _overlap_t(nbs_pad, nch).astype(BF16)
        p_hi, p_lo = _split_bf16(p_sum)
        imp = _dot_nt(p_hi, ov_t) + _dot_nt(p_lo, ov_t)
        j = _iota((SUBLANES, nbs_pad), 1)
        t_blk = p0 // SEL_BLOCK
        forced = jnp.where(j == 0, 1.0, jnp.where(j == t_blk, 1.0, jnp.where(j == t_blk - 1, 1.0, 0.0)))
        visible = jnp.where(j < nbs, jnp.where(j * SEL_BLOCK <= p0, 1.0, 0.0), 0.0)
        score = jnp.where(visible > 0.5, jnp.where(forced > 0.5, BIG, imp), NEG)
        s_row = jnp.broadcast_to(score[0:1], (nbs_pad, nbs_pad))
        s_col = jnp.concatenate([jnp.broadcast_to(score[0:1], (LANES, nbs_pad)).T] * (nbs_pad // LANES), axis=1)
        jr = _iota((nbs_pad, nbs_pad), 0)
        jc = _iota((nbs_pad, nbs_pad), 1)
        tie = jnp.where(jc < jr, 1.0, 0.0)
        beats = jnp.where(s_row > s_col, 1.0, jnp.where(s_row == s_col, tie, 0.0))
        rank = jnp.sum(beats, axis=-1, keepdims=True)
        slot = _iota((nbs_pad, LANES), 1).astype(F32)
        blk = _iota((nbs_pad, LANES), 0).astype(F32)
        picked = jnp.sum(jnp.where(rank == slot, blk, 0.0), axis=0, keepdims=True)
        sel_ref[0, g:g + 1, :] = picked.astype(jnp.int32)


def dec_cmp(ak, av, q, cw, p0):
    w1, pos, b1, w2, b2 = cw
    bsz, G, nch, hid2 = ak.shape
    nbc = nch - CMP_BLOCK // CMP_STRIDE + 1
    nbs = -(-(p0 + 1) // SEL_BLOCK)
    nbs_pad = -(-nbs // LANES) * LANES
    whole = lambda a: pl.BlockSpec(a.shape, lambda b: (0,) * a.ndim)
    aspec = pl.BlockSpec((1, G, nch, hid2), lambda b: (b, 0, 0, 0))
    return pl.pallas_call(
        functools.partial(_dec_cmp_kernel, nbc=nbc, nbs=nbs, nbs_pad=nbs_pad, p0=p0),
        grid=(bsz,),
        in_specs=[aspec, aspec, pl.BlockSpec((1, G, SUBLANES, NSA_HD), lambda b: (b, 0, 0, 0)),
                  whole(pos), whole(w1), whole(b1), whole(w2), whole(b2)],
        out_specs=[pl.BlockSpec((1, G, SUBLANES, NSA_HD), lambda b: (b, 0, 0, 0)),
                   pl.BlockSpec((1, G, LANES), lambda b: (b, 0, 0))],
        out_shape=[jax.ShapeDtypeStruct((bsz, G, SUBLANES, NSA_HD), F32),
                   jax.ShapeDtypeStruct((bsz, G, LANES), jnp.int32)],
        compiler_params=_cparams("parallel"),
    )(ak, av, q, pos, w1, b1, w2, b2)


def _softmax_with_new_key(qb, k_b, v_b, bias, s_new, v_new):
    s = _dot_nt(qb, k_b)
    if bias is not None:
        s = s + bias
    mx = jnp.maximum(jnp.max(s, axis=-1, keepdims=True), s_new)
    p = jnp.exp(s - mx)
    p_new = jnp.exp(s_new - mx)
    return (_dot(p.astype(BF16), v_b) + p_new * v_new) / (jnp.sum(p, axis=-1, keepdims=True) + p_new)


def _dec_attn_kernel(sel_ref, tbl_ref, *refs, n_cache_blocks, n_sel):
    ks_refs, vs_refs = refs[:n_sel], refs[n_sel:2 * n_sel]
    q_ref, new_ref, kw_ref, vw_ref, oc_ref, g_ref, bg_ref, o_ref = refs[2 * n_sel:]
    b, g = pl.program_id(0), pl.program_id(1)
    q = q_ref[0, 0] * (NSA_HD ** -0.5)
    qb = q.astype(BF16)
    new = new_ref[0, 0]

    mine = lambda r, n: r[pl.ds(g, n, stride=NSA_KV), :]
    k_all = jnp.concatenate([mine(r.at[0, 0], SEL_BLOCK) for r in ks_refs], axis=0).astype(BF16)
    v_all = jnp.concatenate([mine(r.at[0, 0], SEL_BLOCK) for r in vs_refs], axis=0).astype(BF16)
    slot = _iota((SUBLANES, n_sel * SEL_BLOCK), 1) // SEL_BLOCK
    bias = jnp.zeros((SUBLANES, n_sel * SEL_BLOCK), F32)
    for s in range(n_sel):
        blk = sel_ref[(b * NSA_KV + g) * n_sel + s]
        bias = jnp.where(slot == s, jnp.where(blk < n_cache_blocks, 0.0, NEG), bias)
    o_slc = _softmax_with_new_key(qb, k_all, v_all, bias, jnp.sum(q * new[0:1], axis=-1, keepdims=True), new[1:2])

    n_win = kw_ref.shape[1] // NSA_KV
    o_win = _softmax_with_new_key(qb, mine(kw_ref.at[0], n_win).astype(BF16), mine(vw_ref.at[0], n_win).astype(BF16),
                                  None,
                                  jnp.sum(q * new[2:3], axis=-1, keepdims=True), new[3:4])

    gate = jnp.broadcast_to(_sigmoid(g_ref[0, 0] + bg_ref[...]), (SUBLANES, LANES))
    head = _iota((SUBLANES, LANES), 0)
    lane = _iota((SUBLANES, LANES), 1)
    o = jnp.zeros((SUBLANES, NSA_HD), F32)
    for br, ob in enumerate((oc_ref[0, 0], o_slc, o_win)):
        o = o + jnp.sum(jnp.where(lane == 3 * head + br, gate, 0.0), axis=-1, keepdims=True) * ob
    o_ref[0, 0] = o


def dec_attn(sel, table, pool_ks, pool_vs, q, new, kw_past, vw_past, o_cmp, gates, b_gate):
    n_pool, page, G, hd = pool_ks.shape
    bsz, n_pages = table.shape
    halves = page // SEL_BLOCK
    wb = kw_past.shape[1]
    assert wb <= WINDOW
    n_sel = sel.shape[-1]
    pk = pool_ks.reshape(n_pool, halves, SEL_BLOCK * G, hd)
    pv = pool_vs.reshape(n_pool, halves, SEL_BLOCK * G, hd)
    n_cache_blocks = n_pages * halves

    def page_map(b, g, sel_ref, tbl_ref, s):
        blk = jnp.minimum(sel_ref[(b * G + g) * n_sel + s], n_cache_blocks - 1)
        return (tbl_ref[b * n_pages + blk // halves], blk % halves, 0, 0)

    bg_map = lambda b, g, sel_ref, tbl_ref: (b, g, 0, 0)
    small = pl.BlockSpec((1, 1, SUBLANES, hd), bg_map)
    wspec = pl.BlockSpec((1, wb * G, hd), lambda b, g, sel_ref, tbl_ref: (b, 0, 0))
    blocks = [pl.BlockSpec((1, 1, SEL_BLOCK * G, hd), functools.partial(page_map, s=s)) for s in range(n_sel)]
    grid_spec = pltpu.PrefetchScalarGridSpec(
        num_scalar_prefetch=2,
        grid=(bsz, G),
        in_specs=blocks + blocks + [small, small, wspec, wspec, small,
                                    pl.BlockSpec((1, 1, 1, LANES), bg_map),
                                    pl.BlockSpec((1, LANES), lambda b, g, sel_ref, tbl_ref: (0, g))],
        out_specs=small,
    )
    return pl.pallas_call(
        functools.partial(_dec_attn_kernel, n_cache_blocks=n_cache_blocks, n_sel=n_sel),
        grid_spec=grid_spec,
        out_shape=jax.ShapeDtypeStruct((bsz, G, SUBLANES, hd), F32),
        compiler_params=_cparams("parallel", "parallel"),
    )(sel.reshape(-1), table.reshape(-1), *([pk] * n_sel), *([pv] * n_sel), q, new,
      kw_past.reshape(bsz, wb * G, hd), vw_past.reshape(bsz, wb * G, hd), o_cmp,
      gates.reshape(bsz, G, 1, LANES), b_gate)


def _gate_weights(w, col0, n):
    return _pad_lanes(w[:, col0:col0 + n])[None], 0


def _nsa_gate_weights(w, b_gate, col0):
    per = 3 * NSA_HPG
    k = w.shape[0]
    wg = w[:, col0:col0 + NSA_KV * per].reshape(k, NSA_KV, per)
    wg = jnp.pad(wg, ((0, 0), (0, 0), (0, LANES - per))).reshape(1, k, NSA_KV * LANES)
    bg = jnp.pad(b_gate.reshape(NSA_KV, per), ((0, 0), (0, LANES - per))).reshape(1, NSA_KV * LANES)
    return (wg, 0), bg


def kernel(x_prompt, x_sample, state_a_C, state_a_n, state_a_m, state_b_conv, cache_c_k_cmp, cache_c_v_cmp, cache_c_k_slc, cache_c_v_slc, cache_c_k_win, cache_c_v_win, page_table, norm_w, ffn_w_in, ffn_w_out, a_w_in, a_b_gates, a_head_norm, a_w_out, b_w_in, b_conv_w, b_w_out, c_w_in, c_b_gate, c_cmp_pos, c_cmp_w1, c_cmp_b1, c_cmp_w2, c_cmp_b2, c_w_out):
    bp, T, D = x_prompt.shape
    bs = x_sample.shape[0]
    assert x_sample.shape[1] == 1
    depth = norm_w.shape[0]
    G, hd = NSA_KV, NSA_HD
    past_len = page_table.shape[1] * cache_c_k_cmp.shape[2]
    xp = x_prompt.reshape(bp * T, D)
    xs = x_sample.reshape(bs, D)
    hp_in = norm_cast(xp, norm_w[0, 0])
    hs_in = norm_cast(xs, norm_w[0, 0])
    pa, sa, pb, sb, pc, sc = [], [], [], [], [], []
    for i in range(depth):
        kind, j = i % 3, i // 3
        nw = norm_w[i]
        if kind == 0:
            hdv = a_head_norm.shape[1]
            n_main = 3 * hdv
            wg = _gate_weights(a_w_in[j], n_main, 2 * MLSTM_HEADS)
            qkvo_p, qkvo_s = proj_plain(hp_in, hs_in, (a_w_in, j), 0, n_main, 1024)
            g_p, g_s = proj_plain(hp_in, hs_in, wg, 0, LANES, LANES)
            hp, c_p, n_p, m_p = mlstm_prompt(qkvo_p, g_p, a_b_gates[j], a_head_norm[j], bp, T)
            hs, c_s, n_s, m_s = mlstm_step(qkvo_s, g_s, a_b_gates[j], a_head_norm[j],
                                           state_a_C[j], state_a_n[j], state_a_m[j])
            pa.append((c_p, n_p, m_p[:, :, 0]))
            sa.append((c_s, n_s, m_s))
            w_out = a_w_out[j].astype(BF16)
        elif kind == 1:
            (bg_p, z_p), (bg_s, z_s) = proj_conv_in(hp_in, hs_in, (b_w_in, j), D)
            hp = conv_gate(bg_p, z_p, b_conv_w[j], T)
            buf = state_b_conv[j]
            hs = conv_gate_step(bg_s, z_s, buf.transpose(1, 0, 2), b_conv_w[j]).astype(BF16)
            pb.append(z_p.reshape(bp, T, D)[:, T - (CONV_W - 1):])
            sb.append(jnp.concatenate([buf, z_s[:, None]], axis=1)[:, -(CONV_W - 1):])
            w_out = b_w_out[j].astype(BF16)
        else:
            nq_cols = NSA_HPG * G * hd
            nheads = G * NSA_HPG
            wg, bgate = _nsa_gate_weights(c_w_in[j], c_b_gate[j], nq_cols + 6 * G * hd)
            cw = _cmp_weights(c_cmp_pos[j], c_cmp_w1[j], c_cmp_b1[j], c_cmp_w2[j], c_cmp_b2[j])
            q_p, q_s = proj_plain(hp_in, hs_in, (c_w_in, j), 0, nq_cols, 1024)
            kv_p, kv_s = proj_groups(hp_in, hs_in, (c_w_in, j), nq_cols, 6, G * hd, 256)
            gt_p, gt_s = proj_plain(hp_in, hs_in, wg, 0, G * LANES, G * LANES)
            ck, cv = cmp_prompt(kv_p[0], kv_p[1], bp, T, cw)
            hp = nsa_attn_prompt(q_p, ck, cv, kv_p[2], kv_p[3], kv_p[4], kv_p[5], gt_p, bgate, bp, T)
            kv_p = [a.reshape(bp, T, G, hd) for a in kv_p]
            keep = min(WINDOW, T)
            pc.append(tuple(kv_p[:4]) + tuple(a[:, T - keep:] for a in kv_p[4:]))
            q_s = jnp.pad(q_s.reshape(bs, G, NSA_HPG, hd), ((0, 0), (0, 0), (0, SUBLANES - NSA_HPG), (0, 0)))
            kv_s = jnp.stack([a.reshape(bs, G, hd) for a in kv_s], axis=1)
            new = jnp.pad(kv_s[:, 2:6].transpose(0, 2, 1, 3), ((0, 0), (0, 0), (0, SUBLANES - 4), (0, 0)))
            ak, av = page_cmp(cache_c_k_cmp[j], cache_c_v_cmp[j], page_table, cw[0])
            o_cmp, ranked = dec_cmp(ak, av, q_s, cw, past_len)
            n_sel = min(N_SELECT, -(-(past_len + 1) // SEL_BLOCK))
            o_s = dec_attn(ranked[:, :, :n_sel], page_table, cache_c_k_slc[j], cache_c_v_slc[j], q_s, new,
                           cache_c_k_win[j], cache_c_v_win[j], o_cmp, gt_s, bgate)
            hs = o_s[:, :, :NSA_HPG].reshape(bs, nheads * hd).astype(BF16)
            keep_s = min(WINDOW, cache_c_k_win.shape[2] + 1)
            win = [jnp.concatenate([c[j], kv_s[:, a][:, None]], axis=1)[:, -keep_s:]
                   for c, a in ((cache_c_k_win, 4), (cache_c_v_win, 5))]
            sc.append(tuple(kv_s[:, a][:, None] for a in range(4)) + tuple(win))
            w_out = c_w_out[j].astype(BF16)
        last = i == depth - 1
        nw_next = norm_w[i + 1, 0] if not last else nw[0]
        f_out = ffn_w_out[i].astype(BF16)
        xp, hp_mid = proj_out(hp, w_out, xp, nw[1], nw[2])
        xs, hs_mid = proj_out(hs, w_out, xs, nw[1], nw[2])
        act_p, act_s = proj_swiglu(hp_mid, hs_mid, (ffn_w_in, i), f_out.shape[0])
        xp, hp_in = proj_out(act_p, f_out, xp, nw[3], nw_next, not last)
        xs, hs_in = proj_out(act_s, f_out, xs, nw[3], nw_next, not last)

    stack = lambda items: [jnp.stack(a) for a in zip(*items)]
    p_a, s_a = stack(pa), stack(sa)
    p_c, s_c = stack(pc), stack(sc)
    return (xp.reshape(bp, T, D), xs.reshape(bs, 1, D), *p_a, *s_a, jnp.stack(pb), jnp.stack(sb), *p_c, *s_c)
```

```python
import functools
import math

import jax
import jax.numpy as jnp
from jax import lax
from jax.experimental import pallas as pl
from jax.experimental.pallas import tpu as pltpu

F32 = jnp.float32
BF16 = jnp.bfloat16

EPS = 1e-6
NEG = -1e30
BIG = 1e30

LANES = 128
SUBLANES = 8
VMEM_LIMIT = 56 * 1024 * 1024

MLSTM_HEADS = 8
MLSTM_CHUNK = 256
FORGET_BIAS_COL = MLSTM_HEADS
CONV_W = 3
NSA_HD = 128
NSA_KV = 4
NSA_HPG = 4
CMP_BLOCK = 32
CMP_STRIDE = 16
SEL_BLOCK = 64
N_SELECT = 16
WINDOW = 512
ATT_TILE = 256
PAGES_PER_STEP = 8


def _cparams(*sem):
    return pltpu.CompilerParams(dimension_semantics=sem, vmem_limit_bytes=VMEM_LIMIT)


def _dot(a, b):
    return jnp.dot(a, b, preferred_element_type=F32)


def _dot_nt(a, b):
    return lax.dot_general(a, b, (((1,), (1,)), ((), ())), preferred_element_type=F32)


def _dot_tn(a, b):
    return lax.dot_general(a, b, (((0,), (0,)), ((), ())), preferred_element_type=F32)


def _split_bf16(x):
    hi = x.astype(BF16)
    lo = (x - hi.astype(F32)).astype(BF16)
    return hi, lo


def _iota(shape, dim):
    return lax.broadcasted_iota(jnp.int32, shape, dim)


def _sigmoid(x):
    return 1.0 / (1.0 + jnp.exp(-x))


def _rms(x):
    return x * lax.rsqrt(jnp.mean(x * x, axis=-1, keepdims=True) + EPS)


def _row_tile(m, want):
    t = min(m, want)
    assert m % t == 0, (m, t)
    return t


def _norm_kernel(x_ref, w_ref, o_ref):
    o_ref[...] = (_rms(x_ref[...]) * w_ref[...]).astype(BF16)


def norm_cast(x, w):
    m, d = x.shape
    tm = _row_tile(m, 512)
    return pl.pallas_call(
        _norm_kernel,
        grid=(m // tm,),
        in_specs=[pl.BlockSpec((tm, d), lambda i: (i, 0)), pl.BlockSpec((1, d), lambda i: (0, 0))],
        out_specs=pl.BlockSpec((tm, d), lambda i: (i, 0)),
        out_shape=jax.ShapeDtypeStruct((m, d), BF16),
        compiler_params=_cparams("parallel"),
    )(x, w.reshape(1, d))


def _proj_kernel(x_ref, xs_ref, *refs, n_groups, n_out, epilogue, transposed):
    w_refs = refs[:n_groups]
    out_refs = refs[n_groups:n_groups + n_out]
    sample_out_refs = refs[n_groups + n_out:n_groups + 2 * n_out]
    wb_ref = refs[n_groups + 2 * n_out]
    mm = _dot_nt if transposed else _dot

    @pl.when(pl.program_id(1) == 0)
    def _():
        for g in range(n_groups):
            wb_ref[g] = w_refs[g][...].astype(BF16)
        xs = xs_ref[...]
        epilogue([mm(xs, wb_ref[g]) for g in range(n_groups)], sample_out_refs)

    x = x_ref[...]
    epilogue([mm(x, wb_ref[g]) for g in range(n_groups)], out_refs)


def _proj(x, xs, w, *, col0, tn, n_tiles, group_stride, n_groups, epilogue, outs, tm_want=1024, transposed=False):
    w, layer = w
    m, k = x.shape
    s = xs.shape[0]
    tm = _row_tile(m, tm_want)
    assert col0 % tn == 0
    b0 = col0 // tn
    if transposed:
        w_block = (None, tn, k)
        w_map = lambda j, i, off: (layer, off + j, 0)
    else:
        w_block = (None, k, tn)
        w_map = lambda j, i, off: (layer, 0, off + j)
    w_specs = [pl.BlockSpec(w_block, functools.partial(w_map, off=b0 + g * group_stride)) for g in range(n_groups)]
    n_out = len(outs)
    res = pl.pallas_call(
        functools.partial(_proj_kernel, n_groups=n_groups, n_out=n_out, epilogue=epilogue, transposed=transposed),
        grid=(n_tiles, m // tm),
        in_specs=[pl.BlockSpec((tm, k), lambda j, i: (i, 0)), pl.BlockSpec((s, k), lambda j, i: (0, 0))] + w_specs,
        out_specs=[pl.BlockSpec((tm, tn), lambda j, i: (i, j))] * n_out
        + [pl.BlockSpec((s, tn), lambda j, i: (0, j))] * n_out,
        out_shape=[jax.ShapeDtypeStruct((m, n), dt) for n, dt in outs]
        + [jax.ShapeDtypeStruct((s, n), dt) for n, dt in outs],
        scratch_shapes=[pltpu.VMEM((n_groups,) + w_block[1:], BF16)],
        compiler_params=_cparams("arbitrary", "arbitrary"),
    )(x, xs, *([w] * n_groups))
    return res[:n_out], res[n_out:]


def _ep_plain(accs, outs):
    for acc, o in zip(accs, outs):
        o[...] = acc


def proj_plain(x, xs, w, col0, n_cols, tn, transposed=False):
    p, s = _proj(x, xs, w, col0=col0, tn=tn, n_tiles=n_cols // tn, group_stride=0, n_groups=1,
                 epilogue=_ep_plain, outs=[(n_cols, F32)], transposed=transposed)
    return p[0], s[0]


def proj_groups(x, xs, w, col0, n_groups, group_cols, tn, transposed=False):
    return _proj(x, xs, w, col0=col0, tn=tn, n_tiles=group_cols // tn, group_stride=group_cols // tn,
                 n_groups=n_groups, epilogue=_ep_plain, outs=[(group_cols, F32)] * n_groups, tm_want=512,
                 transposed=transposed)


def _ep_swiglu(accs, outs):
    g, u = accs
    outs[0][...] = (g * _sigmoid(g) * u).astype(BF16)


def proj_swiglu(x, xs, w, ff):
    tn = 512
    p, s = _proj(x, xs, w, col0=0, tn=tn, n_tiles=ff // tn, group_stride=ff // tn, n_groups=2,
                 epilogue=_ep_swiglu, outs=[(ff, BF16)])
    return p[0], s[0]


def _ep_conv_in(accs, outs):
    bg, cg, u = accs
    outs[0][...] = bg
    outs[1][...] = cg * u


def proj_conv_in(x, xs, w, d):
    tn = 512
    return _proj(x, xs, w, col0=0, tn=tn, n_tiles=d // tn, group_stride=d // tn, n_groups=3,
                 epilogue=_ep_conv_in, outs=[(d, F32), (d, F32)])


def _out_kernel(h_ref, hs_ref, w_ref, r_ref, rs_ref, nwa_ref, nwb_ref, *out_refs, emit_next):
    def finalize(y, r, x_ref, xn_ref):
        x_new = r + _rms(y) * nwa_ref[...]
        x_ref[...] = x_new
        if emit_next:
            xn_ref[...] = (_rms(x_new) * nwb_ref[...]).astype(BF16)

    if emit_next:
        x_ref, xn_ref, xs_ref, xsn_ref = out_refs
    else:
        (x_ref, xs_ref), xn_ref, xsn_ref = out_refs, None, None

    @pl.when(pl.program_id(0) == 0)
    def _():
        finalize(_dot(hs_ref[...], w_ref[...]), rs_ref[...], xs_ref, xsn_ref)

    finalize(_dot(h_ref[...], w_ref[...]), r_ref[...], x_ref, xn_ref)


def proj_out(h, hs, w, resid, resid_s, nw_post, nw_next, emit_next=True):
    w, layer = w
    m, k = h.shape
    s = hs.shape[0]
    d = w.shape[2]
    tm = _row_tile(m, 512 if k <= 2048 else 256)
    row = lambda i: (i, 0)
    fixed = lambda i: (0, 0)
    f32_out = [jax.ShapeDtypeStruct((m, d), F32), jax.ShapeDtypeStruct((s, d), F32)]
    bf16_out = [jax.ShapeDtypeStruct((m, d), BF16), jax.ShapeDtypeStruct((s, d), BF16)]
    specs = [pl.BlockSpec((tm, d), row), pl.BlockSpec((s, d), fixed)]
    if emit_next:
        out_shape = [f32_out[0], bf16_out[0], f32_out[1], bf16_out[1]]
        out_specs = [specs[0], specs[0], specs[1], specs[1]]
    else:
        out_shape, out_specs = f32_out, specs
    res = pl.pallas_call(
        functools.partial(_out_kernel, emit_next=emit_next),
        grid=(m // tm,),
        in_specs=[pl.BlockSpec((tm, k), row),
                  pl.BlockSpec((s, k), fixed),
                  pl.BlockSpec((None, k, d), lambda i: (layer, 0, 0), pipeline_mode=pl.Buffered(1)),
                  pl.BlockSpec((tm, d), row),
                  pl.BlockSpec((s, d), fixed),
                  pl.BlockSpec((1, d), fixed),
                  pl.BlockSpec((1, d), fixed)],
        out_specs=out_specs,
        out_shape=out_shape,
        compiler_params=_cparams("arbitrary"),
    )(h, hs, w, resid, resid_s, nw_post.reshape(1, d), nw_next.reshape(1, d))
    if emit_next:
        return (res[0], res[2]), (res[1], res[3])
    return (res[0], res[1]), (None, None)


def _log_sigmoid(x):
    return jnp.minimum(x, 0.0) - jnp.log1p(jnp.exp(-jnp.abs(x)))


def _mlstm_kernel(q_ref, k_ref, v_ref, o_ref, g_ref, bg_ref, hn_ref, hg_ref, c_ref, n_ref, m_ref, *, L, H, DK, DV):
    @pl.when(pl.program_id(1) == 0)
    def _():
        c_ref[...] = jnp.zeros_like(c_ref)
        n_ref[...] = jnp.zeros_like(n_ref)
        m_ref[...] = jnp.zeros_like(m_ref)

    gpre = g_ref[...] + bg_ref[...]
    lf = _log_sigmoid(gpre)
    rows = _iota((L, L), 0)
    cols = _iota((L, L), 1)
    causal = rows >= cols
    tril = jnp.where(causal, 1.0, 0.0).astype(BF16)
    lf_hi, lf_lo = _split_bf16(lf)
    bcum = _dot(tril, lf_hi) + _dot(tril, lf_lo)
    g_t = gpre.T
    b_t = bcum.T
    for h in range(H):
        f = FORGET_BIAS_COL + h
        a_col = bcum[:, f:f + 1]
        i_col = gpre[:, h:h + 1]
        b_row = b_t[f:f + 1, :]
        i_row = g_t[h:h + 1, :]
        m_prev = m_ref[0, h:h + 1, 0:1]
        log_d = jnp.where(causal, a_col - b_row + i_row, NEG)
        log_inter = a_col + m_prev
        m_t = jnp.maximum(log_inter, jnp.max(log_d, axis=-1, keepdims=True))
        d = jnp.exp(log_d - m_t)
        inter = jnp.exp(log_inter - m_t)
        qh = q_ref[:, h * DK:(h + 1) * DK]
        kh = k_ref[:, h * DK:(h + 1) * DK] * (DK ** -0.5)
        vb = v_ref[:, h * DV:(h + 1) * DV].astype(BF16)
        qb = qh.astype(BF16)
        s = _dot_nt(qb, kh.astype(BF16)) * d
        c_old = c_ref[0, h]
        n_old = n_ref[0, h:h + 1, :]
        num = _dot(s.astype(BF16), vb) + inter * _dot(qb, c_old.astype(BF16))
        den = jnp.sum(s, axis=-1, keepdims=True) + inter * jnp.sum(qh * n_old, axis=-1, keepdims=True)
        hh = _rms(num / jnp.maximum(jnp.abs(den), jnp.exp(-m_t)))
        gate = _sigmoid(o_ref[:, h * DV:(h + 1) * DV])
        hg_ref[:, h * DV:(h + 1) * DV] = (hh * hn_ref[:, h * DV:(h + 1) * DV] * gate).astype(BF16)
        m_new = m_t[L - 1:L, :]
        w_col = jnp.exp(a_col[L - 1:L, :] - a_col + i_col - m_new)
        decay = jnp.exp(log_inter[L - 1:L, :] - m_new)
        kw = kh * w_col
        c_ref[0, h] = decay * c_old + _dot_tn(kw.astype(BF16), vb)
        n_ref[0, h:h + 1, :] = decay * n_old + jnp.sum(kw, axis=0, keepdims=True)
        m_ref[0, h:h + 1, :] = jnp.broadcast_to(m_new, (1, LANES))


def mlstm_prompt(qkvo, gates, b_gates, head_norm, bsz, T):
    H = MLSTM_HEADS
    m = qkvo.shape[0]
    hdv = qkvo.shape[1] // 3
    hdk = hdv // 2
    DK, DV = hdk // H, hdv // H
    L = math.gcd(T, MLSTM_CHUNK)
    nc = T // L
    row = lambda b, c: (b * nc + c, 0)
    state = lambda b, c: (b, 0, 0)
    return pl.pallas_call(
        functools.partial(_mlstm_kernel, L=L, H=H, DK=DK, DV=DV),
        grid=(bsz, nc),
        in_specs=[pl.BlockSpec((L, hdk), row),
                  pl.BlockSpec((L, hdk), lambda b, c: (b * nc + c, 1)),
                  pl.BlockSpec((L, hdv), lambda b, c: (b * nc + c, 1)),
                  pl.BlockSpec((L, hdv), lambda b, c: (b * nc + c, 2)),
                  pl.BlockSpec((L, LANES), row),
                  pl.BlockSpec((1, LANES), lambda b, c: (0, 0)),
                  pl.BlockSpec((1, hdv), lambda b, c: (0, 0))],
        out_specs=[pl.BlockSpec((L, hdv), row),
                   pl.BlockSpec((1, H, DK, DV), lambda b, c: (b, 0, 0, 0)),
                   pl.BlockSpec((1, H, DK), state),
                   pl.BlockSpec((1, H, LANES), state)],
        out_shape=[jax.ShapeDtypeStruct((m, hdv), BF16),
                   jax.ShapeDtypeStruct((bsz, H, DK, DV), F32),
                   jax.ShapeDtypeStruct((bsz, H, DK), F32),
                   jax.ShapeDtypeStruct((bsz, H, LANES), F32)],
        compiler_params=_cparams("parallel", "arbitrary"),
    )(qkvo, qkvo, qkvo, qkvo, gates, _pad_lanes(b_gates.reshape(1, -1)), head_norm.reshape(1, hdv))


def _pad_lanes(x, width=LANES):
    return jnp.pad(x, ((0, 0), (0, width - x.shape[-1])))


def _to_col(row, n):
    eye = _iota((n, n), 0) == _iota((n, n), 1)
    return jnp.sum(jnp.where(eye, jnp.broadcast_to(row, (n, n)), 0.0), axis=-1, keepdims=True)


def _mlstm_step_kernel(x_ref, g_ref, bg_ref, hn_ref, c0_ref, n0_ref, m0_ref, hg_ref, c_ref, n_ref, m_ref,
                       *, H, DK, DV):
    x = x_ref[0]
    gpre = g_ref[0] + bg_ref[...]
    lf = _log_sigmoid(gpre)
    hdk = H * DK
    hdv = H * DV
    for h in range(H):
        f = FORGET_BIAS_COL + h
        ig = gpre[:, h:h + 1]
        m_prev = m0_ref[0, h:h + 1, :]
        log_inter = lf[:, f:f + 1] + m_prev
        m_t = jnp.maximum(log_inter, ig)
        d = jnp.exp(ig - m_t)
        inter = jnp.exp(log_inter - m_t)
        q = x[:, h * DK:(h + 1) * DK]
        k = x[:, hdk + h * DK:hdk + (h + 1) * DK] * (DK ** -0.5)
        v = x[:, 2 * hdk + h * DV:2 * hdk + (h + 1) * DV]
        og = x[:, 2 * hdk + hdv + h * DV:2 * hdk + hdv + (h + 1) * DV]
        c_old = c0_ref[0, h]
        n_old = n0_ref[0, h:h + 1, :]
        s = jnp.sum(q * k, axis=-1, keepdims=True) * d
        q_col = _to_col(q, DK)
        k_col = _to_col(k, DK)
        num = s * v + inter * jnp.sum(q_col * c_old, axis=0, keepdims=True)
        den = s + inter * jnp.sum(q * n_old, axis=-1, keepdims=True)
        hh = _rms(num / jnp.maximum(jnp.abs(den), jnp.exp(-m_t)))
        hg_ref[0, :, h * DV:(h + 1) * DV] = (hh * hn_ref[:, h * DV:(h + 1) * DV] * _sigmoid(og)).astype(BF16)
        c_ref[0, h] = inter * c_old + d * (k_col * v)
        n_ref[0, h:h + 1, :] = inter * n_old + d * k
        m_ref[0, h:h + 1, :] = m_t


def mlstm_step(qkvo, gates, b_gates, head_norm, c0, n0, m0):
    bsz, H, DK, DV = c0.shape
    wid = qkvo.shape[1]
    hdv = H * DV
    one = lambda b: (b, 0, 0)
    hg, c1, n1, m1 = pl.pallas_call(
        functools.partial(_mlstm_step_kernel, H=H, DK=DK, DV=DV),
        grid=(bsz,),
        in_specs=[pl.BlockSpec((1, 1, wid), one),
                  pl.BlockSpec((1, 1, LANES), one),
                  pl.BlockSpec((1, LANES), lambda b: (0, 0)),
                  pl.BlockSpec((1, hdv), lambda b: (0, 0)),
                  pl.BlockSpec((1, H, DK, DV), lambda b: (b, 0, 0, 0)),
                  pl.BlockSpec((1, H, DK), one),
                  pl.BlockSpec((1, H, 1), one)],
        out_specs=[pl.BlockSpec((1, 1, hdv), one),
                   pl.BlockSpec((1, H, DK, DV), lambda b: (b, 0, 0, 0)),
                   pl.BlockSpec((1, H, DK), one),
                   pl.BlockSpec((1, H, 1), one)],
        out_shape=[jax.ShapeDtypeStruct((bsz, 1, hdv), BF16),
                   jax.ShapeDtypeStruct((bsz, H, DK, DV), F32),
                   jax.ShapeDtypeStruct((bsz, H, DK), F32),
                   jax.ShapeDtypeStruct((bsz, H, 1), F32)],
        compiler_params=_cparams("parallel"),
    )(qkvo.reshape(bsz, 1, wid), gates.reshape(bsz, 1, LANES), _pad_lanes(b_gates.reshape(1, -1)),
      head_norm.reshape(1, hdv), c0, n0, m0.reshape(bsz, H, 1))
    return hg.reshape(bsz, hdv), c1, n1, m1.reshape(bsz, H)


def _conv_kernel(bg_ref, z_ref, zp_ref, cw_ref, o_ref, *, tm, T):
    i = pl.program_id(0)
    z = z_ref[...]
    zc = jnp.concatenate([zp_ref[...], z], axis=0)
    tpos = (i * tm + _iota((tm, 1), 0)) % T
    y = cw_ref[CONV_W - 1:CONV_W, :] * z
    for back in range(1, CONV_W):
        zb = zc[SUBLANES - back:SUBLANES - back + tm, :]
        y = y + cw_ref[CONV_W - 1 - back:CONV_W - back, :] * jnp.where(tpos >= back, zb, 0.0)
    o_ref[...] = (bg_ref[...] * y).astype(BF16)


def conv_gate(bg, z, conv_w, T):
    m, d = z.shape
    tm = _row_tile(T, 256)
    per = tm // SUBLANES
    return pl.pallas_call(
        functools.partial(_conv_kernel, tm=tm, T=T),
        grid=(m // tm,),
        in_specs=[pl.BlockSpec((tm, d), lambda i: (i, 0)),
                  pl.BlockSpec((tm, d), lambda i: (i, 0)),
                  pl.BlockSpec((SUBLANES, d), lambda i: (jnp.maximum(i * per - 1, 0), 0)),
                  pl.BlockSpec((CONV_W, d), lambda i: (0, 0))],
        out_specs=pl.BlockSpec((tm, d), lambda i: (i, 0)),
        out_shape=jax.ShapeDtypeStruct((m, d), BF16),
        compiler_params=_cparams("parallel"),
    )(bg, z, z, conv_w)


def _conv_step_kernel(bg_ref, z_ref, buf_ref, cw_ref, o_ref):
    y = cw_ref[CONV_W - 1:CONV_W, :] * z_ref[...]
    for j in range(CONV_W - 1):
        y = y + cw_ref[j:j + 1, :] * buf_ref[j]
    o_ref[...] = bg_ref[...] * y


def conv_gate_step(bg, z, buf, conv_w):
    bsz, d = z.shape
    full = lambda: (0, 0)
    return pl.pallas_call(
        _conv_step_kernel,
        in_specs=[pl.BlockSpec((bsz, d), full), pl.BlockSpec((bsz, d), full),
                  pl.BlockSpec((CONV_W - 1, bsz, d), lambda: (0, 0, 0)), pl.BlockSpec((CONV_W, d), full)],
        out_specs=pl.BlockSpec((bsz, d), full),
        out_shape=jax.ShapeDtypeStruct((bsz, d), F32),
    )(bg, z, buf, conv_w)


def _gelu(x):
    return 0.5 * x * (1.0 + jnp.tanh(math.sqrt(2.0 / math.pi) * (x + 0.044715 * x * x * x)))


def _cmp_finish(a, posw, b1, w2, b2):
    hid = a.shape[1] // 2
    a1 = a[:, hid:]
    a1_next = jnp.concatenate([a1[1:], a1[:1]], axis=0)
    pre = a[:, :hid] + a1_next + b1 + posw[0:1, :hid] + posw[1:2, hid:]
    return _dot(_gelu(pre).astype(BF16), w2) + b2


def _cmp_prompt_kernel(k_ref, v_ref, w1_ref, pos_ref, b1_ref, w2_ref, b2_ref, ck_ref, cv_ref, *, nch):
    for idx, (x_ref, o_ref) in enumerate(((k_ref, ck_ref), (v_ref, cv_ref))):
        w1 = w1_ref[idx]
        lhs = jnp.concatenate([x_ref[pl.ds(p, nch, stride=CMP_STRIDE), :] for p in range(CMP_STRIDE)], axis=1)
        a = _dot(lhs.astype(BF16), w1)
        posw = _dot(pos_ref[idx], w1)
        o_ref[0, 0] = _cmp_finish(a, posw, b1_ref[idx], w2_ref[idx], b2_ref[idx])


def _cmp_weights(cmp_pos, cmp_w1, cmp_b1, cmp_w2, cmp_b2):
    two, ratio, stride, hd, hid = cmp_w1.shape
    w1 = cmp_w1.transpose(0, 2, 3, 1, 4).reshape(two, stride * hd, ratio * hid).astype(BF16)
    pos = cmp_pos.reshape(two, ratio, stride * hd)
    pos = jnp.pad(pos, ((0, 0), (0, SUBLANES - ratio), (0, 0))).astype(BF16)
    return w1, pos, cmp_b1.reshape(two, 1, hid), cmp_w2.astype(BF16), cmp_b2.reshape(two, 1, hd)


def cmp_prompt(kc, vc, bsz, T, cw):
    w1, pos, b1, w2, b2 = cw
    nch = T // CMP_STRIDE
    G = NSA_KV
    out = jax.ShapeDtypeStruct((bsz, G, nch, NSA_HD), F32)
    ospec = pl.BlockSpec((1, 1, nch, NSA_HD), lambda b, g: (b, g, 0, 0))
    whole = lambda a: pl.BlockSpec(a.shape, lambda b, g: (0,) * a.ndim)
    seq = pl.BlockSpec((T, NSA_HD), lambda b, g: (b, g))
    return pl.pallas_call(
        functools.partial(_cmp_prompt_kernel, nch=nch),
        grid=(bsz, G),
        in_specs=[seq, seq, whole(w1), whole(pos), whole(b1), whole(w2), whole(b2)],
        out_specs=[ospec, ospec],
        out_shape=[out, out],
        compiler_params=_cparams("parallel", "parallel"),
    )(kc, vc, w1, pos, b1, w2, b2)


def _overlap_t(nbs_pad, nbc_pad):
    j = _iota((nbs_pad, nbc_pad), 0)
    n = _iota((nbs_pad, nbc_pad), 1)
    lo = jnp.maximum(n * CMP_STRIDE, j * SEL_BLOCK)
    hi = jnp.minimum(n * CMP_STRIDE + CMP_BLOCK, j * SEL_BLOCK + SEL_BLOCK)
    return (jnp.maximum(hi - lo, 0) // CMP_STRIDE).astype(F32)


def _rank_select(score, n_sel, axis):
    n = score.shape[axis]
    idx = _iota(score.shape, axis)
    cnt = jnp.zeros(score.shape, F32)
    for jp in range(n):
        other = lax.slice_in_dim(score, jp, jp + 1, axis=axis)
        tie = jnp.where(idx > jp, 1.0, 0.0)
        cnt = cnt + jnp.where(other > score, 1.0, jnp.where(other == score, tie, 0.0))
    return jnp.where(cnt < n_sel, 1.0, 0.0)


def _attn_first(state, s, vt_b):
    m_ref, l_ref, acc_ref = state
    m = jnp.max(s, axis=0, keepdims=True)
    p = jnp.exp(s - m)
    m_ref[...] = m
    l_ref[...] = jnp.sum(p, axis=0, keepdims=True)
    acc_ref[...] = _dot(vt_b, p.astype(BF16))


def _attn_more(state, s, vt_b):
    m_ref, l_ref, acc_ref = state
    m_old = m_ref[...]
    m_new = jnp.maximum(m_old, jnp.max(s, axis=0, keepdims=True))
    alpha = jnp.exp(m_old - m_new)
    p = jnp.exp(s - m_new)
    m_ref[...] = m_new
    l_ref[...] = alpha * l_ref[...] + jnp.sum(p, axis=0, keepdims=True)
    acc_ref[...] = alpha * acc_ref[...] + _dot(vt_b, p.astype(BF16))


def _nsa_attn_kernel(q_ref, ck_ref, cv_ref, ks_ref, vs_ref, kw_ref, vw_ref, g_ref, bg_ref, o_ref,
                     ksb_ref, vst_ref, kwb_ref, vwt_ref, ms_ref, ls_ref, as_ref, mw_ref, lw_ref, aw_ref,
                     *, tq, hpg, nbc, nbs, nbs_pad):
    qi = pl.program_id(2)
    t0 = qi * tq
    cols = hpg * tq
    tk = tq
    n_kt = ks_ref.shape[0] // tk
    nbc_pad = ck_ref.shape[2]

    @pl.when(qi == 0)
    def _():
        ksb_ref[...] = ks_ref[...].astype(BF16)
        kwb_ref[...] = kw_ref[...].astype(BF16)
        for kt in range(n_kt):
            vst_ref[kt] = vs_ref[kt * tk:(kt + 1) * tk, :].T.astype(BF16)
            vwt_ref[kt] = vw_ref[kt * tk:(kt + 1) * tk, :].T.astype(BF16)

    qb = (jnp.concatenate([q_ref[:, h * NSA_HD:(h + 1) * NSA_HD] for h in range(hpg)], axis=0)
          * (NSA_HD ** -0.5)).astype(BF16)

    n = _iota((nbc_pad, cols), 0)
    t_col = t0 + jnp.concatenate([_iota((nbc_pad, tq), 1)] * hpg, axis=1)
    cmask = jnp.where(n < nbc, jnp.where(n * CMP_STRIDE + (CMP_BLOCK - 1) <= t_col, 1.0, 0.0), 0.0)
    sc = jnp.where(cmask > 0.5, _dot_nt(ck_ref[0, 0].astype(BF16), qb), NEG)
    pe = jnp.exp(sc - jnp.max(sc, axis=0, keepdims=True))
    p_cmp = pe / jnp.sum(pe, axis=0, keepdims=True) * cmask
    o_cmp = _dot(cv_ref[0, 0].T.astype(BF16), p_cmp.astype(BF16))

    p_sum = p_cmp[:, 0:tq]
    for h in range(1, hpg):
        p_sum = p_sum + p_cmp[:, h * tq:(h + 1) * tq]
    ov_t = _overlap_t(nbs_pad, nbc_pad).astype(BF16)
    p_hi, p_lo = _split_bf16(p_sum)
    imp_t = _dot(ov_t, p_hi) + _dot(ov_t, p_lo)
    j = _iota((nbs_pad, tq), 0)
    t = t0 + _iota((nbs_pad, tq), 1)
    t_blk = t // SEL_BLOCK
    forced = jnp.where(j == 0, 1.0, jnp.where(j == t_blk, 1.0, jnp.where(j == t_blk - 1, 1.0, 0.0)))
    visible = jnp.where(j < nbs, jnp.where(j * SEL_BLOCK <= t, 1.0, 0.0), 0.0)
    score = jnp.where(visible > 0.5, jnp.where(forced > 0.5, BIG, imp_t), NEG)
    sel_t = _rank_select(score, min(N_SELECT, nbs), axis=0).astype(BF16)

    k_ofs = _iota((tk, tq), 0)
    q_ofs = _iota((tk, tq), 1)
    causal = k_ofs <= q_ofs
    per_head = lambda bias: jnp.concatenate([bias] * hpg, axis=1)

    def scores(kb_ref, kt):
        k_b = kb_ref[pl.ds(pl.multiple_of(kt * tk, tk), tk), :]
        return _dot_nt(k_b, qb)

    def slc_bias(kt):
        blk = (kt * tk + _iota((tk, nbs_pad), 0)) // SEL_BLOCK
        expand = jnp.where(blk == _iota((tk, nbs_pad), 1), 1.0, 0.0).astype(BF16)
        return (_dot(expand, sel_t) - 1.0) * BIG

    slc = (ms_ref, ls_ref, as_ref)
    _attn_first(slc, scores(ksb_ref, qi) + per_head(jnp.where(causal, slc_bias(qi), NEG)), vst_ref[qi])

    def slc_body(kt, _):
        _attn_more(slc, scores(ksb_ref, kt) + per_head(slc_bias(kt)), vst_ref[kt])
        return 0

    lax.fori_loop(0, qi, slc_body, 0)

    n_back = WINDOW // tk
    win = (mw_ref, lw_ref, aw_ref)
    _attn_first(win, scores(kwb_ref, qi) + per_head(jnp.where(causal, 0.0, NEG)), vwt_ref[qi])
    for back in range(1, n_back + 1):
        @pl.when(qi >= back)
        def _():
            s = scores(kwb_ref, qi - back)
            if back == n_back:
                s = s + per_head(jnp.where(k_ofs >= q_ofs, 0.0, NEG))
            _attn_more(win, s, vwt_ref[qi - back])

    o_slc = as_ref[...] / ls_ref[...]
    o_win = aw_ref[...] / lw_ref[...]
    gate_t = _sigmoid(g_ref[...] + bg_ref[...]).T
    for h in range(hpg):
        sl = slice(h * tq, (h + 1) * tq)
        o_t = (gate_t[3 * h:3 * h + 1, :] * o_cmp[:, sl] + gate_t[3 * h + 1:3 * h + 2, :] * o_slc[:, sl]
               + gate_t[3 * h + 2:3 * h + 3, :] * o_win[:, sl])
        o_ref[:, h * NSA_HD:(h + 1) * NSA_HD] = o_t.T.astype(BF16)


def nsa_attn_prompt(q, ck, cv, ks, vs, kw, vw, gates, b_gate, bsz, T):
    G, HPG = NSA_KV, NSA_HPG
    tq = math.gcd(T, ATT_TILE)
    assert WINDOW % tq == 0
    nq = T // tq
    nch = T // CMP_STRIDE
    nbc = nch - CMP_BLOCK // CMP_STRIDE + 1
    nbs = -(-T // SEL_BLOCK)
    nbs_pad = -(-nbs // SUBLANES) * SUBLANES
    cols = HPG * tq
    kv_scratch = [pltpu.VMEM((T, NSA_HD), BF16), pltpu.VMEM((nq, NSA_HD, tq), BF16)] * 2
    stats = [pltpu.VMEM((1, cols), F32), pltpu.VMEM((1, cols), F32), pltpu.VMEM((NSA_HD, cols), F32)] * 2
    seq = pl.BlockSpec((T, NSA_HD), lambda b, g, i: (b, g))
    cspec = pl.BlockSpec((1, 1, nch, NSA_HD), lambda b, g, i: (b, g, 0, 0))
    qspec = pl.BlockSpec((tq, HPG * NSA_HD), lambda b, g, i: (b * nq + i, g))
    return pl.pallas_call(
        functools.partial(_nsa_attn_kernel, tq=tq, hpg=HPG, nbc=nbc, nbs=nbs, nbs_pad=nbs_pad),
        grid=(bsz, G, nq),
        in_specs=[qspec, cspec, cspec, seq, seq, seq, seq,
                  pl.BlockSpec((tq, LANES), lambda b, g, i: (b * nq + i, g)),
                  pl.BlockSpec((1, LANES), lambda b, g, i: (0, g))],
        out_specs=qspec,
        out_shape=jax.ShapeDtypeStruct((bsz * T, G * HPG * NSA_HD), BF16),
        scratch_shapes=kv_scratch + stats,
        compiler_params=_cparams("arbitrary", "arbitrary", "arbitrary"),
    )(q, ck, cv, ks, vs, kw, vw, gates, b_gate)


def _page_cmp_kernel(tbl_ref, *refs, P):
    k_pages, v_pages = refs[:P], refs[P:2 * P]
    w1_ref, ak_ref, av_ref = refs[2 * P:]
    rows = _iota((LANES, LANES), 0)
    cols = _iota((LANES, LANES), 1)
    cpp = LANES // CMP_STRIDE
    perm = jnp.where(cols == (rows % cpp) * CMP_STRIDE + rows // cpp, 1.0, 0.0).astype(BF16)
    for idx, (pages, a_ref) in enumerate(((k_pages, ak_ref), (v_pages, av_ref))):
        blocks = []
        for g in range(NSA_KV):
            for pg in pages:
                x = _dot(perm, pg[0, pl.ds(g, LANES, stride=NSA_KV), :].astype(BF16))
                blocks.append(jnp.concatenate([x[p * cpp:(p + 1) * cpp, :] for p in range(CMP_STRIDE)], axis=1))
        a = _dot(jnp.concatenate(blocks, axis=0).astype(BF16), w1_ref[idx])
        per_g = P * cpp
        for g in range(NSA_KV):
            a_ref[0, g] = a[g * per_g:(g + 1) * per_g]


def page_cmp(pool_k, pool_v, table, w1):
    n_pool, page, G, hd = pool_k.shape
    assert page == LANES and G == NSA_KV and hd == NSA_HD
    bsz, n_pages = table.shape
    P = math.gcd(n_pages, PAGES_PER_STEP)
    cpp = page // CMP_STRIDE
    hid2 = w1.shape[2]
    pk = pool_k.reshape(n_pool, page * G, hd)
    pv = pool_v.reshape(n_pool, page * G, hd)
    pspec = lambda i: pl.BlockSpec((1, page * G, hd),
                                   functools.partial(lambda b, s, tbl, i: (tbl[b * n_pages + s * P + i], 0, 0), i=i))
    out = jax.ShapeDtypeStruct((bsz, G, n_pages * cpp, hid2), F32)
    ospec = pl.BlockSpec((1, G, P * cpp, hid2), lambda b, s, tbl: (b, 0, s, 0))
    grid_spec = pltpu.PrefetchScalarGridSpec(
        num_scalar_prefetch=1,
        grid=(bsz, n_pages // P),
        in_specs=[pspec(i) for i in range(P)] * 2 + [pl.BlockSpec(w1.shape, lambda b, s, tbl: (0, 0, 0))],
        out_specs=[ospec, ospec],
    )
    return pl.pallas_call(
        functools.partial(_page_cmp_kernel, P=P),
        grid_spec=grid_spec,
        out_shape=[out, out],
        compiler_params=_cparams("parallel", "parallel"),
    )(table.reshape(-1), *([pk] * P), *([pv] * P), w1)


def _dec_cmp_kernel(ak_ref, av_ref, q_ref, pos_ref, w1_ref, b1_ref, w2_ref, b2_ref, o_ref, sel_ref,
                    *, nbc, nbs, nbs_pad, p0):
    nch = ak_ref.shape[2]
    for g in range(NSA_KV):
        cks = []
        for idx, a_ref in enumerate((ak_ref, av_ref)):
            posw = _dot(pos_ref[idx], w1_ref[idx])
            cks.append(_cmp_finish(a_ref[0, g], posw, b1_ref[idx], w2_ref[idx], b2_ref[idx]))
        ck, cv = cks
        qb = (q_ref[0, g] * (NSA_HD ** -0.5)).astype(BF16)
        n = _iota((SUBLANES, nch), 1)
        cmask = jnp.where(n < nbc, jnp.where(n * CMP_STRIDE + (CMP_BLOCK - 1) <= p0, 1.0, 0.0), 0.0)
        sc = jnp.where(cmask > 0.5, _dot_nt(qb, ck.astype(BF16)), NEG)
        pe = jnp.exp(sc - jnp.max(sc, axis=-1, keepdims=True))
        p_cmp = pe / jnp.sum(pe, axis=-1, keepdims=True) * cmask
        o_ref[0, g] = _dot(p_cmp.astype(BF16), cv.astype(BF16))

        head = jnp.where(_iota((SUBLANES, nch), 0) < NSA_HPG, p_cmp, 0.0)
        p_sum = jnp.broadcast_to(jnp.sum(head, axis=0, keepdims=True), (SUBLANES, nch))
        ov_t = _overlap_t(nbs_pad, nch).astype(BF16)
        p_hi, p_lo = _split_bf16(p_sum)
        imp = _dot_nt(p_hi, ov_t) + _dot_nt(p_lo, ov_t)
        j = _iota((SUBLANES, nbs_pad), 1)
        t_blk = p0 // SEL_BLOCK
        forced = jnp.where(j == 0, 1.0, jnp.where(j == t_blk, 1.0, jnp.where(j == t_blk - 1, 1.0, 0.0)))
        visible = jnp.where(j < nbs, jnp.where(j * SEL_BLOCK <= p0, 1.0, 0.0), 0.0)
        score = jnp.where(visible > 0.5, jnp.where(forced > 0.5, BIG, imp), NEG)
        s_row = jnp.broadcast_to(score[0:1], (nbs_pad, nbs_pad))
        s_col = jnp.concatenate([jnp.broadcast_to(score[0:1], (LANES, nbs_pad)).T] * (nbs_pad // LANES), axis=1)
        jr = _iota((nbs_pad, nbs_pad), 0)
        jc = _iota((nbs_pad, nbs_pad), 1)
        tie = jnp.where(jc < jr, 1.0, 0.0)
        beats = jnp.where(s_row > s_col, 1.0, jnp.where(s_row == s_col, tie, 0.0))
        rank = jnp.sum(beats, axis=-1, keepdims=True)
        slot = _iota((nbs_pad, LANES), 1).astype(F32)
        blk = _iota((nbs_pad, LANES), 0).astype(F32)
        picked = jnp.sum(jnp.where(rank == slot, blk, 0.0), axis=0, keepdims=True)
        sel_ref[0, g:g + 1, :] = picked.astype(jnp.int32)


def dec_cmp(ak, av, q, cw, p0):
    w1, pos, b1, w2, b2 = cw
    bsz, G, nch, hid2 = ak.shape
    nbc = nch - CMP_BLOCK // CMP_STRIDE + 1
    nbs = -(-(p0 + 1) // SEL_BLOCK)
    nbs_pad = -(-nbs // LANES) * LANES
    whole = lambda a: pl.BlockSpec(a.shape, lambda b: (0,) * a.ndim)
    aspec = pl.BlockSpec((1, G, nch, hid2), lambda b: (b, 0, 0, 0))
    return pl.pallas_call(
        functools.partial(_dec_cmp_kernel, nbc=nbc, nbs=nbs, nbs_pad=nbs_pad, p0=p0),
        grid=(bsz,),
        in_specs=[aspec, aspec, pl.BlockSpec((1, G, SUBLANES, NSA_HD), lambda b: (b, 0, 0, 0)),
                  whole(pos), whole(w1), whole(b1), whole(w2), whole(b2)],
        out_specs=[pl.BlockSpec((1, G, SUBLANES, NSA_HD), lambda b: (b, 0, 0, 0)),
                   pl.BlockSpec((1, G, LANES), lambda b: (b, 0, 0))],
        out_shape=[jax.ShapeDtypeStruct((bsz, G, SUBLANES, NSA_HD), F32),
                   jax.ShapeDtypeStruct((bsz, G, LANES), jnp.int32)],
        compiler_params=_cparams("parallel"),
    )(ak, av, q, pos, w1, b1, w2, b2)


def _softmax_with_new_key(qb, k_b, v_b, bias, s_new, v_new):
    s = _dot_nt(qb, k_b)
    if bias is not None:
        s = s + bias
    mx = jnp.maximum(jnp.max(s, axis=-1, keepdims=True), s_new)
    p = jnp.exp(s - mx)
    p_new = jnp.exp(s_new - mx)
    return (_dot(p.astype(BF16), v_b) + p_new * v_new) / (jnp.sum(p, axis=-1, keepdims=True) + p_new)


def _dec_attn_kernel(sel_ref, tbl_ref, *refs, n_cache_blocks, n_sel):
    ks_refs, vs_refs = refs[:n_sel], refs[n_sel:2 * n_sel]
    q_ref, new_ref, kw_ref, vw_ref, oc_ref, g_ref, bg_ref, o_ref = refs[2 * n_sel:]
    b, g = pl.program_id(0), pl.program_id(1)
    q = q_ref[0, 0] * (NSA_HD ** -0.5)
    qb = q.astype(BF16)
    new = new_ref[0, 0]

    mine = lambda r, n: r[pl.ds(g, n, stride=NSA_KV), :]
    k_all = jnp.concatenate([mine(r.at[0, 0], SEL_BLOCK) for r in ks_refs], axis=0).astype(BF16)
    v_all = jnp.concatenate([mine(r.at[0, 0], SEL_BLOCK) for r in vs_refs], axis=0).astype(BF16)
    slot = _iota((SUBLANES, n_sel * SEL_BLOCK), 1) // SEL_BLOCK
    bias = jnp.zeros((SUBLANES, n_sel * SEL_BLOCK), F32)
    for s in range(n_sel):
        blk = sel_ref[(b * NSA_KV + g) * n_sel + s]
        bias = jnp.where(slot == s, jnp.where(blk < n_cache_blocks, 0.0, NEG), bias)
    o_slc = _softmax_with_new_key(qb, k_all, v_all, bias, jnp.sum(q * new[0:1], axis=-1, keepdims=True), new[1:2])

    n_win = kw_ref.shape[1] // NSA_KV
    o_win = _softmax_with_new_key(qb, mine(kw_ref.at[0], n_win).astype(BF16), mine(vw_ref.at[0], n_win).astype(BF16),
                                  None,
                                  jnp.sum(q * new[2:3], axis=-1, keepdims=True), new[3:4])

    gate = jnp.broadcast_to(_sigmoid(g_ref[0, 0] + bg_ref[...]), (SUBLANES, LANES))
    head = _iota((SUBLANES, LANES), 0)
    lane = _iota((SUBLANES, LANES), 1)
    o = jnp.zeros((SUBLANES, NSA_HD), F32)
    for br, ob in enumerate((oc_ref[0, 0], o_slc, o_win)):
        o = o + jnp.sum(jnp.where(lane == 3 * head + br, gate, 0.0), axis=-1, keepdims=True) * ob
    o_ref[0, 0] = o


def dec_attn(sel, table, pool_ks, pool_vs, q, new, kw_past, vw_past, o_cmp, gates, b_gate):
    n_pool, page, G, hd = pool_ks.shape
    bsz, n_pages = table.shape
    halves = page // SEL_BLOCK
    wb = kw_past.shape[1]
    assert wb <= WINDOW
    n_sel = sel.shape[-1]
    pk = pool_ks.reshape(n_pool, halves, SEL_BLOCK * G, hd)
    pv = pool_vs.reshape(n_pool, halves, SEL_BLOCK * G, hd)
    n_cache_blocks = n_pages * halves

    def page_map(b, g, sel_ref, tbl_ref, s):
        blk = jnp.minimum(sel_ref[(b * G + g) * n_sel + s], n_cache_blocks - 1)
        return (tbl_ref[b * n_pages + blk // halves], blk % halves, 0, 0)

    bg_map = lambda b, g, sel_ref, tbl_ref: (b, g, 0, 0)
    small = pl.BlockSpec((1, 1, SUBLANES, hd), bg_map)
    wspec = pl.BlockSpec((1, wb * G, hd), lambda b, g, sel_ref, tbl_ref: (b, 0, 0))
    blocks = [pl.BlockSpec((1, 1, SEL_BLOCK * G, hd), functools.partial(page_map, s=s)) for s in range(n_sel)]
    grid_spec = pltpu.PrefetchScalarGridSpec(
        num_scalar_prefetch=2,
        grid=(bsz, G),
        in_specs=blocks + blocks + [small, small, wspec, wspec, small,
                                    pl.BlockSpec((1, 1, 1, LANES), bg_map),
                                    pl.BlockSpec((1, LANES), lambda b, g, sel_ref, tbl_ref: (0, g))],
        out_specs=small,
    )
    return pl.pallas_call(
        functools.partial(_dec_attn_kernel, n_cache_blocks=n_cache_blocks, n_sel=n_sel),
        grid_spec=grid_spec,
        out_shape=jax.ShapeDtypeStruct((bsz, G, SUBLANES, hd), F32),
        compiler_params=_cparams("parallel", "parallel"),
    )(sel.reshape(-1), table.reshape(-1), *([pk] * n_sel), *([pv] * n_sel), q, new,
      kw_past.reshape(bsz, wb * G, hd), vw_past.reshape(bsz, wb * G, hd), o_cmp,
      gates.reshape(bsz, G, 1, LANES), b_gate)


def _gate_weights(wt, layer, row0, n):
    rows = lax.slice(wt, (layer, row0, 0), (layer + 1, row0 + n, wt.shape[2]))
    return jnp.pad(rows, ((0, 0), (0, LANES - n), (0, 0))), 0


def _nsa_gate_weights(wt, layer, b_gate, row0):
    per = 3 * NSA_HPG
    k = wt.shape[2]
    rows = lax.slice(wt, (layer, row0, 0), (layer + 1, row0 + NSA_KV * per, k)).reshape(NSA_KV, per, k)
    wg = jnp.pad(rows, ((0, 0), (0, LANES - per), (0, 0))).reshape(1, NSA_KV * LANES, k)
    bg = jnp.pad(b_gate.reshape(NSA_KV, per), ((0, 0), (0, LANES - per))).reshape(1, NSA_KV * LANES)
    return (wg, 0), bg


def kernel(x_prompt, x_sample, state_a_C, state_a_n, state_a_m, state_b_conv, cache_c_k_cmp, cache_c_v_cmp, cache_c_k_slc, cache_c_v_slc, cache_c_k_win, cache_c_v_win, page_table, norm_w, ffn_w_in, ffn_w_out, a_w_in, a_b_gates, a_head_norm, a_w_out, b_w_in, b_conv_w, b_w_out, c_w_in, c_b_gate, c_cmp_pos, c_cmp_w1, c_cmp_b1, c_cmp_w2, c_cmp_b2, c_w_out):
    bp, T, D = x_prompt.shape
    bs = x_sample.shape[0]
    assert x_sample.shape[1] == 1
    depth = norm_w.shape[0]
    G, hd = NSA_KV, NSA_HD
    past_len = page_table.shape[1] * cache_c_k_cmp.shape[2]
    xp = x_prompt.reshape(bp * T, D)
    xs = x_sample.reshape(bs, D)
    hp_in = norm_cast(xp, norm_w[0, 0])
    hs_in = norm_cast(xs, norm_w[0, 0])
    a_wt = jnp.swapaxes(a_w_in, 1, 2)
    c_wt = jnp.swapaxes(c_w_in, 1, 2)
    w_outs = {0: a_w_out.astype(BF16), 1: b_w_out.astype(BF16), 2: c_w_out.astype(BF16)}
    f_outs = ffn_w_out.astype(BF16)
    pa, sa, pb, sb, pc, sc = [], [], [], [], [], []
    for i in range(depth):
        kind, j = i % 3, i // 3
        nw = norm_w[i]
        w_out = (w_outs[kind], j)
        if kind == 0:
            hdv = a_head_norm.shape[1]
            n_main = 3 * hdv
            wg = _gate_weights(a_wt, j, n_main, 2 * MLSTM_HEADS)
            qkvo_p, qkvo_s = proj_plain(hp_in, hs_in, (a_wt, j), 0, n_main, 1024, transposed=True)
            g_p, g_s = proj_plain(hp_in, hs_in, wg, 0, LANES, LANES, transposed=True)
            hp, c_p, n_p, m_p = mlstm_prompt(qkvo_p, g_p, a_b_gates[j], a_head_norm[j], bp, T)
            hs, c_s, n_s, m_s = mlstm_step(qkvo_s, g_s, a_b_gates[j], a_head_norm[j],
                                           state_a_C[j], state_a_n[j], state_a_m[j])
            pa.append((c_p, n_p, m_p[:, :, 0]))
            sa.append((c_s, n_s, m_s))
        elif kind == 1:
            (bg_p, z_p), (bg_s, z_s) = proj_conv_in(hp_in, hs_in, (b_w_in, j), D)
            hp = conv_gate(bg_p, z_p, b_conv_w[j], T)
            buf = state_b_conv[j]
            hs = conv_gate_step(bg_s, z_s, buf.transpose(1, 0, 2), b_conv_w[j]).astype(BF16)
            pb.append(z_p.reshape(bp, T, D)[:, T - (CONV_W - 1):])
            sb.append(jnp.concatenate([buf, z_s[:, None]], axis=1)[:, -(CONV_W - 1):])
        else:
            nq_cols = NSA_HPG * G * hd
            nheads = G * NSA_HPG
            wg, bgate = _nsa_gate_weights(c_wt, j, c_b_gate[j], nq_cols + 6 * G * hd)
            cw = _cmp_weights(c_cmp_pos[j], c_cmp_w1[j], c_cmp_b1[j], c_cmp_w2[j], c_cmp_b2[j])
            q_p, q_s = proj_plain(hp_in, hs_in, (c_wt, j), 0, nq_cols, 1024, transposed=True)
            kv_p, kv_s = proj_groups(hp_in, hs_in, (c_wt, j), nq_cols, 6, G * hd, 256, transposed=True)
            gt_p, gt_s = proj_plain(hp_in, hs_in, wg, 0, G * LANES, G * LANES, transposed=True)
            ck, cv = cmp_prompt(kv_p[0], kv_p[1], bp, T, cw)
            hp = nsa_attn_prompt(q_p, ck, cv, kv_p[2], kv_p[3], kv_p[4], kv_p[5], gt_p, bgate, bp, T)
            kv_p = [a.reshape(bp, T, G, hd) for a in kv_p]
            keep = min(WINDOW, T)
            pc.append(tuple(kv_p[:4]) + tuple(a[:, T - keep:] for a in kv_p[4:]))
            q_s = jnp.pad(q_s.reshape(bs, G, NSA_HPG, hd), ((0, 0), (0, 0), (0, SUBLANES - NSA_HPG), (0, 0)))
            kv_s = jnp.stack([a.reshape(bs, G, hd) for a in kv_s], axis=1)
            new = jnp.pad(kv_s[:, 2:6].transpose(0, 2, 1, 3), ((0, 0), (0, 0), (0, SUBLANES - 4), (0, 0)))
            ak, av = page_cmp(cache_c_k_cmp[j], cache_c_v_cmp[j], page_table, cw[0])
            o_cmp, ranked = dec_cmp(ak, av, q_s, cw, past_len)
            n_sel = min(N_SELECT, -(-(past_len + 1) // SEL_BLOCK))
            o_s = dec_attn(ranked[:, :, :n_sel], page_table, cache_c_k_slc[j], cache_c_v_slc[j], q_s, new,
                           cache_c_k_win[j], cache_c_v_win[j], o_cmp, gt_s, bgate)
            hs = o_s[:, :, :NSA_HPG].reshape(bs, nheads * hd).astype(BF16)
            keep_s = min(WINDOW, cache_c_k_win.shape[2] + 1)
            win = [jnp.concatenate([c[j], kv_s[:, a][:, None]], axis=1)[:, -keep_s:]
                   for c, a in ((cache_c_k_win, 4), (cache_c_v_win, 5))]
            sc.append(tuple(kv_s[:, a][:, None] for a in range(4)) + tuple(win))
        last = i == depth - 1
        nw_next = norm_w[i + 1, 0] if not last else nw[0]
        (xp, xs), (hp_mid, hs_mid) = proj_out(hp, hs, w_out, xp, xs, nw[1], nw[2])
        act_p, act_s = proj_swiglu(hp_mid, hs_mid, (ffn_w_in, i), f_outs.shape[1])
        (xp, xs), (hp_in, hs_in) = proj_out(act_p, act_s, (f_outs, i), xp, xs, nw[3], nw_next, not last)

    stack = lambda items: [jnp.stack(a) for a in zip(*items)]
    p_a, s_a = stack(pa), stack(sa)
    p_c, s_c = stack(pc), stack(sc)
    return (xp.reshape(bp, T, D), xs.reshape(bs, 1, D), *p_a, *s_a, jnp.stack(pb), jnp.stack(sb), *p_c, *s_c)
```

```python
import functools
import math

import jax
import jax.numpy as jnp
from jax import lax
from jax.experimental import pallas as pl
from jax.experimental.pallas import tpu as pltpu

F32 = jnp.float32
BF16 = jnp.bfloat16

EPS = 1e-6
NEG = -1e30
BIG = 1e30

LANES = 128
SUBLANES = 8
VMEM_LIMIT = 56 * 1024 * 1024

MLSTM_HEADS = 8
MLSTM_CHUNK = 256
FORGET_BIAS_COL = MLSTM_HEADS
CONV_W = 3
NSA_HD = 128
NSA_KV = 4
NSA_HPG = 4
CMP_BLOCK = 32
CMP_STRIDE = 16
SEL_BLOCK = 64
N_SELECT = 16
WINDOW = 512
ATT_TILE = 256
PAGES_PER_STEP = 8


def _cparams(*sem):
    return pltpu.CompilerParams(dimension_semantics=sem, vmem_limit_bytes=VMEM_LIMIT)


def _dot(a, b):
    return jnp.dot(a, b, preferred_element_type=F32)


def _dot_nt(a, b):
    return lax.dot_general(a, b, (((1,), (1,)), ((), ())), preferred_element_type=F32)


def _dot_tn(a, b):
    return lax.dot_general(a, b, (((0,), (0,)), ((), ())), preferred_element_type=F32)


def _split_bf16(x):
    hi = x.astype(BF16)
    lo = (x - hi.astype(F32)).astype(BF16)
    return hi, lo


def _iota(shape, dim):
    return lax.broadcasted_iota(jnp.int32, shape, dim)


def _sigmoid(x):
    return 1.0 / (1.0 + jnp.exp(-x))


def _rms(x):
    return x * lax.rsqrt(jnp.mean(x * x, axis=-1, keepdims=True) + EPS)


def _row_tile(m, want):
    t = min(m, want)
    assert m % t == 0, (m, t)
    return t


def _norm_kernel(x_ref, w_ref, o_ref):
    o_ref[...] = (_rms(x_ref[...]) * w_ref[...]).astype(BF16)


def norm_cast(x, w):
    m, d = x.shape
    tm = _row_tile(m, 512)
    return pl.pallas_call(
        _norm_kernel,
        grid=(m // tm,),
        in_specs=[pl.BlockSpec((tm, d), lambda i: (i, 0)), pl.BlockSpec((1, d), lambda i: (0, 0))],
        out_specs=pl.BlockSpec((tm, d), lambda i: (i, 0)),
        out_shape=jax.ShapeDtypeStruct((m, d), BF16),
        compiler_params=_cparams("parallel"),
    )(x, w.reshape(1, d))


def _proj_kernel(x_ref, xs_ref, *refs, n_groups, n_out, epilogue, transposed):
    w_refs = refs[:n_groups]
    out_refs = refs[n_groups:n_groups + n_out]
    sample_out_refs = refs[n_groups + n_out:n_groups + 2 * n_out]
    wb_ref = refs[n_groups + 2 * n_out]
    mm = _dot_nt if transposed else _dot

    @pl.when(pl.program_id(1) == 0)
    def _():
        for g in range(n_groups):
            wb_ref[g] = w_refs[g][...].astype(BF16)
        xs = xs_ref[...]
        epilogue([mm(xs, wb_ref[g]) for g in range(n_groups)], sample_out_refs)

    x = x_ref[...]
    epilogue([mm(x, wb_ref[g]) for g in range(n_groups)], out_refs)


def _proj(x, xs, w, *, col0, tn, n_tiles, group_stride, n_groups, epilogue, outs, tm_want=1024, transposed=False):
    w, layer = w
    m, k = x.shape
    s = xs.shape[0]
    tm = _row_tile(m, tm_want)
    assert col0 % tn == 0
    b0 = col0 // tn
    if transposed:
        w_block = (None, tn, k)
        w_map = lambda j, i, off: (layer, off + j, 0)
    else:
        w_block = (None, k, tn)
        w_map = lambda j, i, off: (layer, 0, off + j)
    w_specs = [pl.BlockSpec(w_block, functools.partial(w_map, off=b0 + g * group_stride)) for g in range(n_groups)]
    n_out = len(outs)

    def out_for(rows, row_tile, row_map, n, dt, by_rows):
        if not by_rows:
            return jax.ShapeDtypeStruct((rows, n), dt), pl.BlockSpec((row_tile, tn), row_map)
        assert n_tiles == 1 and n == tn
        per = n // LANES
        return (jax.ShapeDtypeStruct((rows * per, LANES), dt),
                pl.BlockSpec((row_tile * per, LANES), lambda j, i: (row_map(j, i)[0], 0)))

    prompt_outs = [out_for(m, tm, lambda j, i: (i, j), *o) for o in outs]
    sample_outs = [out_for(s, s, lambda j, i: (0, j), *o) for o in outs]
    res = pl.pallas_call(
        functools.partial(_proj_kernel, n_groups=n_groups, n_out=n_out, epilogue=epilogue, transposed=transposed),
        grid=(n_tiles, m // tm),
        in_specs=[pl.BlockSpec((tm, k), lambda j, i: (i, 0)), pl.BlockSpec((s, k), lambda j, i: (0, 0))] + w_specs,
        out_specs=[spec for _, spec in prompt_outs + sample_outs],
        out_shape=[shape for shape, _ in prompt_outs + sample_outs],
        scratch_shapes=[pltpu.VMEM((n_groups,) + w_block[1:], BF16)],
        compiler_params=_cparams("arbitrary", "arbitrary"),
    )(x, xs, *([w] * n_groups))
    return res[:n_out], res[n_out:]


def _ep_plain(accs, outs):
    for acc, o in zip(accs, outs):
        o[...] = acc


def proj_plain(x, xs, w, col0, n_cols, tn, transposed=False):
    p, s = _proj(x, xs, w, col0=col0, tn=tn, n_tiles=n_cols // tn, group_stride=0, n_groups=1,
                 epilogue=_ep_plain, outs=[(n_cols, F32, False)], transposed=transposed)
    return p[0], s[0]


def _ep_both_layouts(accs, outs):
    n = len(accs)
    for acc, o_tok, o_rows in zip(accs, outs[:n], outs[n:]):
        o_tok[...] = acc
        per = acc.shape[1] // LANES
        rows = acc.shape[0]
        for c in range(per):
            o_rows[pl.ds(c, rows, stride=per), :] = acc[:, c * LANES:(c + 1) * LANES]


def proj_groups(x, xs, w, col0, n_groups, group_cols, transposed=False):
    outs = [(group_cols, F32, False)] * n_groups + [(group_cols, F32, True)] * n_groups
    p, s = _proj(x, xs, w, col0=col0, tn=group_cols, n_tiles=1, group_stride=1, n_groups=n_groups,
                 epilogue=_ep_both_layouts, outs=outs, tm_want=512, transposed=transposed)
    return (p[:n_groups], p[n_groups:]), (s[:n_groups], s[n_groups:])


def _ep_swiglu(accs, outs):
    g, u = accs
    outs[0][...] = (g * _sigmoid(g) * u).astype(BF16)


def proj_swiglu(x, xs, w, ff):
    tn = 512
    p, s = _proj(x, xs, w, col0=0, tn=tn, n_tiles=ff // tn, group_stride=ff // tn, n_groups=2,
                 epilogue=_ep_swiglu, outs=[(ff, BF16, False)])
    return p[0], s[0]


def _ep_conv_in(accs, outs):
    bg, cg, u = accs
    outs[0][...] = bg
    outs[1][...] = cg * u


def proj_conv_in(x, xs, w, d):
    tn = 512
    return _proj(x, xs, w, col0=0, tn=tn, n_tiles=d // tn, group_stride=d // tn, n_groups=3,
                 epilogue=_ep_conv_in, outs=[(d, F32, False), (d, F32, False)])


def _out_kernel(h_ref, hs_ref, w_ref, r_ref, rs_ref, nwa_ref, nwb_ref, *out_refs, emit_next):
    def finalize(y, r, x_ref, xn_ref):
        x_new = r + _rms(y) * nwa_ref[...]
        x_ref[...] = x_new
        if emit_next:
            xn_ref[...] = (_rms(x_new) * nwb_ref[...]).astype(BF16)

    if emit_next:
        x_ref, xn_ref, xs_ref, xsn_ref = out_refs
    else:
        (x_ref, xs_ref), xn_ref, xsn_ref = out_refs, None, None

    @pl.when(pl.program_id(0) == 0)
    def _():
        finalize(_dot(hs_ref[...], w_ref[...]), rs_ref[...], xs_ref, xsn_ref)

    finalize(_dot(h_ref[...], w_ref[...]), r_ref[...], x_ref, xn_ref)


def proj_out(h, hs, w, resid, resid_s, nw_post, nw_next, emit_next=True):
    w, layer = w
    m, k = h.shape
    s = hs.shape[0]
    d = w.shape[2]
    tm = _row_tile(m, 512 if k <= 2048 else 256)
    row = lambda i: (i, 0)
    fixed = lambda i: (0, 0)
    f32_out = [jax.ShapeDtypeStruct((m, d), F32), jax.ShapeDtypeStruct((s, d), F32)]
    bf16_out = [jax.ShapeDtypeStruct((m, d), BF16), jax.ShapeDtypeStruct((s, d), BF16)]
    specs = [pl.BlockSpec((tm, d), row), pl.BlockSpec((s, d), fixed)]
    if emit_next:
        out_shape = [f32_out[0], bf16_out[0], f32_out[1], bf16_out[1]]
        out_specs = [specs[0], specs[0], specs[1], specs[1]]
    else:
        out_shape, out_specs = f32_out, specs
    res = pl.pallas_call(
        functools.partial(_out_kernel, emit_next=emit_next),
        grid=(m // tm,),
        in_specs=[pl.BlockSpec((tm, k), row),
                  pl.BlockSpec((s, k), fixed),
                  pl.BlockSpec((None, k, d), lambda i: (layer, 0, 0), pipeline_mode=pl.Buffered(1)),
                  pl.BlockSpec((tm, d), row),
                  pl.BlockSpec((s, d), fixed),
                  pl.BlockSpec((1, d), fixed),
                  pl.BlockSpec((1, d), fixed)],
        out_specs=out_specs,
        out_shape=out_shape,
        compiler_params=_cparams("arbitrary"),
    )(h, hs, w, resid, resid_s, nw_post.reshape(1, d), nw_next.reshape(1, d))
    if emit_next:
        return (res[0], res[2]), (res[1], res[3])
    return (res[0], res[1]), (None, None)


def _log_sigmoid(x):
    return jnp.minimum(x, 0.0) - jnp.log1p(jnp.exp(-jnp.abs(x)))


def _mlstm_kernel(q_ref, k_ref, v_ref, o_ref, g_ref, bg_ref, hn_ref, hg_ref, c_ref, n_ref, m_ref, *, L, H, DK, DV):
    @pl.when(pl.program_id(1) == 0)
    def _():
        c_ref[...] = jnp.zeros_like(c_ref)
        n_ref[...] = jnp.zeros_like(n_ref)
        m_ref[...] = jnp.zeros_like(m_ref)

    gpre = g_ref[...] + bg_ref[...]
    lf = _log_sigmoid(gpre)
    rows = _iota((L, L), 0)
    cols = _iota((L, L), 1)
    causal = rows >= cols
    tril = jnp.where(causal, 1.0, 0.0).astype(BF16)
    lf_hi, lf_lo = _split_bf16(lf)
    bcum = _dot(tril, lf_hi) + _dot(tril, lf_lo)
    g_t = gpre.T
    b_t = bcum.T
    for h in range(H):
        f = FORGET_BIAS_COL + h
        a_col = bcum[:, f:f + 1]
        i_col = gpre[:, h:h + 1]
        b_row = b_t[f:f + 1, :]
        i_row = g_t[h:h + 1, :]
        m_prev = m_ref[0, h:h + 1, 0:1]
        log_d = jnp.where(causal, a_col - b_row + i_row, NEG)
        log_inter = a_col + m_prev
        m_t = jnp.maximum(log_inter, jnp.max(log_d, axis=-1, keepdims=True))
        d = jnp.exp(log_d - m_t)
        inter = jnp.exp(log_inter - m_t)
        qh = q_ref[:, h * DK:(h + 1) * DK]
        kh = k_ref[:, h * DK:(h + 1) * DK] * (DK ** -0.5)
        vb = v_ref[:, h * DV:(h + 1) * DV].astype(BF16)
        qb = qh.astype(BF16)
        s = _dot_nt(qb, kh.astype(BF16)) * d
        c_old = c_ref[0, h]
        n_old = n_ref[0, h:h + 1, :]
        num = _dot(s.astype(BF16), vb) + inter * _dot(qb, c_old.astype(BF16))
        den = jnp.sum(s, axis=-1, keepdims=True) + inter * jnp.sum(qh * n_old, axis=-1, keepdims=True)
        hh = _rms(num / jnp.maximum(jnp.abs(den), jnp.exp(-m_t)))
        gate = _sigmoid(o_ref[:, h * DV:(h + 1) * DV])
        hg_ref[:, h * DV:(h + 1) * DV] = (hh * hn_ref[:, h * DV:(h + 1) * DV] * gate).astype(BF16)
        m_new = m_t[L - 1:L, :]
        w_col = jnp.exp(a_col[L - 1:L, :] - a_col + i_col - m_new)
        decay = jnp.exp(log_inter[L - 1:L, :] - m_new)
        kw = kh * w_col
        c_ref[0, h] = decay * c_old + _dot_tn(kw.astype(BF16), vb)
        n_ref[0, h:h + 1, :] = decay * n_old + jnp.sum(kw, axis=0, keepdims=True)
        m_ref[0, h:h + 1, :] = jnp.broadcast_to(m_new, (1, LANES))


def mlstm_prompt(qkvo, gates, b_gates, head_norm, bsz, T):
    H = MLSTM_HEADS
    m = qkvo.shape[0]
    hdv = qkvo.shape[1] // 3
    hdk = hdv // 2
    DK, DV = hdk // H, hdv // H
    L = math.gcd(T, MLSTM_CHUNK)
    nc = T // L
    row = lambda b, c: (b * nc + c, 0)
    state = lambda b, c: (b, 0, 0)
    return pl.pallas_call(
        functools.partial(_mlstm_kernel, L=L, H=H, DK=DK, DV=DV),
        grid=(bsz, nc),
        in_specs=[pl.BlockSpec((L, hdk), row),
                  pl.BlockSpec((L, hdk), lambda b, c: (b * nc + c, 1)),
                  pl.BlockSpec((L, hdv), lambda b, c: (b * nc + c, 1)),
                  pl.BlockSpec((L, hdv), lambda b, c: (b * nc + c, 2)),
                  pl.BlockSpec((L, LANES), row),
                  pl.BlockSpec((1, LANES), lambda b, c: (0, 0)),
                  pl.BlockSpec((1, hdv), lambda b, c: (0, 0))],
        out_specs=[pl.BlockSpec((L, hdv), row),
                   pl.BlockSpec((1, H, DK, DV), lambda b, c: (b, 0, 0, 0)),
                   pl.BlockSpec((1, H, DK), state),
                   pl.BlockSpec((1, H, LANES), state)],
        out_shape=[jax.ShapeDtypeStruct((m, hdv), BF16),
                   jax.ShapeDtypeStruct((bsz, H, DK, DV), F32),
                   jax.ShapeDtypeStruct((bsz, H, DK), F32),
                   jax.ShapeDtypeStruct((bsz, H, LANES), F32)],
        compiler_params=_cparams("parallel", "arbitrary"),
    )(qkvo, qkvo, qkvo, qkvo, gates, _pad_lanes(b_gates.reshape(1, -1)), head_norm.reshape(1, hdv))


def _pad_lanes(x, width=LANES):
    return jnp.pad(x, ((0, 0), (0, width - x.shape[-1])))


def _to_col(row, n):
    eye = _iota((n, n), 0) == _iota((n, n), 1)
    return jnp.sum(jnp.where(eye, jnp.broadcast_to(row, (n, n)), 0.0), axis=-1, keepdims=True)


def _mlstm_step_kernel(x_ref, g_ref, bg_ref, hn_ref, c0_ref, n0_ref, m0_ref, hg_ref, c_ref, n_ref, m_ref,
                       *, H, DK, DV):
    x = x_ref[0]
    gpre = g_ref[0] + bg_ref[...]
    lf = _log_sigmoid(gpre)
    hdk = H * DK
    hdv = H * DV
    for h in range(H):
        f = FORGET_BIAS_COL + h
        ig = gpre[:, h:h + 1]
        m_prev = m0_ref[0, h:h + 1, :]
        log_inter = lf[:, f:f + 1] + m_prev
        m_t = jnp.maximum(log_inter, ig)
        d = jnp.exp(ig - m_t)
        inter = jnp.exp(log_inter - m_t)
        q = x[:, h * DK:(h + 1) * DK]
        k = x[:, hdk + h * DK:hdk + (h + 1) * DK] * (DK ** -0.5)
        v = x[:, 2 * hdk + h * DV:2 * hdk + (h + 1) * DV]
        og = x[:, 2 * hdk + hdv + h * DV:2 * hdk + hdv + (h + 1) * DV]
        c_old = c0_ref[0, h]
        n_old = n0_ref[0, h:h + 1, :]
        s = jnp.sum(q * k, axis=-1, keepdims=True) * d
        q_col = _to_col(q, DK)
        k_col = _to_col(k, DK)
        num = s * v + inter * jnp.sum(q_col * c_old, axis=0, keepdims=True)
        den = s + inter * jnp.sum(q * n_old, axis=-1, keepdims=True)
        hh = _rms(num / jnp.maximum(jnp.abs(den), jnp.exp(-m_t)))
        hg_ref[0, :, h * DV:(h + 1) * DV] = (hh * hn_ref[:, h * DV:(h + 1) * DV] * _sigmoid(og)).astype(BF16)
        c_ref[0, h] = inter * c_old + d * (k_col * v)
        n_ref[0, h:h + 1, :] = inter * n_old + d * k
        m_ref[0, h:h + 1, :] = m_t


def mlstm_step(qkvo, gates, b_gates, head_norm, c0, n0, m0):
    bsz, H, DK, DV = c0.shape
    wid = qkvo.shape[1]
    hdv = H * DV
    one = lambda b: (b, 0, 0)
    hg, c1, n1, m1 = pl.pallas_call(
        functools.partial(_mlstm_step_kernel, H=H, DK=DK, DV=DV),
        grid=(bsz,),
        in_specs=[pl.BlockSpec((1, 1, wid), one),
                  pl.BlockSpec((1, 1, LANES), one),
                  pl.BlockSpec((1, LANES), lambda b: (0, 0)),
                  pl.BlockSpec((1, hdv), lambda b: (0, 0)),
                  pl.BlockSpec((1, H, DK, DV), lambda b: (b, 0, 0, 0)),
                  pl.BlockSpec((1, H, DK), one),
                  pl.BlockSpec((1, H, 1), one)],
        out_specs=[pl.BlockSpec((1, 1, hdv), one),
                   pl.BlockSpec((1, H, DK, DV), lambda b: (b, 0, 0, 0)),
                   pl.BlockSpec((1, H, DK), one),
                   pl.BlockSpec((1, H, 1), one)],
        out_shape=[jax.ShapeDtypeStruct((bsz, 1, hdv), BF16),
                   jax.ShapeDtypeStruct((bsz, H, DK, DV), F32),
                   jax.ShapeDtypeStruct((bsz, H, DK), F32),
                   jax.ShapeDtypeStruct((bsz, H, 1), F32)],
        compiler_params=_cparams("parallel"),
    )(qkvo.reshape(bsz, 1, wid), gates.reshape(bsz, 1, LANES), _pad_lanes(b_gates.reshape(1, -1)),
      head_norm.reshape(1, hdv), c0, n0, m0.reshape(bsz, H, 1))
    return hg.reshape(bsz, hdv), c1, n1, m1.reshape(bsz, H)


def _conv_kernel(bg_ref, z_ref, zp_ref, cw_ref, o_ref, *, tm, T):
    i = pl.program_id(0)
    z = z_ref[...]
    zc = jnp.concatenate([zp_ref[...], z], axis=0)
    tpos = (i * tm + _iota((tm, 1), 0)) % T
    y = cw_ref[CONV_W - 1:CONV_W, :] * z
    for back in range(1, CONV_W):
        zb = zc[SUBLANES - back:SUBLANES - back + tm, :]
        y = y + cw_ref[CONV_W - 1 - back:CONV_W - back, :] * jnp.where(tpos >= back, zb, 0.0)
    o_ref[...] = (bg_ref[...] * y).astype(BF16)


def conv_gate(bg, z, conv_w, T):
    m, d = z.shape
    tm = _row_tile(T, 256)
    per = tm // SUBLANES
    return pl.pallas_call(
        functools.partial(_conv_kernel, tm=tm, T=T),
        grid=(m // tm,),
        in_specs=[pl.BlockSpec((tm, d), lambda i: (i, 0)),
                  pl.BlockSpec((tm, d), lambda i: (i, 0)),
                  pl.BlockSpec((SUBLANES, d), lambda i: (jnp.maximum(i * per - 1, 0), 0)),
                  pl.BlockSpec((CONV_W, d), lambda i: (0, 0))],
        out_specs=pl.BlockSpec((tm, d), lambda i: (i, 0)),
        out_shape=jax.ShapeDtypeStruct((m, d), BF16),
        compiler_params=_cparams("parallel"),
    )(bg, z, z, conv_w)


def _conv_step_kernel(bg_ref, z_ref, buf_ref, cw_ref, o_ref):
    y = cw_ref[CONV_W - 1:CONV_W, :] * z_ref[...]
    for j in range(CONV_W - 1):
        y = y + cw_ref[j:j + 1, :] * buf_ref[j]
    o_ref[...] = bg_ref[...] * y


def conv_gate_step(bg, z, buf, conv_w):
    bsz, d = z.shape
    full = lambda: (0, 0)
    return pl.pallas_call(
        _conv_step_kernel,
        in_specs=[pl.BlockSpec((bsz, d), full), pl.BlockSpec((bsz, d), full),
                  pl.BlockSpec((CONV_W - 1, bsz, d), lambda: (0, 0, 0)), pl.BlockSpec((CONV_W, d), full)],
        out_specs=pl.BlockSpec((bsz, d), full),
        out_shape=jax.ShapeDtypeStruct((bsz, d), F32),
    )(bg, z, buf, conv_w)


def _gelu(x):
    return 0.5 * x * (1.0 + jnp.tanh(math.sqrt(2.0 / math.pi) * (x + 0.044715 * x * x * x)))


def _cmp_finish(a, posw, b1, w2, b2):
    hid = a.shape[1] // 2
    a1 = a[:, hid:]
    a1_next = jnp.concatenate([a1[1:], a1[:1]], axis=0)
    pre = a[:, :hid] + a1_next + b1 + posw[0:1, :hid] + posw[1:2, hid:]
    return _dot(_gelu(pre).astype(BF16), w2) + b2


def _cmp_prompt_kernel(k_ref, v_ref, w1_ref, pos_ref, b1_ref, w2_ref, b2_ref, ck_ref, cv_ref, *, nch):
    for idx, (x_ref, o_ref) in enumerate(((k_ref, ck_ref), (v_ref, cv_ref))):
        w1 = w1_ref[idx]
        lhs = jnp.concatenate([x_ref[pl.ds(p, nch, stride=CMP_STRIDE), :] for p in range(CMP_STRIDE)], axis=1)
        a = _dot(lhs.astype(BF16), w1)
        posw = _dot(pos_ref[idx], w1)
        o_ref[0, 0] = _cmp_finish(a, posw, b1_ref[idx], w2_ref[idx], b2_ref[idx])


def _cmp_weights(cmp_pos, cmp_w1, cmp_b1, cmp_w2, cmp_b2):
    two, ratio, stride, hd, hid = cmp_w1.shape
    w1 = cmp_w1.transpose(0, 2, 3, 1, 4).reshape(two, stride * hd, ratio * hid).astype(BF16)
    pos = cmp_pos.reshape(two, ratio, stride * hd)
    pos = jnp.pad(pos, ((0, 0), (0, SUBLANES - ratio), (0, 0))).astype(BF16)
    return w1, pos, cmp_b1.reshape(two, 1, hid), cmp_w2.astype(BF16), cmp_b2.reshape(two, 1, hd)


def cmp_prompt(kc, vc, bsz, T, cw):
    w1, pos, b1, w2, b2 = cw
    nch = T // CMP_STRIDE
    G = NSA_KV
    out = jax.ShapeDtypeStruct((bsz, G, nch, NSA_HD), F32)
    ospec = pl.BlockSpec((1, 1, nch, NSA_HD), lambda b, g: (b, g, 0, 0))
    whole = lambda a: pl.BlockSpec(a.shape, lambda b, g: (0,) * a.ndim)
    seq = pl.BlockSpec((T, NSA_HD), lambda b, g: (b, g))
    return pl.pallas_call(
        functools.partial(_cmp_prompt_kernel, nch=nch),
        grid=(bsz, G),
        in_specs=[seq, seq, whole(w1), whole(pos), whole(b1), whole(w2), whole(b2)],
        out_specs=[ospec, ospec],
        out_shape=[out, out],
        compiler_params=_cparams("parallel", "parallel"),
    )(kc, vc, w1, pos, b1, w2, b2)


def _overlap_t(nbs_pad, nbc_pad):
    j = _iota((nbs_pad, nbc_pad), 0)
    n = _iota((nbs_pad, nbc_pad), 1)
    lo = jnp.maximum(n * CMP_STRIDE, j * SEL_BLOCK)
    hi = jnp.minimum(n * CMP_STRIDE + CMP_BLOCK, j * SEL_BLOCK + SEL_BLOCK)
    return (jnp.maximum(hi - lo, 0) // CMP_STRIDE).astype(F32)


def _rank_select(score, n_sel, axis):
    n = score.shape[axis]
    idx = _iota(score.shape, axis)
    cnt = jnp.zeros(score.shape, F32)
    for jp in range(n):
        other = lax.slice_in_dim(score, jp, jp + 1, axis=axis)
        tie = jnp.where(idx > jp, 1.0, 0.0)
        cnt = cnt + jnp.where(other > score, 1.0, jnp.where(other == score, tie, 0.0))
    return jnp.where(cnt < n_sel, 1.0, 0.0)


def _attn_first(state, s, vt_b):
    m_ref, l_ref, acc_ref = state
    m = jnp.max(s, axis=0, keepdims=True)
    p = jnp.exp(s - m)
    m_ref[...] = m
    l_ref[...] = jnp.sum(p, axis=0, keepdims=True)
    acc_ref[...] = _dot(vt_b, p.astype(BF16))


def _attn_more(state, s, vt_b):
    m_ref, l_ref, acc_ref = state
    m_old = m_ref[...]
    m_new = jnp.maximum(m_old, jnp.max(s, axis=0, keepdims=True))
    alpha = jnp.exp(m_old - m_new)
    p = jnp.exp(s - m_new)
    m_ref[...] = m_new
    l_ref[...] = alpha * l_ref[...] + jnp.sum(p, axis=0, keepdims=True)
    acc_ref[...] = alpha * acc_ref[...] + _dot(vt_b, p.astype(BF16))


def _nsa_attn_kernel(q_ref, ck_ref, cv_ref, ks_ref, vs_ref, kw_ref, vw_ref, g_ref, bg_ref, o_ref,
                     ksb_ref, vst_ref, kwb_ref, vwt_ref, ms_ref, ls_ref, as_ref, mw_ref, lw_ref, aw_ref,
                     *, tq, hpg, nbc, nbs, nbs_pad):
    qi = pl.program_id(2)
    t0 = qi * tq
    cols = hpg * tq
    tk = tq
    n_kt = ks_ref.shape[0] // tk
    nbc_pad = ck_ref.shape[2]

    @pl.when(qi == 0)
    def _():
        ksb_ref[...] = ks_ref[...].astype(BF16)
        kwb_ref[...] = kw_ref[...].astype(BF16)
        for kt in range(n_kt):
            vst_ref[kt] = vs_ref[kt * tk:(kt + 1) * tk, :].T.astype(BF16)
            vwt_ref[kt] = vw_ref[kt * tk:(kt + 1) * tk, :].T.astype(BF16)

    qb = (jnp.concatenate([q_ref[:, h * NSA_HD:(h + 1) * NSA_HD] for h in range(hpg)], axis=0)
          * (NSA_HD ** -0.5)).astype(BF16)

    n = _iota((nbc_pad, cols), 0)
    t_col = t0 + jnp.concatenate([_iota((nbc_pad, tq), 1)] * hpg, axis=1)
    cmask = jnp.where(n < nbc, jnp.where(n * CMP_STRIDE + (CMP_BLOCK - 1) <= t_col, 1.0, 0.0), 0.0)
    sc = jnp.where(cmask > 0.5, _dot_nt(ck_ref[0, 0].astype(BF16), qb), NEG)
    pe = jnp.exp(sc - jnp.max(sc, axis=0, keepdims=True))
    p_cmp = pe / jnp.sum(pe, axis=0, keepdims=True) * cmask
    o_cmp = _dot(cv_ref[0, 0].T.astype(BF16), p_cmp.astype(BF16))

    p_sum = p_cmp[:, 0:tq]
    for h in range(1, hpg):
        p_sum = p_sum + p_cmp[:, h * tq:(h + 1) * tq]
    ov_t = _overlap_t(nbs_pad, nbc_pad).astype(BF16)
    p_hi, p_lo = _split_bf16(p_sum)
    imp_t = _dot(ov_t, p_hi) + _dot(ov_t, p_lo)
    j = _iota((nbs_pad, tq), 0)
    t = t0 + _iota((nbs_pad, tq), 1)
    t_blk = t // SEL_BLOCK
    forced = jnp.where(j == 0, 1.0, jnp.where(j == t_blk, 1.0, jnp.where(j == t_blk - 1, 1.0, 0.0)))
    visible = jnp.where(j < nbs, jnp.where(j * SEL_BLOCK <= t, 1.0, 0.0), 0.0)
    score = jnp.where(visible > 0.5, jnp.where(forced > 0.5, BIG, imp_t), NEG)
    sel_t = _rank_select(score, min(N_SELECT, nbs), axis=0).astype(BF16)

    k_ofs = _iota((tk, tq), 0)
    q_ofs = _iota((tk, tq), 1)
    causal = k_ofs <= q_ofs
    per_head = lambda bias: jnp.concatenate([bias] * hpg, axis=1)

    def scores(kb_ref, kt):
        k_b = kb_ref[pl.ds(pl.multiple_of(kt * tk, tk), tk), :]
        return _dot_nt(k_b, qb)

    def slc_bias(kt):
        blk = (kt * tk + _iota((tk, nbs_pad), 0)) // SEL_BLOCK
        expand = jnp.where(blk == _iota((tk, nbs_pad), 1), 1.0, 0.0).astype(BF16)
        return (_dot(expand, sel_t) - 1.0) * BIG

    slc = (ms_ref, ls_ref, as_ref)
    _attn_first(slc, scores(ksb_ref, qi) + per_head(jnp.where(causal, slc_bias(qi), NEG)), vst_ref[qi])

    def slc_body(kt, _):
        _attn_more(slc, scores(ksb_ref, kt) + per_head(slc_bias(kt)), vst_ref[kt])
        return 0

    lax.fori_loop(0, qi, slc_body, 0)

    n_back = WINDOW // tk
    win = (mw_ref, lw_ref, aw_ref)
    _attn_first(win, scores(kwb_ref, qi) + per_head(jnp.where(causal, 0.0, NEG)), vwt_ref[qi])
    for back in range(1, n_back + 1):
        @pl.when(qi >= back)
        def _():
            s = scores(kwb_ref, qi - back)
            if back == n_back:
                s = s + per_head(jnp.where(k_ofs >= q_ofs, 0.0, NEG))
            _attn_more(win, s, vwt_ref[qi - back])

    o_slc = as_ref[...] / ls_ref[...]
    o_win = aw_ref[...] / lw_ref[...]
    gate_t = _sigmoid(g_ref[...] + bg_ref[...]).T
    for h in range(hpg):
        sl = slice(h * tq, (h + 1) * tq)
        o_t = (gate_t[3 * h:3 * h + 1, :] * o_cmp[:, sl] + gate_t[3 * h + 1:3 * h + 2, :] * o_slc[:, sl]
               + gate_t[3 * h + 2:3 * h + 3, :] * o_win[:, sl])
        o_ref[:, h * NSA_HD:(h + 1) * NSA_HD] = o_t.T.astype(BF16)


def nsa_attn_prompt(q, ck, cv, ks, vs, kw, vw, gates, b_gate, bsz, T):
    G, HPG = NSA_KV, NSA_HPG
    tq = math.gcd(T, ATT_TILE)
    assert WINDOW % tq == 0
    nq = T // tq
    nch = T // CMP_STRIDE
    nbc = nch - CMP_BLOCK // CMP_STRIDE + 1
    nbs = -(-T // SEL_BLOCK)
    nbs_pad = -(-nbs // SUBLANES) * SUBLANES
    cols = HPG * tq
    kv_scratch = [pltpu.VMEM((T, NSA_HD), BF16), pltpu.VMEM((nq, NSA_HD, tq), BF16)] * 2
    stats = [pltpu.VMEM((1, cols), F32), pltpu.VMEM((1, cols), F32), pltpu.VMEM((NSA_HD, cols), F32)] * 2
    seq = pl.BlockSpec((T, NSA_HD), lambda b, g, i: (b, g))
    cspec = pl.BlockSpec((1, 1, nch, NSA_HD), lambda b, g, i: (b, g, 0, 0))
    qspec = pl.BlockSpec((tq, HPG * NSA_HD), lambda b, g, i: (b * nq + i, g))
    return pl.pallas_call(
        functools.partial(_nsa_attn_kernel, tq=tq, hpg=HPG, nbc=nbc, nbs=nbs, nbs_pad=nbs_pad),
        grid=(bsz, G, nq),
        in_specs=[qspec, cspec, cspec, seq, seq, seq, seq,
                  pl.BlockSpec((tq, LANES), lambda b, g, i: (b * nq + i, g)),
                  pl.BlockSpec((1, LANES), lambda b, g, i: (0, g))],
        out_specs=qspec,
        out_shape=jax.ShapeDtypeStruct((bsz * T, G * HPG * NSA_HD), BF16),
        scratch_shapes=kv_scratch + stats,
        compiler_params=_cparams("arbitrary", "arbitrary", "arbitrary"),
    )(q, ck, cv, ks, vs, kw, vw, gates, b_gate)


CHUNK_ROWS = CMP_STRIDE * NSA_KV
CHUNK_PITCH = CHUNK_ROWS + SUBLANES


def _page_cmp_kernel(tbl_ref, *refs, P):
    k_pages, v_pages = refs[:P], refs[P:2 * P]
    w1_ref, ak_ref, av_ref, pad_ref = refs[2 * P:]
    cpp = LANES // CMP_STRIDE
    for idx, (pages, a_ref) in enumerate(((k_pages, ak_ref), (v_pages, av_ref))):
        for i, pg in enumerate(pages):
            for c in range(cpp):
                pad_ref[idx * P + i, c * CHUNK_PITCH:c * CHUNK_PITCH + CHUNK_ROWS, :] = (
                    pg[0, c * CHUNK_ROWS:(c + 1) * CHUNK_ROWS, :])
        blocks = []
        for g in range(NSA_KV):
            for i in range(P):
                blocks.append(jnp.concatenate(
                    [pad_ref[idx * P + i, pl.ds(p * NSA_KV + g, cpp, stride=CHUNK_PITCH), :]
                     for p in range(CMP_STRIDE)], axis=1))
        a = _dot(jnp.concatenate(blocks, axis=0).astype(BF16), w1_ref[idx])
        per_g = P * cpp
        for g in range(NSA_KV):
            a_ref[0, g] = a[g * per_g:(g + 1) * per_g]


def page_cmp(pool_k, pool_v, table, w1):
    n_pool, page, G, hd = pool_k.shape
    assert page == LANES and G == NSA_KV and hd == NSA_HD
    bsz, n_pages = table.shape
    P = math.gcd(n_pages, PAGES_PER_STEP)
    cpp = page // CMP_STRIDE
    hid2 = w1.shape[2]
    pk = pool_k.reshape(n_pool, page * G, hd)
    pv = pool_v.reshape(n_pool, page * G, hd)
    pspec = lambda i: pl.BlockSpec((1, page * G, hd),
                                   functools.partial(lambda b, s, tbl, i: (tbl[b * n_pages + s * P + i], 0, 0), i=i))
    out = jax.ShapeDtypeStruct((bsz, G, n_pages * cpp, hid2), F32)
    ospec = pl.BlockSpec((1, G, P * cpp, hid2), lambda b, s, tbl: (b, 0, s, 0))
    grid_spec = pltpu.PrefetchScalarGridSpec(
        num_scalar_prefetch=1,
        grid=(bsz, n_pages // P),
        in_specs=[pspec(i) for i in range(P)] * 2 + [pl.BlockSpec(w1.shape, lambda b, s, tbl: (0, 0, 0))],
        out_specs=[ospec, ospec],
        scratch_shapes=[pltpu.VMEM((2 * P, cpp * CHUNK_PITCH, hd), F32)],
    )
    return pl.pallas_call(
        functools.partial(_page_cmp_kernel, P=P),
        grid_spec=grid_spec,
        out_shape=[out, out],
        compiler_params=_cparams("parallel", "parallel"),
    )(table.reshape(-1), *([pk] * P), *([pv] * P), w1)


def _dec_cmp_kernel(ak_ref, av_ref, q_ref, pos_ref, w1_ref, b1_ref, w2_ref, b2_ref, o_ref, sel_ref,
                    *, nbc, nbs, nbs_pad, p0):
    nch = ak_ref.shape[2]
    for g in range(NSA_KV):
        cks = []
        for idx, a_ref in enumerate((ak_ref, av_ref)):
            posw = _dot(pos_ref[idx], w1_ref[idx])
            cks.append(_cmp_finish(a_ref[0, g], posw, b1_ref[idx], w2_ref[idx], b2_ref[idx]))
        ck, cv = cks
        qb = (q_ref[0, g] * (NSA_HD ** -0.5)).astype(BF16)
        n = _iota((SUBLANES, nch), 1)
        cmask = jnp.where(n < nbc, jnp.where(n * CMP_STRIDE + (CMP_BLOCK - 1) <= p0, 1.0, 0.0), 0.0)
        sc = jnp.where(cmask > 0.5, _dot_nt(qb, ck.astype(BF16)), NEG)
        pe = jnp.exp(sc - jnp.max(sc, axis=-1, keepdims=True))
        p_cmp = pe / jnp.sum(pe, axis=-1, keepdims=True) * cmask
        o_ref[0, g] = _dot(p_cmp.astype(BF16), cv.astype(BF16))

        head = jnp.where(_iota((SUBLANES, nch), 0) < NSA_HPG, p_cmp, 0.0)
        p_sum = jnp.broadcast_to(jnp.sum(head, axis=0, keepdims=True), (SUBLANES, nch))
        ov_t = _overlap_t(nbs_pad, nch).astype(BF16)
        p_hi, p_lo = _split_bf16(p_sum)
        imp = _dot_nt(p_hi, ov_t) + _dot_nt(p_lo, ov_t)
        j = _iota((SUBLANES, nbs_pad), 1)
        t_blk = p0 // SEL_BLOCK
        forced = jnp.where(j == 0, 1.0, jnp.where(j == t_blk, 1.0, jnp.where(j == t_blk - 1, 1.0, 0.0)))
        visible = jnp.where(j < nbs, jnp.where(j * SEL_BLOCK <= p0, 1.0, 0.0), 0.0)
        score = jnp.where(visible > 0.5, jnp.where(forced > 0.5, BIG, imp), NEG)
        s_row = jnp.broadcast_to(score[0:1], (nbs_pad, nbs_pad))
        s_col = jnp.concatenate([jnp.broadcast_to(score[0:1], (LANES, nbs_pad)).T] * (nbs_pad // LANES), axis=1)
        jr = _iota((nbs_pad, nbs_pad), 0)
        jc = _iota((nbs_pad, nbs_pad), 1)
        tie = jnp.where(jc < jr, 1.0, 0.0)
        beats = jnp.where(s_row > s_col, 1.0, jnp.where(s_row == s_col, tie, 0.0))
        rank = jnp.sum(beats, axis=-1, keepdims=True)
        slot = _iota((nbs_pad, LANES), 1).astype(F32)
        blk = _iota((nbs_pad, LANES), 0).astype(F32)
        picked = jnp.sum(jnp.where(rank == slot, blk, 0.0), axis=0, keepdims=True)
        sel_ref[0, g:g + 1, :] = picked.astype(jnp.int32)


def dec_cmp(ak, av, q, cw, p0):
    w1, pos, b1, w2, b2 = cw
    bsz, G, nch, hid2 = ak.shape
    nbc = nch - CMP_BLOCK // CMP_STRIDE + 1
    nbs = -(-(p0 + 1) // SEL_BLOCK)
    nbs_pad = -(-nbs // LANES) * LANES
    whole = lambda a: pl.BlockSpec(a.shape, lambda b: (0,) * a.ndim)
    aspec = pl.BlockSpec((1, G, nch, hid2), lambda b: (b, 0, 0, 0))
    return pl.pallas_call(
        functools.partial(_dec_cmp_kernel, nbc=nbc, nbs=nbs, nbs_pad=nbs_pad, p0=p0),
        grid=(bsz,),
        in_specs=[aspec, aspec, pl.BlockSpec((1, G, SUBLANES, NSA_HD), lambda b: (b, 0, 0, 0)),
                  whole(pos), whole(w1), whole(b1), whole(w2), whole(b2)],
        out_specs=[pl.BlockSpec((1, G, SUBLANES, NSA_HD), lambda b: (b, 0, 0, 0)),
                   pl.BlockSpec((1, G, LANES), lambda b: (b, 0, 0))],
        out_shape=[jax.ShapeDtypeStruct((bsz, G, SUBLANES, NSA_HD), F32),
                   jax.ShapeDtypeStruct((bsz, G, LANES), jnp.int32)],
        compiler_params=_cparams("parallel"),
    )(ak, av, q, pos, w1, b1, w2, b2)


def _softmax_with_new_key(qb, k_b, v_b, bias, s_new, v_new):
    s = _dot_nt(qb, k_b)
    if bias is not None:
        s = s + bias
    mx = jnp.maximum(jnp.max(s, axis=-1, keepdims=True), s_new)
    p = jnp.exp(s - mx)
    p_new = jnp.exp(s_new - mx)
    return (_dot(p.astype(BF16), v_b) + p_new * v_new) / (jnp.sum(p, axis=-1, keepdims=True) + p_new)


def _dec_attn_kernel(sel_ref, tbl_ref, *refs, n_cache_blocks, n_sel):
    ks_refs, vs_refs = refs[:n_sel], refs[n_sel:2 * n_sel]
    q_ref, new_ref, kw_ref, vw_ref, oc_ref, g_ref, bg_ref, o_ref = refs[2 * n_sel:]
    b, g = pl.program_id(0), pl.program_id(1)
    q = q_ref[0, 0] * (NSA_HD ** -0.5)
    qb = q.astype(BF16)
    new = new_ref[0, 0]

    mine = lambda r, n: r[pl.ds(g, n, stride=NSA_KV), :]
    k_all = jnp.concatenate([mine(r.at[0, 0], SEL_BLOCK) for r in ks_refs], axis=0).astype(BF16)
    v_all = jnp.concatenate([mine(r.at[0, 0], SEL_BLOCK) for r in vs_refs], axis=0).astype(BF16)
    slot = _iota((SUBLANES, n_sel * SEL_BLOCK), 1) // SEL_BLOCK
    bias = jnp.zeros((SUBLANES, n_sel * SEL_BLOCK), F32)
    for s in range(n_sel):
        blk = sel_ref[(b * NSA_KV + g) * n_sel + s]
        bias = jnp.where(slot == s, jnp.where(blk < n_cache_blocks, 0.0, NEG), bias)
    o_slc = _softmax_with_new_key(qb, k_all, v_all, bias, jnp.sum(q * new[0:1], axis=-1, keepdims=True), new[1:2])

    n_win = kw_ref.shape[1] // NSA_KV
    o_win = _softmax_with_new_key(qb, mine(kw_ref.at[0], n_win).astype(BF16), mine(vw_ref.at[0], n_win).astype(BF16),
                                  None,
                                  jnp.sum(q * new[2:3], axis=-1, keepdims=True), new[3:4])

    gate = jnp.broadcast_to(_sigmoid(g_ref[0, 0] + bg_ref[...]), (SUBLANES, LANES))
    head = _iota((SUBLANES, LANES), 0)
    lane = _iota((SUBLANES, LANES), 1)
    o = jnp.zeros((SUBLANES, NSA_HD), F32)
    for br, ob in enumerate((oc_ref[0, 0], o_slc, o_win)):
        o = o + jnp.sum(jnp.where(lane == 3 * head + br, gate, 0.0), axis=-1, keepdims=True) * ob
    o_ref[0, 0] = o


def dec_attn(sel, table, pool_ks, pool_vs, q, new, kw_past, vw_past, o_cmp, gates, b_gate):
    n_pool, page, G, hd = pool_ks.shape
    bsz, n_pages = table.shape
    halves = page // SEL_BLOCK
    wb = kw_past.shape[1]
    assert wb <= WINDOW
    n_sel = sel.shape[-1]
    pk = pool_ks.reshape(n_pool, halves, SEL_BLOCK * G, hd)
    pv = pool_vs.reshape(n_pool, halves, SEL_BLOCK * G, hd)
    n_cache_blocks = n_pages * halves

    def page_map(b, g, sel_ref, tbl_ref, s):
        blk = jnp.minimum(sel_ref[(b * G + g) * n_sel + s], n_cache_blocks - 1)
        return (tbl_ref[b * n_pages + blk // halves], blk % halves, 0, 0)

    bg_map = lambda b, g, sel_ref, tbl_ref: (b, g, 0, 0)
    small = pl.BlockSpec((1, 1, SUBLANES, hd), bg_map)
    wspec = pl.BlockSpec((1, wb * G, hd), lambda b, g, sel_ref, tbl_ref: (b, 0, 0))
    blocks = [pl.BlockSpec((1, 1, SEL_BLOCK * G, hd), functools.partial(page_map, s=s)) for s in range(n_sel)]
    grid_spec = pltpu.PrefetchScalarGridSpec(
        num_scalar_prefetch=2,
        grid=(bsz, G),
        in_specs=blocks + blocks + [small, small, wspec, wspec, small,
                                    pl.BlockSpec((1, 1, 1, LANES), bg_map),
                                    pl.BlockSpec((1, LANES), lambda b, g, sel_ref, tbl_ref: (0, g))],
        out_specs=small,
    )
    return pl.pallas_call(
        functools.partial(_dec_attn_kernel, n_cache_blocks=n_cache_blocks, n_sel=n_sel),
        grid_spec=grid_spec,
        out_shape=jax.ShapeDtypeStruct((bsz, G, SUBLANES, hd), F32),
        compiler_params=_cparams("parallel", "parallel"),
    )(sel.reshape(-1), table.reshape(-1), *([pk] * n_sel), *([pv] * n_sel), q, new,
      kw_past.reshape(bsz, wb * G, hd), vw_past.reshape(bsz, wb * G, hd), o_cmp,
      gates.reshape(bsz, G, 1, LANES), b_gate)


def _gate_weights(wt, layer, row0, n):
    rows = lax.slice(wt, (layer, row0, 0), (layer + 1, row0 + n, wt.shape[2]))
    return jnp.pad(rows, ((0, 0), (0, LANES - n), (0, 0))), 0


def _nsa_gate_weights(wt, layer, b_gate, row0):
    per = 3 * NSA_HPG
    k = wt.shape[2]
    rows = lax.slice(wt, (layer, row0, 0), (layer + 1, row0 + NSA_KV * per, k)).reshape(NSA_KV, per, k)
    wg = jnp.pad(rows, ((0, 0), (0, LANES - per), (0, 0))).reshape(1, NSA_KV * LANES, k)
    bg = jnp.pad(b_gate.reshape(NSA_KV, per), ((0, 0), (0, LANES - per))).reshape(1, NSA_KV * LANES)
    return (wg, 0), bg


def kernel(x_prompt, x_sample, state_a_C, state_a_n, state_a_m, state_b_conv, cache_c_k_cmp, cache_c_v_cmp, cache_c_k_slc, cache_c_v_slc, cache_c_k_win, cache_c_v_win, page_table, norm_w, ffn_w_in, ffn_w_out, a_w_in, a_b_gates, a_head_norm, a_w_out, b_w_in, b_conv_w, b_w_out, c_w_in, c_b_gate, c_cmp_pos, c_cmp_w1, c_cmp_b1, c_cmp_w2, c_cmp_b2, c_w_out):
    bp, T, D = x_prompt.shape
    bs = x_sample.shape[0]
    assert x_sample.shape[1] == 1
    depth = norm_w.shape[0]
    G, hd = NSA_KV, NSA_HD
    past_len = page_table.shape[1] * cache_c_k_cmp.shape[2]
    xp = x_prompt.reshape(bp * T, D)
    xs = x_sample.reshape(bs, D)
    hp_in = norm_cast(xp, norm_w[0, 0])
    hs_in = norm_cast(xs, norm_w[0, 0])
    a_wt = jnp.swapaxes(a_w_in, 1, 2)
    c_wt = jnp.swapaxes(c_w_in, 1, 2)
    w_outs = {0: a_w_out.astype(BF16), 1: b_w_out.astype(BF16), 2: c_w_out.astype(BF16)}
    f_outs = ffn_w_out.astype(BF16)
    pa, sa, pb, sb, pc, sc = [], [], [], [], [], []
    for i in range(depth):
        kind, j = i % 3, i // 3
        nw = norm_w[i]
        w_out = (w_outs[kind], j)
        if kind == 0:
            hdv = a_head_norm.shape[1]
            n_main = 3 * hdv
            wg = _gate_weights(a_wt, j, n_main, 2 * MLSTM_HEADS)
            qkvo_p, qkvo_s = proj_plain(hp_in, hs_in, (a_wt, j), 0, n_main, 1024, transposed=True)
            g_p, g_s = proj_plain(hp_in, hs_in, wg, 0, LANES, LANES, transposed=True)
            hp, c_p, n_p, m_p = mlstm_prompt(qkvo_p, g_p, a_b_gates[j], a_head_norm[j], bp, T)
            hs, c_s, n_s, m_s = mlstm_step(qkvo_s, g_s, a_b_gates[j], a_head_norm[j],
                                           state_a_C[j], state_a_n[j], state_a_m[j])
            pa.append((c_p, n_p, m_p[:, :, 0]))
            sa.append((c_s, n_s, m_s))
        elif kind == 1:
            (bg_p, z_p), (bg_s, z_s) = proj_conv_in(hp_in, hs_in, (b_w_in, j), D)
            hp = conv_gate(bg_p, z_p, b_conv_w[j], T)
            buf = state_b_conv[j]
            hs = conv_gate_step(bg_s, z_s, buf.transpose(1, 0, 2), b_conv_w[j]).astype(BF16)
            pb.append(z_p.reshape(bp, T, D)[:, T - (CONV_W - 1):])
            sb.append(jnp.concatenate([buf, z_s[:, None]], axis=1)[:, -(CONV_W - 1):])
        else:
            nq_cols = NSA_HPG * G * hd
            nheads = G * NSA_HPG
            wg, bgate = _nsa_gate_weights(c_wt, j, c_b_gate[j], nq_cols + 6 * G * hd)
            cw = _cmp_weights(c_cmp_pos[j], c_cmp_w1[j], c_cmp_b1[j], c_cmp_w2[j], c_cmp_b2[j])
            q_p, q_s = proj_plain(hp_in, hs_in, (c_wt, j), 0, nq_cols, 1024, transposed=True)
            kv_p, kv_rows_p, kv_s = [], [], []
            for half in range(2):
                (tok_p, rows_p), (tok_s, _) = proj_groups(hp_in, hs_in, (c_wt, j), nq_cols + half * 3 * G * hd,
                                                         3, G * hd, transposed=True)
                kv_p, kv_rows_p, kv_s = kv_p + list(tok_p), kv_rows_p + list(rows_p), kv_s + list(tok_s)
            gt_p, gt_s = proj_plain(hp_in, hs_in, wg, 0, G * LANES, G * LANES, transposed=True)
            ck, cv = cmp_prompt(kv_p[0], kv_p[1], bp, T, cw)
            hp = nsa_attn_prompt(q_p, ck, cv, kv_p[2], kv_p[3], kv_p[4], kv_p[5], gt_p, bgate, bp, T)
            kv_rows_p = [a.reshape(bp, T, G, hd) for a in kv_rows_p]
            keep = min(WINDOW, T)
            pc.append(tuple(kv_rows_p[:4]) + tuple(a[:, T - keep:] for a in kv_rows_p[4:]))
            q_s = jnp.pad(q_s.reshape(bs, G, NSA_HPG, hd), ((0, 0), (0, 0), (0, SUBLANES - NSA_HPG), (0, 0)))
            kv_s = jnp.stack([a.reshape(bs, G, hd) for a in kv_s], axis=1)
            new = jnp.pad(kv_s[:, 2:6].transpose(0, 2, 1, 3), ((0, 0), (0, 0), (0, SUBLANES - 4), (0, 0)))
            ak, av = page_cmp(cache_c_k_cmp[j], cache_c_v_cmp[j], page_table, cw[0])
            o_cmp, ranked = dec_cmp(ak, av, q_s, cw, past_len)
            n_sel = min(N_SELECT, -(-(past_len + 1) // SEL_BLOCK))
            o_s = dec_attn(ranked[:, :, :n_sel], page_table, cache_c_k_slc[j], cache_c_v_slc[j], q_s, new,
                           cache_c_k_win[j], cache_c_v_win[j], o_cmp, gt_s, bgate)
            hs = o_s[:, :, :NSA_HPG].reshape(bs, nheads * hd).astype(BF16)
            keep_s = min(WINDOW, cache_c_k_win.shape[2] + 1)
            win = [jnp.concatenate([c[j], kv_s[:, a][:, None]], axis=1)[:, -keep_s:]
                   for c, a in ((cache_c_k_win, 4), (cache_c_v_win, 5))]
            sc.append(tuple(kv_s[:, a][:, None] for a in range(4)) + tuple(win))
        last = i == depth - 1
        nw_next = norm_w[i + 1, 0] if not last else nw[0]
        (xp, xs), (hp_mid, hs_mid) = proj_out(hp, hs, w_out, xp, xs, nw[1], nw[2])
        act_p, act_s = proj_swiglu(hp_mid, hs_mid, (ffn_w_in, i), f_outs.shape[1])
        (xp, xs), (hp_in, hs_in) = proj_out(act_p, act_s, (f_outs, i), xp, xs, nw[3], nw_next, not last)

    stack = lambda items: [jnp.stack(a) for a in zip(*items)]
    p_a, s_a = stack(pa), stack(sa)
    p_c, s_c = stack(pc), stack(sc)
    return (xp.reshape(bp, T, D), xs.reshape(bs, 1, D), *p_a, *s_a, jnp.stack(pb), jnp.stack(sb), *p_c, *s_c)
```

```python
import functools
import math

import jax
import jax.numpy as jnp
from jax import lax
from jax.experimental import pallas as pl
from jax.experimental.pallas import tpu as pltpu

F32 = jnp.float32
BF16 = jnp.bfloat16

EPS = 1e-6
NEG = -1e30
BIG = 1e30

LANES = 128
SUBLANES = 8
VMEM_LIMIT = 56 * 1024 * 1024

MLSTM_HEADS = 8
MLSTM_CHUNK = 256
FORGET_BIAS_COL = MLSTM_HEADS
CONV_W = 3
NSA_HD = 128
NSA_KV = 4
NSA_HPG = 4
CMP_BLOCK = 32
CMP_STRIDE = 16
SEL_BLOCK = 64
N_SELECT = 16
WINDOW = 512
ATT_TILE = 256
PAGES_PER_STEP = 16


def _cparams(*sem):
    return pltpu.CompilerParams(dimension_semantics=sem, vmem_limit_bytes=VMEM_LIMIT)


def _dot(a, b):
    return jnp.dot(a, b, preferred_element_type=F32)


def _dot_nt(a, b):
    return lax.dot_general(a, b, (((1,), (1,)), ((), ())), preferred_element_type=F32)


def _dot_tn(a, b):
    return lax.dot_general(a, b, (((0,), (0,)), ((), ())), preferred_element_type=F32)


def _split_bf16(x):
    hi = x.astype(BF16)
    lo = (x - hi.astype(F32)).astype(BF16)
    return hi, lo


def _iota(shape, dim):
    return lax.broadcasted_iota(jnp.int32, shape, dim)


def _sigmoid(x):
    return 1.0 / (1.0 + jnp.exp(-x))


def _rms(x):
    return x * lax.rsqrt(jnp.mean(x * x, axis=-1, keepdims=True) + EPS)


def _row_tile(m, want):
    t = min(m, want)
    assert m % t == 0, (m, t)
    return t


def _norm_kernel(x_ref, w_ref, o_ref):
    o_ref[...] = (_rms(x_ref[...]) * w_ref[...]).astype(BF16)


def norm_cast(x, w):
    m, d = x.shape
    tm = _row_tile(m, 512)
    return pl.pallas_call(
        _norm_kernel,
        grid=(m // tm,),
        in_specs=[pl.BlockSpec((tm, d), lambda i: (i, 0)), pl.BlockSpec((1, d), lambda i: (0, 0))],
        out_specs=pl.BlockSpec((tm, d), lambda i: (i, 0)),
        out_shape=jax.ShapeDtypeStruct((m, d), BF16),
        compiler_params=_cparams("parallel"),
    )(x, w.reshape(1, d))


def _proj_kernel(x_ref, xs_ref, *refs, n_groups, n_out, epilogue, transposed, with_cast):
    w_refs = refs[:n_groups]
    refs = refs[n_groups:]
    if with_cast:
        cast_in_ref, refs = refs[0], refs[1:]
        refs[2 * n_out][...] = cast_in_ref[...].astype(BF16)
    out_refs = refs[:n_out]
    sample_out_refs = refs[n_out:2 * n_out]
    wb_ref = refs[-1]
    mm = _dot_nt if transposed else _dot

    @pl.when(pl.program_id(1) == 0)
    def _():
        for g in range(n_groups):
            wb_ref[g] = w_refs[g][...].astype(BF16)
        xs = xs_ref[...]
        epilogue([mm(xs, wb_ref[g]) for g in range(n_groups)], sample_out_refs)

    x = x_ref[...]
    epilogue([mm(x, wb_ref[g]) for g in range(n_groups)], out_refs)


def _proj(x, xs, w, *, col0, tn, n_tiles, group_stride, n_groups, epilogue, outs, tm_want=1024, transposed=False,
          cast=None):
    w, layer = w
    m, k = x.shape
    s = xs.shape[0]
    tm = _row_tile(m, tm_want)
    assert col0 % tn == 0
    b0 = col0 // tn
    if transposed:
        w_block = (None, tn, k)
        w_map = lambda j, i, off: (layer, off + j, 0)
    else:
        w_block = (None, k, tn)
        w_map = lambda j, i, off: (layer, 0, off + j)
    w_specs = [pl.BlockSpec(w_block, functools.partial(w_map, off=b0 + g * group_stride)) for g in range(n_groups)]
    n_out = len(outs)

    def out_for(rows, row_tile, row_map, n, dt, by_rows):
        if not by_rows:
            return jax.ShapeDtypeStruct((rows, n), dt), pl.BlockSpec((row_tile, tn), row_map)
        assert n_tiles == 1 and n == tn
        per = n // LANES
        return (jax.ShapeDtypeStruct((rows * per, LANES), dt),
                pl.BlockSpec((row_tile * per, LANES), lambda j, i: (row_map(j, i)[0], 0)))

    prompt_outs = [out_for(m, tm, lambda j, i: (i, j), *o) for o in outs]
    sample_outs = [out_for(s, s, lambda j, i: (0, j), *o) for o in outs]
    all_outs = prompt_outs + sample_outs
    operands = [x, xs] + [w] * n_groups
    in_specs = [pl.BlockSpec((tm, k), lambda j, i: (i, 0)), pl.BlockSpec((s, k), lambda j, i: (0, 0))] + w_specs
    n_i = m // tm
    if cast is not None:
        cw, c_layer = cast
        _, c_rows, c_cols = cw.shape
        bf16_rows = 2 * SUBLANES
        blk = next(r for r in range(bf16_rows, c_rows + 1, bf16_rows)
                   if c_rows % r == 0 and c_rows // r <= n_tiles * n_i)
        last = c_rows // blk - 1
        step = lambda j, i: jnp.minimum(j * n_i + i, last)
        in_specs.append(pl.BlockSpec((None, blk, c_cols), lambda j, i: (c_layer, step(j, i), 0)))
        operands.append(cw)
        all_outs.append((jax.ShapeDtypeStruct((c_rows, c_cols), BF16),
                         pl.BlockSpec((blk, c_cols), lambda j, i: (step(j, i), 0))))
    res = pl.pallas_call(
        functools.partial(_proj_kernel, n_groups=n_groups, n_out=n_out, epilogue=epilogue, transposed=transposed,
                          with_cast=cast is not None),
        grid=(n_tiles, n_i),
        in_specs=in_specs,
        out_specs=[spec for _, spec in all_outs],
        out_shape=[shape for shape, _ in all_outs],
        scratch_shapes=[pltpu.VMEM((n_groups,) + w_block[1:], BF16)],
        compiler_params=_cparams("arbitrary", "arbitrary"),
    )(*operands)
    return res[:n_out], res[n_out:2 * n_out], (res[2 * n_out] if cast is not None else None)


def _ep_plain(accs, outs):
    for acc, o in zip(accs, outs):
        o[...] = acc


def proj_plain(x, xs, w, col0, n_cols, tn, transposed=False, cast=None):
    p, s, c = _proj(x, xs, w, col0=col0, tn=tn, n_tiles=n_cols // tn, group_stride=0, n_groups=1,
                    epilogue=_ep_plain, outs=[(n_cols, F32, False)], transposed=transposed, cast=cast)
    return (p[0], s[0]) if cast is None else (p[0], s[0], c)


def _ep_both_layouts(accs, outs):
    n = len(accs)
    for acc, o_tok, o_rows in zip(accs, outs[:n], outs[n:]):
        o_tok[...] = acc
        per = acc.shape[1] // LANES
        rows = acc.shape[0]
        for c in range(per):
            o_rows[pl.ds(c, rows, stride=per), :] = acc[:, c * LANES:(c + 1) * LANES]


def proj_groups(x, xs, w, col0, n_groups, group_cols, transposed=False):
    outs = [(group_cols, F32, False)] * n_groups + [(group_cols, F32, True)] * n_groups
    p, s, _ = _proj(x, xs, w, col0=col0, tn=group_cols, n_tiles=1, group_stride=1, n_groups=n_groups,
                    epilogue=_ep_both_layouts, outs=outs, tm_want=512, transposed=transposed)
    return (p[:n_groups], p[n_groups:]), (s[:n_groups], s[n_groups:])


def _ep_swiglu(accs, outs):
    g, u = accs
    outs[0][...] = (g * _sigmoid(g) * u).astype(BF16)


def proj_swiglu(x, xs, w, ff, cast):
    tn = 512
    p, s, c = _proj(x, xs, w, col0=0, tn=tn, n_tiles=ff // tn, group_stride=ff // tn, n_groups=2,
                    epilogue=_ep_swiglu, outs=[(ff, BF16, False)], cast=cast)
    return p[0], s[0], c


def _ep_conv_in(accs, outs):
    bg, cg, u = accs
    outs[0][...] = bg
    outs[1][...] = cg * u


def proj_conv_in(x, xs, w, d, cast):
    tn = 512
    return _proj(x, xs, w, col0=0, tn=tn, n_tiles=d // tn, group_stride=d // tn, n_groups=3,
                 epilogue=_ep_conv_in, outs=[(d, F32, False), (d, F32, False)], cast=cast)


def _out_kernel(h_ref, hs_ref, w_ref, r_ref, rs_ref, nwa_ref, nwb_ref, *out_refs, emit_next):
    def finalize(y, r, x_ref, xn_ref):
        x_new = r + _rms(y) * nwa_ref[...]
        x_ref[...] = x_new
        if emit_next:
            xn_ref[...] = (_rms(x_new) * nwb_ref[...]).astype(BF16)

    if emit_next:
        x_ref, xn_ref, xs_ref, xsn_ref = out_refs
    else:
        (x_ref, xs_ref), xn_ref, xsn_ref = out_refs, None, None

    @pl.when(pl.program_id(0) == 0)
    def _():
        finalize(_dot(hs_ref[...], w_ref[...]), rs_ref[...], xs_ref, xsn_ref)

    finalize(_dot(h_ref[...], w_ref[...]), r_ref[...], x_ref, xn_ref)


def proj_out(h, hs, w, resid, resid_s, nw_post, nw_next, emit_next=True):
    m, k = h.shape
    s = hs.shape[0]
    d = w.shape[1]
    tm = _row_tile(m, 512 if k <= 2048 else 256)
    row = lambda i: (i, 0)
    fixed = lambda i: (0, 0)
    f32_out = [jax.ShapeDtypeStruct((m, d), F32), jax.ShapeDtypeStruct((s, d), F32)]
    bf16_out = [jax.ShapeDtypeStruct((m, d), BF16), jax.ShapeDtypeStruct((s, d), BF16)]
    specs = [pl.BlockSpec((tm, d), row), pl.BlockSpec((s, d), fixed)]
    if emit_next:
        out_shape = [f32_out[0], bf16_out[0], f32_out[1], bf16_out[1]]
        out_specs = [specs[0], specs[0], specs[1], specs[1]]
    else:
        out_shape, out_specs = f32_out, specs
    res = pl.pallas_call(
        functools.partial(_out_kernel, emit_next=emit_next),
        grid=(m // tm,),
        in_specs=[pl.BlockSpec((tm, k), row),
                  pl.BlockSpec((s, k), fixed),
                  pl.BlockSpec((k, d), fixed, pipeline_mode=pl.Buffered(1)),
                  pl.BlockSpec((tm, d), row),
                  pl.BlockSpec((s, d), fixed),
                  pl.BlockSpec((1, d), fixed),
                  pl.BlockSpec((1, d), fixed)],
        out_specs=out_specs,
        out_shape=out_shape,
        compiler_params=_cparams("arbitrary"),
    )(h, hs, w, resid, resid_s, nw_post.reshape(1, d), nw_next.reshape(1, d))
    if emit_next:
        return (res[0], res[2]), (res[1], res[3])
    return (res[0], res[1]), (None, None)


def _log_sigmoid(x):
    return jnp.minimum(x, 0.0) - jnp.log1p(jnp.exp(-jnp.abs(x)))


def _mlstm_kernel(q_ref, k_ref, v_ref, o_ref, g_ref, bg_ref, hn_ref, hg_ref, c_ref, n_ref, m_ref, *, L, H, DK, DV):
    @pl.when(pl.program_id(1) == 0)
    def _():
        c_ref[...] = jnp.zeros_like(c_ref)
        n_ref[...] = jnp.zeros_like(n_ref)
        m_ref[...] = jnp.zeros_like(m_ref)

    gpre = g_ref[...] + bg_ref[...]
    lf = _log_sigmoid(gpre)
    rows = _iota((L, L), 0)
    cols = _iota((L, L), 1)
    causal = rows >= cols
    tril = jnp.where(causal, 1.0, 0.0).astype(BF16)
    lf_hi, lf_lo = _split_bf16(lf)
    bcum = _dot(tril, lf_hi) + _dot(tril, lf_lo)
    g_t = gpre.T
    b_t = bcum.T
    for h in range(H):
        f = FORGET_BIAS_COL + h
        a_col = bcum[:, f:f + 1]
        i_col = gpre[:, h:h + 1]
        b_row = b_t[f:f + 1, :]
        i_row = g_t[h:h + 1, :]
        m_prev = m_ref[0, h:h + 1, 0:1]
        log_d = jnp.where(causal, a_col - b_row + i_row, NEG)
        log_inter = a_col + m_prev
        m_t = jnp.maximum(log_inter, jnp.max(log_d, axis=-1, keepdims=True))
        d = jnp.exp(log_d - m_t)
        inter = jnp.exp(log_inter - m_t)
        qh = q_ref[:, h * DK:(h + 1) * DK]
        kh = k_ref[:, h * DK:(h + 1) * DK] * (DK ** -0.5)
        vb = v_ref[:, h * DV:(h + 1) * DV].astype(BF16)
        qb = qh.astype(BF16)
        s = _dot_nt(qb, kh.astype(BF16)) * d
        c_old = c_ref[0, h]
        n_old = n_ref[0, h:h + 1, :]
        num = _dot(s.astype(BF16), vb) + inter * _dot(qb, c_old.astype(BF16))
        den = jnp.sum(s, axis=-1, keepdims=True) + inter * jnp.sum(qh * n_old, axis=-1, keepdims=True)
        hh = _rms(num / jnp.maximum(jnp.abs(den), jnp.exp(-m_t)))
        gate = _sigmoid(o_ref[:, h * DV:(h + 1) * DV])
        hg_ref[:, h * DV:(h + 1) * DV] = (hh * hn_ref[:, h * DV:(h + 1) * DV] * gate).astype(BF16)
        m_new = m_t[L - 1:L, :]
        w_col = jnp.exp(a_col[L - 1:L, :] - a_col + i_col - m_new)
        decay = jnp.exp(log_inter[L - 1:L, :] - m_new)
        kw = kh * w_col
        c_ref[0, h] = decay * c_old + _dot_tn(kw.astype(BF16), vb)
        n_ref[0, h:h + 1, :] = decay * n_old + jnp.sum(kw, axis=0, keepdims=True)
        m_ref[0, h:h + 1, :] = jnp.broadcast_to(m_new, (1, LANES))


def mlstm_prompt(qkvo, gates, b_gates, head_norm, bsz, T):
    H = MLSTM_HEADS
    m = qkvo.shape[0]
    hdv = qkvo.shape[1] // 3
    hdk = hdv // 2
    DK, DV = hdk // H, hdv // H
    L = math.gcd(T, MLSTM_CHUNK)
    nc = T // L
    row = lambda b, c: (b * nc + c, 0)
    state = lambda b, c: (b, 0, 0)
    return pl.pallas_call(
        functools.partial(_mlstm_kernel, L=L, H=H, DK=DK, DV=DV),
        grid=(bsz, nc),
        in_specs=[pl.BlockSpec((L, hdk), row),
                  pl.BlockSpec((L, hdk), lambda b, c: (b * nc + c, 1)),
                  pl.BlockSpec((L, hdv), lambda b, c: (b * nc + c, 1)),
                  pl.BlockSpec((L, hdv), lambda b, c: (b * nc + c, 2)),
                  pl.BlockSpec((L, LANES), row),
                  pl.BlockSpec((1, LANES), lambda b, c: (0, 0)),
                  pl.BlockSpec((1, hdv), lambda b, c: (0, 0))],
        out_specs=[pl.BlockSpec((L, hdv), row),
                   pl.BlockSpec((1, H, DK, DV), lambda b, c: (b, 0, 0, 0)),
                   pl.BlockSpec((1, H, DK), state),
                   pl.BlockSpec((1, H, LANES), state)],
        out_shape=[jax.ShapeDtypeStruct((m, hdv), BF16),
                   jax.ShapeDtypeStruct((bsz, H, DK, DV), F32),
                   jax.ShapeDtypeStruct((bsz, H, DK), F32),
                   jax.ShapeDtypeStruct((bsz, H, LANES), F32)],
        compiler_params=_cparams("parallel", "arbitrary"),
    )(qkvo, qkvo, qkvo, qkvo, gates, _pad_lanes(b_gates.reshape(1, -1)), head_norm.reshape(1, hdv))


def _pad_lanes(x, width=LANES):
    return jnp.pad(x, ((0, 0), (0, width - x.shape[-1])))


def _to_col(row, n):
    eye = _iota((n, n), 0) == _iota((n, n), 1)
    return jnp.sum(jnp.where(eye, jnp.broadcast_to(row, (n, n)), 0.0), axis=-1, keepdims=True)


def _mlstm_step_kernel(x_ref, g_ref, bg_ref, hn_ref, c0_ref, n0_ref, m0_ref, hg_ref, c_ref, n_ref, m_ref,
                       *, H, DK, DV):
    x = x_ref[0]
    gpre = g_ref[0] + bg_ref[...]
    lf = _log_sigmoid(gpre)
    hdk = H * DK
    hdv = H * DV
    for h in range(H):
        f = FORGET_BIAS_COL + h
        ig = gpre[:, h:h + 1]
        m_prev = m0_ref[0, h:h + 1, :]
        log_inter = lf[:, f:f + 1] + m_prev
        m_t = jnp.maximum(log_inter, ig)
        d = jnp.exp(ig - m_t)
        inter = jnp.exp(log_inter - m_t)
        q = x[:, h * DK:(h + 1) * DK]
        k = x[:, hdk + h * DK:hdk + (h + 1) * DK] * (DK ** -0.5)
        v = x[:, 2 * hdk + h * DV:2 * hdk + (h + 1) * DV]
        og = x[:, 2 * hdk + hdv + h * DV:2 * hdk + hdv + (h + 1) * DV]
        c_old = c0_ref[0, h]
        n_old = n0_ref[0, h:h + 1, :]
        s = jnp.sum(q * k, axis=-1, keepdims=True) * d
        q_col = _to_col(q, DK)
        k_col = _to_col(k, DK)
        num = s * v + inter * jnp.sum(q_col * c_old, axis=0, keepdims=True)
        den = s + inter * jnp.sum(q * n_old, axis=-1, keepdims=True)
        hh = _rms(num / jnp.maximum(jnp.abs(den), jnp.exp(-m_t)))
        hg_ref[0, :, h * DV:(h + 1) * DV] = (hh * hn_ref[:, h * DV:(h + 1) * DV] * _sigmoid(og)).astype(BF16)
        c_ref[0, h] = inter * c_old + d * (k_col * v)
        n_ref[0, h:h + 1, :] = inter * n_old + d * k
        m_ref[0, h:h + 1, :] = m_t


def mlstm_step(qkvo, gates, b_gates, head_norm, c0, n0, m0):
    bsz, H, DK, DV = c0.shape
    wid = qkvo.shape[1]
    hdv = H * DV
    one = lambda b: (b, 0, 0)
    hg, c1, n1, m1 = pl.pallas_call(
        functools.partial(_mlstm_step_kernel, H=H, DK=DK, DV=DV),
        grid=(bsz,),
        in_specs=[pl.BlockSpec((1, 1, wid), one),
                  pl.BlockSpec((1, 1, LANES), one),
                  pl.BlockSpec((1, LANES), lambda b: (0, 0)),
                  pl.BlockSpec((1, hdv), lambda b: (0, 0)),
                  pl.BlockSpec((1, H, DK, DV), lambda b: (b, 0, 0, 0)),
                  pl.BlockSpec((1, H, DK), one),
                  pl.BlockSpec((1, H, 1), one)],
        out_specs=[pl.BlockSpec((1, 1, hdv), one),
                   pl.BlockSpec((1, H, DK, DV), lambda b: (b, 0, 0, 0)),
                   pl.BlockSpec((1, H, DK), one),
                   pl.BlockSpec((1, H, 1), one)],
        out_shape=[jax.ShapeDtypeStruct((bsz, 1, hdv), BF16),
                   jax.ShapeDtypeStruct((bsz, H, DK, DV), F32),
                   jax.ShapeDtypeStruct((bsz, H, DK), F32),
                   jax.ShapeDtypeStruct((bsz, H, 1), F32)],
        compiler_params=_cparams("parallel"),
    )(qkvo.reshape(bsz, 1, wid), gates.reshape(bsz, 1, LANES), _pad_lanes(b_gates.reshape(1, -1)),
      head_norm.reshape(1, hdv), c0, n0, m0.reshape(bsz, H, 1))
    return hg.reshape(bsz, hdv), c1, n1, m1.reshape(bsz, H)


def _conv_kernel(bg_ref, z_ref, zp_ref, cw_ref, o_ref, *, tm, T):
    i = pl.program_id(0)
    z = z_ref[...]
    zc = jnp.concatenate([zp_ref[...], z], axis=0)
    tpos = (i * tm + _iota((tm, 1), 0)) % T
    y = cw_ref[CONV_W - 1:CONV_W, :] * z
    for back in range(1, CONV_W):
        zb = zc[SUBLANES - back:SUBLANES - back + tm, :]
        y = y + cw_ref[CONV_W - 1 - back:CONV_W - back, :] * jnp.where(tpos >= back, zb, 0.0)
    o_ref[...] = (bg_ref[...] * y).astype(BF16)


def conv_gate(bg, z, conv_w, T):
    m, d = z.shape
    tm = _row_tile(T, 256)
    per = tm // SUBLANES
    return pl.pallas_call(
        functools.partial(_conv_kernel, tm=tm, T=T),
        grid=(m // tm,),
        in_specs=[pl.BlockSpec((tm, d), lambda i: (i, 0)),
                  pl.BlockSpec((tm, d), lambda i: (i, 0)),
                  pl.BlockSpec((SUBLANES, d), lambda i: (jnp.maximum(i * per - 1, 0), 0)),
                  pl.BlockSpec((CONV_W, d), lambda i: (0, 0))],
        out_specs=pl.BlockSpec((tm, d), lambda i: (i, 0)),
        out_shape=jax.ShapeDtypeStruct((m, d), BF16),
        compiler_params=_cparams("parallel"),
    )(bg, z, z, conv_w)


def _conv_step_kernel(bg_ref, z_ref, buf_ref, cw_ref, o_ref):
    y = cw_ref[CONV_W - 1:CONV_W, :] * z_ref[...]
    for j in range(CONV_W - 1):
        y = y + cw_ref[j:j + 1, :] * buf_ref[j]
    o_ref[...] = bg_ref[...] * y


def conv_gate_step(bg, z, buf, conv_w):
    bsz, d = z.shape
    full = lambda: (0, 0)
    return pl.pallas_call(
        _conv_step_kernel,
        in_specs=[pl.BlockSpec((bsz, d), full), pl.BlockSpec((bsz, d), full),
                  pl.BlockSpec((CONV_W - 1, bsz, d), lambda: (0, 0, 0)), pl.BlockSpec((CONV_W, d), full)],
        out_specs=pl.BlockSpec((bsz, d), full),
        out_shape=jax.ShapeDtypeStruct((bsz, d), F32),
    )(bg, z, buf, conv_w)


def _gelu(x):
    return 0.5 * x * (1.0 + jnp.tanh(math.sqrt(2.0 / math.pi) * (x + 0.044715 * x * x * x)))


def _cmp_finish(a, posw, b1, w2, b2):
    hid = a.shape[1] // 2
    a1 = a[:, hid:]
    a1_next = jnp.concatenate([a1[1:], a1[:1]], axis=0)
    pre = a[:, :hid] + a1_next + b1 + posw[0:1, :hid] + posw[1:2, hid:]
    return _dot(_gelu(pre).astype(BF16), w2) + b2


def _cmp_prompt_kernel(k_ref, v_ref, w1_ref, pos_ref, b1_ref, w2_ref, b2_ref, ck_ref, cv_ref, *, nch):
    for idx, (x_ref, o_ref) in enumerate(((k_ref, ck_ref), (v_ref, cv_ref))):
        w1 = w1_ref[idx]
        lhs = jnp.concatenate([x_ref[pl.ds(p, nch, stride=CMP_STRIDE), :] for p in range(CMP_STRIDE)], axis=1)
        a = _dot(lhs.astype(BF16), w1)
        posw = _dot(pos_ref[idx], w1)
        o_ref[0, 0] = _cmp_finish(a, posw, b1_ref[idx], w2_ref[idx], b2_ref[idx])


def _cmp_weights(cmp_pos, cmp_w1, cmp_b1, cmp_w2, cmp_b2):
    two, ratio, stride, hd, hid = cmp_w1.shape
    w1 = cmp_w1.transpose(0, 2, 3, 1, 4).reshape(two, stride * hd, ratio * hid).astype(BF16)
    pos = cmp_pos.reshape(two, ratio, stride * hd)
    pos = jnp.pad(pos, ((0, 0), (0, SUBLANES - ratio), (0, 0))).astype(BF16)
    return w1, pos, cmp_b1.reshape(two, 1, hid), cmp_w2.astype(BF16), cmp_b2.reshape(two, 1, hd)


def cmp_prompt(kc, vc, bsz, T, cw):
    w1, pos, b1, w2, b2 = cw
    nch = T // CMP_STRIDE
    G = NSA_KV
    out = jax.ShapeDtypeStruct((bsz, G, nch, NSA_HD), F32)
    ospec = pl.BlockSpec((1, 1, nch, NSA_HD), lambda b, g: (b, g, 0, 0))
    whole = lambda a: pl.BlockSpec(a.shape, lambda b, g: (0,) * a.ndim)
    seq = pl.BlockSpec((T, NSA_HD), lambda b, g: (b, g))
    return pl.pallas_call(
        functools.partial(_cmp_prompt_kernel, nch=nch),
        grid=(bsz, G),
        in_specs=[seq, seq, whole(w1), whole(pos), whole(b1), whole(w2), whole(b2)],
        out_specs=[ospec, ospec],
        out_shape=[out, out],
        compiler_params=_cparams("parallel", "parallel"),
    )(kc, vc, w1, pos, b1, w2, b2)


def _overlap_t(nbs_pad, nbc_pad):
    j = _iota((nbs_pad, nbc_pad), 0)
    n = _iota((nbs_pad, nbc_pad), 1)
    lo = jnp.maximum(n * CMP_STRIDE, j * SEL_BLOCK)
    hi = jnp.minimum(n * CMP_STRIDE + CMP_BLOCK, j * SEL_BLOCK + SEL_BLOCK)
    return (jnp.maximum(hi - lo, 0) // CMP_STRIDE).astype(F32)


def _rank_select(score, n_sel, axis):
    n = score.shape[axis]
    idx = _iota(score.shape, axis)
    cnt = jnp.zeros(score.shape, F32)
    for jp in range(n):
        other = lax.slice_in_dim(score, jp, jp + 1, axis=axis)
        tie = jnp.where(idx > jp, 1.0, 0.0)
        cnt = cnt + jnp.where(other > score, 1.0, jnp.where(other == score, tie, 0.0))
    return jnp.where(cnt < n_sel, 1.0, 0.0)


def _attn_first(state, s, vt_b):
    m_ref, l_ref, acc_ref = state
    m = jnp.max(s, axis=0, keepdims=True)
    p = jnp.exp2(s - m)
    m_ref[...] = m
    l_ref[...] = jnp.sum(p, axis=0, keepdims=True)
    acc_ref[...] = _dot(vt_b, p.astype(BF16))


def _attn_more(state, s, vt_b):
    m_ref, l_ref, acc_ref = state
    m_old = m_ref[...]
    m_new = jnp.maximum(m_old, jnp.max(s, axis=0, keepdims=True))
    alpha = jnp.exp2(m_old - m_new)
    p = jnp.exp2(s - m_new)
    m_ref[...] = m_new
    l_ref[...] = alpha * l_ref[...] + jnp.sum(p, axis=0, keepdims=True)
    acc_ref[...] = alpha * acc_ref[...] + _dot(vt_b, p.astype(BF16))


def _nsa_attn_kernel(q_ref, ck_ref, cv_ref, ks_ref, vs_ref, kw_ref, vw_ref, g_ref, bg_ref, o_ref,
                     ksb_ref, vst_ref, kwb_ref, vwt_ref, ms_ref, ls_ref, as_ref, mw_ref, lw_ref, aw_ref,
                     *, tq, hpg, nbc, nbs, nbs_pad):
    qi = pl.program_id(2)
    t0 = qi * tq
    cols = hpg * tq
    tk = tq
    n_kt = ks_ref.shape[0] // tk
    nbc_pad = ck_ref.shape[2]

    @pl.when(qi == 0)
    def _():
        ksb_ref[...] = ks_ref[...].astype(BF16)
        kwb_ref[...] = kw_ref[...].astype(BF16)
        for kt in range(n_kt):
            vst_ref[kt] = vs_ref[kt * tk:(kt + 1) * tk, :].T.astype(BF16)
            vwt_ref[kt] = vw_ref[kt * tk:(kt + 1) * tk, :].T.astype(BF16)

    qb = (jnp.concatenate([q_ref[:, h * NSA_HD:(h + 1) * NSA_HD] for h in range(hpg)], axis=0)
          * (NSA_HD ** -0.5 * math.log2(math.e))).astype(BF16)

    n = _iota((nbc_pad, cols), 0)
    t_col = t0 + jnp.concatenate([_iota((nbc_pad, tq), 1)] * hpg, axis=1)
    cmask = jnp.where(n < nbc, jnp.where(n * CMP_STRIDE + (CMP_BLOCK - 1) <= t_col, 1.0, 0.0), 0.0)
    sc = jnp.where(cmask > 0.5, _dot_nt(ck_ref[0, 0].astype(BF16), qb), NEG)
    pe = jnp.exp2(sc - jnp.max(sc, axis=0, keepdims=True))
    p_cmp = pe / jnp.sum(pe, axis=0, keepdims=True) * cmask
    o_cmp = _dot(cv_ref[0, 0].T.astype(BF16), p_cmp.astype(BF16))

    p_sum = p_cmp[:, 0:tq]
    for h in range(1, hpg):
        p_sum = p_sum + p_cmp[:, h * tq:(h + 1) * tq]
    ov_t = _overlap_t(nbs_pad, nbc_pad).astype(BF16)
    p_hi, p_lo = _split_bf16(p_sum)
    imp_t = _dot(ov_t, p_hi) + _dot(ov_t, p_lo)
    j = _iota((nbs_pad, tq), 0)
    t = t0 + _iota((nbs_pad, tq), 1)
    t_blk = t // SEL_BLOCK
    forced = jnp.where(j == 0, 1.0, jnp.where(j == t_blk, 1.0, jnp.where(j == t_blk - 1, 1.0, 0.0)))
    visible = jnp.where(j < nbs, jnp.where(j * SEL_BLOCK <= t, 1.0, 0.0), 0.0)
    score = jnp.where(visible > 0.5, jnp.where(forced > 0.5, BIG, imp_t), NEG)
    sel_t = _rank_select(score, min(N_SELECT, nbs), axis=0).astype(BF16)

    k_ofs = _iota((tk, tq), 0)
    q_ofs = _iota((tk, tq), 1)
    causal = k_ofs <= q_ofs
    per_head = lambda bias: jnp.concatenate([bias] * hpg, axis=1)

    def scores(kb_ref, kt):
        k_b = kb_ref[pl.ds(pl.multiple_of(kt * tk, tk), tk), :]
        return _dot_nt(k_b, qb)

    def slc_bias(kt):
        blk = (kt * tk + _iota((tk, nbs_pad), 0)) // SEL_BLOCK
        expand = jnp.where(blk == _iota((tk, nbs_pad), 1), 1.0, 0.0).astype(BF16)
        return (_dot(expand, sel_t) - 1.0) * BIG

    slc = (ms_ref, ls_ref, as_ref)
    _attn_first(slc, scores(ksb_ref, qi) + per_head(jnp.where(causal, slc_bias(qi), NEG)), vst_ref[qi])

    def slc_body(kt, _):
        _attn_more(slc, scores(ksb_ref, kt) + per_head(slc_bias(kt)), vst_ref[kt])
        return 0

    lax.fori_loop(0, qi, slc_body, 0)

    n_back = WINDOW // tk
    win = (mw_ref, lw_ref, aw_ref)
    _attn_first(win, scores(kwb_ref, qi) + per_head(jnp.where(causal, 0.0, NEG)), vwt_ref[qi])
    for back in range(1, n_back + 1):
        @pl.when(qi >= back)
        def _():
            s = scores(kwb_ref, qi - back)
            if back == n_back:
                s = s + per_head(jnp.where(k_ofs >= q_ofs, 0.0, NEG))
            _attn_more(win, s, vwt_ref[qi - back])

    o_slc = as_ref[...] / ls_ref[...]
    o_win = aw_ref[...] / lw_ref[...]
    gate_t = _sigmoid(g_ref[...] + bg_ref[...]).T
    for h in range(hpg):
        sl = slice(h * tq, (h + 1) * tq)
        o_t = (gate_t[3 * h:3 * h + 1, :] * o_cmp[:, sl] + gate_t[3 * h + 1:3 * h + 2, :] * o_slc[:, sl]
               + gate_t[3 * h + 2:3 * h + 3, :] * o_win[:, sl])
        o_ref[:, h * NSA_HD:(h + 1) * NSA_HD] = o_t.T.astype(BF16)


def nsa_attn_prompt(q, ck, cv, ks, vs, kw, vw, gates, b_gate, bsz, T):
    G, HPG = NSA_KV, NSA_HPG
    tq = math.gcd(T, ATT_TILE)
    assert WINDOW % tq == 0
    nq = T // tq
    nch = T // CMP_STRIDE
    nbc = nch - CMP_BLOCK // CMP_STRIDE + 1
    nbs = -(-T // SEL_BLOCK)
    nbs_pad = -(-nbs // SUBLANES) * SUBLANES
    cols = HPG * tq
    kv_scratch = [pltpu.VMEM((T, NSA_HD), BF16), pltpu.VMEM((nq, NSA_HD, tq), BF16)] * 2
    stats = [pltpu.VMEM((1, cols), F32), pltpu.VMEM((1, cols), F32), pltpu.VMEM((NSA_HD, cols), F32)] * 2
    seq = pl.BlockSpec((T, NSA_HD), lambda b, g, i: (b, g))
    cspec = pl.BlockSpec((1, 1, nch, NSA_HD), lambda b, g, i: (b, g, 0, 0))
    qspec = pl.BlockSpec((tq, HPG * NSA_HD), lambda b, g, i: (b * nq + i, g))
    return pl.pallas_call(
        functools.partial(_nsa_attn_kernel, tq=tq, hpg=HPG, nbc=nbc, nbs=nbs, nbs_pad=nbs_pad),
        grid=(bsz, G, nq),
        in_specs=[qspec, cspec, cspec, seq, seq, seq, seq,
                  pl.BlockSpec((tq, LANES), lambda b, g, i: (b * nq + i, g)),
                  pl.BlockSpec((1, LANES), lambda b, g, i: (0, g))],
        out_specs=qspec,
        out_shape=jax.ShapeDtypeStruct((bsz * T, G * HPG * NSA_HD), BF16),
        scratch_shapes=kv_scratch + stats,
        compiler_params=_cparams("arbitrary", "arbitrary", "arbitrary"),
    )(q, ck, cv, ks, vs, kw, vw, gates, b_gate)


CHUNK_ROWS = CMP_STRIDE * NSA_KV
CHUNK_PITCH = CHUNK_ROWS + SUBLANES


def _dec_cmp_kernel(tbl_ref, *refs, P, nbc, nbs, nbs_pad, p0):
    k_pages, v_pages = refs[:P], refs[P:2 * P]
    w1_ref, q_ref, pos_ref, b1_ref, w2_ref, b2_ref, o_ref, sel_ref, pad_ref, ak_ref, av_ref = refs[2 * P:]
    cpp = LANES // CMP_STRIDE
    step = pl.program_id(1)
    for idx, (pages, a_ref) in enumerate(((k_pages, ak_ref), (v_pages, av_ref))):
        for i, pg in enumerate(pages):
            for c in range(cpp):
                pad_ref[idx * P + i, c * CHUNK_PITCH:c * CHUNK_PITCH + CHUNK_ROWS, :] = (
                    pg[0, c * CHUNK_ROWS:(c + 1) * CHUNK_ROWS, :])
        blocks = []
        for g in range(NSA_KV):
            for i in range(P):
                blocks.append(jnp.concatenate(
                    [pad_ref[idx * P + i, pl.ds(p * NSA_KV + g, cpp, stride=CHUNK_PITCH), :]
                     for p in range(CMP_STRIDE)], axis=1))
        a = _dot(jnp.concatenate(blocks, axis=0).astype(BF16), w1_ref[idx])
        per_g = P * cpp
        for g in range(NSA_KV):
            a_ref[g, pl.ds(pl.multiple_of(step * per_g, per_g), per_g), :] = a[g * per_g:(g + 1) * per_g]

    @pl.when(step == pl.num_programs(1) - 1)
    def _():
        _dec_cmp_finish(ak_ref, av_ref, q_ref, pos_ref, w1_ref, b1_ref, w2_ref, b2_ref, o_ref, sel_ref,
                        nbc=nbc, nbs=nbs, nbs_pad=nbs_pad, p0=p0)


def _dec_cmp_finish(ak_ref, av_ref, q_ref, pos_ref, w1_ref, b1_ref, w2_ref, b2_ref, o_ref, sel_ref,
                    *, nbc, nbs, nbs_pad, p0):
    nch = ak_ref.shape[1]
    for g in range(NSA_KV):
        cks = []
        for idx, a_ref in enumerate((ak_ref, av_ref)):
            posw = _dot(pos_ref[idx], w1_ref[idx])
            cks.append(_cmp_finish(a_ref[g], posw, b1_ref[idx], w2_ref[idx], b2_ref[idx]))
        ck, cv = cks
        qb = (q_ref[0, g] * (NSA_HD ** -0.5)).astype(BF16)
        n = _iota((SUBLANES, nch), 1)
        cmask = jnp.where(n < nbc, jnp.where(n * CMP_STRIDE + (CMP_BLOCK - 1) <= p0, 1.0, 0.0), 0.0)
        sc = jnp.where(cmask > 0.5, _dot_nt(qb, ck.astype(BF16)), NEG)
        pe = jnp.exp(sc - jnp.max(sc, axis=-1, keepdims=True))
        p_cmp = pe / jnp.sum(pe, axis=-1, keepdims=True) * cmask
        o_ref[0, g] = _dot(p_cmp.astype(BF16), cv.astype(BF16))

        head = jnp.where(_iota((SUBLANES, nch), 0) < NSA_HPG, p_cmp, 0.0)
        p_sum = jnp.broadcast_to(jnp.sum(head, axis=0, keepdims=True), (SUBLANES, nch))
        ov_t = _overlap_t(nbs_pad, nch).astype(BF16)
        p_hi, p_lo = _split_bf16(p_sum)
        imp = _dot_nt(p_hi, ov_t) + _dot_nt(p_lo, ov_t)
        j = _iota((SUBLANES, nbs_pad), 1)
        t_blk = p0 // SEL_BLOCK
        forced = jnp.where(j == 0, 1.0, jnp.where(j == t_blk, 1.0, jnp.where(j == t_blk - 1, 1.0, 0.0)))
        visible = jnp.where(j < nbs, jnp.where(j * SEL_BLOCK <= p0, 1.0, 0.0), 0.0)
        score = jnp.where(visible > 0.5, jnp.where(forced > 0.5, BIG, imp), NEG)
        s_row = jnp.broadcast_to(score[0:1], (nbs_pad, nbs_pad))
        s_col = jnp.concatenate([jnp.broadcast_to(score[0:1], (LANES, nbs_pad)).T] * (nbs_pad // LANES), axis=1)
        jr = _iota((nbs_pad, nbs_pad), 0)
        jc = _iota((nbs_pad, nbs_pad), 1)
        tie = jnp.where(jc < jr, 1.0, 0.0)
        beats = jnp.where(s_row > s_col, 1.0, jnp.where(s_row == s_col, tie, 0.0))
        rank = jnp.sum(beats, axis=-1, keepdims=True)
        slot = _iota((nbs_pad, LANES), 1).astype(F32)
        blk = _iota((nbs_pad, LANES), 0).astype(F32)
        picked = jnp.sum(jnp.where(rank == slot, blk, 0.0), axis=0, keepdims=True)
        sel_ref[0, g:g + 1, :] = picked.astype(jnp.int32)


def dec_cmp(pool_k, pool_v, table, q, cw, p0):
    w1, pos, b1, w2, b2 = cw
    n_pool, page, G, hd = pool_k.shape
    assert page == LANES and G == NSA_KV and hd == NSA_HD
    bsz, n_pages = table.shape
    assert p0 == n_pages * page
    P = math.gcd(n_pages, PAGES_PER_STEP)
    cpp = page // CMP_STRIDE
    nch = n_pages * cpp
    hid2 = w1.shape[2]
    nbc = nch - CMP_BLOCK // CMP_STRIDE + 1
    nbs = -(-(p0 + 1) // SEL_BLOCK)
    nbs_pad = -(-nbs // LANES) * LANES
    pk = pool_k.reshape(n_pool, page * G, hd)
    pv = pool_v.reshape(n_pool, page * G, hd)
    pspec = lambda i: pl.BlockSpec((1, page * G, hd),
                                   functools.partial(lambda b, s, tbl, i: (tbl[b * n_pages + s * P + i], 0, 0), i=i))
    whole = lambda a: pl.BlockSpec(a.shape, lambda b, s, tbl: (0,) * a.ndim)
    per_seq = lambda shape: pl.BlockSpec((1,) + shape, lambda b, s, tbl: (b,) + (0,) * len(shape))
    grid_spec = pltpu.PrefetchScalarGridSpec(
        num_scalar_prefetch=1,
        grid=(bsz, n_pages // P),
        in_specs=[pspec(i) for i in range(P)] * 2
        + [whole(w1), per_seq((G, SUBLANES, hd)), whole(pos), whole(b1), whole(w2), whole(b2)],
        out_specs=[per_seq((G, SUBLANES, hd)), per_seq((G, LANES))],
        scratch_shapes=[pltpu.VMEM((2 * P, cpp * CHUNK_PITCH, hd), F32),
                        pltpu.VMEM((G, nch, hid2), F32), pltpu.VMEM((G, nch, hid2), F32)],
    )
    return pl.pallas_call(
        functools.partial(_dec_cmp_kernel, P=P, nbc=nbc, nbs=nbs, nbs_pad=nbs_pad, p0=p0),
        grid_spec=grid_spec,
        out_shape=[jax.ShapeDtypeStruct((bsz, G, SUBLANES, hd), F32),
                   jax.ShapeDtypeStruct((bsz, G, LANES), jnp.int32)],
        compiler_params=_cparams("arbitrary", "arbitrary"),
    )(table.reshape(-1), *([pk] * P), *([pv] * P), w1, q, pos, b1, w2, b2)


def _softmax_with_new_key(qb, k_b, v_b, bias, s_new, v_new):
    s = _dot_nt(qb, k_b)
    if bias is not None:
        s = s + bias
    mx = jnp.maximum(jnp.max(s, axis=-1, keepdims=True), s_new)
    p = jnp.exp(s - mx)
    p_new = jnp.exp(s_new - mx)
    return (_dot(p.astype(BF16), v_b) + p_new * v_new) / (jnp.sum(p, axis=-1, keepdims=True) + p_new)


def _dec_attn_kernel(sel_ref, tbl_ref, *refs, n_cache_blocks, n_sel):
    ks_refs, vs_refs = refs[:n_sel], refs[n_sel:2 * n_sel]
    q_ref, new_ref, kw_ref, vw_ref, oc_ref, g_ref, bg_ref, o_ref = refs[2 * n_sel:]
    b, g = pl.program_id(0), pl.program_id(1)
    q = q_ref[0, 0] * (NSA_HD ** -0.5)
    qb = q.astype(BF16)
    new = new_ref[0, 0]

    mine = lambda r, n: r[pl.ds(g, n, stride=NSA_KV), :]
    k_all = jnp.concatenate([mine(r.at[0, 0], SEL_BLOCK) for r in ks_refs], axis=0).astype(BF16)
    v_all = jnp.concatenate([mine(r.at[0, 0], SEL_BLOCK) for r in vs_refs], axis=0).astype(BF16)
    slot = _iota((SUBLANES, n_sel * SEL_BLOCK), 1) // SEL_BLOCK
    bias = jnp.zeros((SUBLANES, n_sel * SEL_BLOCK), F32)
    for s in range(n_sel):
        blk = sel_ref[(b * NSA_KV + g) * n_sel + s]
        bias = jnp.where(slot == s, jnp.where(blk < n_cache_blocks, 0.0, NEG), bias)
    o_slc = _softmax_with_new_key(qb, k_all, v_all, bias, jnp.sum(q * new[0:1], axis=-1, keepdims=True), new[1:2])

    n_win = kw_ref.shape[1] // NSA_KV
    o_win = _softmax_with_new_key(qb, mine(kw_ref.at[0], n_win).astype(BF16), mine(vw_ref.at[0], n_win).astype(BF16),
                                  None,
                                  jnp.sum(q * new[2:3], axis=-1, keepdims=True), new[3:4])

    gate = jnp.broadcast_to(_sigmoid(g_ref[0, 0] + bg_ref[...]), (SUBLANES, LANES))
    head = _iota((SUBLANES, LANES), 0)
    lane = _iota((SUBLANES, LANES), 1)
    o = jnp.zeros((SUBLANES, NSA_HD), F32)
    for br, ob in enumerate((oc_ref[0, 0], o_slc, o_win)):
        o = o + jnp.sum(jnp.where(lane == 3 * head + br, gate, 0.0), axis=-1, keepdims=True) * ob
    o_ref[0, 0] = o


def dec_attn(sel, table, pool_ks, pool_vs, q, new, kw_past, vw_past, o_cmp, gates, b_gate):
    n_pool, page, G, hd = pool_ks.shape
    bsz, n_pages = table.shape
    halves = page // SEL_BLOCK
    wb = kw_past.shape[1]
    assert wb <= WINDOW
    n_sel = sel.shape[-1]
    pk = pool_ks.reshape(n_pool, halves, SEL_BLOCK * G, hd)
    pv = pool_vs.reshape(n_pool, halves, SEL_BLOCK * G, hd)
    n_cache_blocks = n_pages * halves

    def page_map(b, g, sel_ref, tbl_ref, s):
        blk = jnp.minimum(sel_ref[(b * G + g) * n_sel + s], n_cache_blocks - 1)
        return (tbl_ref[b * n_pages + blk // halves], blk % halves, 0, 0)

    bg_map = lambda b, g, sel_ref, tbl_ref: (b, g, 0, 0)
    small = pl.BlockSpec((1, 1, SUBLANES, hd), bg_map)
    wspec = pl.BlockSpec((1, wb * G, hd), lambda b, g, sel_ref, tbl_ref: (b, 0, 0))
    blocks = [pl.BlockSpec((1, 1, SEL_BLOCK * G, hd), functools.partial(page_map, s=s)) for s in range(n_sel)]
    grid_spec = pltpu.PrefetchScalarGridSpec(
        num_scalar_prefetch=2,
        grid=(bsz, G),
        in_specs=blocks + blocks + [small, small, wspec, wspec, small,
                                    pl.BlockSpec((1, 1, 1, LANES), bg_map),
                                    pl.BlockSpec((1, LANES), lambda b, g, sel_ref, tbl_ref: (0, g))],
        out_specs=small,
    )
    return pl.pallas_call(
        functools.partial(_dec_attn_kernel, n_cache_blocks=n_cache_blocks, n_sel=n_sel),
        grid_spec=grid_spec,
        out_shape=jax.ShapeDtypeStruct((bsz, G, SUBLANES, hd), F32),
        compiler_params=_cparams("parallel", "parallel"),
    )(sel.reshape(-1), table.reshape(-1), *([pk] * n_sel), *([pv] * n_sel), q, new,
      kw_past.reshape(bsz, wb * G, hd), vw_past.reshape(bsz, wb * G, hd), o_cmp,
      gates.reshape(bsz, G, 1, LANES), b_gate)


def _gate_weights(wt, layer, row0, n):
    rows = lax.slice(wt, (layer, row0, 0), (layer + 1, row0 + n, wt.shape[2]))
    return jnp.pad(rows, ((0, 0), (0, LANES - n), (0, 0))), 0


def _nsa_gate_weights(wt, layer, b_gate, row0):
    per = 3 * NSA_HPG
    k = wt.shape[2]
    rows = lax.slice(wt, (layer, row0, 0), (layer + 1, row0 + NSA_KV * per, k)).reshape(NSA_KV, per, k)
    wg = jnp.pad(rows, ((0, 0), (0, LANES - per), (0, 0))).reshape(1, NSA_KV * LANES, k)
    bg = jnp.pad(b_gate.reshape(NSA_KV, per), ((0, 0), (0, LANES - per))).reshape(1, NSA_KV * LANES)
    return (wg, 0), bg


def kernel(x_prompt, x_sample, state_a_C, state_a_n, state_a_m, state_b_conv, cache_c_k_cmp, cache_c_v_cmp, cache_c_k_slc, cache_c_v_slc, cache_c_k_win, cache_c_v_win, page_table, norm_w, ffn_w_in, ffn_w_out, a_w_in, a_b_gates, a_head_norm, a_w_out, b_w_in, b_conv_w, b_w_out, c_w_in, c_b_gate, c_cmp_pos, c_cmp_w1, c_cmp_b1, c_cmp_w2, c_cmp_b2, c_w_out):
    bp, T, D = x_prompt.shape
    bs = x_sample.shape[0]
    assert x_sample.shape[1] == 1
    depth = norm_w.shape[0]
    G, hd = NSA_KV, NSA_HD
    past_len = page_table.shape[1] * cache_c_k_cmp.shape[2]
    xp = x_prompt.reshape(bp * T, D)
    xs = x_sample.reshape(bs, D)
    hp_in = norm_cast(xp, norm_w[0, 0])
    hs_in = norm_cast(xs, norm_w[0, 0])
    a_wt = jnp.swapaxes(a_w_in, 1, 2)
    c_wt = jnp.swapaxes(c_w_in, 1, 2)
    w_outs = {0: a_w_out, 1: b_w_out, 2: c_w_out}
    pa, sa, pb, sb, pc, sc = [], [], [], [], [], []
    for i in range(depth):
        kind, j = i % 3, i // 3
        nw = norm_w[i]
        w_out_f32 = (w_outs[kind], j)
        if kind == 0:
            hdv = a_head_norm.shape[1]
            n_main = 3 * hdv
            wg = _gate_weights(a_wt, j, n_main, 2 * MLSTM_HEADS)
            qkvo_p, qkvo_s, w_out = proj_plain(hp_in, hs_in, (a_wt, j), 0, n_main, 1024, transposed=True,
                                               cast=w_out_f32)
            g_p, g_s = proj_plain(hp_in, hs_in, wg, 0, LANES, LANES, transposed=True)
            hp, c_p, n_p, m_p = mlstm_prompt(qkvo_p, g_p, a_b_gates[j], a_head_norm[j], bp, T)
            hs, c_s, n_s, m_s = mlstm_step(qkvo_s, g_s, a_b_gates[j], a_head_norm[j],
                                           state_a_C[j], state_a_n[j], state_a_m[j])
            pa.append((c_p, n_p, m_p[:, :, 0]))
            sa.append((c_s, n_s, m_s))
        elif kind == 1:
            (bg_p, z_p), (bg_s, z_s), w_out = proj_conv_in(hp_in, hs_in, (b_w_in, j), D, w_out_f32)
            hp = conv_gate(bg_p, z_p, b_conv_w[j], T)
            buf = state_b_conv[j]
            hs = conv_gate_step(bg_s, z_s, buf.transpose(1, 0, 2), b_conv_w[j]).astype(BF16)
            pb.append(z_p.reshape(bp, T, D)[:, T - (CONV_W - 1):])
            sb.append(jnp.concatenate([buf, z_s[:, None]], axis=1)[:, -(CONV_W - 1):])
        else:
            nq_cols = NSA_HPG * G * hd
            nheads = G * NSA_HPG
            wg, bgate = _nsa_gate_weights(c_wt, j, c_b_gate[j], nq_cols + 6 * G * hd)
            cw = _cmp_weights(c_cmp_pos[j], c_cmp_w1[j], c_cmp_b1[j], c_cmp_w2[j], c_cmp_b2[j])
            q_p, q_s, w_out = proj_plain(hp_in, hs_in, (c_wt, j), 0, nq_cols, 1024, transposed=True, cast=w_out_f32)
            kv_p, kv_rows_p, kv_s = [], [], []
            for half in range(2):
                (tok_p, rows_p), (tok_s, _) = proj_groups(hp_in, hs_in, (c_wt, j), nq_cols + half * 3 * G * hd,
                                                         3, G * hd, transposed=True)
                kv_p, kv_rows_p, kv_s = kv_p + list(tok_p), kv_rows_p + list(rows_p), kv_s + list(tok_s)
            gt_p, gt_s = proj_plain(hp_in, hs_in, wg, 0, G * LANES, G * LANES, transposed=True)
            ck, cv = cmp_prompt(kv_p[0], kv_p[1], bp, T, cw)
            hp = nsa_attn_prompt(q_p, ck, cv, kv_p[2], kv_p[3], kv_p[4], kv_p[5], gt_p, bgate, bp, T)
            kv_rows_p = [a.reshape(bp, T, G, hd) for a in kv_rows_p]
            keep = min(WINDOW, T)
            pc.append(tuple(kv_rows_p[:4]) + tuple(a[:, T - keep:] for a in kv_rows_p[4:]))
            q_s = jnp.pad(q_s.reshape(bs, G, NSA_HPG, hd), ((0, 0), (0, 0), (0, SUBLANES - NSA_HPG), (0, 0)))
            kv_s = jnp.stack([a.reshape(bs, G, hd) for a in kv_s], axis=1)
            new = jnp.pad(kv_s[:, 2:6].transpose(0, 2, 1, 3), ((0, 0), (0, 0), (0, SUBLANES - 4), (0, 0)))
            o_cmp, ranked = dec_cmp(cache_c_k_cmp[j], cache_c_v_cmp[j], page_table, q_s, cw, past_len)
            n_sel = min(N_SELECT, -(-(past_len + 1) // SEL_BLOCK))
            o_s = dec_attn(ranked[:, :, :n_sel], page_table, cache_c_k_slc[j], cache_c_v_slc[j], q_s, new,
                           cache_c_k_win[j], cache_c_v_win[j], o_cmp, gt_s, bgate)
            hs = o_s[:, :, :NSA_HPG].reshape(bs, nheads * hd).astype(BF16)
            keep_s = min(WINDOW, cache_c_k_win.shape[2] + 1)
            win = [jnp.concatenate([c[j], kv_s[:, a][:, None]], axis=1)[:, -keep_s:]
                   for c, a in ((cache_c_k_win, 4), (cache_c_v_win, 5))]
            sc.append(tuple(kv_s[:, a][:, None] for a in range(4)) + tuple(win))
        last = i == depth - 1
        nw_next = norm_w[i + 1, 0] if not last else nw[0]
        (xp, xs), (hp_mid, hs_mid) = proj_out(hp, hs, w_out, xp, xs, nw[1], nw[2])
        act_p, act_s, f_out = proj_swiglu(hp_mid, hs_mid, (ffn_w_in, i), ffn_w_out.shape[1], (ffn_w_out, i))
        (xp, xs), (hp_in, hs_in) = proj_out(act_p, act_s, f_out, xp, xs, nw[3], nw_next, not last)

    stack = lambda items: [jnp.stack(a) for a in zip(*items)]
    p_a, s_a = stack(pa), stack(sa)
    p_c, s_c = stack(pc), stack(sc)
    return (xp.reshape(bp, T, D), xs.reshape(bs, 1, D), *p_a, *s_a, jnp.stack(pb), jnp.stack(sb), *p_c, *s_c)
```

```python
import functools
import math

import jax
import jax.numpy as jnp
from jax import lax
from jax.experimental import pallas as pl
from jax.experimental.pallas import tpu as pltpu

F32 = jnp.float32
BF16 = jnp.bfloat16

EPS = 1e-6
NEG = -1e30
BIG = 1e30

LANES = 128
SUBLANES = 8
VMEM_LIMIT = 56 * 1024 * 1024

MLSTM_HEADS = 8
MLSTM_CHUNK = 256
FORGET_BIAS_COL = MLSTM_HEADS
CONV_W = 3
NSA_HD = 128
NSA_KV = 4
NSA_HPG = 4
CMP_BLOCK = 32
CMP_STRIDE = 16
SEL_BLOCK = 64
N_SELECT = 16
WINDOW = 512
ATT_TILE = 256
PAGES_PER_STEP = 16
OUT_SUB_ROWS = 256


def _cparams(*sem):
    return pltpu.CompilerParams(dimension_semantics=sem, vmem_limit_bytes=VMEM_LIMIT)


def _dot(a, b):
    return jnp.dot(a, b, preferred_element_type=F32)


def _dot_nt(a, b):
    return lax.dot_general(a, b, (((1,), (1,)), ((), ())), preferred_element_type=F32)


def _dot_tn(a, b):
    return lax.dot_general(a, b, (((0,), (0,)), ((), ())), preferred_element_type=F32)


def _split_bf16(x):
    hi = x.astype(BF16)
    lo = (x - hi.astype(F32)).astype(BF16)
    return hi, lo


def _iota(shape, dim):
    return lax.broadcasted_iota(jnp.int32, shape, dim)


def _sigmoid(x):
    return 1.0 / (1.0 + jnp.exp(-x))


def _rms(x):
    return x * lax.rsqrt(jnp.mean(x * x, axis=-1, keepdims=True) + EPS)


def _row_tile(m, want):
    t = min(m, want)
    assert m % t == 0, (m, t)
    return t


def _norm_kernel(x_ref, w_ref, o_ref):
    o_ref[...] = (_rms(x_ref[...]) * w_ref[...]).astype(BF16)


def norm_cast(x, w):
    m, d = x.shape
    tm = _row_tile(m, 512)
    return pl.pallas_call(
        _norm_kernel,
        grid=(m // tm,),
        in_specs=[pl.BlockSpec((tm, d), lambda i: (i, 0)), pl.BlockSpec((1, d), lambda i: (0, 0))],
        out_specs=pl.BlockSpec((tm, d), lambda i: (i, 0)),
        out_shape=jax.ShapeDtypeStruct((m, d), BF16),
        compiler_params=_cparams("parallel"),
    )(x, w.reshape(1, d))


def _proj_kernel(x_ref, xs_ref, *refs, n_groups, n_out, epilogue, transposed, with_cast):
    w_refs = refs[:n_groups]
    refs = refs[n_groups:]
    if with_cast:
        cast_in_ref, refs = refs[0], refs[1:]
        refs[2 * n_out][...] = cast_in_ref[...].astype(BF16)
    out_refs = refs[:n_out]
    sample_out_refs = refs[n_out:2 * n_out]
    wb_ref = refs[-1]
    mm = _dot_nt if transposed else _dot

    @pl.when(pl.program_id(1) == 0)
    def _():
        for g in range(n_groups):
            wb_ref[g] = w_refs[g][...].astype(BF16)
        xs = xs_ref[...]
        epilogue([mm(xs, wb_ref[g]) for g in range(n_groups)], sample_out_refs)

    x = x_ref[...]
    epilogue([mm(x, wb_ref[g]) for g in range(n_groups)], out_refs)


def _proj(x, xs, w, *, col0, tn, n_tiles, group_stride, n_groups, epilogue, outs, tm_want=1024, transposed=False,
          cast=None):
    w, layer = w
    m, k = x.shape
    s = xs.shape[0]
    tm = _row_tile(m, tm_want)
    assert col0 % tn == 0
    b0 = col0 // tn
    if transposed:
        w_block = (None, tn, k)
        w_map = lambda j, i, off: (layer, off + j, 0)
    else:
        w_block = (None, k, tn)
        w_map = lambda j, i, off: (layer, 0, off + j)
    w_specs = [pl.BlockSpec(w_block, functools.partial(w_map, off=b0 + g * group_stride)) for g in range(n_groups)]
    n_out = len(outs)

    def out_for(rows, row_tile, row_map, n, dt, by_rows):
        if not by_rows:
            return jax.ShapeDtypeStruct((rows, n), dt), pl.BlockSpec((row_tile, tn), row_map)
        assert n_tiles == 1 and n == tn
        per = n // LANES
        return (jax.ShapeDtypeStruct((rows * per, LANES), dt),
                pl.BlockSpec((row_tile * per, LANES), lambda j, i: (row_map(j, i)[0], 0)))

    prompt_outs = [out_for(m, tm, lambda j, i: (i, j), *o) for o in outs]
    sample_outs = [out_for(s, s, lambda j, i: (0, j), *o) for o in outs]
    all_outs = prompt_outs + sample_outs
    operands = [x, xs] + [w] * n_groups
    in_specs = [pl.BlockSpec((tm, k), lambda j, i: (i, 0)), pl.BlockSpec((s, k), lambda j, i: (0, 0))] + w_specs
    n_i = m // tm
    if cast is not None:
        cw, c_layer = cast
        _, c_rows, c_cols = cw.shape
        bf16_rows = 2 * SUBLANES
        blk = next(r for r in range(bf16_rows, c_rows + 1, bf16_rows)
                   if c_rows % r == 0 and c_rows // r <= n_tiles * n_i)
        last = c_rows // blk - 1
        step = lambda j, i: jnp.minimum(j * n_i + i, last)
        in_specs.append(pl.BlockSpec((None, blk, c_cols), lambda j, i: (c_layer, step(j, i), 0)))
        operands.append(cw)
        all_outs.append((jax.ShapeDtypeStruct((c_rows, c_cols), BF16),
                         pl.BlockSpec((blk, c_cols), lambda j, i: (step(j, i), 0))))
    res = pl.pallas_call(
        functools.partial(_proj_kernel, n_groups=n_groups, n_out=n_out, epilogue=epilogue, transposed=transposed,
                          with_cast=cast is not None),
        grid=(n_tiles, n_i),
        in_specs=in_specs,
        out_specs=[spec for _, spec in all_outs],
        out_shape=[shape for shape, _ in all_outs],
        scratch_shapes=[pltpu.VMEM((n_groups,) + w_block[1:], BF16)],
        compiler_params=_cparams("arbitrary", "arbitrary"),
    )(*operands)
    return res[:n_out], res[n_out:2 * n_out], (res[2 * n_out] if cast is not None else None)


def _ep_plain(accs, outs):
    for acc, o in zip(accs, outs):
        o[...] = acc


def proj_plain(x, xs, w, col0, n_cols, tn, transposed=False, cast=None):
    p, s, c = _proj(x, xs, w, col0=col0, tn=tn, n_tiles=n_cols // tn, group_stride=0, n_groups=1,
                    epilogue=_ep_plain, outs=[(n_cols, F32, False)], transposed=transposed, cast=cast)
    return (p[0], s[0]) if cast is None else (p[0], s[0], c)


def _ep_both_layouts(accs, outs):
    n = len(accs)
    for acc, o_tok, o_rows in zip(accs, outs[:n], outs[n:]):
        o_tok[...] = acc
        per = acc.shape[1] // LANES
        rows = acc.shape[0]
        for c in range(per):
            o_rows[pl.ds(c, rows, stride=per), :] = acc[:, c * LANES:(c + 1) * LANES]


def proj_groups(x, xs, w, col0, n_groups, group_cols, transposed=False):
    outs = [(group_cols, F32, False)] * n_groups + [(group_cols, F32, True)] * n_groups
    p, s, _ = _proj(x, xs, w, col0=col0, tn=group_cols, n_tiles=1, group_stride=1, n_groups=n_groups,
                    epilogue=_ep_both_layouts, outs=outs, tm_want=512, transposed=transposed)
    return (p[:n_groups], p[n_groups:]), (s[:n_groups], s[n_groups:])


def _ep_swiglu(accs, outs):
    g, u = accs
    outs[0][...] = (g * _sigmoid(g) * u).astype(BF16)


def proj_swiglu(x, xs, w, ff, cast):
    tn = 512
    p, s, c = _proj(x, xs, w, col0=0, tn=tn, n_tiles=ff // tn, group_stride=ff // tn, n_groups=2,
                    epilogue=_ep_swiglu, outs=[(ff, BF16, False)], cast=cast)
    return p[0], s[0], c


def _ep_conv_in(accs, outs):
    bg, cg, u = accs
    outs[0][...] = bg
    outs[1][...] = cg * u


def proj_conv_in(x, xs, w, d, cast):
    tn = 512
    return _proj(x, xs, w, col0=0, tn=tn, n_tiles=d // tn, group_stride=d // tn, n_groups=3,
                 epilogue=_ep_conv_in, outs=[(d, F32, False), (d, F32, False)], cast=cast)


def _conv_gated(bg_ref, z_ref, zp_ref, cw_ref, T):
    tm = z_ref.shape[0]
    z = z_ref[...]
    zc = jnp.concatenate([zp_ref[...], z], axis=0)
    tpos = (pl.program_id(0) * tm + _iota((tm, 1), 0)) % T
    y = cw_ref[CONV_W - 1:CONV_W, :] * z
    for back in range(1, CONV_W):
        zb = zc[SUBLANES - back:SUBLANES - back + tm, :]
        y = y + cw_ref[CONV_W - 1 - back:CONV_W - back, :] * jnp.where(tpos >= back, zb, 0.0)
    return (bg_ref[...] * y).astype(BF16)


def _out_kernel(*refs, emit_next, conv_T):
    if conv_T:
        h = _conv_gated(*refs[:4], conv_T)
        rows_of = lambda rows: h[rows, :]
        refs = refs[4:]
    else:
        h_ref = refs[0]
        rows_of = lambda rows: h_ref[rows, :]
        refs = refs[1:]
    hs_ref, w_ref, r_ref, rs_ref, nwa_ref, nwb_ref, *out_refs = refs

    def finalize(y, r, x_ref, xn_ref):
        x_new = r + _rms(y) * nwa_ref[...]
        x_ref[...] = x_new
        if emit_next:
            xn_ref[...] = (_rms(x_new) * nwb_ref[...]).astype(BF16)

    if emit_next:
        x_ref, xn_ref, xs_ref, xsn_ref = out_refs
    else:
        (x_ref, xs_ref), xn_ref, xsn_ref = out_refs, None, None

    @pl.when(pl.program_id(0) == 0)
    def _():
        finalize(_dot(hs_ref[...], w_ref[...]), rs_ref[...], xs_ref, xsn_ref)

    tm = r_ref.shape[0]
    sub = math.gcd(tm, OUT_SUB_ROWS)
    for r0 in range(0, tm, sub):
        rows = slice(r0, r0 + sub)
        finalize(_dot(rows_of(rows), w_ref[...]), r_ref[rows, :], x_ref.at[rows, :],
                 xn_ref.at[rows, :] if emit_next else None)


def proj_out(h, hs, w, resid, resid_s, nw_post, nw_next, emit_next=True, conv=None):
    m, d = resid.shape
    k = w.shape[0]
    s = hs.shape[0]
    tm = _row_tile(m, 512 if k <= 2048 else 256)
    row = lambda i: (i, 0)
    fixed = lambda i: (0, 0)
    if conv is None:
        h_ops, h_specs, conv_T = [h], [pl.BlockSpec((tm, k), row)], 0
    else:
        bg, z, conv_w, conv_T = conv
        assert conv_T % tm == 0
        per = tm // SUBLANES
        h_ops = [bg, z, z, conv_w]
        h_specs = [pl.BlockSpec((tm, k), row), pl.BlockSpec((tm, k), row),
                   pl.BlockSpec((SUBLANES, k), lambda i: (jnp.maximum(i * per - 1, 0), 0)),
                   pl.BlockSpec((CONV_W, k), fixed)]
    f32_out = [jax.ShapeDtypeStruct((m, d), F32), jax.ShapeDtypeStruct((s, d), F32)]
    bf16_out = [jax.ShapeDtypeStruct((m, d), BF16), jax.ShapeDtypeStruct((s, d), BF16)]
    specs = [pl.BlockSpec((tm, d), row), pl.BlockSpec((s, d), fixed)]
    if emit_next:
        out_shape = [f32_out[0], bf16_out[0], f32_out[1], bf16_out[1]]
        out_specs = [specs[0], specs[0], specs[1], specs[1]]
    else:
        out_shape, out_specs = f32_out, specs
    res = pl.pallas_call(
        functools.partial(_out_kernel, emit_next=emit_next, conv_T=conv_T),
        grid=(m // tm,),
        in_specs=h_specs + [pl.BlockSpec((s, k), fixed),
                            pl.BlockSpec((k, d), fixed, pipeline_mode=pl.Buffered(1)),
                            pl.BlockSpec((tm, d), row),
                            pl.BlockSpec((s, d), fixed),
                            pl.BlockSpec((1, d), fixed),
                            pl.BlockSpec((1, d), fixed)],
        out_specs=out_specs,
        out_shape=out_shape,
        compiler_params=_cparams("arbitrary"),
    )(*h_ops, hs, w, resid, resid_s, nw_post.reshape(1, d), nw_next.reshape(1, d))
    if emit_next:
        return (res[0], res[2]), (res[1], res[3])
    return (res[0], res[1]), (None, None)


def _log_sigmoid(x):
    return jnp.minimum(x, 0.0) - jnp.log1p(jnp.exp(-jnp.abs(x)))


def _mlstm_kernel(q_ref, k_ref, v_ref, o_ref, g_ref, bg_ref, hn_ref, hg_ref, c_ref, n_ref, m_ref, cn_ref,
                  *, L, H, DK, DV):
    chunk = pl.program_id(1)

    @pl.when(chunk == 0)
    def _():
        cn_ref[...] = jnp.zeros_like(cn_ref)
        m_ref[...] = jnp.zeros_like(m_ref)

    gpre = g_ref[...] + bg_ref[...]
    lf = _log_sigmoid(gpre)
    rows = _iota((L, L), 0)
    cols = _iota((L, L), 1)
    causal = rows >= cols
    tril = jnp.where(causal, 1.0, 0.0).astype(BF16)
    lf_hi, lf_lo = _split_bf16(lf)
    bcum = _dot(tril, lf_hi) + _dot(tril, lf_lo)
    g_t = gpre.T
    b_t = bcum.T
    for h in range(H):
        f = FORGET_BIAS_COL + h
        a_col = bcum[:, f:f + 1]
        i_col = gpre[:, h:h + 1]
        b_row = b_t[f:f + 1, :]
        i_row = g_t[h:h + 1, :]
        m_prev = m_ref[0, h:h + 1, 0:1]
        log_d = jnp.where(causal, a_col - b_row + i_row, NEG)
        log_inter = a_col + m_prev
        m_t = jnp.maximum(log_inter, jnp.max(log_d, axis=-1, keepdims=True))
        d = jnp.exp(log_d - m_t)
        inter = jnp.exp(log_inter - m_t)
        qh = q_ref[:, h * DK:(h + 1) * DK]
        kh = k_ref[:, h * DK:(h + 1) * DK] * (DK ** -0.5)
        v1 = jnp.concatenate([v_ref[:, h * DV:(h + 1) * DV].astype(BF16), jnp.ones((L, LANES), BF16)], axis=1)
        qb = qh.astype(BF16)
        s = _dot_nt(qb, kh.astype(BF16)) * d
        cn_old = cn_ref[h]
        both = _dot(s.astype(BF16), v1) + inter * _dot(qb, cn_old.astype(BF16))
        num, den = both[:, :DV], both[:, DV:]
        scale = 1.0 / jnp.maximum(jnp.abs(den), jnp.exp(-m_t))
        hh = _rms(num * jnp.concatenate([scale] * (DV // LANES), axis=1))
        gate = _sigmoid(o_ref[:, h * DV:(h + 1) * DV])
        hg_ref[:, h * DV:(h + 1) * DV] = (hh * hn_ref[:, h * DV:(h + 1) * DV] * gate).astype(BF16)
        m_new = m_t[L - 1:L, :]
        w_col = jnp.exp(a_col[L - 1:L, :] - a_col + i_col - m_new)
        decay = jnp.exp(log_inter[L - 1:L, :] - m_new)
        cn_new = decay * cn_old + _dot_tn((kh * w_col).astype(BF16), v1)
        cn_ref[h] = cn_new
        m_ref[0, h:h + 1, :] = jnp.broadcast_to(m_new, (1, LANES))

        @pl.when(chunk == pl.num_programs(1) - 1)
        def _():
            c_ref[0, h] = cn_new[:, :DV]
            n_ref[0, h:h + 1, :] = cn_new[:, DV:].T[0:1, :]


def mlstm_prompt(qkvo, gates, b_gates, head_norm, bsz, T):
    H = MLSTM_HEADS
    m = qkvo.shape[0]
    hdv = qkvo.shape[1] // 3
    hdk = hdv // 2
    DK, DV = hdk // H, hdv // H
    L = math.gcd(T, MLSTM_CHUNK)
    nc = T // L
    row = lambda b, c: (b * nc + c, 0)
    state = lambda b, c: (b, 0, 0)
    return pl.pallas_call(
        functools.partial(_mlstm_kernel, L=L, H=H, DK=DK, DV=DV),
        grid=(bsz, nc),
        in_specs=[pl.BlockSpec((L, hdk), row),
                  pl.BlockSpec((L, hdk), lambda b, c: (b * nc + c, 1)),
                  pl.BlockSpec((L, hdv), lambda b, c: (b * nc + c, 1)),
                  pl.BlockSpec((L, hdv), lambda b, c: (b * nc + c, 2)),
                  pl.BlockSpec((L, LANES), row),
                  pl.BlockSpec((1, LANES), lambda b, c: (0, 0)),
                  pl.BlockSpec((1, hdv), lambda b, c: (0, 0))],
        out_specs=[pl.BlockSpec((L, hdv), row),
                   pl.BlockSpec((1, H, DK, DV), lambda b, c: (b, 0, 0, 0)),
                   pl.BlockSpec((1, H, DK), state),
                   pl.BlockSpec((1, H, LANES), state)],
        out_shape=[jax.ShapeDtypeStruct((m, hdv), BF16),
                   jax.ShapeDtypeStruct((bsz, H, DK, DV), F32),
                   jax.ShapeDtypeStruct((bsz, H, DK), F32),
                   jax.ShapeDtypeStruct((bsz, H, LANES), F32)],
        scratch_shapes=[pltpu.VMEM((H, DK, DV + LANES), F32)],
        compiler_params=_cparams("arbitrary", "arbitrary"),
    )(qkvo, qkvo, qkvo, qkvo, gates, _pad_lanes(b_gates.reshape(1, -1)), head_norm.reshape(1, hdv))


def _pad_lanes(x, width=LANES):
    return jnp.pad(x, ((0, 0), (0, width - x.shape[-1])))


def _to_col(row, n):
    eye = _iota((n, n), 0) == _iota((n, n), 1)
    return jnp.sum(jnp.where(eye, jnp.broadcast_to(row, (n, n)), 0.0), axis=-1, keepdims=True)


def _mlstm_step_kernel(x_ref, g_ref, bg_ref, hn_ref, c0_ref, n0_ref, m0_ref, hg_ref, c_ref, n_ref, m_ref,
                       *, H, DK, DV):
    x = x_ref[0]
    gpre = g_ref[0] + bg_ref[...]
    lf = _log_sigmoid(gpre)
    hdk = H * DK
    hdv = H * DV
    for h in range(H):
        f = FORGET_BIAS_COL + h
        ig = gpre[:, h:h + 1]
        m_prev = m0_ref[0, h:h + 1, :]
        log_inter = lf[:, f:f + 1] + m_prev
        m_t = jnp.maximum(log_inter, ig)
        d = jnp.exp(ig - m_t)
        inter = jnp.exp(log_inter - m_t)
        q = x[:, h * DK:(h + 1) * DK]
        k = x[:, hdk + h * DK:hdk + (h + 1) * DK] * (DK ** -0.5)
        v = x[:, 2 * hdk + h * DV:2 * hdk + (h + 1) * DV]
        og = x[:, 2 * hdk + hdv + h * DV:2 * hdk + hdv + (h + 1) * DV]
        c_old = c0_ref[0, h]
        n_old = n0_ref[0, h:h + 1, :]
        s = jnp.sum(q * k, axis=-1, keepdims=True) * d
        q_col = _to_col(q, DK)
        k_col = _to_col(k, DK)
        num = s * v + inter * jnp.sum(q_col * c_old, axis=0, keepdims=True)
        den = s + inter * jnp.sum(q * n_old, axis=-1, keepdims=True)
        hh = _rms(num / jnp.maximum(jnp.abs(den), jnp.exp(-m_t)))
        hg_ref[0, :, h * DV:(h + 1) * DV] = (hh * hn_ref[:, h * DV:(h + 1) * DV] * _sigmoid(og)).astype(BF16)
        c_ref[0, h] = inter * c_old + d * (k_col * v)
        n_ref[0, h:h + 1, :] = inter * n_old + d * k
        m_ref[0, h:h + 1, :] = m_t


def mlstm_step(qkvo, gates, b_gates, head_norm, c0, n0, m0):
    bsz, H, DK, DV = c0.shape
    wid = qkvo.shape[1]
    hdv = H * DV
    one = lambda b: (b, 0, 0)
    hg, c1, n1, m1 = pl.pallas_call(
        functools.partial(_mlstm_step_kernel, H=H, DK=DK, DV=DV),
        grid=(bsz,),
        in_specs=[pl.BlockSpec((1, 1, wid), one),
                  pl.BlockSpec((1, 1, LANES), one),
                  pl.BlockSpec((1, LANES), lambda b: (0, 0)),
                  pl.BlockSpec((1, hdv), lambda b: (0, 0)),
                  pl.BlockSpec((1, H, DK, DV), lambda b: (b, 0, 0, 0)),
                  pl.BlockSpec((1, H, DK), one),
                  pl.BlockSpec((1, H, 1), one)],
        out_specs=[pl.BlockSpec((1, 1, hdv), one),
                   pl.BlockSpec((1, H, DK, DV), lambda b: (b, 0, 0, 0)),
                   pl.BlockSpec((1, H, DK), one),
                   pl.BlockSpec((1, H, 1), one)],
        out_shape=[jax.ShapeDtypeStruct((bsz, 1, hdv), BF16),
                   jax.ShapeDtypeStruct((bsz, H, DK, DV), F32),
                   jax.ShapeDtypeStruct((bsz, H, DK), F32),
                   jax.ShapeDtypeStruct((bsz, H, 1), F32)],
        compiler_params=_cparams("parallel"),
    )(qkvo.reshape(bsz, 1, wid), gates.reshape(bsz, 1, LANES), _pad_lanes(b_gates.reshape(1, -1)),
      head_norm.reshape(1, hdv), c0, n0, m0.reshape(bsz, H, 1))
    return hg.reshape(bsz, hdv), c1, n1, m1.reshape(bsz, H)


def _conv_step_kernel(bg_ref, z_ref, buf_ref, cw_ref, o_ref):
    y = cw_ref[CONV_W - 1:CONV_W, :] * z_ref[...]
    for j in range(CONV_W - 1):
        y = y + cw_ref[j:j + 1, :] * buf_ref[j]
    o_ref[...] = bg_ref[...] * y


def conv_gate_step(bg, z, buf, conv_w):
    bsz, d = z.shape
    full = lambda: (0, 0)
    return pl.pallas_call(
        _conv_step_kernel,
        in_specs=[pl.BlockSpec((bsz, d), full), pl.BlockSpec((bsz, d), full),
                  pl.BlockSpec((CONV_W - 1, bsz, d), lambda: (0, 0, 0)), pl.BlockSpec((CONV_W, d), full)],
        out_specs=pl.BlockSpec((bsz, d), full),
        out_shape=jax.ShapeDtypeStruct((bsz, d), F32),
    )(bg, z, buf, conv_w)


def _gelu(x):
    return 0.5 * x * (1.0 + jnp.tanh(math.sqrt(2.0 / math.pi) * (x + 0.044715 * x * x * x)))


def _cmp_finish(a, posw, b1, w2, b2):
    hid = a.shape[1] // 2
    a1 = a[:, hid:]
    a1_next = jnp.concatenate([a1[1:], a1[:1]], axis=0)
    pre = a[:, :hid] + a1_next + b1 + posw[0:1, :hid] + posw[1:2, hid:]
    return _dot(_gelu(pre).astype(BF16), w2) + b2


def _cmp_prompt_kernel(k_ref, v_ref, w1_ref, pos_ref, b1_ref, w2_ref, b2_ref, ck_ref, cv_ref, *, nch):
    for idx, (x_ref, o_ref) in enumerate(((k_ref, ck_ref), (v_ref, cv_ref))):
        w1 = w1_ref[idx]
        lhs = jnp.concatenate([x_ref[pl.ds(p, nch, stride=CMP_STRIDE), :] for p in range(CMP_STRIDE)], axis=1)
        a = _dot(lhs.astype(BF16), w1)
        posw = _dot(pos_ref[idx], w1)
        o_ref[0, 0] = _cmp_finish(a, posw, b1_ref[idx], w2_ref[idx], b2_ref[idx])


def _cmp_weights(cmp_pos, cmp_w1, cmp_b1, cmp_w2, cmp_b2):
    two, ratio, stride, hd, hid = cmp_w1.shape
    w1 = cmp_w1.transpose(0, 2, 3, 1, 4).reshape(two, stride * hd, ratio * hid).astype(BF16)
    pos = cmp_pos.reshape(two, ratio, stride * hd)
    pos = jnp.pad(pos, ((0, 0), (0, SUBLANES - ratio), (0, 0))).astype(BF16)
    return w1, pos, cmp_b1.reshape(two, 1, hid), cmp_w2.astype(BF16), cmp_b2.reshape(two, 1, hd)


def cmp_prompt(kc, vc, bsz, T, cw):
    w1, pos, b1, w2, b2 = cw
    nch = T // CMP_STRIDE
    G = NSA_KV
    out = jax.ShapeDtypeStruct((bsz, G, nch, NSA_HD), F32)
    ospec = pl.BlockSpec((1, 1, nch, NSA_HD), lambda b, g: (b, g, 0, 0))
    whole = lambda a: pl.BlockSpec(a.shape, lambda b, g: (0,) * a.ndim)
    seq = pl.BlockSpec((T, NSA_HD), lambda b, g: (b, g))
    return pl.pallas_call(
        functools.partial(_cmp_prompt_kernel, nch=nch),
        grid=(bsz, G),
        in_specs=[seq, seq, whole(w1), whole(pos), whole(b1), whole(w2), whole(b2)],
        out_specs=[ospec, ospec],
        out_shape=[out, out],
        compiler_params=_cparams("parallel", "parallel"),
    )(kc, vc, w1, pos, b1, w2, b2)


def _overlap_t(nbs_pad, nbc_pad):
    j = _iota((nbs_pad, nbc_pad), 0)
    n = _iota((nbs_pad, nbc_pad), 1)
    lo = jnp.maximum(n * CMP_STRIDE, j * SEL_BLOCK)
    hi = jnp.minimum(n * CMP_STRIDE + CMP_BLOCK, j * SEL_BLOCK + SEL_BLOCK)
    return (jnp.maximum(hi - lo, 0) // CMP_STRIDE).astype(F32)


def _rank_select(score, n_sel, axis):
    n = score.shape[axis]
    idx = _iota(score.shape, axis)
    cnt = jnp.zeros(score.shape, F32)
    for jp in range(n):
        other = lax.slice_in_dim(score, jp, jp + 1, axis=axis)
        tie = jnp.where(idx > jp, 1.0, 0.0)
        cnt = cnt + jnp.where(other > score, 1.0, jnp.where(other == score, tie, 0.0))
    return jnp.where(cnt < n_sel, 1.0, 0.0)


def _attn_first(state, s, vt_b):
    m_ref, l_ref, acc_ref = state
    m = jnp.max(s, axis=0, keepdims=True)
    p = jnp.exp2(s - m)
    m_ref[...] = m
    l_ref[...] = jnp.sum(p, axis=0, keepdims=True)
    acc_ref[...] = _dot(vt_b, p.astype(BF16))


def _attn_more(state, s, vt_b):
    m_ref, l_ref, acc_ref = state
    m_old = m_ref[...]
    m_new = jnp.maximum(m_old, jnp.max(s, axis=0, keepdims=True))
    alpha = jnp.exp2(m_old - m_new)
    p = jnp.exp2(s - m_new)
    m_ref[...] = m_new
    l_ref[...] = alpha * l_ref[...] + jnp.sum(p, axis=0, keepdims=True)
    acc_ref[...] = alpha * acc_ref[...] + _dot(vt_b, p.astype(BF16))


def _nsa_attn_kernel(q_ref, ck_ref, cv_ref, ks_ref, vs_ref, kw_ref, vw_ref, g_ref, bg_ref, o_ref,
                     ksb_ref, vst_ref, kwb_ref, vwt_ref, ms_ref, ls_ref, as_ref, mw_ref, lw_ref, aw_ref,
                     *, tq, hpg, nbc, nbs, nbs_pad):
    qi = pl.program_id(2)
    t0 = qi * tq
    cols = hpg * tq
    tk = tq
    n_kt = ks_ref.shape[0] // tk
    nbc_pad = ck_ref.shape[2]

    @pl.when(qi == 0)
    def _():
        ksb_ref[...] = ks_ref[...].astype(BF16)
        kwb_ref[...] = kw_ref[...].astype(BF16)
        for kt in range(n_kt):
            vst_ref[kt] = vs_ref[kt * tk:(kt + 1) * tk, :].T.astype(BF16)
            vwt_ref[kt] = vw_ref[kt * tk:(kt + 1) * tk, :].T.astype(BF16)

    qb = (jnp.concatenate([q_ref[:, h * NSA_HD:(h + 1) * NSA_HD] for h in range(hpg)], axis=0)
          * (NSA_HD ** -0.5 * math.log2(math.e))).astype(BF16)

    n = _iota((nbc_pad, cols), 0)
    t_col = t0 + jnp.concatenate([_iota((nbc_pad, tq), 1)] * hpg, axis=1)
    cmask = jnp.where(n < nbc, jnp.where(n * CMP_STRIDE + (CMP_BLOCK - 1) <= t_col, 1.0, 0.0), 0.0)
    sc = jnp.where(cmask > 0.5, _dot_nt(ck_ref[0, 0].astype(BF16), qb), NEG)
    pe = jnp.exp2(sc - jnp.max(sc, axis=0, keepdims=True))
    p_cmp = pe / jnp.sum(pe, axis=0, keepdims=True) * cmask
    o_cmp = _dot(cv_ref[0, 0].T.astype(BF16), p_cmp.astype(BF16))

    p_sum = p_cmp[:, 0:tq]
    for h in range(1, hpg):
        p_sum = p_sum + p_cmp[:, h * tq:(h + 1) * tq]
    ov_t = _overlap_t(nbs_pad, nbc_pad).astype(BF16)
    p_hi, p_lo = _split_bf16(p_sum)
    imp_t = _dot(ov_t, p_hi) + _dot(ov_t, p_lo)
    j = _iota((nbs_pad, tq), 0)
    t = t0 + _iota((nbs_pad, tq), 1)
    t_blk = t // SEL_BLOCK
    forced = jnp.where(j == 0, 1.0, jnp.where(j == t_blk, 1.0, jnp.where(j == t_blk - 1, 1.0, 0.0)))
    visible = jnp.where(j < nbs, jnp.where(j * SEL_BLOCK <= t, 1.0, 0.0), 0.0)
    score = jnp.where(visible > 0.5, jnp.where(forced > 0.5, BIG, imp_t), NEG)
    sel_t = _rank_select(score, min(N_SELECT, nbs), axis=0).astype(BF16)

    k_ofs = _iota((tk, tq), 0)
    q_ofs = _iota((tk, tq), 1)
    causal = k_ofs <= q_ofs
    per_head = lambda bias: jnp.concatenate([bias] * hpg, axis=1)

    def scores(kb_ref, kt):
        k_b = kb_ref[pl.ds(pl.multiple_of(kt * tk, tk), tk), :]
        return _dot_nt(k_b, qb)

    def slc_bias(kt):
        blk = (kt * tk + _iota((tk, nbs_pad), 0)) // SEL_BLOCK
        expand = jnp.where(blk == _iota((tk, nbs_pad), 1), 1.0, 0.0).astype(BF16)
        return (_dot(expand, sel_t) - 1.0) * BIG

    slc = (ms_ref, ls_ref, as_ref)
    _attn_first(slc, scores(ksb_ref, qi) + per_head(jnp.where(causal, slc_bias(qi), NEG)), vst_ref[qi])

    def slc_body(kt, _):
        _attn_more(slc, scores(ksb_ref, kt) + per_head(slc_bias(kt)), vst_ref[kt])
        return 0

    lax.fori_loop(0, qi, slc_body, 0)

    n_back = WINDOW // tk
    win = (mw_ref, lw_ref, aw_ref)
    _attn_first(win, scores(kwb_ref, qi) + per_head(jnp.where(causal, 0.0, NEG)), vwt_ref[qi])
    for back in range(1, n_back + 1):
        @pl.when(qi >= back)
        def _():
            s = scores(kwb_ref, qi - back)
            if back == n_back:
                s = s + per_head(jnp.where(k_ofs >= q_ofs, 0.0, NEG))
            _attn_more(win, s, vwt_ref[qi - back])

    o_slc = as_ref[...] / ls_ref[...]
    o_win = aw_ref[...] / lw_ref[...]
    gate_t = _sigmoid(g_ref[...] + bg_ref[...]).T
    for h in range(hpg):
        sl = slice(h * tq, (h + 1) * tq)
        o_t = (gate_t[3 * h:3 * h + 1, :] * o_cmp[:, sl] + gate_t[3 * h + 1:3 * h + 2, :] * o_slc[:, sl]
               + gate_t[3 * h + 2:3 * h + 3, :] * o_win[:, sl])
        o_ref[:, h * NSA_HD:(h + 1) * NSA_HD] = o_t.T.astype(BF16)


def nsa_attn_prompt(q, ck, cv, ks, vs, kw, vw, gates, b_gate, bsz, T):
    G, HPG = NSA_KV, NSA_HPG
    tq = math.gcd(T, ATT_TILE)
    assert WINDOW % tq == 0
    nq = T // tq
    nch = T // CMP_STRIDE
    nbc = nch - CMP_BLOCK // CMP_STRIDE + 1
    nbs = -(-T // SEL_BLOCK)
    nbs_pad = -(-nbs // SUBLANES) * SUBLANES
    cols = HPG * tq
    kv_scratch = [pltpu.VMEM((T, NSA_HD), BF16), pltpu.VMEM((nq, NSA_HD, tq), BF16)] * 2
    stats = [pltpu.VMEM((1, cols), F32), pltpu.VMEM((1, cols), F32), pltpu.VMEM((NSA_HD, cols), F32)] * 2
    seq = pl.BlockSpec((T, NSA_HD), lambda b, g, i: (b, g))
    cspec = pl.BlockSpec((1, 1, nch, NSA_HD), lambda b, g, i: (b, g, 0, 0))
    qspec = pl.BlockSpec((tq, HPG * NSA_HD), lambda b, g, i: (b * nq + i, g))
    return pl.pallas_call(
        functools.partial(_nsa_attn_kernel, tq=tq, hpg=HPG, nbc=nbc, nbs=nbs, nbs_pad=nbs_pad),
        grid=(bsz, G, nq),
        in_specs=[qspec, cspec, cspec, seq, seq, seq, seq,
                  pl.BlockSpec((tq, LANES), lambda b, g, i: (b * nq + i, g)),
                  pl.BlockSpec((1, LANES), lambda b, g, i: (0, g))],
        out_specs=qspec,
        out_shape=jax.ShapeDtypeStruct((bsz * T, G * HPG * NSA_HD), BF16),
        scratch_shapes=kv_scratch + stats,
        compiler_params=_cparams("arbitrary", "arbitrary", "arbitrary"),
    )(q, ck, cv, ks, vs, kw, vw, gates, b_gate)


CHUNK_ROWS = CMP_STRIDE * NSA_KV
CHUNK_PITCH = CHUNK_ROWS + SUBLANES


def _dec_cmp_kernel(tbl_ref, *refs, P, nbc, nbs, nbs_pad, p0):
    k_pages, v_pages = refs[:P], refs[P:2 * P]
    w1_ref, q_ref, pos_ref, b1_ref, w2_ref, b2_ref, o_ref, sel_ref, pad_ref, ak_ref, av_ref = refs[2 * P:]
    cpp = LANES // CMP_STRIDE
    step = pl.program_id(1)
    for idx, (pages, a_ref) in enumerate(((k_pages, ak_ref), (v_pages, av_ref))):
        for i, pg in enumerate(pages):
            for c in range(cpp):
                pad_ref[idx * P + i, c * CHUNK_PITCH:c * CHUNK_PITCH + CHUNK_ROWS, :] = (
                    pg[0, c * CHUNK_ROWS:(c + 1) * CHUNK_ROWS, :])
        blocks = []
        for g in range(NSA_KV):
            for i in range(P):
                blocks.append(jnp.concatenate(
                    [pad_ref[idx * P + i, pl.ds(p * NSA_KV + g, cpp, stride=CHUNK_PITCH), :]
                     for p in range(CMP_STRIDE)], axis=1))
        a = _dot(jnp.concatenate(blocks, axis=0).astype(BF16), w1_ref[idx])
        per_g = P * cpp
        for g in range(NSA_KV):
            a_ref[g, pl.ds(pl.multiple_of(step * per_g, per_g), per_g), :] = a[g * per_g:(g + 1) * per_g]

    @pl.when(step == pl.num_programs(1) - 1)
    def _():
        _dec_cmp_finish(ak_ref, av_ref, q_ref, pos_ref, w1_ref, b1_ref, w2_ref, b2_ref, o_ref, sel_ref,
                        nbc=nbc, nbs=nbs, nbs_pad=nbs_pad, p0=p0)


def _dec_cmp_finish(ak_ref, av_ref, q_ref, pos_ref, w1_ref, b1_ref, w2_ref, b2_ref, o_ref, sel_ref,
                    *, nbc, nbs, nbs_pad, p0):
    nch = ak_ref.shape[1]
    for g in range(NSA_KV):
        cks = []
        for idx, a_ref in enumerate((ak_ref, av_ref)):
            posw = _dot(pos_ref[idx], w1_ref[idx])
            cks.append(_cmp_finish(a_ref[g], posw, b1_ref[idx], w2_ref[idx], b2_ref[idx]))
        ck, cv = cks
        qb = (q_ref[0, g] * (NSA_HD ** -0.5)).astype(BF16)
        n = _iota((SUBLANES, nch), 1)
        cmask = jnp.where(n < nbc, jnp.where(n * CMP_STRIDE + (CMP_BLOCK - 1) <= p0, 1.0, 0.0), 0.0)
        sc = jnp.where(cmask > 0.5, _dot_nt(qb, ck.astype(BF16)), NEG)
        pe = jnp.exp(sc - jnp.max(sc, axis=-1, keepdims=True))
        p_cmp = pe / jnp.sum(pe, axis=-1, keepdims=True) * cmask
        o_ref[0, g] = _dot(p_cmp.astype(BF16), cv.astype(BF16))

        head = jnp.where(_iota((SUBLANES, nch), 0) < NSA_HPG, p_cmp, 0.0)
        p_sum = jnp.broadcast_to(jnp.sum(head, axis=0, keepdims=True), (SUBLANES, nch))
        ov_t = _overlap_t(nbs_pad, nch).astype(BF16)
        p_hi, p_lo = _split_bf16(p_sum)
        imp = _dot_nt(p_hi, ov_t) + _dot_nt(p_lo, ov_t)
        j = _iota((SUBLANES, nbs_pad), 1)
        t_blk = p0 // SEL_BLOCK
        forced = jnp.where(j == 0, 1.0, jnp.where(j == t_blk, 1.0, jnp.where(j == t_blk - 1, 1.0, 0.0)))
        visible = jnp.where(j < nbs, jnp.where(j * SEL_BLOCK <= p0, 1.0, 0.0), 0.0)
        score = jnp.where(visible > 0.5, jnp.where(forced > 0.5, BIG, imp), NEG)
        s_row = jnp.broadcast_to(score[0:1], (nbs_pad, nbs_pad))
        s_col = jnp.concatenate([jnp.broadcast_to(score[0:1], (LANES, nbs_pad)).T] * (nbs_pad // LANES), axis=1)
        jr = _iota((nbs_pad, nbs_pad), 0)
        jc = _iota((nbs_pad, nbs_pad), 1)
        tie = jnp.where(jc < jr, 1.0, 0.0)
        beats = jnp.where(s_row > s_col, 1.0, jnp.where(s_row == s_col, tie, 0.0))
        rank = jnp.sum(beats, axis=-1, keepdims=True)
        slot = _iota((nbs_pad, LANES), 1).astype(F32)
        blk = _iota((nbs_pad, LANES), 0).astype(F32)
        picked = jnp.sum(jnp.where(rank == slot, blk, 0.0), axis=0, keepdims=True)
        sel_ref[0, g:g + 1, :] = picked.astype(jnp.int32)


def dec_cmp(pool_k, pool_v, table, q, cw, p0):
    w1, pos, b1, w2, b2 = cw
    n_pool, page, G, hd = pool_k.shape
    assert page == LANES and G == NSA_KV and hd == NSA_HD
    bsz, n_pages = table.shape
    assert p0 == n_pages * page
    P = math.gcd(n_pages, PAGES_PER_STEP)
    cpp = page // CMP_STRIDE
    nch = n_pages * cpp
    hid2 = w1.shape[2]
    nbc = nch - CMP_BLOCK // CMP_STRIDE + 1
    nbs = -(-(p0 + 1) // SEL_BLOCK)
    nbs_pad = -(-nbs // LANES) * LANES
    pk = pool_k.reshape(n_pool, page * G, hd)
    pv = pool_v.reshape(n_pool, page * G, hd)
    pspec = lambda i: pl.BlockSpec((1, page * G, hd),
                                   functools.partial(lambda b, s, tbl, i: (tbl[b * n_pages + s * P + i], 0, 0), i=i))
    whole = lambda a: pl.BlockSpec(a.shape, lambda b, s, tbl: (0,) * a.ndim)
    per_seq = lambda shape: pl.BlockSpec((1,) + shape, lambda b, s, tbl: (b,) + (0,) * len(shape))
    grid_spec = pltpu.PrefetchScalarGridSpec(
        num_scalar_prefetch=1,
        grid=(bsz, n_pages // P),
        in_specs=[pspec(i) for i in range(P)] * 2
        + [whole(w1), per_seq((G, SUBLANES, hd)), whole(pos), whole(b1), whole(w2), whole(b2)],
        out_specs=[per_seq((G, SUBLANES, hd)), per_seq((G, LANES))],
        scratch_shapes=[pltpu.VMEM((2 * P, cpp * CHUNK_PITCH, hd), F32),
                        pltpu.VMEM((G, nch, hid2), F32), pltpu.VMEM((G, nch, hid2), F32)],
    )
    return pl.pallas_call(
        functools.partial(_dec_cmp_kernel, P=P, nbc=nbc, nbs=nbs, nbs_pad=nbs_pad, p0=p0),
        grid_spec=grid_spec,
        out_shape=[jax.ShapeDtypeStruct((bsz, G, SUBLANES, hd), F32),
                   jax.ShapeDtypeStruct((bsz, G, LANES), jnp.int32)],
        compiler_params=_cparams("arbitrary", "arbitrary"),
    )(table.reshape(-1), *([pk] * P), *([pv] * P), w1, q, pos, b1, w2, b2)


def _softmax_with_new_key(qb, k_b, v_b, bias, s_new, v_new):
    s = _dot_nt(qb, k_b)
    if bias is not None:
        s = s + bias
    mx = jnp.maximum(jnp.max(s, axis=-1, keepdims=True), s_new)
    p = jnp.exp(s - mx)
    p_new = jnp.exp(s_new - mx)
    return (_dot(p.astype(BF16), v_b) + p_new * v_new) / (jnp.sum(p, axis=-1, keepdims=True) + p_new)


def _dec_attn_kernel(sel_ref, tbl_ref, *refs, n_cache_blocks, n_sel):
    ks_refs, vs_refs = refs[:n_sel], refs[n_sel:2 * n_sel]
    q_ref, new_ref, kw_ref, vw_ref, oc_ref, g_ref, bg_ref, o_ref = refs[2 * n_sel:]
    b, g = pl.program_id(0), pl.program_id(1)
    q = q_ref[0, 0] * (NSA_HD ** -0.5)
    qb = q.astype(BF16)
    new = new_ref[0, 0]

    mine = lambda r, n: r[pl.ds(g, n, stride=NSA_KV), :]
    k_all = jnp.concatenate([mine(r.at[0, 0], SEL_BLOCK) for r in ks_refs], axis=0).astype(BF16)
    v_all = jnp.concatenate([mine(r.at[0, 0], SEL_BLOCK) for r in vs_refs], axis=0).astype(BF16)
    slot = _iota((SUBLANES, n_sel * SEL_BLOCK), 1) // SEL_BLOCK
    bias = jnp.zeros((SUBLANES, n_sel * SEL_BLOCK), F32)
    for s in range(n_sel):
        blk = sel_ref[(b * NSA_KV + g) * n_sel + s]
        bias = jnp.where(slot == s, jnp.where(blk < n_cache_blocks, 0.0, NEG), bias)
    o_slc = _softmax_with_new_key(qb, k_all, v_all, bias, jnp.sum(q * new[0:1], axis=-1, keepdims=True), new[1:2])

    n_win = kw_ref.shape[1] // NSA_KV
    o_win = _softmax_with_new_key(qb, mine(kw_ref.at[0], n_win).astype(BF16), mine(vw_ref.at[0], n_win).astype(BF16),
                                  None,
                                  jnp.sum(q * new[2:3], axis=-1, keepdims=True), new[3:4])

    gate = jnp.broadcast_to(_sigmoid(g_ref[0, 0] + bg_ref[...]), (SUBLANES, LANES))
    head = _iota((SUBLANES, LANES), 0)
    lane = _iota((SUBLANES, LANES), 1)
    o = jnp.zeros((SUBLANES, NSA_HD), F32)
    for br, ob in enumerate((oc_ref[0, 0], o_slc, o_win)):
        o = o + jnp.sum(jnp.where(lane == 3 * head + br, gate, 0.0), axis=-1, keepdims=True) * ob
    o_ref[0, 0] = o


def dec_attn(sel, table, pool_ks, pool_vs, q, new, kw_past, vw_past, o_cmp, gates, b_gate):
    n_pool, page, G, hd = pool_ks.shape
    bsz, n_pages = table.shape
    halves = page // SEL_BLOCK
    wb = kw_past.shape[1]
    assert wb <= WINDOW
    n_sel = sel.shape[-1]
    pk = pool_ks.reshape(n_pool, halves, SEL_BLOCK * G, hd)
    pv = pool_vs.reshape(n_pool, halves, SEL_BLOCK * G, hd)
    n_cache_blocks = n_pages * halves

    def page_map(b, g, sel_ref, tbl_ref, s):
        blk = jnp.minimum(sel_ref[(b * G + g) * n_sel + s], n_cache_blocks - 1)
        return (tbl_ref[b * n_pages + blk // halves], blk % halves, 0, 0)

    bg_map = lambda b, g, sel_ref, tbl_ref: (b, g, 0, 0)
    small = pl.BlockSpec((1, 1, SUBLANES, hd), bg_map)
    wspec = pl.BlockSpec((1, wb * G, hd), lambda b, g, sel_ref, tbl_ref: (b, 0, 0))
    blocks = [pl.BlockSpec((1, 1, SEL_BLOCK * G, hd), functools.partial(page_map, s=s)) for s in range(n_sel)]
    grid_spec = pltpu.PrefetchScalarGridSpec(
        num_scalar_prefetch=2,
        grid=(bsz, G),
        in_specs=blocks + blocks + [small, small, wspec, wspec, small,
                                    pl.BlockSpec((1, 1, 1, LANES), bg_map),
                                    pl.BlockSpec((1, LANES), lambda b, g, sel_ref, tbl_ref: (0, g))],
        out_specs=small,
    )
    return pl.pallas_call(
        functools.partial(_dec_attn_kernel, n_cache_blocks=n_cache_blocks, n_sel=n_sel),
        grid_spec=grid_spec,
        out_shape=jax.ShapeDtypeStruct((bsz, G, SUBLANES, hd), F32),
        compiler_params=_cparams("parallel", "parallel"),
    )(sel.reshape(-1), table.reshape(-1), *([pk] * n_sel), *([pv] * n_sel), q, new,
      kw_past.reshape(bsz, wb * G, hd), vw_past.reshape(bsz, wb * G, hd), o_cmp,
      gates.reshape(bsz, G, 1, LANES), b_gate)


def _gate_weights(wt, layer, row0, n):
    rows = lax.slice(wt, (layer, row0, 0), (layer + 1, row0 + n, wt.shape[2]))
    return jnp.pad(rows, ((0, 0), (0, LANES - n), (0, 0))), 0


def _nsa_gate_weights(wt, layer, b_gate, row0):
    per = 3 * NSA_HPG
    k = wt.shape[2]
    rows = lax.slice(wt, (layer, row0, 0), (layer + 1, row0 + NSA_KV * per, k)).reshape(NSA_KV, per, k)
    wg = jnp.pad(rows, ((0, 0), (0, LANES - per), (0, 0))).reshape(1, NSA_KV * LANES, k)
    bg = jnp.pad(b_gate.reshape(NSA_KV, per), ((0, 0), (0, LANES - per))).reshape(1, NSA_KV * LANES)
    return (wg, 0), bg


def kernel(x_prompt, x_sample, state_a_C, state_a_n, state_a_m, state_b_conv, cache_c_k_cmp, cache_c_v_cmp, cache_c_k_slc, cache_c_v_slc, cache_c_k_win, cache_c_v_win, page_table, norm_w, ffn_w_in, ffn_w_out, a_w_in, a_b_gates, a_head_norm, a_w_out, b_w_in, b_conv_w, b_w_out, c_w_in, c_b_gate, c_cmp_pos, c_cmp_w1, c_cmp_b1, c_cmp_w2, c_cmp_b2, c_w_out):
    bp, T, D = x_prompt.shape
    bs = x_sample.shape[0]
    assert x_sample.shape[1] == 1
    depth = norm_w.shape[0]
    G, hd = NSA_KV, NSA_HD
    past_len = page_table.shape[1] * cache_c_k_cmp.shape[2]
    xp = x_prompt.reshape(bp * T, D)
    xs = x_sample.reshape(bs, D)
    hp_in = norm_cast(xp, norm_w[0, 0])
    hs_in = norm_cast(xs, norm_w[0, 0])
    a_wt = jnp.swapaxes(a_w_in, 1, 2)
    c_wt = jnp.swapaxes(c_w_in, 1, 2)
    w_outs = {0: a_w_out, 1: b_w_out, 2: c_w_out}
    pa, sa, pb, sb, pc, sc = [], [], [], [], [], []
    for i in range(depth):
        kind, j = i % 3, i // 3
        nw = norm_w[i]
        w_out_f32 = (w_outs[kind], j)
        conv = None
        if kind == 0:
            hdv = a_head_norm.shape[1]
            n_main = 3 * hdv
            wg = _gate_weights(a_wt, j, n_main, 2 * MLSTM_HEADS)
            qkvo_p, qkvo_s, w_out = proj_plain(hp_in, hs_in, (a_wt, j), 0, n_main, 1024, transposed=True,
                                               cast=w_out_f32)
            g_p, g_s = proj_plain(hp_in, hs_in, wg, 0, LANES, LANES, transposed=True)
            hp, c_p, n_p, m_p = mlstm_prompt(qkvo_p, g_p, a_b_gates[j], a_head_norm[j], bp, T)
            hs, c_s, n_s, m_s = mlstm_step(qkvo_s, g_s, a_b_gates[j], a_head_norm[j],
                                           state_a_C[j], state_a_n[j], state_a_m[j])
            pa.append((c_p, n_p, m_p[:, :, 0]))
            sa.append((c_s, n_s, m_s))
        elif kind == 1:
            (bg_p, z_p), (bg_s, z_s), w_out = proj_conv_in(hp_in, hs_in, (b_w_in, j), D, w_out_f32)
            hp, conv = None, (bg_p, z_p, b_conv_w[j], T)
            buf = state_b_conv[j]
            hs = conv_gate_step(bg_s, z_s, buf.transpose(1, 0, 2), b_conv_w[j]).astype(BF16)
            pb.append(z_p.reshape(bp, T, D)[:, T - (CONV_W - 1):])
            sb.append(jnp.concatenate([buf, z_s[:, None]], axis=1)[:, -(CONV_W - 1):])
        else:
            nq_cols = NSA_HPG * G * hd
            nheads = G * NSA_HPG
            wg, bgate = _nsa_gate_weights(c_wt, j, c_b_gate[j], nq_cols + 6 * G * hd)
            cw = _cmp_weights(c_cmp_pos[j], c_cmp_w1[j], c_cmp_b1[j], c_cmp_w2[j], c_cmp_b2[j])
            q_p, q_s, w_out = proj_plain(hp_in, hs_in, (c_wt, j), 0, nq_cols, 1024, transposed=True, cast=w_out_f32)
            kv_p, kv_rows_p, kv_s = [], [], []
            for half in range(2):
                (tok_p, rows_p), (tok_s, _) = proj_groups(hp_in, hs_in, (c_wt, j), nq_cols + half * 3 * G * hd,
                                                         3, G * hd, transposed=True)
                kv_p, kv_rows_p, kv_s = kv_p + list(tok_p), kv_rows_p + list(rows_p), kv_s + list(tok_s)
            gt_p, gt_s = proj_plain(hp_in, hs_in, wg, 0, G * LANES, G * LANES, transposed=True)
            ck, cv = cmp_prompt(kv_p[0], kv_p[1], bp, T, cw)
            hp = nsa_attn_prompt(q_p, ck, cv, kv_p[2], kv_p[3], kv_p[4], kv_p[5], gt_p, bgate, bp, T)
            kv_rows_p = [a.reshape(bp, T, G, hd) for a in kv_rows_p]
            keep = min(WINDOW, T)
            pc.append(tuple(kv_rows_p[:4]) + tuple(a[:, T - keep:] for a in kv_rows_p[4:]))
            q_s = jnp.pad(q_s.reshape(bs, G, NSA_HPG, hd), ((0, 0), (0, 0), (0, SUBLANES - NSA_HPG), (0, 0)))
            kv_s = jnp.stack([a.reshape(bs, G, hd) for a in kv_s], axis=1)
            new = jnp.pad(kv_s[:, 2:6].transpose(0, 2, 1, 3), ((0, 0), (0, 0), (0, SUBLANES - 4), (0, 0)))
            o_cmp, ranked = dec_cmp(cache_c_k_cmp[j], cache_c_v_cmp[j], page_table, q_s, cw, past_len)
            n_sel = min(N_SELECT, -(-(past_len + 1) // SEL_BLOCK))
            o_s = dec_attn(ranked[:, :, :n_sel], page_table, cache_c_k_slc[j], cache_c_v_slc[j], q_s, new,
                           cache_c_k_win[j], cache_c_v_win[j], o_cmp, gt_s, bgate)
            hs = o_s[:, :, :NSA_HPG].reshape(bs, nheads * hd).astype(BF16)
            keep_s = min(WINDOW, cache_c_k_win.shape[2] + 1)
            win = [jnp.concatenate([c[j], kv_s[:, a][:, None]], axis=1)[:, -keep_s:]
                   for c, a in ((cache_c_k_win, 4), (cache_c_v_win, 5))]
            sc.append(tuple(kv_s[:, a][:, None] for a in range(4)) + tuple(win))
        last = i == depth - 1
        nw_next = norm_w[i + 1, 0] if not last else nw[0]
        (xp, xs), (hp_mid, hs_mid) = proj_out(hp, hs, w_out, xp, xs, nw[1], nw[2], conv=conv)
        act_p, act_s, f_out = proj_swiglu(hp_mid, hs_mid, (ffn_w_in, i), ffn_w_out.shape[1], (ffn_w_out, i))
        (xp, xs), (hp_in, hs_in) = proj_out(act_p, act_s, f_out, xp, xs, nw[3], nw_next, not last)

    stack = lambda items: [jnp.stack(a) for a in zip(*items)]
    p_a, s_a = stack(pa), stack(sa)
    p_c, s_c = stack(pc), stack(sc)
    return (xp.reshape(bp, T, D), xs.reshape(bs, 1, D), *p_a, *s_a, jnp.stack(pb), jnp.stack(sb), *p_c, *s_c)
```

```python
import functools
import math

import jax
import jax.numpy as jnp
from jax import lax
from jax.experimental import pallas as pl
from jax.experimental.pallas import tpu as pltpu

F32 = jnp.float32
BF16 = jnp.bfloat16

EPS = 1e-6
NEG = -1e30
BIG = 1e30

LANES = 128
SUBLANES = 8
VMEM_LIMIT = 56 * 1024 * 1024

MLSTM_HEADS = 8
MLSTM_CHUNK = 256
FORGET_BIAS_COL = MLSTM_HEADS
CONV_W = 3
NSA_HD = 128
NSA_KV = 4
NSA_HPG = 4
CMP_BLOCK = 32
CMP_STRIDE = 16
SEL_BLOCK = 64
N_SELECT = 16
WINDOW = 512
ATT_TILE = 256
PAGES_PER_STEP = 16
OUT_SUB_ROWS = 256


def _cparams(*sem):
    return pltpu.CompilerParams(dimension_semantics=sem, vmem_limit_bytes=VMEM_LIMIT)


def _dot(a, b):
    return jnp.dot(a, b, preferred_element_type=F32)


def _dot_nt(a, b):
    return lax.dot_general(a, b, (((1,), (1,)), ((), ())), preferred_element_type=F32)


def _dot_tn(a, b):
    return lax.dot_general(a, b, (((0,), (0,)), ((), ())), preferred_element_type=F32)


def _split_bf16(x):
    hi = x.astype(BF16)
    lo = (x - hi.astype(F32)).astype(BF16)
    return hi, lo


def _iota(shape, dim):
    return lax.broadcasted_iota(jnp.int32, shape, dim)


def _sigmoid(x):
    return 1.0 / (1.0 + jnp.exp(-x))


def _rms(x):
    return x * lax.rsqrt(jnp.mean(x * x, axis=-1, keepdims=True) + EPS)


def _row_tile(m, want):
    t = min(m, want)
    assert m % t == 0, (m, t)
    return t


def _norm_kernel(x_ref, w_ref, o_ref):
    o_ref[...] = (_rms(x_ref[...]) * w_ref[...]).astype(BF16)


def norm_cast(x, w):
    m, d = x.shape
    tm = _row_tile(m, 512)
    return pl.pallas_call(
        _norm_kernel,
        grid=(m // tm,),
        in_specs=[pl.BlockSpec((tm, d), lambda i: (i, 0)), pl.BlockSpec((1, d), lambda i: (0, 0))],
        out_specs=pl.BlockSpec((tm, d), lambda i: (i, 0)),
        out_shape=jax.ShapeDtypeStruct((m, d), BF16),
        compiler_params=_cparams("parallel"),
    )(x, w.reshape(1, d))


def _proj_kernel(x_ref, xs_ref, *refs, n_groups, n_out, epilogue, transposed, with_cast):
    w_refs = refs[:n_groups]
    refs = refs[n_groups:]
    if with_cast:
        cast_in_ref, refs = refs[0], refs[1:]
        refs[2 * n_out][...] = cast_in_ref[...].astype(BF16)
    out_refs = refs[:n_out]
    sample_out_refs = refs[n_out:2 * n_out]
    wb_ref = refs[-1]
    mm = _dot_nt if transposed else _dot

    @pl.when(pl.program_id(1) == 0)
    def _():
        for g in range(n_groups):
            wb_ref[g] = w_refs[g][...].astype(BF16)
        xs = xs_ref[...]
        epilogue([mm(xs, wb_ref[g]) for g in range(n_groups)], sample_out_refs)

    x = x_ref[...]
    epilogue([mm(x, wb_ref[g]) for g in range(n_groups)], out_refs)


def _proj(x, xs, w, *, col0, tn, n_tiles, group_stride, n_groups, epilogue, outs, tm_want=1024, transposed=False,
          cast=None, sample_f32=False):
    w, layer = w
    m, k = x.shape
    s = xs.shape[0]
    tm = _row_tile(m, tm_want)
    assert col0 % tn == 0
    b0 = col0 // tn
    if transposed:
        w_block = (None, tn, k)
        w_map = lambda j, i, off: (layer, off + j, 0)
    else:
        w_block = (None, k, tn)
        w_map = lambda j, i, off: (layer, 0, off + j)
    w_specs = [pl.BlockSpec(w_block, functools.partial(w_map, off=b0 + g * group_stride)) for g in range(n_groups)]
    n_out = len(outs)

    def out_for(rows, row_tile, row_map, n, dt, by_rows):
        if not by_rows:
            return jax.ShapeDtypeStruct((rows, n), dt), pl.BlockSpec((row_tile, tn), row_map)
        assert n_tiles == 1 and n == tn
        per = n // LANES
        return (jax.ShapeDtypeStruct((rows * per, LANES), dt),
                pl.BlockSpec((row_tile * per, LANES), lambda j, i: (row_map(j, i)[0], 0)))

    prompt_outs = [out_for(m, tm, lambda j, i: (i, j), *o) for o in outs]
    sample_outs = [out_for(s, s, lambda j, i: (0, j), n, F32 if sample_f32 else dt, by_rows)
                   for n, dt, by_rows in outs]
    all_outs = prompt_outs + sample_outs
    operands = [x, xs] + [w] * n_groups
    in_specs = [pl.BlockSpec((tm, k), lambda j, i: (i, 0)), pl.BlockSpec((s, k), lambda j, i: (0, 0))] + w_specs
    n_i = m // tm
    if cast is not None:
        cw, c_layer = cast
        _, c_rows, c_cols = cw.shape
        bf16_rows = 2 * SUBLANES
        blk = next(r for r in range(bf16_rows, c_rows + 1, bf16_rows)
                   if c_rows % r == 0 and c_rows // r <= n_tiles * n_i)
        last = c_rows // blk - 1
        step = lambda j, i: jnp.minimum(j * n_i + i, last)
        in_specs.append(pl.BlockSpec((None, blk, c_cols), lambda j, i: (c_layer, step(j, i), 0)))
        operands.append(cw)
        all_outs.append((jax.ShapeDtypeStruct((c_rows, c_cols), BF16),
                         pl.BlockSpec((blk, c_cols), lambda j, i: (step(j, i), 0))))
    res = pl.pallas_call(
        functools.partial(_proj_kernel, n_groups=n_groups, n_out=n_out, epilogue=epilogue, transposed=transposed,
                          with_cast=cast is not None),
        grid=(n_tiles, n_i),
        in_specs=in_specs,
        out_specs=[spec for _, spec in all_outs],
        out_shape=[shape for shape, _ in all_outs],
        scratch_shapes=[pltpu.VMEM((n_groups,) + w_block[1:], BF16)],
        compiler_params=_cparams("arbitrary", "arbitrary"),
    )(*operands)
    return res[:n_out], res[n_out:2 * n_out], (res[2 * n_out] if cast is not None else None)


def _ep_plain(accs, outs):
    for acc, o in zip(accs, outs):
        o[...] = acc.astype(o.dtype)


def proj_plain(x, xs, w, col0, n_cols, tn, transposed=False, cast=None, prompt_dtype=F32):
    p, s, c = _proj(x, xs, w, col0=col0, tn=tn, n_tiles=n_cols // tn, group_stride=0, n_groups=1,
                    epilogue=_ep_plain, outs=[(n_cols, prompt_dtype, False)], transposed=transposed, cast=cast,
                    sample_f32=True)
    return (p[0], s[0]) if cast is None else (p[0], s[0], c)


def _ep_both_layouts(accs, outs):
    n = len(accs)
    for acc, o_tok, o_rows in zip(accs, outs[:n], outs[n:]):
        o_tok[...] = acc
        per = acc.shape[1] // LANES
        rows = acc.shape[0]
        for c in range(per):
            o_rows[pl.ds(c, rows, stride=per), :] = acc[:, c * LANES:(c + 1) * LANES]


def proj_groups(x, xs, w, col0, n_groups, group_cols, transposed=False):
    outs = [(group_cols, F32, False)] * n_groups + [(group_cols, F32, True)] * n_groups
    p, s, _ = _proj(x, xs, w, col0=col0, tn=group_cols, n_tiles=1, group_stride=1, n_groups=n_groups,
                    epilogue=_ep_both_layouts, outs=outs, tm_want=512, transposed=transposed)
    return (p[:n_groups], p[n_groups:]), (s[:n_groups], s[n_groups:])


def _ep_swiglu(accs, outs):
    g, u = accs
    outs[0][...] = (g * _sigmoid(g) * u).astype(BF16)


def proj_swiglu(x, xs, w, ff, cast):
    tn = 512
    p, s, c = _proj(x, xs, w, col0=0, tn=tn, n_tiles=ff // tn, group_stride=ff // tn, n_groups=2,
                    epilogue=_ep_swiglu, outs=[(ff, BF16, False)], cast=cast)
    return p[0], s[0], c


def _ep_conv_in(accs, outs):
    bg, cg, u = accs
    outs[0][...] = bg
    outs[1][...] = cg * u


def proj_conv_in(x, xs, w, d, cast):
    tn = 512
    return _proj(x, xs, w, col0=0, tn=tn, n_tiles=d // tn, group_stride=d // tn, n_groups=3,
                 epilogue=_ep_conv_in, outs=[(d, F32, False), (d, F32, False)], cast=cast)


def _conv_gated(bg_ref, z_ref, zp_ref, cw_ref, T):
    tm = z_ref.shape[0]
    z = z_ref[...]
    zc = jnp.concatenate([zp_ref[...], z], axis=0)
    tpos = (pl.program_id(0) * tm + _iota((tm, 1), 0)) % T
    y = cw_ref[CONV_W - 1:CONV_W, :] * z
    for back in range(1, CONV_W):
        zb = zc[SUBLANES - back:SUBLANES - back + tm, :]
        y = y + cw_ref[CONV_W - 1 - back:CONV_W - back, :] * jnp.where(tpos >= back, zb, 0.0)
    return (bg_ref[...] * y).astype(BF16)


def _out_kernel(*refs, emit_next, conv_T):
    if conv_T:
        h = _conv_gated(*refs[:4], conv_T)
        rows_of = lambda rows: h[rows, :]
        refs = refs[4:]
    else:
        h_ref = refs[0]
        rows_of = lambda rows: h_ref[rows, :]
        refs = refs[1:]
    hs_ref, w_ref, r_ref, rs_ref, nwa_ref, nwb_ref, *out_refs = refs

    def finalize(y, r, x_ref, xn_ref):
        x_new = r + _rms(y) * nwa_ref[...]
        x_ref[...] = x_new
        if emit_next:
            xn_ref[...] = (_rms(x_new) * nwb_ref[...]).astype(BF16)

    if emit_next:
        x_ref, xn_ref, xs_ref, xsn_ref = out_refs
    else:
        (x_ref, xs_ref), xn_ref, xsn_ref = out_refs, None, None

    @pl.when(pl.program_id(0) == 0)
    def _():
        finalize(_dot(hs_ref[...], w_ref[...]), rs_ref[...], xs_ref, xsn_ref)

    tm = r_ref.shape[0]
    sub = math.gcd(tm, OUT_SUB_ROWS)
    for r0 in range(0, tm, sub):
        rows = slice(r0, r0 + sub)
        finalize(_dot(rows_of(rows), w_ref[...]), r_ref[rows, :], x_ref.at[rows, :],
                 xn_ref.at[rows, :] if emit_next else None)


def proj_out(h, hs, w, resid, resid_s, nw_post, nw_next, emit_next=True, conv=None):
    m, d = resid.shape
    k = w.shape[0]
    s = hs.shape[0]
    tm = _row_tile(m, 512 if k <= 2048 else 256)
    row = lambda i: (i, 0)
    fixed = lambda i: (0, 0)
    if conv is None:
        h_ops, h_specs, conv_T = [h], [pl.BlockSpec((tm, k), row)], 0
    else:
        bg, z, conv_w, conv_T = conv
        assert conv_T % tm == 0
        per = tm // SUBLANES
        h_ops = [bg, z, z, conv_w]
        h_specs = [pl.BlockSpec((tm, k), row), pl.BlockSpec((tm, k), row),
                   pl.BlockSpec((SUBLANES, k), lambda i: (jnp.maximum(i * per - 1, 0), 0)),
                   pl.BlockSpec((CONV_W, k), fixed)]
    f32_out = [jax.ShapeDtypeStruct((m, d), F32), jax.ShapeDtypeStruct((s, d), F32)]
    bf16_out = [jax.ShapeDtypeStruct((m, d), BF16), jax.ShapeDtypeStruct((s, d), BF16)]
    specs = [pl.BlockSpec((tm, d), row), pl.BlockSpec((s, d), fixed)]
    if emit_next:
        out_shape = [f32_out[0], bf16_out[0], f32_out[1], bf16_out[1]]
        out_specs = [specs[0], specs[0], specs[1], specs[1]]
    else:
        out_shape, out_specs = f32_out, specs
    res = pl.pallas_call(
        functools.partial(_out_kernel, emit_next=emit_next, conv_T=conv_T),
        grid=(m // tm,),
        in_specs=h_specs + [pl.BlockSpec((s, k), fixed),
                            pl.BlockSpec((k, d), fixed, pipeline_mode=pl.Buffered(1)),
                            pl.BlockSpec((tm, d), row),
                            pl.BlockSpec((s, d), fixed),
                            pl.BlockSpec((1, d), fixed),
                            pl.BlockSpec((1, d), fixed)],
        out_specs=out_specs,
        out_shape=out_shape,
        compiler_params=_cparams("arbitrary"),
    )(*h_ops, hs, w, resid, resid_s, nw_post.reshape(1, d), nw_next.reshape(1, d))
    if emit_next:
        return (res[0], res[2]), (res[1], res[3])
    return (res[0], res[1]), (None, None)


def _log_sigmoid(x):
    return jnp.minimum(x, 0.0) - jnp.log1p(jnp.exp(-jnp.abs(x)))


def _mlstm_kernel(q_ref, k_ref, v_ref, o_ref, g_ref, bg_ref, hn_ref, hg_ref, c_ref, n_ref, m_ref, cn_ref,
                  *, L, H, DK, DV):
    chunk = pl.program_id(1)

    @pl.when(chunk == 0)
    def _():
        cn_ref[...] = jnp.zeros_like(cn_ref)
        m_ref[...] = jnp.zeros_like(m_ref)

    gpre = g_ref[...] + bg_ref[...]
    lf = _log_sigmoid(gpre)
    rows = _iota((L, L), 0)
    cols = _iota((L, L), 1)
    causal = rows >= cols
    tril = jnp.where(causal, 1.0, 0.0).astype(BF16)
    lf_hi, lf_lo = _split_bf16(lf)
    bcum = _dot(tril, lf_hi) + _dot(tril, lf_lo)
    g_t = gpre.T
    b_t = bcum.T
    for h in range(H):
        f = FORGET_BIAS_COL + h
        a_col = bcum[:, f:f + 1]
        i_col = gpre[:, h:h + 1]
        b_row = b_t[f:f + 1, :]
        i_row = g_t[h:h + 1, :]
        m_prev = m_ref[0, h:h + 1, 0:1]
        log_d = jnp.where(causal, a_col - b_row + i_row, NEG)
        log_inter = a_col + m_prev
        m_t = jnp.maximum(log_inter, jnp.max(log_d, axis=-1, keepdims=True))
        d = jnp.exp(log_d - m_t)
        inter = jnp.exp(log_inter - m_t)
        kh = k_ref[:, h * DK:(h + 1) * DK].astype(F32) * (DK ** -0.5)
        v1 = jnp.concatenate([v_ref[:, h * DV:(h + 1) * DV], jnp.ones((L, LANES), BF16)], axis=1)
        qb = q_ref[:, h * DK:(h + 1) * DK]
        s = _dot_nt(qb, kh.astype(BF16)) * d
        cn_old = cn_ref[h]
        both = _dot(s.astype(BF16), v1) + inter * _dot(qb, cn_old.astype(BF16))
        num, den = both[:, :DV], both[:, DV:]
        scale = 1.0 / jnp.maximum(jnp.abs(den), jnp.exp(-m_t))
        hh = _rms(num * jnp.concatenate([scale] * (DV // LANES), axis=1))
        gate = _sigmoid(o_ref[:, h * DV:(h + 1) * DV])
        hg_ref[:, h * DV:(h + 1) * DV] = (hh * hn_ref[:, h * DV:(h + 1) * DV] * gate).astype(BF16)
        m_new = m_t[L - 1:L, :]
        w_col = jnp.exp(a_col[L - 1:L, :] - a_col + i_col - m_new)
        decay = jnp.exp(log_inter[L - 1:L, :] - m_new)
        cn_new = decay * cn_old + _dot_tn((kh * w_col).astype(BF16), v1)
        cn_ref[h] = cn_new
        m_ref[0, h:h + 1, :] = jnp.broadcast_to(m_new, (1, LANES))

        @pl.when(chunk == pl.num_programs(1) - 1)
        def _():
            c_ref[0, h] = cn_new[:, :DV]
            n_ref[0, h:h + 1, :] = cn_new[:, DV:].T[0:1, :]


def mlstm_prompt(qkv, o, gates, b_gates, head_norm, bsz, T):
    H = MLSTM_HEADS
    m, hdv = o.shape
    hdk = hdv // 2
    DK, DV = hdk // H, hdv // H
    L = math.gcd(T, MLSTM_CHUNK)
    nc = T // L
    row = lambda b, c: (b * nc + c, 0)
    state = lambda b, c: (b, 0, 0)
    return pl.pallas_call(
        functools.partial(_mlstm_kernel, L=L, H=H, DK=DK, DV=DV),
        grid=(bsz, nc),
        in_specs=[pl.BlockSpec((L, hdk), row),
                  pl.BlockSpec((L, hdk), lambda b, c: (b * nc + c, 1)),
                  pl.BlockSpec((L, hdv), lambda b, c: (b * nc + c, 1)),
                  pl.BlockSpec((L, hdv), row),
                  pl.BlockSpec((L, LANES), row),
                  pl.BlockSpec((1, LANES), lambda b, c: (0, 0)),
                  pl.BlockSpec((1, hdv), lambda b, c: (0, 0))],
        out_specs=[pl.BlockSpec((L, hdv), row),
                   pl.BlockSpec((1, H, DK, DV), lambda b, c: (b, 0, 0, 0)),
                   pl.BlockSpec((1, H, DK), state),
                   pl.BlockSpec((1, H, LANES), state)],
        out_shape=[jax.ShapeDtypeStruct((m, hdv), BF16),
                   jax.ShapeDtypeStruct((bsz, H, DK, DV), F32),
                   jax.ShapeDtypeStruct((bsz, H, DK), F32),
                   jax.ShapeDtypeStruct((bsz, H, LANES), F32)],
        scratch_shapes=[pltpu.VMEM((H, DK, DV + LANES), F32)],
        compiler_params=_cparams("arbitrary", "arbitrary"),
    )(qkv, qkv, qkv, o, gates, _pad_lanes(b_gates.reshape(1, -1)), head_norm.reshape(1, hdv))


def _pad_lanes(x, width=LANES):
    return jnp.pad(x, ((0, 0), (0, width - x.shape[-1])))


def _to_col(row, n):
    eye = _iota((n, n), 0) == _iota((n, n), 1)
    return jnp.sum(jnp.where(eye, jnp.broadcast_to(row, (n, n)), 0.0), axis=-1, keepdims=True)


def _mlstm_step_kernel(x_ref, g_ref, bg_ref, hn_ref, c0_ref, n0_ref, m0_ref, hg_ref, c_ref, n_ref, m_ref,
                       *, H, DK, DV):
    x = x_ref[0]
    gpre = g_ref[0] + bg_ref[...]
    lf = _log_sigmoid(gpre)
    hdk = H * DK
    hdv = H * DV
    for h in range(H):
        f = FORGET_BIAS_COL + h
        ig = gpre[:, h:h + 1]
        m_prev = m0_ref[0, h:h + 1, :]
        log_inter = lf[:, f:f + 1] + m_prev
        m_t = jnp.maximum(log_inter, ig)
        d = jnp.exp(ig - m_t)
        inter = jnp.exp(log_inter - m_t)
        q = x[:, h * DK:(h + 1) * DK]
        k = x[:, hdk + h * DK:hdk + (h + 1) * DK] * (DK ** -0.5)
        v = x[:, 2 * hdk + h * DV:2 * hdk + (h + 1) * DV]
        og = x[:, 2 * hdk + hdv + h * DV:2 * hdk + hdv + (h + 1) * DV]
        c_old = c0_ref[0, h]
        n_old = n0_ref[0, h:h + 1, :]
        s = jnp.sum(q * k, axis=-1, keepdims=True) * d
        q_col = _to_col(q, DK)
        k_col = _to_col(k, DK)
        num = s * v + inter * jnp.sum(q_col * c_old, axis=0, keepdims=True)
        den = s + inter * jnp.sum(q * n_old, axis=-1, keepdims=True)
        hh = _rms(num / jnp.maximum(jnp.abs(den), jnp.exp(-m_t)))
        hg_ref[0, :, h * DV:(h + 1) * DV] = (hh * hn_ref[:, h * DV:(h + 1) * DV] * _sigmoid(og)).astype(BF16)
        c_ref[0, h] = inter * c_old + d * (k_col * v)
        n_ref[0, h:h + 1, :] = inter * n_old + d * k
        m_ref[0, h:h + 1, :] = m_t


def mlstm_step(qkvo, gates, b_gates, head_norm, c0, n0, m0):
    bsz, H, DK, DV = c0.shape
    wid = qkvo.shape[1]
    hdv = H * DV
    one = lambda b: (b, 0, 0)
    hg, c1, n1, m1 = pl.pallas_call(
        functools.partial(_mlstm_step_kernel, H=H, DK=DK, DV=DV),
        grid=(bsz,),
        in_specs=[pl.BlockSpec((1, 1, wid), one),
                  pl.BlockSpec((1, 1, LANES), one),
                  pl.BlockSpec((1, LANES), lambda b: (0, 0)),
                  pl.BlockSpec((1, hdv), lambda b: (0, 0)),
                  pl.BlockSpec((1, H, DK, DV), lambda b: (b, 0, 0, 0)),
                  pl.BlockSpec((1, H, DK), one),
                  pl.BlockSpec((1, H, 1), one)],
        out_specs=[pl.BlockSpec((1, 1, hdv), one),
                   pl.BlockSpec((1, H, DK, DV), lambda b: (b, 0, 0, 0)),
                   pl.BlockSpec((1, H, DK), one),
                   pl.BlockSpec((1, H, 1), one)],
        out_shape=[jax.ShapeDtypeStruct((bsz, 1, hdv), BF16),
                   jax.ShapeDtypeStruct((bsz, H, DK, DV), F32),
                   jax.ShapeDtypeStruct((bsz, H, DK), F32),
                   jax.ShapeDtypeStruct((bsz, H, 1), F32)],
        compiler_params=_cparams("parallel"),
    )(qkvo.reshape(bsz, 1, wid), gates.reshape(bsz, 1, LANES), _pad_lanes(b_gates.reshape(1, -1)),
      head_norm.reshape(1, hdv), c0, n0, m0.reshape(bsz, H, 1))
    return hg.reshape(bsz, hdv), c1, n1, m1.reshape(bsz, H)


def _conv_step_kernel(bg_ref, z_ref, buf_ref, cw_ref, o_ref):
    y = cw_ref[CONV_W - 1:CONV_W, :] * z_ref[...]
    for j in range(CONV_W - 1):
        y = y + cw_ref[j:j + 1, :] * buf_ref[j]
    o_ref[...] = bg_ref[...] * y


def conv_gate_step(bg, z, buf, conv_w):
    bsz, d = z.shape
    full = lambda: (0, 0)
    return pl.pallas_call(
        _conv_step_kernel,
        in_specs=[pl.BlockSpec((bsz, d), full), pl.BlockSpec((bsz, d), full),
                  pl.BlockSpec((CONV_W - 1, bsz, d), lambda: (0, 0, 0)), pl.BlockSpec((CONV_W, d), full)],
        out_specs=pl.BlockSpec((bsz, d), full),
        out_shape=jax.ShapeDtypeStruct((bsz, d), F32),
    )(bg, z, buf, conv_w)


def _gelu(x):
    return 0.5 * x * (1.0 + jnp.tanh(math.sqrt(2.0 / math.pi) * (x + 0.044715 * x * x * x)))


def _cmp_finish(a, posw, b1, w2, b2):
    hid = a.shape[1] // 2
    a1 = a[:, hid:]
    a1_next = jnp.concatenate([a1[1:], a1[:1]], axis=0)
    pre = a[:, :hid] + a1_next + b1 + posw[0:1, :hid] + posw[1:2, hid:]
    return _dot(_gelu(pre).astype(BF16), w2) + b2


def _cmp_prompt_kernel(k_ref, v_ref, w1_ref, pos_ref, b1_ref, w2_ref, b2_ref, ck_ref, cv_ref, *, nch):
    for idx, (x_ref, o_ref) in enumerate(((k_ref, ck_ref), (v_ref, cv_ref))):
        w1 = w1_ref[idx]
        lhs = jnp.concatenate([x_ref[pl.ds(p, nch, stride=CMP_STRIDE), :] for p in range(CMP_STRIDE)], axis=1)
        a = _dot(lhs.astype(BF16), w1)
        posw = _dot(pos_ref[idx], w1)
        o_ref[0, 0] = _cmp_finish(a, posw, b1_ref[idx], w2_ref[idx], b2_ref[idx])


def _cmp_weights(cmp_pos, cmp_w1, cmp_b1, cmp_w2, cmp_b2):
    two, ratio, stride, hd, hid = cmp_w1.shape
    w1 = cmp_w1.transpose(0, 2, 3, 1, 4).reshape(two, stride * hd, ratio * hid).astype(BF16)
    pos = cmp_pos.reshape(two, ratio, stride * hd)
    pos = jnp.pad(pos, ((0, 0), (0, SUBLANES - ratio), (0, 0))).astype(BF16)
    return w1, pos, cmp_b1.reshape(two, 1, hid), cmp_w2.astype(BF16), cmp_b2.reshape(two, 1, hd)


def cmp_prompt(kc, vc, bsz, T, cw):
    w1, pos, b1, w2, b2 = cw
    nch = T // CMP_STRIDE
    G = NSA_KV
    out = jax.ShapeDtypeStruct((bsz, G, nch, NSA_HD), F32)
    ospec = pl.BlockSpec((1, 1, nch, NSA_HD), lambda b, g: (b, g, 0, 0))
    whole = lambda a: pl.BlockSpec(a.shape, lambda b, g: (0,) * a.ndim)
    seq = pl.BlockSpec((T, NSA_HD), lambda b, g: (b, g))
    return pl.pallas_call(
        functools.partial(_cmp_prompt_kernel, nch=nch),
        grid=(bsz, G),
        in_specs=[seq, seq, whole(w1), whole(pos), whole(b1), whole(w2), whole(b2)],
        out_specs=[ospec, ospec],
        out_shape=[out, out],
        compiler_params=_cparams("parallel", "parallel"),
    )(kc, vc, w1, pos, b1, w2, b2)


def _overlap_t(nbs_pad, nbc_pad):
    j = _iota((nbs_pad, nbc_pad), 0)
    n = _iota((nbs_pad, nbc_pad), 1)
    lo = jnp.maximum(n * CMP_STRIDE, j * SEL_BLOCK)
    hi = jnp.minimum(n * CMP_STRIDE + CMP_BLOCK, j * SEL_BLOCK + SEL_BLOCK)
    return (jnp.maximum(hi - lo, 0) // CMP_STRIDE).astype(F32)


def _rank_select(score, n_sel, axis):
    n = score.shape[axis]
    idx = _iota(score.shape, axis)
    cnt = jnp.zeros(score.shape, F32)
    for jp in range(n):
        other = lax.slice_in_dim(score, jp, jp + 1, axis=axis)
        tie = jnp.where(idx > jp, 1.0, 0.0)
        cnt = cnt + jnp.where(other > score, 1.0, jnp.where(other == score, tie, 0.0))
    return jnp.where(cnt < n_sel, 1.0, 0.0)


def _attn_first(state, s, vt_b):
    m_ref, l_ref, acc_ref = state
    m = jnp.max(s, axis=0, keepdims=True)
    p = jnp.exp2(s - m)
    m_ref[...] = m
    l_ref[...] = jnp.sum(p, axis=0, keepdims=True)
    acc_ref[...] = _dot(vt_b, p.astype(BF16))


def _attn_more(state, s, vt_b):
    m_ref, l_ref, acc_ref = state
    m_old = m_ref[...]
    m_new = jnp.maximum(m_old, jnp.max(s, axis=0, keepdims=True))
    alpha = jnp.exp2(m_old - m_new)
    p = jnp.exp2(s - m_new)
    m_ref[...] = m_new
    l_ref[...] = alpha * l_ref[...] + jnp.sum(p, axis=0, keepdims=True)
    acc_ref[...] = alpha * acc_ref[...] + _dot(vt_b, p.astype(BF16))


def _nsa_attn_kernel(q_ref, ck_ref, cv_ref, ks_ref, vs_ref, kw_ref, vw_ref, g_ref, bg_ref, o_ref,
                     ksb_ref, vst_ref, kwb_ref, vwt_ref, ms_ref, ls_ref, as_ref, mw_ref, lw_ref, aw_ref,
                     *, tq, hpg, nbc, nbs, nbs_pad):
    qi = pl.program_id(2)
    t0 = qi * tq
    cols = hpg * tq
    tk = tq
    n_kt = ks_ref.shape[0] // tk
    nbc_pad = ck_ref.shape[2]

    @pl.when(qi == 0)
    def _():
        ksb_ref[...] = ks_ref[...].astype(BF16)
        kwb_ref[...] = kw_ref[...].astype(BF16)
        for kt in range(n_kt):
            vst_ref[kt] = vs_ref[kt * tk:(kt + 1) * tk, :].T.astype(BF16)
            vwt_ref[kt] = vw_ref[kt * tk:(kt + 1) * tk, :].T.astype(BF16)

    qb = (jnp.concatenate([q_ref[:, h * NSA_HD:(h + 1) * NSA_HD] for h in range(hpg)], axis=0)
          * (NSA_HD ** -0.5 * math.log2(math.e))).astype(BF16)

    n = _iota((nbc_pad, cols), 0)
    t_col = t0 + jnp.concatenate([_iota((nbc_pad, tq), 1)] * hpg, axis=1)
    cmask = jnp.where(n < nbc, jnp.where(n * CMP_STRIDE + (CMP_BLOCK - 1) <= t_col, 1.0, 0.0), 0.0)
    sc = jnp.where(cmask > 0.5, _dot_nt(ck_ref[0, 0].astype(BF16), qb), NEG)
    pe = jnp.exp2(sc - jnp.max(sc, axis=0, keepdims=True))
    p_cmp = pe / jnp.sum(pe, axis=0, keepdims=True) * cmask
    o_cmp = _dot(cv_ref[0, 0].T.astype(BF16), p_cmp.astype(BF16))

    p_sum = p_cmp[:, 0:tq]
    for h in range(1, hpg):
        p_sum = p_sum + p_cmp[:, h * tq:(h + 1) * tq]
    ov_t = _overlap_t(nbs_pad, nbc_pad).astype(BF16)
    p_hi, p_lo = _split_bf16(p_sum)
    imp_t = _dot(ov_t, p_hi) + _dot(ov_t, p_lo)
    j = _iota((nbs_pad, tq), 0)
    t = t0 + _iota((nbs_pad, tq), 1)
    t_blk = t // SEL_BLOCK
    forced = jnp.where(j == 0, 1.0, jnp.where(j == t_blk, 1.0, jnp.where(j == t_blk - 1, 1.0, 0.0)))
    visible = jnp.where(j < nbs, jnp.where(j * SEL_BLOCK <= t, 1.0, 0.0), 0.0)
    score = jnp.where(visible > 0.5, jnp.where(forced > 0.5, BIG, imp_t), NEG)
    sel_t = _rank_select(score, min(N_SELECT, nbs), axis=0).astype(BF16)

    k_ofs = _iota((tk, tq), 0)
    q_ofs = _iota((tk, tq), 1)
    causal = k_ofs <= q_ofs
    per_head = lambda bias: jnp.concatenate([bias] * hpg, axis=1)

    def scores(kb_ref, kt):
        k_b = kb_ref[pl.ds(pl.multiple_of(kt * tk, tk), tk), :]
        return _dot_nt(k_b, qb)

    def slc_bias(kt):
        blk = (kt * tk + _iota((tk, nbs_pad), 0)) // SEL_BLOCK
        expand = jnp.where(blk == _iota((tk, nbs_pad), 1), 1.0, 0.0).astype(BF16)
        return (_dot(expand, sel_t) - 1.0) * BIG

    slc = (ms_ref, ls_ref, as_ref)
    _attn_first(slc, scores(ksb_ref, qi) + per_head(jnp.where(causal, slc_bias(qi), NEG)), vst_ref[qi])

    def slc_body(kt, _):
        _attn_more(slc, scores(ksb_ref, kt) + per_head(slc_bias(kt)), vst_ref[kt])
        return 0

    lax.fori_loop(0, qi, slc_body, 0)

    n_back = WINDOW // tk
    win = (mw_ref, lw_ref, aw_ref)
    _attn_first(win, scores(kwb_ref, qi) + per_head(jnp.where(causal, 0.0, NEG)), vwt_ref[qi])
    for back in range(1, n_back + 1):
        @pl.when(qi >= back)
        def _():
            s = scores(kwb_ref, qi - back)
            if back == n_back:
                s = s + per_head(jnp.where(k_ofs >= q_ofs, 0.0, NEG))
            _attn_more(win, s, vwt_ref[qi - back])

    o_slc = as_ref[...] / ls_ref[...]
    o_win = aw_ref[...] / lw_ref[...]
    gate_t = _sigmoid(g_ref[...] + bg_ref[...]).T
    for h in range(hpg):
        sl = slice(h * tq, (h + 1) * tq)
        o_t = (gate_t[3 * h:3 * h + 1, :] * o_cmp[:, sl] + gate_t[3 * h + 1:3 * h + 2, :] * o_slc[:, sl]
               + gate_t[3 * h + 2:3 * h + 3, :] * o_win[:, sl])
        o_ref[:, h * NSA_HD:(h + 1) * NSA_HD] = o_t.T.astype(BF16)


def nsa_attn_prompt(q, ck, cv, ks, vs, kw, vw, gates, b_gate, bsz, T):
    G, HPG = NSA_KV, NSA_HPG
    tq = math.gcd(T, ATT_TILE)
    assert WINDOW % tq == 0
    nq = T // tq
    nch = T // CMP_STRIDE
    nbc = nch - CMP_BLOCK // CMP_STRIDE + 1
    nbs = -(-T // SEL_BLOCK)
    nbs_pad = -(-nbs // SUBLANES) * SUBLANES
    cols = HPG * tq
    kv_scratch = [pltpu.VMEM((T, NSA_HD), BF16), pltpu.VMEM((nq, NSA_HD, tq), BF16)] * 2
    stats = [pltpu.VMEM((1, cols), F32), pltpu.VMEM((1, cols), F32), pltpu.VMEM((NSA_HD, cols), F32)] * 2
    seq = pl.BlockSpec((T, NSA_HD), lambda b, g, i: (b, g))
    cspec = pl.BlockSpec((1, 1, nch, NSA_HD), lambda b, g, i: (b, g, 0, 0))
    qspec = pl.BlockSpec((tq, HPG * NSA_HD), lambda b, g, i: (b * nq + i, g))
    return pl.pallas_call(
        functools.partial(_nsa_attn_kernel, tq=tq, hpg=HPG, nbc=nbc, nbs=nbs, nbs_pad=nbs_pad),
        grid=(bsz, G, nq),
        in_specs=[qspec, cspec, cspec, seq, seq, seq, seq,
                  pl.BlockSpec((tq, LANES), lambda b, g, i: (b * nq + i, g)),
                  pl.BlockSpec((1, LANES), lambda b, g, i: (0, g))],
        out_specs=qspec,
        out_shape=jax.ShapeDtypeStruct((bsz * T, G * HPG * NSA_HD), BF16),
        scratch_shapes=kv_scratch + stats,
        compiler_params=_cparams("arbitrary", "arbitrary", "arbitrary"),
    )(q, ck, cv, ks, vs, kw, vw, gates, b_gate)


CHUNK_ROWS = CMP_STRIDE * NSA_KV
CHUNK_PITCH = CHUNK_ROWS + SUBLANES


def _dec_cmp_kernel(tbl_ref, *refs, P, nbc, nbs, nbs_pad, p0):
    k_pages, v_pages = refs[:P], refs[P:2 * P]
    w1_ref, q_ref, pos_ref, b1_ref, w2_ref, b2_ref, o_ref, sel_ref, pad_ref, ak_ref, av_ref = refs[2 * P:]
    cpp = LANES // CMP_STRIDE
    step = pl.program_id(1)
    for idx, (pages, a_ref) in enumerate(((k_pages, ak_ref), (v_pages, av_ref))):
        for i, pg in enumerate(pages):
            for c in range(cpp):
                pad_ref[idx * P + i, c * CHUNK_PITCH:c * CHUNK_PITCH + CHUNK_ROWS, :] = (
                    pg[0, c * CHUNK_ROWS:(c + 1) * CHUNK_ROWS, :])
        blocks = []
        for g in range(NSA_KV):
            for i in range(P):
                blocks.append(jnp.concatenate(
                    [pad_ref[idx * P + i, pl.ds(p * NSA_KV + g, cpp, stride=CHUNK_PITCH), :]
                     for p in range(CMP_STRIDE)], axis=1))
        a = _dot(jnp.concatenate(blocks, axis=0).astype(BF16), w1_ref[idx])
        per_g = P * cpp
        for g in range(NSA_KV):
            a_ref[g, pl.ds(pl.multiple_of(step * per_g, per_g), per_g), :] = a[g * per_g:(g + 1) * per_g]

    @pl.when(step == pl.num_programs(1) - 1)
    def _():
        _dec_cmp_finish(ak_ref, av_ref, q_ref, pos_ref, w1_ref, b1_ref, w2_ref, b2_ref, o_ref, sel_ref,
                        nbc=nbc, nbs=nbs, nbs_pad=nbs_pad, p0=p0)


def _dec_cmp_finish(ak_ref, av_ref, q_ref, pos_ref, w1_ref, b1_ref, w2_ref, b2_ref, o_ref, sel_ref,
                    *, nbc, nbs, nbs_pad, p0):
    nch = ak_ref.shape[1]
    for g in range(NSA_KV):
        cks = []
        for idx, a_ref in enumerate((ak_ref, av_ref)):
            posw = _dot(pos_ref[idx], w1_ref[idx])
            cks.append(_cmp_finish(a_ref[g], posw, b1_ref[idx], w2_ref[idx], b2_ref[idx]))
        ck, cv = cks
        qb = (q_ref[0, g] * (NSA_HD ** -0.5)).astype(BF16)
        n = _iota((SUBLANES, nch), 1)
        cmask = jnp.where(n < nbc, jnp.where(n * CMP_STRIDE + (CMP_BLOCK - 1) <= p0, 1.0, 0.0), 0.0)
        sc = jnp.where(cmask > 0.5, _dot_nt(qb, ck.astype(BF16)), NEG)
        pe = jnp.exp(sc - jnp.max(sc, axis=-1, keepdims=True))
        p_cmp = pe / jnp.sum(pe, axis=-1, keepdims=True) * cmask
        o_ref[0, g] = _dot(p_cmp.astype(BF16), cv.astype(BF16))

        head = jnp.where(_iota((SUBLANES, nch), 0) < NSA_HPG, p_cmp, 0.0)
        p_sum = jnp.broadcast_to(jnp.sum(head, axis=0, keepdims=True), (SUBLANES, nch))
        ov_t = _overlap_t(nbs_pad, nch).astype(BF16)
        p_hi, p_lo = _split_bf16(p_sum)
        imp = _dot_nt(p_hi, ov_t) + _dot_nt(p_lo, ov_t)
        j = _iota((SUBLANES, nbs_pad), 1)
        t_blk = p0 // SEL_BLOCK
        forced = jnp.where(j == 0, 1.0, jnp.where(j == t_blk, 1.0, jnp.where(j == t_blk - 1, 1.0, 0.0)))
        visible = jnp.where(j < nbs, jnp.where(j * SEL_BLOCK <= p0, 1.0, 0.0), 0.0)
        score = jnp.where(visible > 0.5, jnp.where(forced > 0.5, BIG, imp), NEG)
        s_row = jnp.broadcast_to(score[0:1], (nbs_pad, nbs_pad))
        s_col = jnp.concatenate([jnp.broadcast_to(score[0:1], (LANES, nbs_pad)).T] * (nbs_pad // LANES), axis=1)
        jr = _iota((nbs_pad, nbs_pad), 0)
        jc = _iota((nbs_pad, nbs_pad), 1)
        tie = jnp.where(jc < jr, 1.0, 0.0)
        beats = jnp.where(s_row > s_col, 1.0, jnp.where(s_row == s_col, tie, 0.0))
        rank = jnp.sum(beats, axis=-1, keepdims=True)
        slot = _iota((nbs_pad, LANES), 1).astype(F32)
        blk = _iota((nbs_pad, LANES), 0).astype(F32)
        picked = jnp.sum(jnp.where(rank == slot, blk, 0.0), axis=0, keepdims=True)
        sel_ref[0, g:g + 1, :] = picked.astype(jnp.int32)


def dec_cmp(pool_k, pool_v, table, q, cw, p0):
    w1, pos, b1, w2, b2 = cw
    n_pool, page, G, hd = pool_k.shape
    assert page == LANES and G == NSA_KV and hd == NSA_HD
    bsz, n_pages = table.shape
    assert p0 == n_pages * page
    P = math.gcd(n_pages, PAGES_PER_STEP)
    cpp = page // CMP_STRIDE
    nch = n_pages * cpp
    hid2 = w1.shape[2]
    nbc = nch - CMP_BLOCK // CMP_STRIDE + 1
    nbs = -(-(p0 + 1) // SEL_BLOCK)
    nbs_pad = -(-nbs // LANES) * LANES
    pk = pool_k.reshape(n_pool, page * G, hd)
    pv = pool_v.reshape(n_pool, page * G, hd)
    pspec = lambda i: pl.BlockSpec((1, page * G, hd),
                                   functools.partial(lambda b, s, tbl, i: (tbl[b * n_pages + s * P + i], 0, 0), i=i))
    whole = lambda a: pl.BlockSpec(a.shape, lambda b, s, tbl: (0,) * a.ndim)
    per_seq = lambda shape: pl.BlockSpec((1,) + shape, lambda b, s, tbl: (b,) + (0,) * len(shape))
    grid_spec = pltpu.PrefetchScalarGridSpec(
        num_scalar_prefetch=1,
        grid=(bsz, n_pages // P),
        in_specs=[pspec(i) for i in range(P)] * 2
        + [whole(w1), per_seq((G, SUBLANES, hd)), whole(pos), whole(b1), whole(w2), whole(b2)],
        out_specs=[per_seq((G, SUBLANES, hd)), per_seq((G, LANES))],
        scratch_shapes=[pltpu.VMEM((2 * P, cpp * CHUNK_PITCH, hd), F32),
                        pltpu.VMEM((G, nch, hid2), F32), pltpu.VMEM((G, nch, hid2), F32)],
    )
    return pl.pallas_call(
        functools.partial(_dec_cmp_kernel, P=P, nbc=nbc, nbs=nbs, nbs_pad=nbs_pad, p0=p0),
        grid_spec=grid_spec,
        out_shape=[jax.ShapeDtypeStruct((bsz, G, SUBLANES, hd), F32),
                   jax.ShapeDtypeStruct((bsz, G, LANES), jnp.int32)],
        compiler_params=_cparams("arbitrary", "arbitrary"),
    )(table.reshape(-1), *([pk] * P), *([pv] * P), w1, q, pos, b1, w2, b2)


def _softmax_with_new_key(qb, k_b, v_b, bias, s_new, v_new):
    s = _dot_nt(qb, k_b)
    if bias is not None:
        s = s + bias
    mx = jnp.maximum(jnp.max(s, axis=-1, keepdims=True), s_new)
    p = jnp.exp(s - mx)
    p_new = jnp.exp(s_new - mx)
    return (_dot(p.astype(BF16), v_b) + p_new * v_new) / (jnp.sum(p, axis=-1, keepdims=True) + p_new)


def _dec_attn_kernel(sel_ref, tbl_ref, *refs, n_cache_blocks, n_sel):
    ks_refs, vs_refs = refs[:n_sel], refs[n_sel:2 * n_sel]
    q_ref, new_ref, kw_ref, vw_ref, oc_ref, g_ref, bg_ref, o_ref = refs[2 * n_sel:]
    b, g = pl.program_id(0), pl.program_id(1)
    q = q_ref[0, 0] * (NSA_HD ** -0.5)
    qb = q.astype(BF16)
    new = new_ref[0, 0]

    mine = lambda r, n: r[pl.ds(g, n, stride=NSA_KV), :]
    k_all = jnp.concatenate([mine(r.at[0, 0], SEL_BLOCK) for r in ks_refs], axis=0).astype(BF16)
    v_all = jnp.concatenate([mine(r.at[0, 0], SEL_BLOCK) for r in vs_refs], axis=0).astype(BF16)
    slot = _iota((SUBLANES, n_sel * SEL_BLOCK), 1) // SEL_BLOCK
    bias = jnp.zeros((SUBLANES, n_sel * SEL_BLOCK), F32)
    for s in range(n_sel):
        blk = sel_ref[(b * NSA_KV + g) * n_sel + s]
        bias = jnp.where(slot == s, jnp.where(blk < n_cache_blocks, 0.0, NEG), bias)
    o_slc = _softmax_with_new_key(qb, k_all, v_all, bias, jnp.sum(q * new[0:1], axis=-1, keepdims=True), new[1:2])

    n_win = kw_ref.shape[1] // NSA_KV
    o_win = _softmax_with_new_key(qb, mine(kw_ref.at[0], n_win).astype(BF16), mine(vw_ref.at[0], n_win).astype(BF16),
                                  None,
                                  jnp.sum(q * new[2:3], axis=-1, keepdims=True), new[3:4])

    gate = jnp.broadcast_to(_sigmoid(g_ref[0, 0] + bg_ref[...]), (SUBLANES, LANES))
    head = _iota((SUBLANES, LANES), 0)
    lane = _iota((SUBLANES, LANES), 1)
    o = jnp.zeros((SUBLANES, NSA_HD), F32)
    for br, ob in enumerate((oc_ref[0, 0], o_slc, o_win)):
        o = o + jnp.sum(jnp.where(lane == 3 * head + br, gate, 0.0), axis=-1, keepdims=True) * ob
    o_ref[0, 0] = o


def dec_attn(sel, table, pool_ks, pool_vs, q, new, kw_past, vw_past, o_cmp, gates, b_gate):
    n_pool, page, G, hd = pool_ks.shape
    bsz, n_pages = table.shape
    halves = page // SEL_BLOCK
    wb = kw_past.shape[1]
    assert wb <= WINDOW
    n_sel = sel.shape[-1]
    pk = pool_ks.reshape(n_pool, halves, SEL_BLOCK * G, hd)
    pv = pool_vs.reshape(n_pool, halves, SEL_BLOCK * G, hd)
    n_cache_blocks = n_pages * halves

    def page_map(b, g, sel_ref, tbl_ref, s):
        blk = jnp.minimum(sel_ref[(b * G + g) * n_sel + s], n_cache_blocks - 1)
        return (tbl_ref[b * n_pages + blk // halves], blk % halves, 0, 0)

    bg_map = lambda b, g, sel_ref, tbl_ref: (b, g, 0, 0)
    small = pl.BlockSpec((1, 1, SUBLANES, hd), bg_map)
    wspec = pl.BlockSpec((1, wb * G, hd), lambda b, g, sel_ref, tbl_ref: (b, 0, 0))
    blocks = [pl.BlockSpec((1, 1, SEL_BLOCK * G, hd), functools.partial(page_map, s=s)) for s in range(n_sel)]
    grid_spec = pltpu.PrefetchScalarGridSpec(
        num_scalar_prefetch=2,
        grid=(bsz, G),
        in_specs=blocks + blocks + [small, small, wspec, wspec, small,
                                    pl.BlockSpec((1, 1, 1, LANES), bg_map),
                                    pl.BlockSpec((1, LANES), lambda b, g, sel_ref, tbl_ref: (0, g))],
        out_specs=small,
    )
    return pl.pallas_call(
        functools.partial(_dec_attn_kernel, n_cache_blocks=n_cache_blocks, n_sel=n_sel),
        grid_spec=grid_spec,
        out_shape=jax.ShapeDtypeStruct((bsz, G, SUBLANES, hd), F32),
        compiler_params=_cparams("parallel", "parallel"),
    )(sel.reshape(-1), table.reshape(-1), *([pk] * n_sel), *([pv] * n_sel), q, new,
      kw_past.reshape(bsz, wb * G, hd), vw_past.reshape(bsz, wb * G, hd), o_cmp,
      gates.reshape(bsz, G, 1, LANES), b_gate)


def _gate_weights(wt, layer, row0, n):
    rows = lax.slice(wt, (layer, row0, 0), (layer + 1, row0 + n, wt.shape[2]))
    return jnp.pad(rows, ((0, 0), (0, LANES - n), (0, 0))), 0


def _nsa_gate_weights(wt, layer, b_gate, row0):
    per = 3 * NSA_HPG
    k = wt.shape[2]
    rows = lax.slice(wt, (layer, row0, 0), (layer + 1, row0 + NSA_KV * per, k)).reshape(NSA_KV, per, k)
    wg = jnp.pad(rows, ((0, 0), (0, LANES - per), (0, 0))).reshape(1, NSA_KV * LANES, k)
    bg = jnp.pad(b_gate.reshape(NSA_KV, per), ((0, 0), (0, LANES - per))).reshape(1, NSA_KV * LANES)
    return (wg, 0), bg


def kernel(x_prompt, x_sample, state_a_C, state_a_n, state_a_m, state_b_conv, cache_c_k_cmp, cache_c_v_cmp, cache_c_k_slc, cache_c_v_slc, cache_c_k_win, cache_c_v_win, page_table, norm_w, ffn_w_in, ffn_w_out, a_w_in, a_b_gates, a_head_norm, a_w_out, b_w_in, b_conv_w, b_w_out, c_w_in, c_b_gate, c_cmp_pos, c_cmp_w1, c_cmp_b1, c_cmp_w2, c_cmp_b2, c_w_out):
    bp, T, D = x_prompt.shape
    bs = x_sample.shape[0]
    assert x_sample.shape[1] == 1
    depth = norm_w.shape[0]
    G, hd = NSA_KV, NSA_HD
    past_len = page_table.shape[1] * cache_c_k_cmp.shape[2]
    xp = x_prompt.reshape(bp * T, D)
    xs = x_sample.reshape(bs, D)
    hp_in = norm_cast(xp, norm_w[0, 0])
    hs_in = norm_cast(xs, norm_w[0, 0])
    a_wt = jnp.swapaxes(a_w_in, 1, 2)
    c_wt = jnp.swapaxes(c_w_in, 1, 2)
    w_outs = {0: a_w_out, 1: b_w_out, 2: c_w_out}
    pa, sa, pb, sb, pc, sc = [], [], [], [], [], []
    for i in range(depth):
        kind, j = i % 3, i // 3
        nw = norm_w[i]
        w_out_f32 = (w_outs[kind], j)
        conv = None
        if kind == 0:
            hdv = a_head_norm.shape[1]
            n_main = 3 * hdv
            wg = _gate_weights(a_wt, j, n_main, 2 * MLSTM_HEADS)
            qkv_p, qkv_s, w_out = proj_plain(hp_in, hs_in, (a_wt, j), 0, 2 * hdv, 1024, transposed=True,
                                             cast=w_out_f32, prompt_dtype=BF16)
            o_p, o_s = proj_plain(hp_in, hs_in, (a_wt, j), 2 * hdv, hdv, 1024, transposed=True)
            qkvo_s = jnp.concatenate([qkv_s, o_s], axis=1)
            g_p, g_s = proj_plain(hp_in, hs_in, wg, 0, LANES, LANES, transposed=True)
            hp, c_p, n_p, m_p = mlstm_prompt(qkv_p, o_p, g_p, a_b_gates[j], a_head_norm[j], bp, T)
            hs, c_s, n_s, m_s = mlstm_step(qkvo_s, g_s, a_b_gates[j], a_head_norm[j],
                                           state_a_C[j], state_a_n[j], state_a_m[j])
            pa.append((c_p, n_p, m_p[:, :, 0]))
            sa.append((c_s, n_s, m_s))
        elif kind == 1:
            (bg_p, z_p), (bg_s, z_s), w_out = proj_conv_in(hp_in, hs_in, (b_w_in, j), D, w_out_f32)
            hp, conv = None, (bg_p, z_p, b_conv_w[j], T)
            buf = state_b_conv[j]
            hs = conv_gate_step(bg_s, z_s, buf.transpose(1, 0, 2), b_conv_w[j]).astype(BF16)
            pb.append(z_p.reshape(bp, T, D)[:, T - (CONV_W - 1):])
            sb.append(jnp.concatenate([buf, z_s[:, None]], axis=1)[:, -(CONV_W - 1):])
        else:
            nq_cols = NSA_HPG * G * hd
            nheads = G * NSA_HPG
            wg, bgate = _nsa_gate_weights(c_wt, j, c_b_gate[j], nq_cols + 6 * G * hd)
            cw = _cmp_weights(c_cmp_pos[j], c_cmp_w1[j], c_cmp_b1[j], c_cmp_w2[j], c_cmp_b2[j])
            q_p, q_s, w_out = proj_plain(hp_in, hs_in, (c_wt, j), 0, nq_cols, 1024, transposed=True, cast=w_out_f32)
            kv_p, kv_rows_p, kv_s = [], [], []
            for half in range(2):
                (tok_p, rows_p), (tok_s, _) = proj_groups(hp_in, hs_in, (c_wt, j), nq_cols + half * 3 * G * hd,
                                                         3, G * hd, transposed=True)
                kv_p, kv_rows_p, kv_s = kv_p + list(tok_p), kv_rows_p + list(rows_p), kv_s + list(tok_s)
            gt_p, gt_s = proj_plain(hp_in, hs_in, wg, 0, G * LANES, G * LANES, transposed=True)
            ck, cv = cmp_prompt(kv_p[0], kv_p[1], bp, T, cw)
            hp = nsa_attn_prompt(q_p, ck, cv, kv_p[2], kv_p[3], kv_p[4], kv_p[5], gt_p, bgate, bp, T)
            kv_rows_p = [a.reshape(bp, T, G, hd) for a in kv_rows_p]
            keep = min(WINDOW, T)
            pc.append(tuple(kv_rows_p[:4]) + tuple(a[:, T - keep:] for a in kv_rows_p[4:]))
            q_s = jnp.pad(q_s.reshape(bs, G, NSA_HPG, hd), ((0, 0), (0, 0), (0, SUBLANES - NSA_HPG), (0, 0)))
            kv_s = jnp.stack([a.reshape(bs, G, hd) for a in kv_s], axis=1)
            new = jnp.pad(kv_s[:, 2:6].transpose(0, 2, 1, 3), ((0, 0), (0, 0), (0, SUBLANES - 4), (0, 0)))
            o_cmp, ranked = dec_cmp(cache_c_k_cmp[j], cache_c_v_cmp[j], page_table, q_s, cw, past_len)
            n_sel = min(N_SELECT, -(-(past_len + 1) // SEL_BLOCK))
            o_s = dec_attn(ranked[:, :, :n_sel], page_table, cache_c_k_slc[j], cache_c_v_slc[j], q_s, new,
                           cache_c_k_win[j], cache_c_v_win[j], o_cmp, gt_s, bgate)
            hs = o_s[:, :, :NSA_HPG].reshape(bs, nheads * hd).astype(BF16)
            keep_s = min(WINDOW, cache_c_k_win.shape[2] + 1)
            win = [jnp.concatenate([c[j], kv_s[:, a][:, None]], axis=1)[:, -keep_s:]
                   for c, a in ((cache_c_k_win, 4), (cache_c_v_win, 5))]
            sc.append(tuple(kv_s[:, a][:, None] for a in range(4)) + tuple(win))
        last = i == depth - 1
        nw_next = norm_w[i + 1, 0] if not last else nw[0]
        (xp, xs), (hp_mid, hs_mid) = proj_out(hp, hs, w_out, xp, xs, nw[1], nw[2], conv=conv)
        act_p, act_s, f_out = proj_swiglu(hp_mid, hs_mid, (ffn_w_in, i), ffn_w_out.shape[1], (ffn_w_out, i))
        (xp, xs), (hp_in, hs_in) = proj_out(act_p, act_s, f_out, xp, xs, nw[3], nw_next, not last)

    stack = lambda items: [jnp.stack(a) for a in zip(*items)]
    p_a, s_a = stack(pa), stack(sa)
    p_c, s_c = stack(pc), stack(sc)
    return (xp.reshape(bp, T, D), xs.reshape(bs, 1, D), *p_a, *s_a, jnp.stack(pb), jnp.stack(sb), *p_c, *s_c)
```

```python
import functools
import math

import jax
import jax.numpy as jnp
from jax import lax
from jax.experimental import pallas as pl
from jax.experimental.pallas import tpu as pltpu

F32 = jnp.float32
BF16 = jnp.bfloat16

EPS = 1e-6
NEG = -1e30
BIG = 1e30

LANES = 128
SUBLANES = 8
VMEM_LIMIT = 56 * 1024 * 1024

MLSTM_HEADS = 8
MLSTM_CHUNK = 256
FORGET_BIAS_COL = MLSTM_HEADS
CONV_W = 3
NSA_HD = 128
NSA_KV = 4
NSA_HPG = 4
CMP_BLOCK = 32
CMP_STRIDE = 16
SEL_BLOCK = 64
N_SELECT = 16
WINDOW = 512
ATT_TILE = 256
PAGES_PER_STEP = 16
OUT_SUB_ROWS = 256


def _cparams(*sem):
    return pltpu.CompilerParams(dimension_semantics=sem, vmem_limit_bytes=VMEM_LIMIT)


def _dot(a, b):
    return jnp.dot(a, b, preferred_element_type=F32)


def _dot_nt(a, b):
    return lax.dot_general(a, b, (((1,), (1,)), ((), ())), preferred_element_type=F32)


def _dot_tn(a, b):
    return lax.dot_general(a, b, (((0,), (0,)), ((), ())), preferred_element_type=F32)


def _split_bf16(x):
    hi = x.astype(BF16)
    lo = (x - hi.astype(F32)).astype(BF16)
    return hi, lo


def _iota(shape, dim):
    return lax.broadcasted_iota(jnp.int32, shape, dim)


def _sigmoid(x):
    return 1.0 / (1.0 + jnp.exp(-x))


def _rms(x):
    return x * lax.rsqrt(jnp.mean(x * x, axis=-1, keepdims=True) + EPS)


def _row_tile(m, want):
    t = min(m, want)
    assert m % t == 0, (m, t)
    return t


def _norm_kernel(x_ref, w_ref, o_ref):
    o_ref[...] = (_rms(x_ref[...]) * w_ref[...]).astype(BF16)


def norm_cast(x, w):
    m, d = x.shape
    tm = _row_tile(m, 512)
    return pl.pallas_call(
        _norm_kernel,
        grid=(m // tm,),
        in_specs=[pl.BlockSpec((tm, d), lambda i: (i, 0)), pl.BlockSpec((1, d), lambda i: (0, 0))],
        out_specs=pl.BlockSpec((tm, d), lambda i: (i, 0)),
        out_shape=jax.ShapeDtypeStruct((m, d), BF16),
        compiler_params=_cparams("parallel"),
    )(x, w.reshape(1, d))


def _proj_kernel(x_ref, xs_ref, *refs, n_groups, n_out, epilogue, transposed, with_cast):
    w_refs = refs[:n_groups]
    refs = refs[n_groups:]
    if with_cast:
        cast_in_ref, refs = refs[0], refs[1:]
        refs[2 * n_out][...] = cast_in_ref[...].astype(BF16)
    out_refs = refs[:n_out]
    sample_out_refs = refs[n_out:2 * n_out]
    wb_ref = refs[-1]
    mm = _dot_nt if transposed else _dot

    @pl.when(pl.program_id(1) == 0)
    def _():
        for g in range(n_groups):
            wb_ref[g] = w_refs[g][...].astype(BF16)
        xs = xs_ref[...]
        epilogue([mm(xs, wb_ref[g]) for g in range(n_groups)], sample_out_refs)

    x = x_ref[...]
    epilogue([mm(x, wb_ref[g]) for g in range(n_groups)], out_refs)


def _proj(x, xs, w, *, col0, tn, n_tiles, group_stride, n_groups, epilogue, outs, tm_want=1024, transposed=False,
          cast=None, sample_f32=False):
    w, layer = w
    m, k = x.shape
    s = xs.shape[0]
    tm = _row_tile(m, tm_want)
    assert col0 % tn == 0
    b0 = col0 // tn
    if transposed:
        w_block = (None, tn, k)
        w_map = lambda j, i, off: (layer, off + j, 0)
    else:
        w_block = (None, k, tn)
        w_map = lambda j, i, off: (layer, 0, off + j)
    w_specs = [pl.BlockSpec(w_block, functools.partial(w_map, off=b0 + g * group_stride)) for g in range(n_groups)]
    n_out = len(outs)

    def out_for(rows, row_tile, row_map, n, dt, by_rows):
        if not by_rows:
            return jax.ShapeDtypeStruct((rows, n), dt), pl.BlockSpec((row_tile, tn), row_map)
        assert n_tiles == 1 and n == tn
        per = n // LANES
        return (jax.ShapeDtypeStruct((rows * per, LANES), dt),
                pl.BlockSpec((row_tile * per, LANES), lambda j, i: (row_map(j, i)[0], 0)))

    prompt_outs = [out_for(m, tm, lambda j, i: (i, j), *o) for o in outs]
    sample_outs = [out_for(s, s, lambda j, i: (0, j), n, F32 if sample_f32 else dt, by_rows)
                   for n, dt, by_rows in outs]
    all_outs = prompt_outs + sample_outs
    operands = [x, xs] + [w] * n_groups
    in_specs = [pl.BlockSpec((tm, k), lambda j, i: (i, 0)), pl.BlockSpec((s, k), lambda j, i: (0, 0))] + w_specs
    n_i = m // tm
    if cast is not None:
        cw, c_layer = cast
        _, c_rows, c_cols = cw.shape
        bf16_rows = 2 * SUBLANES
        blk = next(r for r in range(bf16_rows, c_rows + 1, bf16_rows)
                   if c_rows % r == 0 and c_rows // r <= n_tiles * n_i)
        last = c_rows // blk - 1
        step = lambda j, i: jnp.minimum(j * n_i + i, last)
        in_specs.append(pl.BlockSpec((None, blk, c_cols), lambda j, i: (c_layer, step(j, i), 0)))
        operands.append(cw)
        all_outs.append((jax.ShapeDtypeStruct((c_rows, c_cols), BF16),
                         pl.BlockSpec((blk, c_cols), lambda j, i: (step(j, i), 0))))
    res = pl.pallas_call(
        functools.partial(_proj_kernel, n_groups=n_groups, n_out=n_out, epilogue=epilogue, transposed=transposed,
                          with_cast=cast is not None),
        grid=(n_tiles, n_i),
        in_specs=in_specs,
        out_specs=[spec for _, spec in all_outs],
        out_shape=[shape for shape, _ in all_outs],
        scratch_shapes=[pltpu.VMEM((n_groups,) + w_block[1:], BF16)],
        compiler_params=_cparams("arbitrary", "arbitrary"),
    )(*operands)
    return res[:n_out], res[n_out:2 * n_out], (res[2 * n_out] if cast is not None else None)


def _ep_plain(accs, outs):
    for acc, o in zip(accs, outs):
        o[...] = acc.astype(o.dtype)


def proj_plain(x, xs, w, col0, n_cols, tn, transposed=False, cast=None, prompt_dtype=F32):
    p, s, c = _proj(x, xs, w, col0=col0, tn=tn, n_tiles=n_cols // tn, group_stride=0, n_groups=1,
                    epilogue=_ep_plain, outs=[(n_cols, prompt_dtype, False)], transposed=transposed, cast=cast,
                    sample_f32=True)
    return (p[0], s[0]) if cast is None else (p[0], s[0], c)


def _ep_both_layouts(accs, outs):
    n = len(accs)
    for acc, o_tok, o_rows in zip(accs, outs[:n], outs[n:]):
        o_tok[...] = acc
        per = acc.shape[1] // LANES
        rows = acc.shape[0]
        for c in range(per):
            o_rows[pl.ds(c, rows, stride=per), :] = acc[:, c * LANES:(c + 1) * LANES]


def proj_groups(x, xs, w, col0, n_groups, group_cols, transposed=False):
    outs = [(group_cols, F32, False)] * n_groups + [(group_cols, F32, True)] * n_groups
    p, s, _ = _proj(x, xs, w, col0=col0, tn=group_cols, n_tiles=1, group_stride=1, n_groups=n_groups,
                    epilogue=_ep_both_layouts, outs=outs, tm_want=512, transposed=transposed)
    return (p[:n_groups], p[n_groups:]), (s[:n_groups], s[n_groups:])


def _ep_swiglu(accs, outs):
    g, u = accs
    outs[0][...] = (g * _sigmoid(g) * u).astype(BF16)


def proj_swiglu(x, xs, w, ff, cast):
    tn = 512
    p, s, c = _proj(x, xs, w, col0=0, tn=tn, n_tiles=ff // tn, group_stride=ff // tn, n_groups=2,
                    epilogue=_ep_swiglu, outs=[(ff, BF16, False)], cast=cast)
    return p[0], s[0], c


def _ep_conv_in(accs, outs):
    bg, cg, u = accs
    outs[0][...] = bg
    outs[1][...] = cg * u


def proj_conv_in(x, xs, w, d, cast):
    tn = 512
    return _proj(x, xs, w, col0=0, tn=tn, n_tiles=d // tn, group_stride=d // tn, n_groups=3,
                 epilogue=_ep_conv_in, outs=[(d, F32, False), (d, F32, False)], cast=cast)


def _conv_gated(bg_ref, z_ref, zp_ref, cw_ref, T):
    tm = z_ref.shape[0]
    z = z_ref[...]
    zc = jnp.concatenate([zp_ref[...], z], axis=0)
    tpos = (pl.program_id(0) * tm + _iota((tm, 1), 0)) % T
    y = cw_ref[CONV_W - 1:CONV_W, :] * z
    for back in range(1, CONV_W):
        zb = zc[SUBLANES - back:SUBLANES - back + tm, :]
        y = y + cw_ref[CONV_W - 1 - back:CONV_W - back, :] * jnp.where(tpos >= back, zb, 0.0)
    return (bg_ref[...] * y).astype(BF16)


def _out_kernel(*refs, emit_next, conv_T):
    if conv_T:
        h = _conv_gated(*refs[:4], conv_T)
        rows_of = lambda rows: h[rows, :]
        refs = refs[4:]
    else:
        h_ref = refs[0]
        rows_of = lambda rows: h_ref[rows, :]
        refs = refs[1:]
    hs_ref, w_ref, r_ref, rs_ref, nwa_ref, nwb_ref, *out_refs = refs

    def finalize(y, r, x_ref, xn_ref):
        x_new = r + _rms(y) * nwa_ref[...]
        x_ref[...] = x_new
        if emit_next:
            xn_ref[...] = (_rms(x_new) * nwb_ref[...]).astype(BF16)

    if emit_next:
        x_ref, xn_ref, xs_ref, xsn_ref = out_refs
    else:
        (x_ref, xs_ref), xn_ref, xsn_ref = out_refs, None, None

    @pl.when(pl.program_id(0) == 0)
    def _():
        finalize(_dot(hs_ref[...], w_ref[...]), rs_ref[...], xs_ref, xsn_ref)

    tm = r_ref.shape[0]
    sub = math.gcd(tm, OUT_SUB_ROWS)
    for r0 in range(0, tm, sub):
        rows = slice(r0, r0 + sub)
        finalize(_dot(rows_of(rows), w_ref[...]), r_ref[rows, :], x_ref.at[rows, :],
                 xn_ref.at[rows, :] if emit_next else None)


def proj_out(h, hs, w, resid, resid_s, nw_post, nw_next, emit_next=True, conv=None):
    m, d = resid.shape
    k = w.shape[0]
    s = hs.shape[0]
    tm = _row_tile(m, 512 if k <= 2048 else 256)
    row = lambda i: (i, 0)
    fixed = lambda i: (0, 0)
    if conv is None:
        h_ops, h_specs, conv_T = [h], [pl.BlockSpec((tm, k), row)], 0
    else:
        bg, z, conv_w, conv_T = conv
        assert conv_T % tm == 0
        per = tm // SUBLANES
        h_ops = [bg, z, z, conv_w]
        h_specs = [pl.BlockSpec((tm, k), row), pl.BlockSpec((tm, k), row),
                   pl.BlockSpec((SUBLANES, k), lambda i: (jnp.maximum(i * per - 1, 0), 0)),
                   pl.BlockSpec((CONV_W, k), fixed)]
    f32_out = [jax.ShapeDtypeStruct((m, d), F32), jax.ShapeDtypeStruct((s, d), F32)]
    bf16_out = [jax.ShapeDtypeStruct((m, d), BF16), jax.ShapeDtypeStruct((s, d), BF16)]
    specs = [pl.BlockSpec((tm, d), row), pl.BlockSpec((s, d), fixed)]
    if emit_next:
        out_shape = [f32_out[0], bf16_out[0], f32_out[1], bf16_out[1]]
        out_specs = [specs[0], specs[0], specs[1], specs[1]]
    else:
        out_shape, out_specs = f32_out, specs
    res = pl.pallas_call(
        functools.partial(_out_kernel, emit_next=emit_next, conv_T=conv_T),
        grid=(m // tm,),
        in_specs=h_specs + [pl.BlockSpec((s, k), fixed),
                            pl.BlockSpec((k, d), fixed, pipeline_mode=pl.Buffered(1)),
                            pl.BlockSpec((tm, d), row),
                            pl.BlockSpec((s, d), fixed),
                            pl.BlockSpec((1, d), fixed),
                            pl.BlockSpec((1, d), fixed)],
        out_specs=out_specs,
        out_shape=out_shape,
        compiler_params=_cparams("arbitrary"),
    )(*h_ops, hs, w, resid, resid_s, nw_post.reshape(1, d), nw_next.reshape(1, d))
    if emit_next:
        return (res[0], res[2]), (res[1], res[3])
    return (res[0], res[1]), (None, None)


def _log_sigmoid(x):
    return jnp.minimum(x, 0.0) - jnp.log1p(jnp.exp(-jnp.abs(x)))


def _mlstm_kernel(q_ref, k_ref, v_ref, o_ref, g_ref, bg_ref, hn_ref, hg_ref, c_ref, n_ref, m_ref, cn_ref,
                  *, L, H, DK, DV):
    chunk = pl.program_id(1)

    @pl.when(chunk == 0)
    def _():
        cn_ref[...] = jnp.zeros_like(cn_ref)
        m_ref[...] = jnp.zeros_like(m_ref)

    gpre = g_ref[...] + bg_ref[...]
    lf = _log_sigmoid(gpre)
    rows = _iota((L, L), 0)
    cols = _iota((L, L), 1)
    causal = rows >= cols
    tril = jnp.where(causal, 1.0, 0.0).astype(BF16)
    lf_hi, lf_lo = _split_bf16(lf)
    bcum = _dot(tril, lf_hi) + _dot(tril, lf_lo)
    g_t = gpre.T
    b_t = bcum.T
    for h in range(H):
        f = FORGET_BIAS_COL + h
        a_col = bcum[:, f:f + 1]
        i_col = gpre[:, h:h + 1]
        b_row = b_t[f:f + 1, :]
        i_row = g_t[h:h + 1, :]
        m_prev = m_ref[0, h:h + 1, 0:1]
        log_d = jnp.where(causal, a_col - b_row + i_row, NEG)
        log_inter = a_col + m_prev
        m_t = jnp.maximum(log_inter, jnp.max(log_d, axis=-1, keepdims=True))
        d = jnp.exp(log_d - m_t)
        inter = jnp.exp(log_inter - m_t)
        kh = k_ref[:, h * DK:(h + 1) * DK].astype(F32) * (DK ** -0.5)
        v1 = jnp.concatenate([v_ref[:, h * DV:(h + 1) * DV], jnp.ones((L, LANES), BF16)], axis=1)
        qb = q_ref[:, h * DK:(h + 1) * DK]
        s = _dot_nt(qb, kh.astype(BF16)) * d
        cn_old = cn_ref[h]
        both = _dot(s.astype(BF16), v1) + inter * _dot(qb, cn_old.astype(BF16))
        num, den = both[:, :DV], both[:, DV:]
        scale = 1.0 / jnp.maximum(jnp.abs(den), jnp.exp(-m_t))
        hh = _rms(num * jnp.concatenate([scale] * (DV // LANES), axis=1))
        gate = _sigmoid(o_ref[:, h * DV:(h + 1) * DV])
        hg_ref[:, h * DV:(h + 1) * DV] = (hh * hn_ref[:, h * DV:(h + 1) * DV] * gate).astype(BF16)
        m_new = m_t[L - 1:L, :]
        w_col = jnp.exp(a_col[L - 1:L, :] - a_col + i_col - m_new)
        decay = jnp.exp(log_inter[L - 1:L, :] - m_new)
        cn_new = decay * cn_old + _dot_tn((kh * w_col).astype(BF16), v1)
        cn_ref[h] = cn_new
        m_ref[0, h:h + 1, :] = jnp.broadcast_to(m_new, (1, LANES))

        @pl.when(chunk == pl.num_programs(1) - 1)
        def _():
            c_ref[0, h] = cn_new[:, :DV]
            n_ref[0, h:h + 1, :] = cn_new[:, DV:].T[0:1, :]


def mlstm_prompt(qkv, o, gates, b_gates, head_norm, bsz, T):
    H = MLSTM_HEADS
    m, hdv = o.shape
    hdk = hdv // 2
    DK, DV = hdk // H, hdv // H
    L = math.gcd(T, MLSTM_CHUNK)
    nc = T // L
    row = lambda b, c: (b * nc + c, 0)
    state = lambda b, c: (b, 0, 0)
    return pl.pallas_call(
        functools.partial(_mlstm_kernel, L=L, H=H, DK=DK, DV=DV),
        grid=(bsz, nc),
        in_specs=[pl.BlockSpec((L, hdk), row),
                  pl.BlockSpec((L, hdk), lambda b, c: (b * nc + c, 1)),
                  pl.BlockSpec((L, hdv), lambda b, c: (b * nc + c, 1)),
                  pl.BlockSpec((L, hdv), row),
                  pl.BlockSpec((L, LANES), row),
                  pl.BlockSpec((1, LANES), lambda b, c: (0, 0)),
                  pl.BlockSpec((1, hdv), lambda b, c: (0, 0))],
        out_specs=[pl.BlockSpec((L, hdv), row),
                   pl.BlockSpec((1, H, DK, DV), lambda b, c: (b, 0, 0, 0)),
                   pl.BlockSpec((1, H, DK), state),
                   pl.BlockSpec((1, H, LANES), state)],
        out_shape=[jax.ShapeDtypeStruct((m, hdv), BF16),
                   jax.ShapeDtypeStruct((bsz, H, DK, DV), F32),
                   jax.ShapeDtypeStruct((bsz, H, DK), F32),
                   jax.ShapeDtypeStruct((bsz, H, LANES), F32)],
        scratch_shapes=[pltpu.VMEM((H, DK, DV + LANES), F32)],
        compiler_params=_cparams("arbitrary", "arbitrary"),
    )(qkv, qkv, qkv, o, gates, _pad_lanes(b_gates.reshape(1, -1)), head_norm.reshape(1, hdv))


def _pad_lanes(x, width=LANES):
    return jnp.pad(x, ((0, 0), (0, width - x.shape[-1])))


def _to_col(row, n):
    eye = _iota((n, n), 0) == _iota((n, n), 1)
    return jnp.sum(jnp.where(eye, jnp.broadcast_to(row, (n, n)), 0.0), axis=-1, keepdims=True)


def _mlstm_step_kernel(x_ref, g_ref, bg_ref, hn_ref, c0_ref, n0_ref, m0_ref, hg_ref, c_ref, n_ref, m_ref,
                       *, H, DK, DV):
    x = x_ref[0]
    gpre = g_ref[0] + bg_ref[...]
    lf = _log_sigmoid(gpre)
    hdk = H * DK
    hdv = H * DV
    for h in range(H):
        f = FORGET_BIAS_COL + h
        ig = gpre[:, h:h + 1]
        m_prev = m0_ref[0, h:h + 1, :]
        log_inter = lf[:, f:f + 1] + m_prev
        m_t = jnp.maximum(log_inter, ig)
        d = jnp.exp(ig - m_t)
        inter = jnp.exp(log_inter - m_t)
        q = x[:, h * DK:(h + 1) * DK]
        k = x[:, hdk + h * DK:hdk + (h + 1) * DK] * (DK ** -0.5)
        v = x[:, 2 * hdk + h * DV:2 * hdk + (h + 1) * DV]
        og = x[:, 2 * hdk + hdv + h * DV:2 * hdk + hdv + (h + 1) * DV]
        c_old = c0_ref[0, h]
        n_old = n0_ref[0, h:h + 1, :]
        s = jnp.sum(q * k, axis=-1, keepdims=True) * d
        q_col = _to_col(q, DK)
        k_col = _to_col(k, DK)
        num = s * v + inter * jnp.sum(q_col * c_old, axis=0, keepdims=True)
        den = s + inter * jnp.sum(q * n_old, axis=-1, keepdims=True)
        hh = _rms(num / jnp.maximum(jnp.abs(den), jnp.exp(-m_t)))
        hg_ref[0, :, h * DV:(h + 1) * DV] = (hh * hn_ref[:, h * DV:(h + 1) * DV] * _sigmoid(og)).astype(BF16)
        c_ref[0, h] = inter * c_old + d * (k_col * v)
        n_ref[0, h:h + 1, :] = inter * n_old + d * k
        m_ref[0, h:h + 1, :] = m_t


def mlstm_step(qkvo, gates, b_gates, head_norm, c0, n0, m0):
    bsz, H, DK, DV = c0.shape
    wid = qkvo.shape[1]
    hdv = H * DV
    one = lambda b: (b, 0, 0)
    hg, c1, n1, m1 = pl.pallas_call(
        functools.partial(_mlstm_step_kernel, H=H, DK=DK, DV=DV),
        grid=(bsz,),
        in_specs=[pl.BlockSpec((1, 1, wid), one),
                  pl.BlockSpec((1, 1, LANES), one),
                  pl.BlockSpec((1, LANES), lambda b: (0, 0)),
                  pl.BlockSpec((1, hdv), lambda b: (0, 0)),
                  pl.BlockSpec((1, H, DK, DV), lambda b: (b, 0, 0, 0)),
                  pl.BlockSpec((1, H, DK), one),
                  pl.BlockSpec((1, H, 1), one)],
        out_specs=[pl.BlockSpec((1, 1, hdv), one),
                   pl.BlockSpec((1, H, DK, DV), lambda b: (b, 0, 0, 0)),
                   pl.BlockSpec((1, H, DK), one),
                   pl.BlockSpec((1, H, 1), one)],
        out_shape=[jax.ShapeDtypeStruct((bsz, 1, hdv), BF16),
                   jax.ShapeDtypeStruct((bsz, H, DK, DV), F32),
                   jax.ShapeDtypeStruct((bsz, H, DK), F32),
                   jax.ShapeDtypeStruct((bsz, H, 1), F32)],
        compiler_params=_cparams("parallel"),
    )(qkvo.reshape(bsz, 1, wid), gates.reshape(bsz, 1, LANES), _pad_lanes(b_gates.reshape(1, -1)),
      head_norm.reshape(1, hdv), c0, n0, m0.reshape(bsz, H, 1))
    return hg.reshape(bsz, hdv), c1, n1, m1.reshape(bsz, H)


def _conv_step_kernel(bg_ref, z_ref, buf_ref, cw_ref, o_ref):
    y = cw_ref[CONV_W - 1:CONV_W, :] * z_ref[...]
    for j in range(CONV_W - 1):
        y = y + cw_ref[j:j + 1, :] * buf_ref[j]
    o_ref[...] = bg_ref[...] * y


def conv_gate_step(bg, z, buf, conv_w):
    bsz, d = z.shape
    full = lambda: (0, 0)
    return pl.pallas_call(
        _conv_step_kernel,
        in_specs=[pl.BlockSpec((bsz, d), full), pl.BlockSpec((bsz, d), full),
                  pl.BlockSpec((CONV_W - 1, bsz, d), lambda: (0, 0, 0)), pl.BlockSpec((CONV_W, d), full)],
        out_specs=pl.BlockSpec((bsz, d), full),
        out_shape=jax.ShapeDtypeStruct((bsz, d), F32),
    )(bg, z, buf, conv_w)


def _gelu(x):
    return 0.5 * x * (1.0 + jnp.tanh(math.sqrt(2.0 / math.pi) * (x + 0.044715 * x * x * x)))


def _cmp_finish(a, posw, b1, w2, b2):
    hid = a.shape[1] // 2
    a1 = a[:, hid:]
    a1_next = jnp.concatenate([a1[1:], a1[:1]], axis=0)
    pre = a[:, :hid] + a1_next + b1 + posw[0:1, :hid] + posw[1:2, hid:]
    return _dot(_gelu(pre).astype(BF16), w2) + b2


def _cmp_prompt_kernel(k_ref, v_ref, w1_ref, pos_ref, b1_ref, w2_ref, b2_ref, ck_ref, cv_ref, *, nch):
    for idx, (x_ref, o_ref) in enumerate(((k_ref, ck_ref), (v_ref, cv_ref))):
        w1 = w1_ref[idx]
        lhs = jnp.concatenate([x_ref[pl.ds(p, nch, stride=CMP_STRIDE), :] for p in range(CMP_STRIDE)], axis=1)
        a = _dot(lhs.astype(BF16), w1)
        posw = _dot(pos_ref[idx], w1)
        o_ref[0, 0] = _cmp_finish(a, posw, b1_ref[idx], w2_ref[idx], b2_ref[idx])


def _cmp_weights(cmp_pos, cmp_w1, cmp_b1, cmp_w2, cmp_b2):
    two, ratio, stride, hd, hid = cmp_w1.shape
    w1 = cmp_w1.transpose(0, 2, 3, 1, 4).reshape(two, stride * hd, ratio * hid).astype(BF16)
    pos = cmp_pos.reshape(two, ratio, stride * hd)
    pos = jnp.pad(pos, ((0, 0), (0, SUBLANES - ratio), (0, 0))).astype(BF16)
    return w1, pos, cmp_b1.reshape(two, 1, hid), cmp_w2.astype(BF16), cmp_b2.reshape(two, 1, hd)


def cmp_prompt(kc, vc, bsz, T, cw):
    w1, pos, b1, w2, b2 = cw
    nch = T // CMP_STRIDE
    G = NSA_KV
    out = jax.ShapeDtypeStruct((bsz, G, nch, NSA_HD), F32)
    ospec = pl.BlockSpec((1, 1, nch, NSA_HD), lambda b, g: (b, g, 0, 0))
    whole = lambda a: pl.BlockSpec(a.shape, lambda b, g: (0,) * a.ndim)
    seq = pl.BlockSpec((T, NSA_HD), lambda b, g: (b, g))
    return pl.pallas_call(
        functools.partial(_cmp_prompt_kernel, nch=nch),
        grid=(bsz, G),
        in_specs=[seq, seq, whole(w1), whole(pos), whole(b1), whole(w2), whole(b2)],
        out_specs=[ospec, ospec],
        out_shape=[out, out],
        compiler_params=_cparams("parallel", "parallel"),
    )(kc, vc, w1, pos, b1, w2, b2)


def _overlap_t(nbs_pad, nbc_pad):
    j = _iota((nbs_pad, nbc_pad), 0)
    n = _iota((nbs_pad, nbc_pad), 1)
    lo = jnp.maximum(n * CMP_STRIDE, j * SEL_BLOCK)
    hi = jnp.minimum(n * CMP_STRIDE + CMP_BLOCK, j * SEL_BLOCK + SEL_BLOCK)
    return (jnp.maximum(hi - lo, 0) // CMP_STRIDE).astype(F32)


def _rank_select(score, n_sel, axis):
    n = score.shape[axis]
    idx = _iota(score.shape, axis)
    cnt = jnp.zeros(score.shape, F32)
    for jp in range(n):
        other = lax.slice_in_dim(score, jp, jp + 1, axis=axis)
        tie = jnp.where(idx > jp, 1.0, 0.0)
        cnt = cnt + jnp.where(other > score, 1.0, jnp.where(other == score, tie, 0.0))
    return jnp.where(cnt < n_sel, 1.0, 0.0)


def _attn_first(state, s, vt_b):
    m_ref, l_ref, acc_ref = state
    m = jnp.max(s, axis=0, keepdims=True)
    p = jnp.exp2(s - m)
    m_ref[...] = m
    l_ref[...] = jnp.sum(p, axis=0, keepdims=True)
    acc_ref[...] = _dot(vt_b, p.astype(BF16))


def _attn_more(state, s, vt_b):
    m_ref, l_ref, acc_ref = state
    m_old = m_ref[...]
    m_new = jnp.maximum(m_old, jnp.max(s, axis=0, keepdims=True))
    alpha = jnp.exp2(m_old - m_new)
    p = jnp.exp2(s - m_new)
    m_ref[...] = m_new
    l_ref[...] = alpha * l_ref[...] + jnp.sum(p, axis=0, keepdims=True)
    acc_ref[...] = alpha * acc_ref[...] + _dot(vt_b, p.astype(BF16))


def _attn_empty(state):
    m_ref, l_ref, acc_ref = state
    m_ref[...] = jnp.full(m_ref.shape, NEG, F32)
    l_ref[...] = jnp.zeros_like(l_ref)
    acc_ref[...] = jnp.zeros_like(acc_ref)


def _attn_merged(state_a, state_b):
    (ma_ref, la_ref, acca_ref), (mb_ref, lb_ref, accb_ref) = state_a, state_b
    m = jnp.maximum(ma_ref[...], mb_ref[...])
    wa = jnp.exp2(ma_ref[...] - m)
    wb = jnp.exp2(mb_ref[...] - m)
    return (wa * acca_ref[...] + wb * accb_ref[...]) / (wa * la_ref[...] + wb * lb_ref[...])


def _nsa_attn_kernel(q_ref, ck_ref, cv_ref, ks_ref, vs_ref, kw_ref, vw_ref, g_ref, bg_ref, o_ref,
                     ksb_ref, vst_ref, kwb_ref, vwt_ref, *states, tq, hpg, nbc, nbs, nbs_pad):
    qi = pl.program_id(2)
    t0 = qi * tq
    cols = hpg * tq
    tk = tq
    n_kt = ks_ref.shape[0] // tk
    nbc_pad = ck_ref.shape[2]

    @pl.when(qi == 0)
    def _():
        ksb_ref[...] = ks_ref[...].astype(BF16)
        kwb_ref[...] = kw_ref[...].astype(BF16)
        for kt in range(n_kt):
            vst_ref[kt] = vs_ref[kt * tk:(kt + 1) * tk, :].T.astype(BF16)
            vwt_ref[kt] = vw_ref[kt * tk:(kt + 1) * tk, :].T.astype(BF16)

    qb = (jnp.concatenate([q_ref[:, h * NSA_HD:(h + 1) * NSA_HD] for h in range(hpg)], axis=0)
          * (NSA_HD ** -0.5 * math.log2(math.e))).astype(BF16)

    n = _iota((nbc_pad, cols), 0)
    t_col = t0 + jnp.concatenate([_iota((nbc_pad, tq), 1)] * hpg, axis=1)
    cmask = jnp.where(n < nbc, jnp.where(n * CMP_STRIDE + (CMP_BLOCK - 1) <= t_col, 1.0, 0.0), 0.0)
    sc = jnp.where(cmask > 0.5, _dot_nt(ck_ref[0, 0].astype(BF16), qb), NEG)
    pe = jnp.exp2(sc - jnp.max(sc, axis=0, keepdims=True))
    p_cmp = pe / jnp.sum(pe, axis=0, keepdims=True) * cmask
    o_cmp = _dot(cv_ref[0, 0].T.astype(BF16), p_cmp.astype(BF16))

    p_sum = p_cmp[:, 0:tq]
    for h in range(1, hpg):
        p_sum = p_sum + p_cmp[:, h * tq:(h + 1) * tq]
    ov_t = _overlap_t(nbs_pad, nbc_pad).astype(BF16)
    p_hi, p_lo = _split_bf16(p_sum)
    imp_t = _dot(ov_t, p_hi) + _dot(ov_t, p_lo)
    j = _iota((nbs_pad, tq), 0)
    t = t0 + _iota((nbs_pad, tq), 1)
    t_blk = t // SEL_BLOCK
    forced = jnp.where(j == 0, 1.0, jnp.where(j == t_blk, 1.0, jnp.where(j == t_blk - 1, 1.0, 0.0)))
    visible = jnp.where(j < nbs, jnp.where(j * SEL_BLOCK <= t, 1.0, 0.0), 0.0)
    score = jnp.where(visible > 0.5, jnp.where(forced > 0.5, BIG, imp_t), NEG)
    sel_t = _rank_select(score, min(N_SELECT, nbs), axis=0).astype(BF16)

    k_ofs = _iota((tk, tq), 0)
    q_ofs = _iota((tk, tq), 1)
    causal = k_ofs <= q_ofs
    per_head = lambda bias: jnp.concatenate([bias] * hpg, axis=1)

    def scores(kb_ref, kt):
        k_b = kb_ref[pl.ds(pl.multiple_of(kt * tk, tk), tk), :]
        return _dot_nt(k_b, qb)

    def slc_bias(kt):
        blk = (kt * tk + _iota((tk, nbs_pad), 0)) // SEL_BLOCK
        expand = jnp.where(blk == _iota((tk, nbs_pad), 1), 1.0, 0.0).astype(BF16)
        return (_dot(expand, sel_t) - 1.0) * BIG

    slc_a, slc_b, win_a, win_b = states[0:3], states[3:6], states[6:9], states[9:12]

    def slc_tile(kt):
        return scores(ksb_ref, kt) + per_head(slc_bias(kt)), vst_ref[kt]

    def win_tile(back):
        s = scores(kwb_ref, qi - back)
        if back == n_back:
            s = s + per_head(jnp.where(k_ofs >= q_ofs, 0.0, NEG))
        return s, vwt_ref[qi - back]

    n_back = WINDOW // tk
    _attn_first(slc_a, scores(ksb_ref, qi) + per_head(jnp.where(causal, slc_bias(qi), NEG)), vst_ref[qi])
    _attn_first(win_a, scores(kwb_ref, qi) + per_head(jnp.where(causal, 0.0, NEG)), vwt_ref[qi])
    for st in (slc_b, win_b):
        _attn_empty(st)

    for back in range(1, n_back + 1, 2):
        both = back + 1 <= n_back

        @pl.when(qi >= back + 1 if both else qi >= back)
        def _():
            _attn_more(win_b, *win_tile(back))
            if both:
                _attn_more(win_a, *win_tile(back + 1))

        if both:
            @pl.when(qi == back)
            def _():
                _attn_more(win_b, *win_tile(back))

    def slc_pair(pair, _):
        _attn_more(slc_a, *slc_tile(2 * pair))
        _attn_more(slc_b, *slc_tile(2 * pair + 1))
        return 0

    lax.fori_loop(0, lax.shift_right_logical(qi, 1), slc_pair, 0)

    @pl.when(lax.bitwise_and(qi, 1) == 1)
    def _():
        _attn_more(slc_a, *slc_tile(qi - 1))

    o_slc = _attn_merged(slc_a, slc_b)
    o_win = _attn_merged(win_a, win_b)
    gate_t = _sigmoid(g_ref[...] + bg_ref[...]).T
    for h in range(hpg):
        sl = slice(h * tq, (h + 1) * tq)
        o_t = (gate_t[3 * h:3 * h + 1, :] * o_cmp[:, sl] + gate_t[3 * h + 1:3 * h + 2, :] * o_slc[:, sl]
               + gate_t[3 * h + 2:3 * h + 3, :] * o_win[:, sl])
        o_ref[:, h * NSA_HD:(h + 1) * NSA_HD] = o_t.T.astype(BF16)


def nsa_attn_prompt(q, ck, cv, ks, vs, kw, vw, gates, b_gate, bsz, T):
    G, HPG = NSA_KV, NSA_HPG
    tq = math.gcd(T, ATT_TILE)
    assert WINDOW % tq == 0
    nq = T // tq
    nch = T // CMP_STRIDE
    nbc = nch - CMP_BLOCK // CMP_STRIDE + 1
    nbs = -(-T // SEL_BLOCK)
    nbs_pad = -(-nbs // SUBLANES) * SUBLANES
    cols = HPG * tq
    kv_scratch = [pltpu.VMEM((T, NSA_HD), BF16), pltpu.VMEM((nq, NSA_HD, tq), BF16)] * 2
    stats = [pltpu.VMEM((1, cols), F32), pltpu.VMEM((1, cols), F32), pltpu.VMEM((NSA_HD, cols), F32)] * 4
    seq = pl.BlockSpec((T, NSA_HD), lambda b, g, i: (b, g))
    cspec = pl.BlockSpec((1, 1, nch, NSA_HD), lambda b, g, i: (b, g, 0, 0))
    qspec = pl.BlockSpec((tq, HPG * NSA_HD), lambda b, g, i: (b * nq + i, g))
    return pl.pallas_call(
        functools.partial(_nsa_attn_kernel, tq=tq, hpg=HPG, nbc=nbc, nbs=nbs, nbs_pad=nbs_pad),
        grid=(bsz, G, nq),
        in_specs=[qspec, cspec, cspec, seq, seq, seq, seq,
                  pl.BlockSpec((tq, LANES), lambda b, g, i: (b * nq + i, g)),
                  pl.BlockSpec((1, LANES), lambda b, g, i: (0, g))],
        out_specs=qspec,
        out_shape=jax.ShapeDtypeStruct((bsz * T, G * HPG * NSA_HD), BF16),
        scratch_shapes=kv_scratch + stats,
        compiler_params=_cparams("arbitrary", "arbitrary", "arbitrary"),
    )(q, ck, cv, ks, vs, kw, vw, gates, b_gate)


CHUNK_ROWS = CMP_STRIDE * NSA_KV
CHUNK_PITCH = CHUNK_ROWS + SUBLANES


def _dec_cmp_kernel(tbl_ref, *refs, P, nbc, nbs, nbs_pad, p0):
    k_pages, v_pages = refs[:P], refs[P:2 * P]
    w1_ref, q_ref, pos_ref, b1_ref, w2_ref, b2_ref, o_ref, sel_ref, pad_ref, ak_ref, av_ref = refs[2 * P:]
    cpp = LANES // CMP_STRIDE
    step = pl.program_id(1)
    for idx, (pages, a_ref) in enumerate(((k_pages, ak_ref), (v_pages, av_ref))):
        for i, pg in enumerate(pages):
            for c in range(cpp):
                pad_ref[idx * P + i, c * CHUNK_PITCH:c * CHUNK_PITCH + CHUNK_ROWS, :] = (
                    pg[0, c * CHUNK_ROWS:(c + 1) * CHUNK_ROWS, :])
        blocks = []
        for g in range(NSA_KV):
            for i in range(P):
                blocks.append(jnp.concatenate(
                    [pad_ref[idx * P + i, pl.ds(p * NSA_KV + g, cpp, stride=CHUNK_PITCH), :]
                     for p in range(CMP_STRIDE)], axis=1))
        a = _dot(jnp.concatenate(blocks, axis=0).astype(BF16), w1_ref[idx])
        per_g = P * cpp
        for g in range(NSA_KV):
            a_ref[g, pl.ds(pl.multiple_of(step * per_g, per_g), per_g), :] = a[g * per_g:(g + 1) * per_g]

    @pl.when(step == pl.num_programs(1) - 1)
    def _():
        _dec_cmp_finish(ak_ref, av_ref, q_ref, pos_ref, w1_ref, b1_ref, w2_ref, b2_ref, o_ref, sel_ref,
                        nbc=nbc, nbs=nbs, nbs_pad=nbs_pad, p0=p0)


def _dec_cmp_finish(ak_ref, av_ref, q_ref, pos_ref, w1_ref, b1_ref, w2_ref, b2_ref, o_ref, sel_ref,
                    *, nbc, nbs, nbs_pad, p0):
    nch = ak_ref.shape[1]
    for g in range(NSA_KV):
        cks = []
        for idx, a_ref in enumerate((ak_ref, av_ref)):
            posw = _dot(pos_ref[idx], w1_ref[idx])
            cks.append(_cmp_finish(a_ref[g], posw, b1_ref[idx], w2_ref[idx], b2_ref[idx]))
        ck, cv = cks
        qb = (q_ref[0, g] * (NSA_HD ** -0.5)).astype(BF16)
        n = _iota((SUBLANES, nch), 1)
        cmask = jnp.where(n < nbc, jnp.where(n * CMP_STRIDE + (CMP_BLOCK - 1) <= p0, 1.0, 0.0), 0.0)
        sc = jnp.where(cmask > 0.5, _dot_nt(qb, ck.astype(BF16)), NEG)
        pe = jnp.exp(sc - jnp.max(sc, axis=-1, keepdims=True))
        p_cmp = pe / jnp.sum(pe, axis=-1, keepdims=True) * cmask
        o_ref[0, g] = _dot(p_cmp.astype(BF16), cv.astype(BF16))

        head = jnp.where(_iota((SUBLANES, nch), 0) < NSA_HPG, p_cmp, 0.0)
        p_sum = jnp.broadcast_to(jnp.sum(head, axis=0, keepdims=True), (SUBLANES, nch))
        ov_t = _overlap_t(nbs_pad, nch).astype(BF16)
        p_hi, p_lo = _split_bf16(p_sum)
        imp = _dot_nt(p_hi, ov_t) + _dot_nt(p_lo, ov_t)
        j = _iota((SUBLANES, nbs_pad), 1)
        t_blk = p0 // SEL_BLOCK
        forced = jnp.where(j == 0, 1.0, jnp.where(j == t_blk, 1.0, jnp.where(j == t_blk - 1, 1.0, 0.0)))
        visible = jnp.where(j < nbs, jnp.where(j * SEL_BLOCK <= p0, 1.0, 0.0), 0.0)
        score = jnp.where(visible > 0.5, jnp.where(forced > 0.5, BIG, imp), NEG)
        s_row = jnp.broadcast_to(score[0:1], (nbs_pad, nbs_pad))
        s_col = jnp.concatenate([jnp.broadcast_to(score[0:1], (LANES, nbs_pad)).T] * (nbs_pad // LANES), axis=1)
        jr = _iota((nbs_pad, nbs_pad), 0)
        jc = _iota((nbs_pad, nbs_pad), 1)
        tie = jnp.where(jc < jr, 1.0, 0.0)
        beats = jnp.where(s_row > s_col, 1.0, jnp.where(s_row == s_col, tie, 0.0))
        rank = jnp.sum(beats, axis=-1, keepdims=True)
        slot = _iota((nbs_pad, LANES), 1).astype(F32)
        blk = _iota((nbs_pad, LANES), 0).astype(F32)
        picked = jnp.sum(jnp.where(rank == slot, blk, 0.0), axis=0, keepdims=True)
        sel_ref[0, g:g + 1, :] = picked.astype(jnp.int32)


def dec_cmp(pool_k, pool_v, table, q, cw, p0):
    w1, pos, b1, w2, b2 = cw
    n_pool, page, G, hd = pool_k.shape
    assert page == LANES and G == NSA_KV and hd == NSA_HD
    bsz, n_pages = table.shape
    assert p0 == n_pages * page
    P = math.gcd(n_pages, PAGES_PER_STEP)
    cpp = page // CMP_STRIDE
    nch = n_pages * cpp
    hid2 = w1.shape[2]
    nbc = nch - CMP_BLOCK // CMP_STRIDE + 1
    nbs = -(-(p0 + 1) // SEL_BLOCK)
    nbs_pad = -(-nbs // LANES) * LANES
    pk = pool_k.reshape(n_pool, page * G, hd)
    pv = pool_v.reshape(n_pool, page * G, hd)
    pspec = lambda i: pl.BlockSpec((1, page * G, hd),
                                   functools.partial(lambda b, s, tbl, i: (tbl[b * n_pages + s * P + i], 0, 0), i=i))
    whole = lambda a: pl.BlockSpec(a.shape, lambda b, s, tbl: (0,) * a.ndim)
    per_seq = lambda shape: pl.BlockSpec((1,) + shape, lambda b, s, tbl: (b,) + (0,) * len(shape))
    grid_spec = pltpu.PrefetchScalarGridSpec(
        num_scalar_prefetch=1,
        grid=(bsz, n_pages // P),
        in_specs=[pspec(i) for i in range(P)] * 2
        + [whole(w1), per_seq((G, SUBLANES, hd)), whole(pos), whole(b1), whole(w2), whole(b2)],
        out_specs=[per_seq((G, SUBLANES, hd)), per_seq((G, LANES))],
        scratch_shapes=[pltpu.VMEM((2 * P, cpp * CHUNK_PITCH, hd), F32),
                        pltpu.VMEM((G, nch, hid2), F32), pltpu.VMEM((G, nch, hid2), F32)],
    )
    return pl.pallas_call(
        functools.partial(_dec_cmp_kernel, P=P, nbc=nbc, nbs=nbs, nbs_pad=nbs_pad, p0=p0),
        grid_spec=grid_spec,
        out_shape=[jax.ShapeDtypeStruct((bsz, G, SUBLANES, hd), F32),
                   jax.ShapeDtypeStruct((bsz, G, LANES), jnp.int32)],
        compiler_params=_cparams("arbitrary", "arbitrary"),
    )(table.reshape(-1), *([pk] * P), *([pv] * P), w1, q, pos, b1, w2, b2)


def _softmax_with_new_key(qb, k_b, v_b, bias, s_new, v_new):
    s = _dot_nt(qb, k_b)
    if bias is not None:
        s = s + bias
    mx = jnp.maximum(jnp.max(s, axis=-1, keepdims=True), s_new)
    p = jnp.exp(s - mx)
    p_new = jnp.exp(s_new - mx)
    return (_dot(p.astype(BF16), v_b) + p_new * v_new) / (jnp.sum(p, axis=-1, keepdims=True) + p_new)


def _dec_attn_kernel(sel_ref, tbl_ref, *refs, n_cache_blocks, n_sel):
    ks_refs, vs_refs = refs[:n_sel], refs[n_sel:2 * n_sel]
    q_ref, new_ref, kw_ref, vw_ref, oc_ref, g_ref, bg_ref, o_ref = refs[2 * n_sel:]
    b, g = pl.program_id(0), pl.program_id(1)
    q = q_ref[0, 0] * (NSA_HD ** -0.5)
    qb = q.astype(BF16)
    new = new_ref[0, 0]

    mine = lambda r, n: r[pl.ds(g, n, stride=NSA_KV), :]
    k_all = jnp.concatenate([mine(r.at[0, 0], SEL_BLOCK) for r in ks_refs], axis=0).astype(BF16)
    v_all = jnp.concatenate([mine(r.at[0, 0], SEL_BLOCK) for r in vs_refs], axis=0).astype(BF16)
    slot = _iota((SUBLANES, n_sel * SEL_BLOCK), 1) // SEL_BLOCK
    bias = jnp.zeros((SUBLANES, n_sel * SEL_BLOCK), F32)
    for s in range(n_sel):
        blk = sel_ref[(b * NSA_KV + g) * n_sel + s]
        bias = jnp.where(slot == s, jnp.where(blk < n_cache_blocks, 0.0, NEG), bias)
    o_slc = _softmax_with_new_key(qb, k_all, v_all, bias, jnp.sum(q * new[0:1], axis=-1, keepdims=True), new[1:2])

    n_win = kw_ref.shape[1] // NSA_KV
    o_win = _softmax_with_new_key(qb, mine(kw_ref.at[0], n_win).astype(BF16), mine(vw_ref.at[0], n_win).astype(BF16),
                                  None,
                                  jnp.sum(q * new[2:3], axis=-1, keepdims=True), new[3:4])

    gate = jnp.broadcast_to(_sigmoid(g_ref[0, 0] + bg_ref[...]), (SUBLANES, LANES))
    head = _iota((SUBLANES, LANES), 0)
    lane = _iota((SUBLANES, LANES), 1)
    o = jnp.zeros((SUBLANES, NSA_HD), F32)
    for br, ob in enumerate((oc_ref[0, 0], o_slc, o_win)):
        o = o + jnp.sum(jnp.where(lane == 3 * head + br, gate, 0.0), axis=-1, keepdims=True) * ob
    o_ref[0, 0] = o


def dec_attn(sel, table, pool_ks, pool_vs, q, new, kw_past, vw_past, o_cmp, gates, b_gate):
    n_pool, page, G, hd = pool_ks.shape
    bsz, n_pages = table.shape
    halves = page // SEL_BLOCK
    wb = kw_past.shape[1]
    assert wb <= WINDOW
    n_sel = sel.shape[-1]
    pk = pool_ks.reshape(n_pool, halves, SEL_BLOCK * G, hd)
    pv = pool_vs.reshape(n_pool, halves, SEL_BLOCK * G, hd)
    n_cache_blocks = n_pages * halves

    def page_map(b, g, sel_ref, tbl_ref, s):
        blk = jnp.minimum(sel_ref[(b * G + g) * n_sel + s], n_cache_blocks - 1)
        return (tbl_ref[b * n_pages + blk // halves], blk % halves, 0, 0)

    bg_map = lambda b, g, sel_ref, tbl_ref: (b, g, 0, 0)
    small = pl.BlockSpec((1, 1, SUBLANES, hd), bg_map)
    wspec = pl.BlockSpec((1, wb * G, hd), lambda b, g, sel_ref, tbl_ref: (b, 0, 0))
    blocks = [pl.BlockSpec((1, 1, SEL_BLOCK * G, hd), functools.partial(page_map, s=s)) for s in range(n_sel)]
    grid_spec = pltpu.PrefetchScalarGridSpec(
        num_scalar_prefetch=2,
        grid=(bsz, G),
        in_specs=blocks + blocks + [small, small, wspec, wspec, small,
                                    pl.BlockSpec((1, 1, 1, LANES), bg_map),
                                    pl.BlockSpec((1, LANES), lambda b, g, sel_ref, tbl_ref: (0, g))],
        out_specs=small,
    )
    return pl.pallas_call(
        functools.partial(_dec_attn_kernel, n_cache_blocks=n_cache_blocks, n_sel=n_sel),
        grid_spec=grid_spec,
        out_shape=jax.ShapeDtypeStruct((bsz, G, SUBLANES, hd), F32),
        compiler_params=_cparams("parallel", "parallel"),
    )(sel.reshape(-1), table.reshape(-1), *([pk] * n_sel), *([pv] * n_sel), q, new,
      kw_past.reshape(bsz, wb * G, hd), vw_past.reshape(bsz, wb * G, hd), o_cmp,
      gates.reshape(bsz, G, 1, LANES), b_gate)


def _gate_weights(wt, layer, row0, n):
    rows = lax.slice(wt, (layer, row0, 0), (layer + 1, row0 + n, wt.shape[2]))
    return jnp.pad(rows, ((0, 0), (0, LANES - n), (0, 0))), 0


def _nsa_gate_weights(wt, layer, b_gate, row0):
    per = 3 * NSA_HPG
    k = wt.shape[2]
    rows = lax.slice(wt, (layer, row0, 0), (layer + 1, row0 + NSA_KV * per, k)).reshape(NSA_KV, per, k)
    wg = jnp.pad(rows, ((0, 0), (0, LANES - per), (0, 0))).reshape(1, NSA_KV * LANES, k)
    bg = jnp.pad(b_gate.reshape(NSA_KV, per), ((0, 0), (0, LANES - per))).reshape(1, NSA_KV * LANES)
    return (wg, 0), bg


def kernel(x_prompt, x_sample, state_a_C, state_a_n, state_a_m, state_b_conv, cache_c_k_cmp, cache_c_v_cmp, cache_c_k_slc, cache_c_v_slc, cache_c_k_win, cache_c_v_win, page_table, norm_w, ffn_w_in, ffn_w_out, a_w_in, a_b_gates, a_head_norm, a_w_out, b_w_in, b_conv_w, b_w_out, c_w_in, c_b_gate, c_cmp_pos, c_cmp_w1, c_cmp_b1, c_cmp_w2, c_cmp_b2, c_w_out):
    bp, T, D = x_prompt.shape
    bs = x_sample.shape[0]
    assert x_sample.shape[1] == 1
    depth = norm_w.shape[0]
    G, hd = NSA_KV, NSA_HD
    past_len = page_table.shape[1] * cache_c_k_cmp.shape[2]
    xp = x_prompt.reshape(bp * T, D)
    xs = x_sample.reshape(bs, D)
    hp_in = norm_cast(xp, norm_w[0, 0])
    hs_in = norm_cast(xs, norm_w[0, 0])
    a_wt = jnp.swapaxes(a_w_in, 1, 2)
    c_wt = jnp.swapaxes(c_w_in, 1, 2)
    w_outs = {0: a_w_out, 1: b_w_out, 2: c_w_out}
    pa, sa, pb, sb, pc, sc = [], [], [], [], [], []
    for i in range(depth):
        kind, j = i % 3, i // 3
        nw = norm_w[i]
        w_out_f32 = (w_outs[kind], j)
        conv = None
        if kind == 0:
            hdv = a_head_norm.shape[1]
            n_main = 3 * hdv
            wg = _gate_weights(a_wt, j, n_main, 2 * MLSTM_HEADS)
            qkv_p, qkv_s, w_out = proj_plain(hp_in, hs_in, (a_wt, j), 0, 2 * hdv, 1024, transposed=True,
                                             cast=w_out_f32, prompt_dtype=BF16)
            o_p, o_s = proj_plain(hp_in, hs_in, (a_wt, j), 2 * hdv, hdv, 1024, transposed=True)
            qkvo_s = jnp.concatenate([qkv_s, o_s], axis=1)
            g_p, g_s = proj_plain(hp_in, hs_in, wg, 0, LANES, LANES, transposed=True)
            hp, c_p, n_p, m_p = mlstm_prompt(qkv_p, o_p, g_p, a_b_gates[j], a_head_norm[j], bp, T)
            hs, c_s, n_s, m_s = mlstm_step(qkvo_s, g_s, a_b_gates[j], a_head_norm[j],
                                           state_a_C[j], state_a_n[j], state_a_m[j])
            pa.append((c_p, n_p, m_p[:, :, 0]))
            sa.append((c_s, n_s, m_s))
        elif kind == 1:
            (bg_p, z_p), (bg_s, z_s), w_out = proj_conv_in(hp_in, hs_in, (b_w_in, j), D, w_out_f32)
            hp, conv = None, (bg_p, z_p, b_conv_w[j], T)
            buf = state_b_conv[j]
            hs = conv_gate_step(bg_s, z_s, buf.transpose(1, 0, 2), b_conv_w[j]).astype(BF16)
            pb.append(z_p.reshape(bp, T, D)[:, T - (CONV_W - 1):])
            sb.append(jnp.concatenate([buf, z_s[:, None]], axis=1)[:, -(CONV_W - 1):])
        else:
            nq_cols = NSA_HPG * G * hd
            nheads = G * NSA_HPG
            wg, bgate = _nsa_gate_weights(c_wt, j, c_b_gate[j], nq_cols + 6 * G * hd)
            cw = _cmp_weights(c_cmp_pos[j], c_cmp_w1[j], c_cmp_b1[j], c_cmp_w2[j], c_cmp_b2[j])
            q_p, q_s, w_out = proj_plain(hp_in, hs_in, (c_wt, j), 0, nq_cols, 1024, transposed=True, cast=w_out_f32)
            kv_p, kv_rows_p, kv_s = [], [], []
            for half in range(2):
                (tok_p, rows_p), (tok_s, _) = proj_groups(hp_in, hs_in, (c_wt, j), nq_cols + half * 3 * G * hd,
                                                         3, G * hd, transposed=True)
                kv_p, kv_rows_p, kv_s = kv_p + list(tok_p), kv_rows_p + list(rows_p), kv_s + list(tok_s)
            gt_p, gt_s = proj_plain(hp_in, hs_in, wg, 0, G * LANES, G * LANES, transposed=True)
            ck, cv = cmp_prompt(kv_p[0], kv_p[1], bp, T, cw)
            hp = nsa_attn_prompt(q_p, ck, cv, kv_p[2], kv_p[3], kv_p[4], kv_p[5], gt_p, bgate, bp, T)
            kv_rows_p = [a.reshape(bp, T, G, hd) for a in kv_rows_p]
            keep = min(WINDOW, T)
            pc.append(tuple(kv_rows_p[:4]) + tuple(a[:, T - keep:] for a in kv_rows_p[4:]))
            q_s = jnp.pad(q_s.reshape(bs, G, NSA_HPG, hd), ((0, 0), (0, 0), (0, SUBLANES - NSA_HPG), (0, 0)))
            kv_s = jnp.stack([a.reshape(bs, G, hd) for a in kv_s], axis=1)
            new = jnp.pad(kv_s[:, 2:6].transpose(0, 2, 1, 3), ((0, 0), (0, 0), (0, SUBLANES - 4), (0, 0)))
            o_cmp, ranked = dec_cmp(cache_c_k_cmp[j], cache_c_v_cmp[j], page_table, q_s, cw, past_len)
            n_sel = min(N_SELECT, -(-(past_len + 1) // SEL_BLOCK))
            o_s = dec_attn(ranked[:, :, :n_sel], page_table, cache_c_k_slc[j], cache_c_v_slc[j], q_s, new,
                           cache_c_k_win[j], cache_c_v_win[j], o_cmp, gt_s, bgate)
            hs = o_s[:, :, :NSA_HPG].reshape(bs, nheads * hd).astype(BF16)
            keep_s = min(WINDOW, cache_c_k_win.shape[2] + 1)
            win = [jnp.concatenate([c[j], kv_s[:, a][:, None]], axis=1)[:, -keep_s:]
                   for c, a in ((cache_c_k_win, 4), (cache_c_v_win, 5))]
            sc.append(tuple(kv_s[:, a][:, None] for a in range(4)) + tuple(win))
        last = i == depth - 1
        nw_next = norm_w[i + 1, 0] if not last else nw[0]
        (xp, xs), (hp_mid, hs_mid) = proj_out(hp, hs, w_out, xp, xs, nw[1], nw[2], conv=conv)
        act_p, act_s, f_out = proj_swiglu(hp_mid, hs_mid, (ffn_w_in, i), ffn_w_out.shape[1], (ffn_w_out, i))
        (xp, xs), (hp_in, hs_in) = proj_out(act_p, act_s, f_out, xp, xs, nw[3], nw_next, not last)

    stack = lambda items: [jnp.stack(a) for a in zip(*items)]
    p_a, s_a = stack(pa), stack(sa)
    p_c, s_c = stack(pc), stack(sc)
    return (xp.reshape(bp, T, D), xs.reshape(bs, 1, D), *p_a, *s_a, jnp.stack(pb), jnp.stack(sb), *p_c, *s_c)
```

```python
import functools
import math

import jax
import jax.numpy as jnp
from jax import lax
from jax.experimental import pallas as pl
from jax.experimental.pallas import tpu as pltpu

F32 = jnp.float32
BF16 = jnp.bfloat16

EPS = 1e-6
NEG = -1e30
BIG = 1e30

LANES = 128
SUBLANES = 8
VMEM_LIMIT = 56 * 1024 * 1024

MLSTM_HEADS = 8
MLSTM_CHUNK = 256
FORGET_BIAS_COL = MLSTM_HEADS
CONV_W = 3
NSA_HD = 128
NSA_KV = 4
NSA_HPG = 4
CMP_BLOCK = 32
CMP_STRIDE = 16
SEL_BLOCK = 64
N_SELECT = 16
WINDOW = 512
ATT_TILE = 256
PAGES_PER_STEP = 16
OUT_SUB_ROWS = 256
OUT_VMEM_SLACK = 6 * 1024 * 1024


def _cparams(*sem, vmem=VMEM_LIMIT):
    return pltpu.CompilerParams(dimension_semantics=sem, vmem_limit_bytes=vmem)


def _dot(a, b):
    return jnp.dot(a, b, preferred_element_type=F32)


def _dot_nt(a, b):
    return lax.dot_general(a, b, (((1,), (1,)), ((), ())), preferred_element_type=F32)


def _dot_tn(a, b):
    return lax.dot_general(a, b, (((0,), (0,)), ((), ())), preferred_element_type=F32)


def _split_bf16(x):
    hi = x.astype(BF16)
    lo = (x - hi.astype(F32)).astype(BF16)
    return hi, lo


def _iota(shape, dim):
    return lax.broadcasted_iota(jnp.int32, shape, dim)


def _sigmoid(x):
    return 1.0 / (1.0 + jnp.exp(-x))


def _rms(x):
    return x * lax.rsqrt(jnp.mean(x * x, axis=-1, keepdims=True) + EPS)


def _row_tile(m, want):
    t = min(m, want)
    assert m % t == 0, (m, t)
    return t


def _norm_kernel(x_ref, w_ref, o_ref):
    o_ref[...] = (_rms(x_ref[...]) * w_ref[...]).astype(BF16)


def norm_cast(x, w):
    m, d = x.shape
    tm = _row_tile(m, 512)
    return pl.pallas_call(
        _norm_kernel,
        grid=(m // tm,),
        in_specs=[pl.BlockSpec((tm, d), lambda i: (i, 0)), pl.BlockSpec((1, d), lambda i: (0, 0))],
        out_specs=pl.BlockSpec((tm, d), lambda i: (i, 0)),
        out_shape=jax.ShapeDtypeStruct((m, d), BF16),
        compiler_params=_cparams("parallel"),
    )(x, w.reshape(1, d))


def _proj_kernel(x_ref, xs_ref, *refs, n_groups, n_out, epilogue, transposed, with_cast):
    w_refs = refs[:n_groups]
    refs = refs[n_groups:]
    if with_cast:
        cast_in_ref, refs = refs[0], refs[1:]
        refs[2 * n_out][...] = cast_in_ref[...].astype(BF16)
    out_refs = refs[:n_out]
    sample_out_refs = refs[n_out:2 * n_out]
    wb_ref = refs[-1]
    mm = _dot_nt if transposed else _dot

    @pl.when(pl.program_id(1) == 0)
    def _():
        for g in range(n_groups):
            wb_ref[g] = w_refs[g][...].astype(BF16)
        xs = xs_ref[...]
        epilogue([mm(xs, wb_ref[g]) for g in range(n_groups)], sample_out_refs)

    x = x_ref[...]
    epilogue([mm(x, wb_ref[g]) for g in range(n_groups)], out_refs)


def _proj(x, xs, w, *, col0, tn, n_tiles, group_stride, n_groups, epilogue, outs, tm_want=1024, transposed=False,
          cast=None, sample_f32=False):
    w, layer = w
    m, k = x.shape
    s = xs.shape[0]
    tm = _row_tile(m, tm_want)
    assert col0 % tn == 0
    b0 = col0 // tn
    if transposed:
        w_block = (None, tn, k)
        w_map = lambda j, i, off: (layer, off + j, 0)
    else:
        w_block = (None, k, tn)
        w_map = lambda j, i, off: (layer, 0, off + j)
    w_specs = [pl.BlockSpec(w_block, functools.partial(w_map, off=b0 + g * group_stride)) for g in range(n_groups)]
    n_out = len(outs)

    def out_for(rows, row_tile, row_map, n, dt, by_rows):
        if not by_rows:
            return jax.ShapeDtypeStruct((rows, n), dt), pl.BlockSpec((row_tile, tn), row_map)
        assert n_tiles == 1 and n == tn
        per = n // LANES
        return (jax.ShapeDtypeStruct((rows * per, LANES), dt),
                pl.BlockSpec((row_tile * per, LANES), lambda j, i: (row_map(j, i)[0], 0)))

    prompt_outs = [out_for(m, tm, lambda j, i: (i, j), *o) for o in outs]
    sample_outs = [out_for(s, s, lambda j, i: (0, j), n, F32 if sample_f32 else dt, by_rows)
                   for n, dt, by_rows in outs]
    all_outs = prompt_outs + sample_outs
    operands = [x, xs] + [w] * n_groups
    in_specs = [pl.BlockSpec((tm, k), lambda j, i: (i, 0)), pl.BlockSpec((s, k), lambda j, i: (0, 0))] + w_specs
    n_i = m // tm
    if cast is not None:
        cw, c_layer = cast
        _, c_rows, c_cols = cw.shape
        bf16_rows = 2 * SUBLANES
        blk = next(r for r in range(bf16_rows, c_rows + 1, bf16_rows)
                   if c_rows % r == 0 and c_rows // r <= n_tiles * n_i)
        last = c_rows // blk - 1
        step = lambda j, i: jnp.minimum(j * n_i + i, last)
        in_specs.append(pl.BlockSpec((None, blk, c_cols), lambda j, i: (c_layer, step(j, i), 0)))
        operands.append(cw)
        all_outs.append((jax.ShapeDtypeStruct((c_rows, c_cols), BF16),
                         pl.BlockSpec((blk, c_cols), lambda j, i: (step(j, i), 0))))
    res = pl.pallas_call(
        functools.partial(_proj_kernel, n_groups=n_groups, n_out=n_out, epilogue=epilogue, transposed=transposed,
                          with_cast=cast is not None),
        grid=(n_tiles, n_i),
        in_specs=in_specs,
        out_specs=[spec for _, spec in all_outs],
        out_shape=[shape for shape, _ in all_outs],
        scratch_shapes=[pltpu.VMEM((n_groups,) + w_block[1:], BF16)],
        compiler_params=_cparams("arbitrary", "arbitrary"),
    )(*operands)
    return res[:n_out], res[n_out:2 * n_out], (res[2 * n_out] if cast is not None else None)


def _ep_plain(accs, outs):
    for acc, o in zip(accs, outs):
        o[...] = acc.astype(o.dtype)


def proj_plain(x, xs, w, col0, n_cols, tn, transposed=False, cast=None, prompt_dtype=F32):
    p, s, c = _proj(x, xs, w, col0=col0, tn=tn, n_tiles=n_cols // tn, group_stride=0, n_groups=1,
                    epilogue=_ep_plain, outs=[(n_cols, prompt_dtype, False)], transposed=transposed, cast=cast,
                    sample_f32=True)
    return (p[0], s[0]) if cast is None else (p[0], s[0], c)


def _ep_both_layouts(accs, outs):
    n = len(accs)
    for acc, o_tok, o_rows in zip(accs, outs[:n], outs[n:]):
        o_tok[...] = acc
        per = acc.shape[1] // LANES
        rows = acc.shape[0]
        for c in range(per):
            o_rows[pl.ds(c, rows, stride=per), :] = acc[:, c * LANES:(c + 1) * LANES]


def proj_groups(x, xs, w, col0, n_groups, group_cols, transposed=False):
    outs = [(group_cols, F32, False)] * n_groups + [(group_cols, F32, True)] * n_groups
    p, s, _ = _proj(x, xs, w, col0=col0, tn=group_cols, n_tiles=1, group_stride=1, n_groups=n_groups,
                    epilogue=_ep_both_layouts, outs=outs, tm_want=512, transposed=transposed)
    return (p[:n_groups], p[n_groups:]), (s[:n_groups], s[n_groups:])


def _ep_swiglu(accs, outs):
    g, u = accs
    outs[0][...] = (g * _sigmoid(g) * u).astype(BF16)


def proj_swiglu(x, xs, w, ff, cast):
    tn = 512
    p, s, c = _proj(x, xs, w, col0=0, tn=tn, n_tiles=ff // tn, group_stride=ff // tn, n_groups=2,
                    epilogue=_ep_swiglu, outs=[(ff, BF16, False)], cast=cast)
    return p[0], s[0], c


def _ep_conv_in(accs, outs):
    bg, cg, u = accs
    outs[0][...] = bg
    outs[1][...] = cg * u


def proj_conv_in(x, xs, w, d, cast):
    tn = 512
    return _proj(x, xs, w, col0=0, tn=tn, n_tiles=d // tn, group_stride=d // tn, n_groups=3,
                 epilogue=_ep_conv_in, outs=[(d, F32, False), (d, F32, False)], cast=cast)


def _conv_gated(bg_ref, z_ref, zp_ref, cw_ref, T):
    tm = z_ref.shape[0]
    z = z_ref[...]
    zc = jnp.concatenate([zp_ref[...], z], axis=0)
    tpos = (pl.program_id(0) * tm + _iota((tm, 1), 0)) % T
    y = cw_ref[CONV_W - 1:CONV_W, :] * z
    for back in range(1, CONV_W):
        zb = zc[SUBLANES - back:SUBLANES - back + tm, :]
        y = y + cw_ref[CONV_W - 1 - back:CONV_W - back, :] * jnp.where(tpos >= back, zb, 0.0)
    return (bg_ref[...] * y).astype(BF16)


def _out_kernel(*refs, emit_next, conv_T):
    if conv_T:
        h = _conv_gated(*refs[:4], conv_T)
        rows_of = lambda rows: h[rows, :]
        refs = refs[4:]
    else:
        h_ref = refs[0]
        rows_of = lambda rows: h_ref[rows, :]
        refs = refs[1:]
    hs_ref, w_ref, r_ref, rs_ref, nwa_ref, nwb_ref, *out_refs = refs

    def finalize(y, r, x_ref, xn_ref):
        x_new = r + _rms(y) * nwa_ref[...]
        x_ref[...] = x_new
        if emit_next:
            xn_ref[...] = (_rms(x_new) * nwb_ref[...]).astype(BF16)

    if emit_next:
        x_ref, xn_ref, xs_ref, xsn_ref = out_refs
    else:
        (x_ref, xs_ref), xn_ref, xsn_ref = out_refs, None, None

    @pl.when(pl.program_id(0) == 0)
    def _():
        finalize(_dot(hs_ref[...], w_ref[...]), rs_ref[...], xs_ref, xsn_ref)

    tm = r_ref.shape[0]
    sub = math.gcd(tm, OUT_SUB_ROWS)
    for r0 in range(0, tm, sub):
        rows = slice(r0, r0 + sub)
        finalize(_dot(rows_of(rows), w_ref[...]), r_ref[rows, :], x_ref.at[rows, :],
                 xn_ref.at[rows, :] if emit_next else None)


def proj_out(h, hs, w, resid, resid_s, nw_post, nw_next, emit_next=True, conv=None):
    m, d = resid.shape
    k = w.shape[0]
    s = hs.shape[0]
    tm = _row_tile(m, 2 * OUT_SUB_ROWS)
    h_bytes = tm * k * (2 if conv is None else 8)
    conv_tmp = 0 if conv is None else h_bytes
    vmem = k * d * 2 + 2 * (h_bytes + tm * d * (4 + 4 + (2 if emit_next else 0))) + conv_tmp + OUT_VMEM_SLACK
    row = lambda i: (i, 0)
    fixed = lambda i: (0, 0)
    if conv is None:
        h_ops, h_specs, conv_T = [h], [pl.BlockSpec((tm, k), row)], 0
    else:
        bg, z, conv_w, conv_T = conv
        assert conv_T % tm == 0
        per = tm // SUBLANES
        h_ops = [bg, z, z, conv_w]
        h_specs = [pl.BlockSpec((tm, k), row), pl.BlockSpec((tm, k), row),
                   pl.BlockSpec((SUBLANES, k), lambda i: (jnp.maximum(i * per - 1, 0), 0)),
                   pl.BlockSpec((CONV_W, k), fixed)]
    f32_out = [jax.ShapeDtypeStruct((m, d), F32), jax.ShapeDtypeStruct((s, d), F32)]
    bf16_out = [jax.ShapeDtypeStruct((m, d), BF16), jax.ShapeDtypeStruct((s, d), BF16)]
    specs = [pl.BlockSpec((tm, d), row), pl.BlockSpec((s, d), fixed)]
    if emit_next:
        out_shape = [f32_out[0], bf16_out[0], f32_out[1], bf16_out[1]]
        out_specs = [specs[0], specs[0], specs[1], specs[1]]
    else:
        out_shape, out_specs = f32_out, specs
    res = pl.pallas_call(
        functools.partial(_out_kernel, emit_next=emit_next, conv_T=conv_T),
        grid=(m // tm,),
        in_specs=h_specs + [pl.BlockSpec((s, k), fixed),
                            pl.BlockSpec((k, d), fixed, pipeline_mode=pl.Buffered(1)),
                            pl.BlockSpec((tm, d), row),
                            pl.BlockSpec((s, d), fixed),
                            pl.BlockSpec((1, d), fixed),
                            pl.BlockSpec((1, d), fixed)],
        out_specs=out_specs,
        out_shape=out_shape,
        compiler_params=_cparams("arbitrary", vmem=vmem),
    )(*h_ops, hs, w, resid, resid_s, nw_post.reshape(1, d), nw_next.reshape(1, d))
    if emit_next:
        return (res[0], res[2]), (res[1], res[3])
    return (res[0], res[1]), (None, None)


def _log_sigmoid(x):
    return jnp.minimum(x, 0.0) - jnp.log1p(jnp.exp(-jnp.abs(x)))


def _mlstm_kernel(q_ref, k_ref, v_ref, o_ref, g_ref, bg_ref, hn_ref, hg_ref, c_ref, n_ref, m_ref, cn_ref,
                  *, L, H, DK, DV):
    chunk = pl.program_id(1)

    @pl.when(chunk == 0)
    def _():
        cn_ref[...] = jnp.zeros_like(cn_ref)
        m_ref[...] = jnp.zeros_like(m_ref)

    gpre = g_ref[...] + bg_ref[...]
    lf = _log_sigmoid(gpre)
    rows = _iota((L, L), 0)
    cols = _iota((L, L), 1)
    causal = rows >= cols
    tril = jnp.where(causal, 1.0, 0.0).astype(BF16)
    lf_hi, lf_lo = _split_bf16(lf)
    bcum = _dot(tril, lf_hi) + _dot(tril, lf_lo)
    g_t = gpre.T
    b_t = bcum.T
    for h in range(H):
        f = FORGET_BIAS_COL + h
        a_col = bcum[:, f:f + 1]
        i_col = gpre[:, h:h + 1]
        b_row = b_t[f:f + 1, :]
        i_row = g_t[h:h + 1, :]
        m_prev = m_ref[0, h:h + 1, 0:1]
        log_d = jnp.where(causal, a_col - b_row + i_row, NEG)
        log_inter = a_col + m_prev
        m_t = jnp.maximum(log_inter, jnp.max(log_d, axis=-1, keepdims=True))
        d = jnp.exp(log_d - m_t)
        inter = jnp.exp(log_inter - m_t)
        kh = k_ref[:, h * DK:(h + 1) * DK].astype(F32) * (DK ** -0.5)
        v1 = jnp.concatenate([v_ref[:, h * DV:(h + 1) * DV], jnp.ones((L, LANES), BF16)], axis=1)
        qb = q_ref[:, h * DK:(h + 1) * DK]
        s = _dot_nt(qb, kh.astype(BF16)) * d
        cn_old = cn_ref[h]
        both = _dot(s.astype(BF16), v1) + inter * _dot(qb, cn_old.astype(BF16))
        num, den = both[:, :DV], both[:, DV:]
        scale = 1.0 / jnp.maximum(jnp.abs(den), jnp.exp(-m_t))
        hh = _rms(num * jnp.concatenate([scale] * (DV // LANES), axis=1))
        gate = _sigmoid(o_ref[:, h * DV:(h + 1) * DV])
        hg_ref[:, h * DV:(h + 1) * DV] = (hh * hn_ref[:, h * DV:(h + 1) * DV] * gate).astype(BF16)
        m_new = m_t[L - 1:L, :]
        w_col = jnp.exp(a_col[L - 1:L, :] - a_col + i_col - m_new)
        decay = jnp.exp(log_inter[L - 1:L, :] - m_new)
        cn_new = decay * cn_old + _dot_tn((kh * w_col).astype(BF16), v1)
        cn_ref[h] = cn_new
        m_ref[0, h:h + 1, :] = jnp.broadcast_to(m_new, (1, LANES))

        @pl.when(chunk == pl.num_programs(1) - 1)
        def _():
            c_ref[0, h] = cn_new[:, :DV]
            n_ref[0, h:h + 1, :] = cn_new[:, DV:].T[0:1, :]


def mlstm_prompt(qkv, o, gates, b_gates, head_norm, bsz, T):
    H = MLSTM_HEADS
    m, hdv = o.shape
    hdk = hdv // 2
    DK, DV = hdk // H, hdv // H
    L = math.gcd(T, MLSTM_CHUNK)
    nc = T // L
    row = lambda b, c: (b * nc + c, 0)
    state = lambda b, c: (b, 0, 0)
    return pl.pallas_call(
        functools.partial(_mlstm_kernel, L=L, H=H, DK=DK, DV=DV),
        grid=(bsz, nc),
        in_specs=[pl.BlockSpec((L, hdk), row),
                  pl.BlockSpec((L, hdk), lambda b, c: (b * nc + c, 1)),
                  pl.BlockSpec((L, hdv), lambda b, c: (b * nc + c, 1)),
                  pl.BlockSpec((L, hdv), row),
                  pl.BlockSpec((L, LANES), row),
                  pl.BlockSpec((1, LANES), lambda b, c: (0, 0)),
                  pl.BlockSpec((1, hdv), lambda b, c: (0, 0))],
        out_specs=[pl.BlockSpec((L, hdv), row),
                   pl.BlockSpec((1, H, DK, DV), lambda b, c: (b, 0, 0, 0)),
                   pl.BlockSpec((1, H, DK), state),
                   pl.BlockSpec((1, H, LANES), state)],
        out_shape=[jax.ShapeDtypeStruct((m, hdv), BF16),
                   jax.ShapeDtypeStruct((bsz, H, DK, DV), F32),
                   jax.ShapeDtypeStruct((bsz, H, DK), F32),
                   jax.ShapeDtypeStruct((bsz, H, LANES), F32)],
        scratch_shapes=[pltpu.VMEM((H, DK, DV + LANES), F32)],
        compiler_params=_cparams("arbitrary", "arbitrary"),
    )(qkv, qkv, qkv, o, gates, _pad_lanes(b_gates.reshape(1, -1)), head_norm.reshape(1, hdv))


def _pad_lanes(x, width=LANES):
    return jnp.pad(x, ((0, 0), (0, width - x.shape[-1])))


def _to_col(row, n):
    eye = _iota((n, n), 0) == _iota((n, n), 1)
    return jnp.sum(jnp.where(eye, jnp.broadcast_to(row, (n, n)), 0.0), axis=-1, keepdims=True)


def _mlstm_step_kernel(x_ref, g_ref, bg_ref, hn_ref, c0_ref, n0_ref, m0_ref, hg_ref, c_ref, n_ref, m_ref,
                       *, H, DK, DV):
    x = x_ref[0]
    gpre = g_ref[0] + bg_ref[...]
    lf = _log_sigmoid(gpre)
    hdk = H * DK
    hdv = H * DV
    for h in range(H):
        f = FORGET_BIAS_COL + h
        ig = gpre[:, h:h + 1]
        m_prev = m0_ref[0, h:h + 1, :]
        log_inter = lf[:, f:f + 1] + m_prev
        m_t = jnp.maximum(log_inter, ig)
        d = jnp.exp(ig - m_t)
        inter = jnp.exp(log_inter - m_t)
        q = x[:, h * DK:(h + 1) * DK]
        k = x[:, hdk + h * DK:hdk + (h + 1) * DK] * (DK ** -0.5)
        v = x[:, 2 * hdk + h * DV:2 * hdk + (h + 1) * DV]
        og = x[:, 2 * hdk + hdv + h * DV:2 * hdk + hdv + (h + 1) * DV]
        c_old = c0_ref[0, h]
        n_old = n0_ref[0, h:h + 1, :]
        s = jnp.sum(q * k, axis=-1, keepdims=True) * d
        q_col = _to_col(q, DK)
        k_col = _to_col(k, DK)
        num = s * v + inter * jnp.sum(q_col * c_old, axis=0, keepdims=True)
        den = s + inter * jnp.sum(q * n_old, axis=-1, keepdims=True)
        hh = _rms(num / jnp.maximum(jnp.abs(den), jnp.exp(-m_t)))
        hg_ref[0, :, h * DV:(h + 1) * DV] = (hh * hn_ref[:, h * DV:(h + 1) * DV] * _sigmoid(og)).astype(BF16)
        c_ref[0, h] = inter * c_old + d * (k_col * v)
        n_ref[0, h:h + 1, :] = inter * n_old + d * k
        m_ref[0, h:h + 1, :] = m_t


def mlstm_step(qkvo, gates, b_gates, head_norm, c0, n0, m0):
    bsz, H, DK, DV = c0.shape
    wid = qkvo.shape[1]
    hdv = H * DV
    one = lambda b: (b, 0, 0)
    hg, c1, n1, m1 = pl.pallas_call(
        functools.partial(_mlstm_step_kernel, H=H, DK=DK, DV=DV),
        grid=(bsz,),
        in_specs=[pl.BlockSpec((1, 1, wid), one),
                  pl.BlockSpec((1, 1, LANES), one),
                  pl.BlockSpec((1, LANES), lambda b: (0, 0)),
                  pl.BlockSpec((1, hdv), lambda b: (0, 0)),
                  pl.BlockSpec((1, H, DK, DV), lambda b: (b, 0, 0, 0)),
                  pl.BlockSpec((1, H, DK), one),
                  pl.BlockSpec((1, H, 1), one)],
        out_specs=[pl.BlockSpec((1, 1, hdv), one),
                   pl.BlockSpec((1, H, DK, DV), lambda b: (b, 0, 0, 0)),
                   pl.BlockSpec((1, H, DK), one),
                   pl.BlockSpec((1, H, 1), one)],
        out_shape=[jax.ShapeDtypeStruct((bsz, 1, hdv), BF16),
                   jax.ShapeDtypeStruct((bsz, H, DK, DV), F32),
                   jax.ShapeDtypeStruct((bsz, H, DK), F32),
                   jax.ShapeDtypeStruct((bsz, H, 1), F32)],
        compiler_params=_cparams("parallel"),
    )(qkvo.reshape(bsz, 1, wid), gates.reshape(bsz, 1, LANES), _pad_lanes(b_gates.reshape(1, -1)),
      head_norm.reshape(1, hdv), c0, n0, m0.reshape(bsz, H, 1))
    return hg.reshape(bsz, hdv), c1, n1, m1.reshape(bsz, H)


def _conv_step_kernel(bg_ref, z_ref, buf_ref, cw_ref, o_ref):
    y = cw_ref[CONV_W - 1:CONV_W, :] * z_ref[...]
    for j in range(CONV_W - 1):
        y = y + cw_ref[j:j + 1, :] * buf_ref[j]
    o_ref[...] = bg_ref[...] * y


def conv_gate_step(bg, z, buf, conv_w):
    bsz, d = z.shape
    full = lambda: (0, 0)
    return pl.pallas_call(
        _conv_step_kernel,
        in_specs=[pl.BlockSpec((bsz, d), full), pl.BlockSpec((bsz, d), full),
                  pl.BlockSpec((CONV_W - 1, bsz, d), lambda: (0, 0, 0)), pl.BlockSpec((CONV_W, d), full)],
        out_specs=pl.BlockSpec((bsz, d), full),
        out_shape=jax.ShapeDtypeStruct((bsz, d), F32),
    )(bg, z, buf, conv_w)


def _gelu(x):
    return 0.5 * x * (1.0 + jnp.tanh(math.sqrt(2.0 / math.pi) * (x + 0.044715 * x * x * x)))


def _cmp_finish(a, posw, b1, w2, b2):
    hid = a.shape[1] // 2
    a1 = a[:, hid:]
    a1_next = jnp.concatenate([a1[1:], a1[:1]], axis=0)
    pre = a[:, :hid] + a1_next + b1 + posw[0:1, :hid] + posw[1:2, hid:]
    return _dot(_gelu(pre).astype(BF16), w2) + b2


def _cmp_prompt_kernel(k_ref, v_ref, w1_ref, pos_ref, b1_ref, w2_ref, b2_ref, ck_ref, cv_ref, *, nch):
    for idx, (x_ref, o_ref) in enumerate(((k_ref, ck_ref), (v_ref, cv_ref))):
        w1 = w1_ref[idx]
        lhs = jnp.concatenate([x_ref[pl.ds(p, nch, stride=CMP_STRIDE), :] for p in range(CMP_STRIDE)], axis=1)
        a = _dot(lhs.astype(BF16), w1)
        posw = _dot(pos_ref[idx], w1)
        o_ref[0, 0] = _cmp_finish(a, posw, b1_ref[idx], w2_ref[idx], b2_ref[idx])


def _cmp_weights(cmp_pos, cmp_w1, cmp_b1, cmp_w2, cmp_b2):
    two, ratio, stride, hd, hid = cmp_w1.shape
    w1 = cmp_w1.transpose(0, 2, 3, 1, 4).reshape(two, stride * hd, ratio * hid).astype(BF16)
    pos = cmp_pos.reshape(two, ratio, stride * hd)
    pos = jnp.pad(pos, ((0, 0), (0, SUBLANES - ratio), (0, 0))).astype(BF16)
    return w1, pos, cmp_b1.reshape(two, 1, hid), cmp_w2.astype(BF16), cmp_b2.reshape(two, 1, hd)


def cmp_prompt(kc, vc, bsz, T, cw):
    w1, pos, b1, w2, b2 = cw
    nch = T // CMP_STRIDE
    G = NSA_KV
    out = jax.ShapeDtypeStruct((bsz, G, nch, NSA_HD), F32)
    ospec = pl.BlockSpec((1, 1, nch, NSA_HD), lambda b, g: (b, g, 0, 0))
    whole = lambda a: pl.BlockSpec(a.shape, lambda b, g: (0,) * a.ndim)
    seq = pl.BlockSpec((T, NSA_HD), lambda b, g: (b, g))
    return pl.pallas_call(
        functools.partial(_cmp_prompt_kernel, nch=nch),
        grid=(bsz, G),
        in_specs=[seq, seq, whole(w1), whole(pos), whole(b1), whole(w2), whole(b2)],
        out_specs=[ospec, ospec],
        out_shape=[out, out],
        compiler_params=_cparams("parallel", "parallel"),
    )(kc, vc, w1, pos, b1, w2, b2)


def _overlap_t(nbs_pad, nbc_pad):
    j = _iota((nbs_pad, nbc_pad), 0)
    n = _iota((nbs_pad, nbc_pad), 1)
    lo = jnp.maximum(n * CMP_STRIDE, j * SEL_BLOCK)
    hi = jnp.minimum(n * CMP_STRIDE + CMP_BLOCK, j * SEL_BLOCK + SEL_BLOCK)
    return (jnp.maximum(hi - lo, 0) // CMP_STRIDE).astype(F32)


def _rank_select(score, n_sel, axis):
    n = score.shape[axis]
    idx = _iota(score.shape, axis)
    cnt = jnp.zeros(score.shape, F32)
    for jp in range(n):
        other = lax.slice_in_dim(score, jp, jp + 1, axis=axis)
        tie = jnp.where(idx > jp, 1.0, 0.0)
        cnt = cnt + jnp.where(other > score, 1.0, jnp.where(other == score, tie, 0.0))
    return jnp.where(cnt < n_sel, 1.0, 0.0)


def _attn_first(state, s, vt_b):
    m_ref, l_ref, acc_ref = state
    m = jnp.max(s, axis=0, keepdims=True)
    p = jnp.exp2(s - m)
    m_ref[...] = m
    l_ref[...] = jnp.sum(p, axis=0, keepdims=True)
    acc_ref[...] = _dot(vt_b, p.astype(BF16))


def _attn_more(state, s, vt_b):
    m_ref, l_ref, acc_ref = state
    m_old = m_ref[...]
    m_new = jnp.maximum(m_old, jnp.max(s, axis=0, keepdims=True))
    alpha = jnp.exp2(m_old - m_new)
    p = jnp.exp2(s - m_new)
    m_ref[...] = m_new
    l_ref[...] = alpha * l_ref[...] + jnp.sum(p, axis=0, keepdims=True)
    acc_ref[...] = alpha * acc_ref[...] + _dot(vt_b, p.astype(BF16))


def _attn_empty(state):
    m_ref, l_ref, acc_ref = state
    m_ref[...] = jnp.full(m_ref.shape, NEG, F32)
    l_ref[...] = jnp.zeros_like(l_ref)
    acc_ref[...] = jnp.zeros_like(acc_ref)


def _attn_merged(state_a, state_b):
    (ma_ref, la_ref, acca_ref), (mb_ref, lb_ref, accb_ref) = state_a, state_b
    m = jnp.maximum(ma_ref[...], mb_ref[...])
    wa = jnp.exp2(ma_ref[...] - m)
    wb = jnp.exp2(mb_ref[...] - m)
    return (wa * acca_ref[...] + wb * accb_ref[...]) / (wa * la_ref[...] + wb * lb_ref[...])


def _nsa_attn_kernel(q_ref, ck_ref, cv_ref, ks_ref, vs_ref, kw_ref, vw_ref, g_ref, bg_ref, o_ref,
                     ksb_ref, vst_ref, kwb_ref, vwt_ref, *states, tq, hpg, nbc, nbs, nbs_pad):
    qi = pl.program_id(2)
    t0 = qi * tq
    cols = hpg * tq
    tk = tq
    n_kt = ks_ref.shape[0] // tk
    nbc_pad = ck_ref.shape[2]

    @pl.when(qi == 0)
    def _():
        ksb_ref[...] = ks_ref[...].astype(BF16)
        kwb_ref[...] = kw_ref[...].astype(BF16)
        for kt in range(n_kt):
            vst_ref[kt] = vs_ref[kt * tk:(kt + 1) * tk, :].T.astype(BF16)
            vwt_ref[kt] = vw_ref[kt * tk:(kt + 1) * tk, :].T.astype(BF16)

    qb = (jnp.concatenate([q_ref[:, h * NSA_HD:(h + 1) * NSA_HD] for h in range(hpg)], axis=0)
          * (NSA_HD ** -0.5 * math.log2(math.e))).astype(BF16)

    n = _iota((nbc_pad, cols), 0)
    t_col = t0 + jnp.concatenate([_iota((nbc_pad, tq), 1)] * hpg, axis=1)
    cmask = jnp.where(n < nbc, jnp.where(n * CMP_STRIDE + (CMP_BLOCK - 1) <= t_col, 1.0, 0.0), 0.0)
    sc = jnp.where(cmask > 0.5, _dot_nt(ck_ref[0, 0].astype(BF16), qb), NEG)
    pe = jnp.exp2(sc - jnp.max(sc, axis=0, keepdims=True))
    p_cmp = pe / jnp.sum(pe, axis=0, keepdims=True) * cmask
    o_cmp = _dot(cv_ref[0, 0].T.astype(BF16), p_cmp.astype(BF16))

    p_sum = p_cmp[:, 0:tq]
    for h in range(1, hpg):
        p_sum = p_sum + p_cmp[:, h * tq:(h + 1) * tq]
    ov_t = _overlap_t(nbs_pad, nbc_pad).astype(BF16)
    p_hi, p_lo = _split_bf16(p_sum)
    imp_t = _dot(ov_t, p_hi) + _dot(ov_t, p_lo)
    j = _iota((nbs_pad, tq), 0)
    t = t0 + _iota((nbs_pad, tq), 1)
    t_blk = t // SEL_BLOCK
    forced = jnp.where(j == 0, 1.0, jnp.where(j == t_blk, 1.0, jnp.where(j == t_blk - 1, 1.0, 0.0)))
    visible = jnp.where(j < nbs, jnp.where(j * SEL_BLOCK <= t, 1.0, 0.0), 0.0)
    score = jnp.where(visible > 0.5, jnp.where(forced > 0.5, BIG, imp_t), NEG)
    sel_t = _rank_select(score, min(N_SELECT, nbs), axis=0).astype(BF16)

    k_ofs = _iota((tk, tq), 0)
    q_ofs = _iota((tk, tq), 1)
    causal = k_ofs <= q_ofs
    per_head = lambda bias: jnp.concatenate([bias] * hpg, axis=1)

    def scores(kb_ref, kt):
        k_b = kb_ref[pl.ds(pl.multiple_of(kt * tk, tk), tk), :]
        return _dot_nt(k_b, qb)

    def slc_bias(kt):
        blk = (kt * tk + _iota((tk, nbs_pad), 0)) // SEL_BLOCK
        expand = jnp.where(blk == _iota((tk, nbs_pad), 1), 1.0, 0.0).astype(BF16)
        return (_dot(expand, sel_t) - 1.0) * BIG

    slc_a, slc_b, win_a, win_b = states[0:3], states[3:6], states[6:9], states[9:12]

    def slc_tile(kt):
        return scores(ksb_ref, kt) + per_head(slc_bias(kt)), vst_ref[kt]

    def win_tile(back):
        s = scores(kwb_ref, qi - back)
        if back == n_back:
            s = s + per_head(jnp.where(k_ofs >= q_ofs, 0.0, NEG))
        return s, vwt_ref[qi - back]

    n_back = WINDOW // tk
    _attn_first(slc_a, scores(ksb_ref, qi) + per_head(jnp.where(causal, slc_bias(qi), NEG)), vst_ref[qi])
    _attn_first(win_a, scores(kwb_ref, qi) + per_head(jnp.where(causal, 0.0, NEG)), vwt_ref[qi])
    for st in (slc_b, win_b):
        _attn_empty(st)

    for back in range(1, n_back + 1, 2):
        both = back + 1 <= n_back

        @pl.when(qi >= back + 1 if both else qi >= back)
        def _():
            _attn_more(win_b, *win_tile(back))
            if both:
                _attn_more(win_a, *win_tile(back + 1))

        if both:
            @pl.when(qi == back)
            def _():
                _attn_more(win_b, *win_tile(back))

    def slc_pair(pair, _):
        _attn_more(slc_a, *slc_tile(2 * pair))
        _attn_more(slc_b, *slc_tile(2 * pair + 1))
        return 0

    lax.fori_loop(0, lax.shift_right_logical(qi, 1), slc_pair, 0)

    @pl.when(lax.bitwise_and(qi, 1) == 1)
    def _():
        _attn_more(slc_a, *slc_tile(qi - 1))

    o_slc = _attn_merged(slc_a, slc_b)
    o_win = _attn_merged(win_a, win_b)
    gate_t = _sigmoid(g_ref[...] + bg_ref[...]).T
    for h in range(hpg):
        sl = slice(h * tq, (h + 1) * tq)
        o_t = (gate_t[3 * h:3 * h + 1, :] * o_cmp[:, sl] + gate_t[3 * h + 1:3 * h + 2, :] * o_slc[:, sl]
               + gate_t[3 * h + 2:3 * h + 3, :] * o_win[:, sl])
        o_ref[:, h * NSA_HD:(h + 1) * NSA_HD] = o_t.T.astype(BF16)


def nsa_attn_prompt(q, ck, cv, ks, vs, kw, vw, gates, b_gate, bsz, T):
    G, HPG = NSA_KV, NSA_HPG
    tq = math.gcd(T, ATT_TILE)
    assert WINDOW % tq == 0
    nq = T // tq
    nch = T // CMP_STRIDE
    nbc = nch - CMP_BLOCK // CMP_STRIDE + 1
    nbs = -(-T // SEL_BLOCK)
    nbs_pad = -(-nbs // SUBLANES) * SUBLANES
    cols = HPG * tq
    kv_scratch = [pltpu.VMEM((T, NSA_HD), BF16), pltpu.VMEM((nq, NSA_HD, tq), BF16)] * 2
    stats = [pltpu.VMEM((1, cols), F32), pltpu.VMEM((1, cols), F32), pltpu.VMEM((NSA_HD, cols), F32)] * 4
    seq = pl.BlockSpec((T, NSA_HD), lambda b, g, i: (b, g))
    cspec = pl.BlockSpec((1, 1, nch, NSA_HD), lambda b, g, i: (b, g, 0, 0))
    qspec = pl.BlockSpec((tq, HPG * NSA_HD), lambda b, g, i: (b * nq + i, g))
    return pl.pallas_call(
        functools.partial(_nsa_attn_kernel, tq=tq, hpg=HPG, nbc=nbc, nbs=nbs, nbs_pad=nbs_pad),
        grid=(bsz, G, nq),
        in_specs=[qspec, cspec, cspec, seq, seq, seq, seq,
                  pl.BlockSpec((tq, LANES), lambda b, g, i: (b * nq + i, g)),
                  pl.BlockSpec((1, LANES), lambda b, g, i: (0, g))],
        out_specs=qspec,
        out_shape=jax.ShapeDtypeStruct((bsz * T, G * HPG * NSA_HD), BF16),
        scratch_shapes=kv_scratch + stats,
        compiler_params=_cparams("arbitrary", "arbitrary", "arbitrary"),
    )(q, ck, cv, ks, vs, kw, vw, gates, b_gate)


CHUNK_ROWS = CMP_STRIDE * NSA_KV
CHUNK_PITCH = CHUNK_ROWS + SUBLANES


def _dec_cmp_kernel(tbl_ref, *refs, P, nbc, nbs, nbs_pad, p0):
    k_pages, v_pages = refs[:P], refs[P:2 * P]
    w1_ref, q_ref, pos_ref, b1_ref, w2_ref, b2_ref, o_ref, sel_ref, pad_ref, ak_ref, av_ref = refs[2 * P:]
    cpp = LANES // CMP_STRIDE
    step = pl.program_id(1)
    for idx, (pages, a_ref) in enumerate(((k_pages, ak_ref), (v_pages, av_ref))):
        for i, pg in enumerate(pages):
            for c in range(cpp):
                pad_ref[idx * P + i, c * CHUNK_PITCH:c * CHUNK_PITCH + CHUNK_ROWS, :] = (
                    pg[0, c * CHUNK_ROWS:(c + 1) * CHUNK_ROWS, :])
        blocks = []
        for g in range(NSA_KV):
            for i in range(P):
                blocks.append(jnp.concatenate(
                    [pad_ref[idx * P + i, pl.ds(p * NSA_KV + g, cpp, stride=CHUNK_PITCH), :]
                     for p in range(CMP_STRIDE)], axis=1))
        a = _dot(jnp.concatenate(blocks, axis=0).astype(BF16), w1_ref[idx])
        per_g = P * cpp
        for g in range(NSA_KV):
            a_ref[g, pl.ds(pl.multiple_of(step * per_g, per_g), per_g), :] = a[g * per_g:(g + 1) * per_g]

    @pl.when(step == pl.num_programs(1) - 1)
    def _():
        _dec_cmp_finish(ak_ref, av_ref, q_ref, pos_ref, w1_ref, b1_ref, w2_ref, b2_ref, o_ref, sel_ref,
                        nbc=nbc, nbs=nbs, nbs_pad=nbs_pad, p0=p0)


def _dec_cmp_finish(ak_ref, av_ref, q_ref, pos_ref, w1_ref, b1_ref, w2_ref, b2_ref, o_ref, sel_ref,
                    *, nbc, nbs, nbs_pad, p0):
    nch = ak_ref.shape[1]
    for g in range(NSA_KV):
        cks = []
        for idx, a_ref in enumerate((ak_ref, av_ref)):
            posw = _dot(pos_ref[idx], w1_ref[idx])
            cks.append(_cmp_finish(a_ref[g], posw, b1_ref[idx], w2_ref[idx], b2_ref[idx]))
        ck, cv = cks
        qb = (q_ref[0, g] * (NSA_HD ** -0.5)).astype(BF16)
        n = _iota((SUBLANES, nch), 1)
        cmask = jnp.where(n < nbc, jnp.where(n * CMP_STRIDE + (CMP_BLOCK - 1) <= p0, 1.0, 0.0), 0.0)
        sc = jnp.where(cmask > 0.5, _dot_nt(qb, ck.astype(BF16)), NEG)
        pe = jnp.exp(sc - jnp.max(sc, axis=-1, keepdims=True))
        p_cmp = pe / jnp.sum(pe, axis=-1, keepdims=True) * cmask
        o_ref[0, g] = _dot(p_cmp.astype(BF16), cv.astype(BF16))

        head = jnp.where(_iota((SUBLANES, nch), 0) < NSA_HPG, p_cmp, 0.0)
        p_sum = jnp.broadcast_to(jnp.sum(head, axis=0, keepdims=True), (SUBLANES, nch))
        ov_t = _overlap_t(nbs_pad, nch).astype(BF16)
        p_hi, p_lo = _split_bf16(p_sum)
        imp = _dot_nt(p_hi, ov_t) + _dot_nt(p_lo, ov_t)
        j = _iota((SUBLANES, nbs_pad), 1)
        t_blk = p0 // SEL_BLOCK
        forced = jnp.where(j == 0, 1.0, jnp.where(j == t_blk, 1.0, jnp.where(j == t_blk - 1, 1.0, 0.0)))
        visible = jnp.where(j < nbs, jnp.where(j * SEL_BLOCK <= p0, 1.0, 0.0), 0.0)
        score = jnp.where(visible > 0.5, jnp.where(forced > 0.5, BIG, imp), NEG)
        s_row = jnp.broadcast_to(score[0:1], (nbs_pad, nbs_pad))
        s_col = jnp.concatenate([jnp.broadcast_to(score[0:1], (LANES, nbs_pad)).T] * (nbs_pad // LANES), axis=1)
        jr = _iota((nbs_pad, nbs_pad), 0)
        jc = _iota((nbs_pad, nbs_pad), 1)
        tie = jnp.where(jc < jr, 1.0, 0.0)
        beats = jnp.where(s_row > s_col, 1.0, jnp.where(s_row == s_col, tie, 0.0))
        rank = jnp.sum(beats, axis=-1, keepdims=True)
        slot = _iota((nbs_pad, LANES), 1).astype(F32)
        blk = _iota((nbs_pad, LANES), 0).astype(F32)
        picked = jnp.sum(jnp.where(rank == slot, blk, 0.0), axis=0, keepdims=True)
        sel_ref[0, g:g + 1, :] = picked.astype(jnp.int32)


def dec_cmp(pool_k, pool_v, table, q, cw, p0):
    w1, pos, b1, w2, b2 = cw
    n_pool, page, G, hd = pool_k.shape
    assert page == LANES and G == NSA_KV and hd == NSA_HD
    bsz, n_pages = table.shape
    assert p0 == n_pages * page
    P = math.gcd(n_pages, PAGES_PER_STEP)
    cpp = page // CMP_STRIDE
    nch = n_pages * cpp
    hid2 = w1.shape[2]
    nbc = nch - CMP_BLOCK // CMP_STRIDE + 1
    nbs = -(-(p0 + 1) // SEL_BLOCK)
    nbs_pad = -(-nbs // LANES) * LANES
    pk = pool_k.reshape(n_pool, page * G, hd)
    pv = pool_v.reshape(n_pool, page * G, hd)
    pspec = lambda i: pl.BlockSpec((1, page * G, hd),
                                   functools.partial(lambda b, s, tbl, i: (tbl[b * n_pages + s * P + i], 0, 0), i=i))
    whole = lambda a: pl.BlockSpec(a.shape, lambda b, s, tbl: (0,) * a.ndim)
    per_seq = lambda shape: pl.BlockSpec((1,) + shape, lambda b, s, tbl: (b,) + (0,) * len(shape))
    grid_spec = pltpu.PrefetchScalarGridSpec(
        num_scalar_prefetch=1,
        grid=(bsz, n_pages // P),
        in_specs=[pspec(i) for i in range(P)] * 2
        + [whole(w1), per_seq((G, SUBLANES, hd)), whole(pos), whole(b1), whole(w2), whole(b2)],
        out_specs=[per_seq((G, SUBLANES, hd)), per_seq((G, LANES))],
        scratch_shapes=[pltpu.VMEM((2 * P, cpp * CHUNK_PITCH, hd), F32),
                        pltpu.VMEM((G, nch, hid2), F32), pltpu.VMEM((G, nch, hid2), F32)],
    )
    return pl.pallas_call(
        functools.partial(_dec_cmp_kernel, P=P, nbc=nbc, nbs=nbs, nbs_pad=nbs_pad, p0=p0),
        grid_spec=grid_spec,
        out_shape=[jax.ShapeDtypeStruct((bsz, G, SUBLANES, hd), F32),
                   jax.ShapeDtypeStruct((bsz, G, LANES), jnp.int32)],
        compiler_params=_cparams("arbitrary", "arbitrary"),
    )(table.reshape(-1), *([pk] * P), *([pv] * P), w1, q, pos, b1, w2, b2)


def _softmax_with_new_key(qb, k_b, v_b, bias, s_new, v_new):
    s = _dot_nt(qb, k_b)
    if bias is not None:
        s = s + bias
    mx = jnp.maximum(jnp.max(s, axis=-1, keepdims=True), s_new)
    p = jnp.exp(s - mx)
    p_new = jnp.exp(s_new - mx)
    return (_dot(p.astype(BF16), v_b) + p_new * v_new) / (jnp.sum(p, axis=-1, keepdims=True) + p_new)


def _dec_attn_kernel(sel_ref, tbl_ref, *refs, n_cache_blocks, n_sel):
    ks_refs, vs_refs = refs[:n_sel], refs[n_sel:2 * n_sel]
    q_ref, new_ref, kw_ref, vw_ref, oc_ref, g_ref, bg_ref, o_ref = refs[2 * n_sel:]
    b, g = pl.program_id(0), pl.program_id(1)
    q = q_ref[0, 0] * (NSA_HD ** -0.5)
    qb = q.astype(BF16)
    new = new_ref[0, 0]

    mine = lambda r, n: r[pl.ds(g, n, stride=NSA_KV), :]
    k_all = jnp.concatenate([mine(r.at[0, 0], SEL_BLOCK) for r in ks_refs], axis=0).astype(BF16)
    v_all = jnp.concatenate([mine(r.at[0, 0], SEL_BLOCK) for r in vs_refs], axis=0).astype(BF16)
    slot = _iota((SUBLANES, n_sel * SEL_BLOCK), 1) // SEL_BLOCK
    bias = jnp.zeros((SUBLANES, n_sel * SEL_BLOCK), F32)
    for s in range(n_sel):
        blk = sel_ref[(b * NSA_KV + g) * n_sel + s]
        bias = jnp.where(slot == s, jnp.where(blk < n_cache_blocks, 0.0, NEG), bias)
    o_slc = _softmax_with_new_key(qb, k_all, v_all, bias, jnp.sum(q * new[0:1], axis=-1, keepdims=True), new[1:2])

    n_win = kw_ref.shape[1] // NSA_KV
    o_win = _softmax_with_new_key(qb, mine(kw_ref.at[0], n_win).astype(BF16), mine(vw_ref.at[0], n_win).astype(BF16),
                                  None,
                                  jnp.sum(q * new[2:3], axis=-1, keepdims=True), new[3:4])

    gate = jnp.broadcast_to(_sigmoid(g_ref[0, 0] + bg_ref[...]), (SUBLANES, LANES))
    head = _iota((SUBLANES, LANES), 0)
    lane = _iota((SUBLANES, LANES), 1)
    o = jnp.zeros((SUBLANES, NSA_HD), F32)
    for br, ob in enumerate((oc_ref[0, 0], o_slc, o_win)):
        o = o + jnp.sum(jnp.where(lane == 3 * head + br, gate, 0.0), axis=-1, keepdims=True) * ob
    o_ref[0, 0] = o


def dec_attn(sel, table, pool_ks, pool_vs, q, new, kw_past, vw_past, o_cmp, gates, b_gate):
    n_pool, page, G, hd = pool_ks.shape
    bsz, n_pages = table.shape
    halves = page // SEL_BLOCK
    wb = kw_past.shape[1]
    assert wb <= WINDOW
    n_sel = sel.shape[-1]
    pk = pool_ks.reshape(n_pool, halves, SEL_BLOCK * G, hd)
    pv = pool_vs.reshape(n_pool, halves, SEL_BLOCK * G, hd)
    n_cache_blocks = n_pages * halves

    def page_map(b, g, sel_ref, tbl_ref, s):
        blk = jnp.minimum(sel_ref[(b * G + g) * n_sel + s], n_cache_blocks - 1)
        return (tbl_ref[b * n_pages + blk // halves], blk % halves, 0, 0)

    bg_map = lambda b, g, sel_ref, tbl_ref: (b, g, 0, 0)
    small = pl.BlockSpec((1, 1, SUBLANES, hd), bg_map)
    wspec = pl.BlockSpec((1, wb * G, hd), lambda b, g, sel_ref, tbl_ref: (b, 0, 0))
    blocks = [pl.BlockSpec((1, 1, SEL_BLOCK * G, hd), functools.partial(page_map, s=s)) for s in range(n_sel)]
    grid_spec = pltpu.PrefetchScalarGridSpec(
        num_scalar_prefetch=2,
        grid=(bsz, G),
        in_specs=blocks + blocks + [small, small, wspec, wspec, small,
                                    pl.BlockSpec((1, 1, 1, LANES), bg_map),
                                    pl.BlockSpec((1, LANES), lambda b, g, sel_ref, tbl_ref: (0, g))],
        out_specs=small,
    )
    return pl.pallas_call(
        functools.partial(_dec_attn_kernel, n_cache_blocks=n_cache_blocks, n_sel=n_sel),
        grid_spec=grid_spec,
        out_shape=jax.ShapeDtypeStruct((bsz, G, SUBLANES, hd), F32),
        compiler_params=_cparams("parallel", "parallel"),
    )(sel.reshape(-1), table.reshape(-1), *([pk] * n_sel), *([pv] * n_sel), q, new,
      kw_past.reshape(bsz, wb * G, hd), vw_past.reshape(bsz, wb * G, hd), o_cmp,
      gates.reshape(bsz, G, 1, LANES), b_gate)


def _gate_weights(wt, layer, row0, n):
    rows = lax.slice(wt, (layer, row0, 0), (layer + 1, row0 + n, wt.shape[2]))
    return jnp.pad(rows, ((0, 0), (0, LANES - n), (0, 0))), 0


def _nsa_gate_weights(wt, layer, b_gate, row0):
    per = 3 * NSA_HPG
    k = wt.shape[2]
    rows = lax.slice(wt, (layer, row0, 0), (layer + 1, row0 + NSA_KV * per, k)).reshape(NSA_KV, per, k)
    wg = jnp.pad(rows, ((0, 0), (0, LANES - per), (0, 0))).reshape(1, NSA_KV * LANES, k)
    bg = jnp.pad(b_gate.reshape(NSA_KV, per), ((0, 0), (0, LANES - per))).reshape(1, NSA_KV * LANES)
    return (wg, 0), bg


def kernel(x_prompt, x_sample, state_a_C, state_a_n, state_a_m, state_b_conv, cache_c_k_cmp, cache_c_v_cmp, cache_c_k_slc, cache_c_v_slc, cache_c_k_win, cache_c_v_win, page_table, norm_w, ffn_w_in, ffn_w_out, a_w_in, a_b_gates, a_head_norm, a_w_out, b_w_in, b_conv_w, b_w_out, c_w_in, c_b_gate, c_cmp_pos, c_cmp_w1, c_cmp_b1, c_cmp_w2, c_cmp_b2, c_w_out):
    bp, T, D = x_prompt.shape
    bs = x_sample.shape[0]
    assert x_sample.shape[1] == 1
    depth = norm_w.shape[0]
    G, hd = NSA_KV, NSA_HD
    past_len = page_table.shape[1] * cache_c_k_cmp.shape[2]
    xp = x_prompt.reshape(bp * T, D)
    xs = x_sample.reshape(bs, D)
    hp_in = norm_cast(xp, norm_w[0, 0])
    hs_in = norm_cast(xs, norm_w[0, 0])
    a_wt = jnp.swapaxes(a_w_in, 1, 2)
    c_wt = jnp.swapaxes(c_w_in, 1, 2)
    w_outs = {0: a_w_out, 1: b_w_out, 2: c_w_out}
    pa, sa, pb, sb, pc, sc = [], [], [], [], [], []
    for i in range(depth):
        kind, j = i % 3, i // 3
        nw = norm_w[i]
        w_out_f32 = (w_outs[kind], j)
        conv = None
        if kind == 0:
            hdv = a_head_norm.shape[1]
            n_main = 3 * hdv
            wg = _gate_weights(a_wt, j, n_main, 2 * MLSTM_HEADS)
            qkv_p, qkv_s, w_out = proj_plain(hp_in, hs_in, (a_wt, j), 0, 2 * hdv, 1024, transposed=True,
                                             cast=w_out_f32, prompt_dtype=BF16)
            o_p, o_s = proj_plain(hp_in, hs_in, (a_wt, j), 2 * hdv, hdv, 1024, transposed=True)
            qkvo_s = jnp.concatenate([qkv_s, o_s], axis=1)
            g_p, g_s = proj_plain(hp_in, hs_in, wg, 0, LANES, LANES, transposed=True)
            hp, c_p, n_p, m_p = mlstm_prompt(qkv_p, o_p, g_p, a_b_gates[j], a_head_norm[j], bp, T)
            hs, c_s, n_s, m_s = mlstm_step(qkvo_s, g_s, a_b_gates[j], a_head_norm[j],
                                           state_a_C[j], state_a_n[j], state_a_m[j])
            pa.append((c_p, n_p, m_p[:, :, 0]))
            sa.append((c_s, n_s, m_s))
        elif kind == 1:
            (bg_p, z_p), (bg_s, z_s), w_out = proj_conv_in(hp_in, hs_in, (b_w_in, j), D, w_out_f32)
            hp, conv = None, (bg_p, z_p, b_conv_w[j], T)
            buf = state_b_conv[j]
            hs = conv_gate_step(bg_s, z_s, buf.transpose(1, 0, 2), b_conv_w[j]).astype(BF16)
            pb.append(z_p.reshape(bp, T, D)[:, T - (CONV_W - 1):])
            sb.append(jnp.concatenate([buf, z_s[:, None]], axis=1)[:, -(CONV_W - 1):])
        else:
            nq_cols = NSA_HPG * G * hd
            nheads = G * NSA_HPG
            wg, bgate = _nsa_gate_weights(c_wt, j, c_b_gate[j], nq_cols + 6 * G * hd)
            cw = _cmp_weights(c_cmp_pos[j], c_cmp_w1[j], c_cmp_b1[j], c_cmp_w2[j], c_cmp_b2[j])
            q_p, q_s, w_out = proj_plain(hp_in, hs_in, (c_wt, j), 0, nq_cols, 1024, transposed=True, cast=w_out_f32)
            kv_p, kv_rows_p, kv_s = [], [], []
            for half in range(2):
                (tok_p, rows_p), (tok_s, _) = proj_groups(hp_in, hs_in, (c_wt, j), nq_cols + half * 3 * G * hd,
                                                         3, G * hd, transposed=True)
                kv_p, kv_rows_p, kv_s = kv_p + list(tok_p), kv_rows_p + list(rows_p), kv_s + list(tok_s)
            gt_p, gt_s = proj_plain(hp_in, hs_in, wg, 0, G * LANES, G * LANES, transposed=True)
            ck, cv = cmp_prompt(kv_p[0], kv_p[1], bp, T, cw)
            hp = nsa_attn_prompt(q_p, ck, cv, kv_p[2], kv_p[3], kv_p[4], kv_p[5], gt_p, bgate, bp, T)
            kv_rows_p = [a.reshape(bp, T, G, hd) for a in kv_rows_p]
            keep = min(WINDOW, T)
            pc.append(tuple(kv_rows_p[:4]) + tuple(a[:, T - keep:] for a in kv_rows_p[4:]))
            q_s = jnp.pad(q_s.reshape(bs, G, NSA_HPG, hd), ((0, 0), (0, 0), (0, SUBLANES - NSA_HPG), (0, 0)))
            kv_s = jnp.stack([a.reshape(bs, G, hd) for a in kv_s], axis=1)
            new = jnp.pad(kv_s[:, 2:6].transpose(0, 2, 1, 3), ((0, 0), (0, 0), (0, SUBLANES - 4), (0, 0)))
            o_cmp, ranked = dec_cmp(cache_c_k_cmp[j], cache_c_v_cmp[j], page_table, q_s, cw, past_len)
            n_sel = min(N_SELECT, -(-(past_len + 1) // SEL_BLOCK))
            o_s = dec_attn(ranked[:, :, :n_sel], page_table, cache_c_k_slc[j], cache_c_v_slc[j], q_s, new,
                           cache_c_k_win[j], cache_c_v_win[j], o_cmp, gt_s, bgate)
            hs = o_s[:, :, :NSA_HPG].reshape(bs, nheads * hd).astype(BF16)
            keep_s = min(WINDOW, cache_c_k_win.shape[2] + 1)
            win = [jnp.concatenate([c[j], kv_s[:, a][:, None]], axis=1)[:, -keep_s:]
                   for c, a in ((cache_c_k_win, 4), (cache_c_v_win, 5))]
            sc.append(tuple(kv_s[:, a][:, None] for a in range(4)) + tuple(win))
        last = i == depth - 1
        nw_next = norm_w[i + 1, 0] if not last else nw[0]
        (xp, xs), (hp_mid, hs_mid) = proj_out(hp, hs, w_out, xp, xs, nw[1], nw[2], conv=conv)
        act_p, act_s, f_out = proj_swiglu(hp_mid, hs_mid, (ffn_w_in, i), ffn_w_out.shape[1], (ffn_w_out, i))
        (xp, xs), (hp_in, hs_in) = proj_out(act_p, act_s, f_out, xp, xs, nw[3], nw_next, not last)

    stack = lambda items: [jnp.stack(a) for a in zip(*items)]
    p_a, s_a = stack(pa), stack(sa)
    p_c, s_c = stack(pc), stack(sc)
    return (xp.reshape(bp, T, D), xs.reshape(bs, 1, D), *p_a, *s_a, jnp.stack(pb), jnp.stack(sb), *p_c, *s_c)
```

```python
import functools
import math

import jax
import jax.numpy as jnp
from jax import lax
from jax.experimental import pallas as pl
from jax.experimental.pallas import tpu as pltpu

F32 = jnp.float32
BF16 = jnp.bfloat16

EPS = 1e-6
NEG = -1e30
BIG = 1e30

LANES = 128
SUBLANES = 8
VMEM_LIMIT = 56 * 1024 * 1024

MLSTM_HEADS = 8
MLSTM_CHUNK = 512
FORGET_BIAS_COL = MLSTM_HEADS
CONV_W = 3
NSA_HD = 128
NSA_KV = 4
NSA_HPG = 4
CMP_BLOCK = 32
CMP_STRIDE = 16
SEL_BLOCK = 64
N_SELECT = 16
WINDOW = 512
ATT_TILE = 256
PAGES_PER_STEP = 16
OUT_SUB_ROWS = 256
OUT_VMEM_SLACK = 6 * 1024 * 1024


def _cparams(*sem, vmem=VMEM_LIMIT):
    return pltpu.CompilerParams(dimension_semantics=sem, vmem_limit_bytes=vmem)


def _dot(a, b):
    return jnp.dot(a, b, preferred_element_type=F32)


def _dot_nt(a, b):
    return lax.dot_general(a, b, (((1,), (1,)), ((), ())), preferred_element_type=F32)


def _dot_tn(a, b):
    return lax.dot_general(a, b, (((0,), (0,)), ((), ())), preferred_element_type=F32)


def _split_bf16(x):
    hi = x.astype(BF16)
    lo = (x - hi.astype(F32)).astype(BF16)
    return hi, lo


def _iota(shape, dim):
    return lax.broadcasted_iota(jnp.int32, shape, dim)


def _sigmoid(x):
    return 1.0 / (1.0 + jnp.exp(-x))


def _rms(x):
    return x * lax.rsqrt(jnp.mean(x * x, axis=-1, keepdims=True) + EPS)


def _row_tile(m, want):
    t = min(m, want)
    assert m % t == 0, (m, t)
    return t


def _norm_kernel(x_ref, w_ref, o_ref):
    o_ref[...] = (_rms(x_ref[...]) * w_ref[...]).astype(BF16)


def norm_cast(x, w):
    m, d = x.shape
    tm = _row_tile(m, 512)
    return pl.pallas_call(
        _norm_kernel,
        grid=(m // tm,),
        in_specs=[pl.BlockSpec((tm, d), lambda i: (i, 0)), pl.BlockSpec((1, d), lambda i: (0, 0))],
        out_specs=pl.BlockSpec((tm, d), lambda i: (i, 0)),
        out_shape=jax.ShapeDtypeStruct((m, d), BF16),
        compiler_params=_cparams("parallel"),
    )(x, w.reshape(1, d))


def _proj_kernel(x_ref, xs_ref, *refs, n_groups, n_out, epilogue, transposed, with_cast):
    w_refs = refs[:n_groups]
    refs = refs[n_groups:]
    if with_cast:
        cast_in_ref, refs = refs[0], refs[1:]
        refs[2 * n_out][...] = cast_in_ref[...].astype(BF16)
    out_refs = refs[:n_out]
    sample_out_refs = refs[n_out:2 * n_out]
    wb_ref = refs[-1]
    mm = _dot_nt if transposed else _dot

    @pl.when(pl.program_id(1) == 0)
    def _():
        for g in range(n_groups):
            wb_ref[g] = w_refs[g][...].astype(BF16)
        xs = xs_ref[...]
        epilogue([mm(xs, wb_ref[g]) for g in range(n_groups)], sample_out_refs)

    x = x_ref[...]
    epilogue([mm(x, wb_ref[g]) for g in range(n_groups)], out_refs)


def _proj(x, xs, w, *, col0, tn, n_tiles, group_stride, n_groups, epilogue, outs, tm_want=1024, transposed=False,
          cast=None, sample_f32=False):
    w, layer = w
    m, k = x.shape
    s = xs.shape[0]
    tm = _row_tile(m, tm_want)
    assert col0 % tn == 0
    b0 = col0 // tn
    if transposed:
        w_block = (None, tn, k)
        w_map = lambda j, i, off: (layer, off + j, 0)
    else:
        w_block = (None, k, tn)
        w_map = lambda j, i, off: (layer, 0, off + j)
    w_specs = [pl.BlockSpec(w_block, functools.partial(w_map, off=b0 + g * group_stride)) for g in range(n_groups)]
    n_out = len(outs)

    def out_for(rows, row_tile, row_map, n, dt, by_rows):
        if not by_rows:
            return jax.ShapeDtypeStruct((rows, n), dt), pl.BlockSpec((row_tile, tn), row_map)
        assert n_tiles == 1 and n == tn
        per = n // LANES
        return (jax.ShapeDtypeStruct((rows * per, LANES), dt),
                pl.BlockSpec((row_tile * per, LANES), lambda j, i: (row_map(j, i)[0], 0)))

    prompt_outs = [out_for(m, tm, lambda j, i: (i, j), *o) for o in outs]
    sample_outs = [out_for(s, s, lambda j, i: (0, j), n, F32 if sample_f32 else dt, by_rows)
                   for n, dt, by_rows in outs]
    all_outs = prompt_outs + sample_outs
    operands = [x, xs] + [w] * n_groups
    in_specs = [pl.BlockSpec((tm, k), lambda j, i: (i, 0)), pl.BlockSpec((s, k), lambda j, i: (0, 0))] + w_specs
    n_i = m // tm
    if cast is not None:
        cw, c_layer = cast
        _, c_rows, c_cols = cw.shape
        bf16_rows = 2 * SUBLANES
        blk = next(r for r in range(bf16_rows, c_rows + 1, bf16_rows)
                   if c_rows % r == 0 and c_rows // r <= n_tiles * n_i)
        last = c_rows // blk - 1
        step = lambda j, i: jnp.minimum(j * n_i + i, last)
        in_specs.append(pl.BlockSpec((None, blk, c_cols), lambda j, i: (c_layer, step(j, i), 0)))
        operands.append(cw)
        all_outs.append((jax.ShapeDtypeStruct((c_rows, c_cols), BF16),
                         pl.BlockSpec((blk, c_cols), lambda j, i: (step(j, i), 0))))
    res = pl.pallas_call(
        functools.partial(_proj_kernel, n_groups=n_groups, n_out=n_out, epilogue=epilogue, transposed=transposed,
                          with_cast=cast is not None),
        grid=(n_tiles, n_i),
        in_specs=in_specs,
        out_specs=[spec for _, spec in all_outs],
        out_shape=[shape for shape, _ in all_outs],
        scratch_shapes=[pltpu.VMEM((n_groups,) + w_block[1:], BF16)],
        compiler_params=_cparams("arbitrary", "arbitrary"),
    )(*operands)
    return res[:n_out], res[n_out:2 * n_out], (res[2 * n_out] if cast is not None else None)


def _ep_plain(accs, outs):
    for acc, o in zip(accs, outs):
        o[...] = acc.astype(o.dtype)


def proj_plain(x, xs, w, col0, n_cols, tn, transposed=False, cast=None, prompt_dtype=F32):
    p, s, c = _proj(x, xs, w, col0=col0, tn=tn, n_tiles=n_cols // tn, group_stride=0, n_groups=1,
                    epilogue=_ep_plain, outs=[(n_cols, prompt_dtype, False)], transposed=transposed, cast=cast,
                    sample_f32=True)
    return (p[0], s[0]) if cast is None else (p[0], s[0], c)


def _ep_both_layouts(accs, outs):
    n = len(accs)
    for acc, o_tok, o_rows in zip(accs, outs[:n], outs[n:]):
        o_tok[...] = acc
        per = acc.shape[1] // LANES
        rows = acc.shape[0]
        for c in range(per):
            o_rows[pl.ds(c, rows, stride=per), :] = acc[:, c * LANES:(c + 1) * LANES]


def proj_groups(x, xs, w, col0, n_groups, group_cols, transposed=False):
    outs = [(group_cols, F32, False)] * n_groups + [(group_cols, F32, True)] * n_groups
    p, s, _ = _proj(x, xs, w, col0=col0, tn=group_cols, n_tiles=1, group_stride=1, n_groups=n_groups,
                    epilogue=_ep_both_layouts, outs=outs, tm_want=512, transposed=transposed)
    return (p[:n_groups], p[n_groups:]), (s[:n_groups], s[n_groups:])


def _ep_swiglu(accs, outs):
    g, u = accs
    outs[0][...] = (g * _sigmoid(g) * u).astype(BF16)


def proj_swiglu(x, xs, w, ff, cast):
    tn = 512
    p, s, c = _proj(x, xs, w, col0=0, tn=tn, n_tiles=ff // tn, group_stride=ff // tn, n_groups=2,
                    epilogue=_ep_swiglu, outs=[(ff, BF16, False)], cast=cast)
    return p[0], s[0], c


def _ep_conv_in(accs, outs):
    bg, cg, u = accs
    outs[0][...] = bg
    outs[1][...] = cg * u


def proj_conv_in(x, xs, w, d, cast):
    tn = 512
    return _proj(x, xs, w, col0=0, tn=tn, n_tiles=d // tn, group_stride=d // tn, n_groups=3,
                 epilogue=_ep_conv_in, outs=[(d, F32, False), (d, F32, False)], cast=cast)


def _conv_gated(bg_ref, z_ref, zp_ref, cw_ref, T):
    tm = z_ref.shape[0]
    z = z_ref[...]
    zc = jnp.concatenate([zp_ref[...], z], axis=0)
    tpos = (pl.program_id(0) * tm + _iota((tm, 1), 0)) % T
    y = cw_ref[CONV_W - 1:CONV_W, :] * z
    for back in range(1, CONV_W):
        zb = zc[SUBLANES - back:SUBLANES - back + tm, :]
        y = y + cw_ref[CONV_W - 1 - back:CONV_W - back, :] * jnp.where(tpos >= back, zb, 0.0)
    return (bg_ref[...] * y).astype(BF16)


def _out_kernel(*refs, emit_next, conv_T):
    if conv_T:
        h = _conv_gated(*refs[:4], conv_T)
        rows_of = lambda rows: h[rows, :]
        refs = refs[4:]
    else:
        h_ref = refs[0]
        rows_of = lambda rows: h_ref[rows, :]
        refs = refs[1:]
    hs_ref, w_ref, r_ref, rs_ref, nwa_ref, nwb_ref, *out_refs = refs

    def finalize(y, r, x_ref, xn_ref):
        x_new = r + _rms(y) * nwa_ref[...]
        x_ref[...] = x_new
        if emit_next:
            xn_ref[...] = (_rms(x_new) * nwb_ref[...]).astype(BF16)

    if emit_next:
        x_ref, xn_ref, xs_ref, xsn_ref = out_refs
    else:
        (x_ref, xs_ref), xn_ref, xsn_ref = out_refs, None, None

    @pl.when(pl.program_id(0) == 0)
    def _():
        finalize(_dot(hs_ref[...], w_ref[...]), rs_ref[...], xs_ref, xsn_ref)

    tm = r_ref.shape[0]
    sub = math.gcd(tm, OUT_SUB_ROWS)
    for r0 in range(0, tm, sub):
        rows = slice(r0, r0 + sub)
        finalize(_dot(rows_of(rows), w_ref[...]), r_ref[rows, :], x_ref.at[rows, :],
                 xn_ref.at[rows, :] if emit_next else None)


def proj_out(h, hs, w, resid, resid_s, nw_post, nw_next, emit_next=True, conv=None):
    m, d = resid.shape
    k = w.shape[0]
    s = hs.shape[0]
    tm = _row_tile(m, 2 * OUT_SUB_ROWS)
    h_bytes = tm * k * (2 if conv is None else 8)
    conv_tmp = 0 if conv is None else h_bytes
    vmem = k * d * 2 + 2 * (h_bytes + tm * d * (4 + 4 + (2 if emit_next else 0))) + conv_tmp + OUT_VMEM_SLACK
    row = lambda i: (i, 0)
    fixed = lambda i: (0, 0)
    if conv is None:
        h_ops, h_specs, conv_T = [h], [pl.BlockSpec((tm, k), row)], 0
    else:
        bg, z, conv_w, conv_T = conv
        assert conv_T % tm == 0
        per = tm // SUBLANES
        h_ops = [bg, z, z, conv_w]
        h_specs = [pl.BlockSpec((tm, k), row), pl.BlockSpec((tm, k), row),
                   pl.BlockSpec((SUBLANES, k), lambda i: (jnp.maximum(i * per - 1, 0), 0)),
                   pl.BlockSpec((CONV_W, k), fixed)]
    f32_out = [jax.ShapeDtypeStruct((m, d), F32), jax.ShapeDtypeStruct((s, d), F32)]
    bf16_out = [jax.ShapeDtypeStruct((m, d), BF16), jax.ShapeDtypeStruct((s, d), BF16)]
    specs = [pl.BlockSpec((tm, d), row), pl.BlockSpec((s, d), fixed)]
    if emit_next:
        out_shape = [f32_out[0], bf16_out[0], f32_out[1], bf16_out[1]]
        out_specs = [specs[0], specs[0], specs[1], specs[1]]
    else:
        out_shape, out_specs = f32_out, specs
    res = pl.pallas_call(
        functools.partial(_out_kernel, emit_next=emit_next, conv_T=conv_T),
        grid=(m // tm,),
        in_specs=h_specs + [pl.BlockSpec((s, k), fixed),
                            pl.BlockSpec((k, d), fixed, pipeline_mode=pl.Buffered(1)),
                            pl.BlockSpec((tm, d), row),
                            pl.BlockSpec((s, d), fixed),
                            pl.BlockSpec((1, d), fixed),
                            pl.BlockSpec((1, d), fixed)],
        out_specs=out_specs,
        out_shape=out_shape,
        compiler_params=_cparams("arbitrary", vmem=vmem),
    )(*h_ops, hs, w, resid, resid_s, nw_post.reshape(1, d), nw_next.reshape(1, d))
    if emit_next:
        return (res[0], res[2]), (res[1], res[3])
    return (res[0], res[1]), (None, None)


def _log_sigmoid(x):
    return jnp.minimum(x, 0.0) - jnp.log1p(jnp.exp(-jnp.abs(x)))


def _mlstm_kernel(q_ref, k_ref, v_ref, o_ref, g_ref, bg_ref, hn_ref, hg_ref, c_ref, n_ref, m_ref, cn_ref,
                  *, L, H, DK, DV):
    chunk = pl.program_id(1)

    @pl.when(chunk == 0)
    def _():
        cn_ref[...] = jnp.zeros_like(cn_ref)
        m_ref[...] = jnp.zeros_like(m_ref)

    gpre = g_ref[...] + bg_ref[...]
    lf = _log_sigmoid(gpre)
    rows = _iota((L, L), 0)
    cols = _iota((L, L), 1)
    causal = rows >= cols
    tril = jnp.where(causal, 1.0, 0.0).astype(BF16)
    lf_hi, lf_lo = _split_bf16(lf)
    bcum = _dot(tril, lf_hi) + _dot(tril, lf_lo)
    g_t = gpre.T
    b_t = bcum.T
    for h in range(H):
        f = FORGET_BIAS_COL + h
        a_col = bcum[:, f:f + 1]
        i_col = gpre[:, h:h + 1]
        b_row = b_t[f:f + 1, :]
        i_row = g_t[h:h + 1, :]
        m_prev = m_ref[0, h:h + 1, 0:1]
        log_d = jnp.where(causal, a_col - b_row + i_row, NEG)
        log_inter = a_col + m_prev
        m_t = jnp.maximum(log_inter, jnp.max(log_d, axis=-1, keepdims=True))
        d = jnp.exp(log_d - m_t)
        inter = jnp.exp(log_inter - m_t)
        kh = k_ref[:, h * DK:(h + 1) * DK].astype(F32) * (DK ** -0.5)
        v1 = jnp.concatenate([v_ref[:, h * DV:(h + 1) * DV], jnp.ones((L, LANES), BF16)], axis=1)
        qb = q_ref[:, h * DK:(h + 1) * DK]
        s = _dot_nt(qb, kh.astype(BF16)) * d
        cn_old = cn_ref[h]
        both = _dot(s.astype(BF16), v1) + inter * _dot(qb, cn_old.astype(BF16))
        num, den = both[:, :DV], both[:, DV:]
        scale = 1.0 / jnp.maximum(jnp.abs(den), jnp.exp(-m_t))
        hh = _rms(num * jnp.concatenate([scale] * (DV // LANES), axis=1))
        gate = _sigmoid(o_ref[:, h * DV:(h + 1) * DV])
        hg_ref[:, h * DV:(h + 1) * DV] = (hh * hn_ref[:, h * DV:(h + 1) * DV] * gate).astype(BF16)
        m_new = m_t[L - 1:L, :]
        w_col = jnp.exp(a_col[L - 1:L, :] - a_col + i_col - m_new)
        decay = jnp.exp(log_inter[L - 1:L, :] - m_new)
        cn_new = decay * cn_old + _dot_tn((kh * w_col).astype(BF16), v1)
        cn_ref[h] = cn_new
        m_ref[0, h:h + 1, :] = jnp.broadcast_to(m_new, (1, LANES))

        @pl.when(chunk == pl.num_programs(1) - 1)
        def _():
            c_ref[0, h] = cn_new[:, :DV]
            n_ref[0, h:h + 1, :] = cn_new[:, DV:].T[0:1, :]


def mlstm_prompt(qkv, o, gates, b_gates, head_norm, bsz, T):
    H = MLSTM_HEADS
    m, hdv = o.shape
    hdk = hdv // 2
    DK, DV = hdk // H, hdv // H
    L = math.gcd(T, MLSTM_CHUNK)
    nc = T // L
    row = lambda b, c: (b * nc + c, 0)
    state = lambda b, c: (b, 0, 0)
    return pl.pallas_call(
        functools.partial(_mlstm_kernel, L=L, H=H, DK=DK, DV=DV),
        grid=(bsz, nc),
        in_specs=[pl.BlockSpec((L, hdk), row),
                  pl.BlockSpec((L, hdk), lambda b, c: (b * nc + c, 1)),
                  pl.BlockSpec((L, hdv), lambda b, c: (b * nc + c, 1)),
                  pl.BlockSpec((L, hdv), row),
                  pl.BlockSpec((L, LANES), row),
                  pl.BlockSpec((1, LANES), lambda b, c: (0, 0)),
                  pl.BlockSpec((1, hdv), lambda b, c: (0, 0))],
        out_specs=[pl.BlockSpec((L, hdv), row),
                   pl.BlockSpec((1, H, DK, DV), lambda b, c: (b, 0, 0, 0)),
                   pl.BlockSpec((1, H, DK), state),
                   pl.BlockSpec((1, H, LANES), state)],
        out_shape=[jax.ShapeDtypeStruct((m, hdv), BF16),
                   jax.ShapeDtypeStruct((bsz, H, DK, DV), F32),
                   jax.ShapeDtypeStruct((bsz, H, DK), F32),
                   jax.ShapeDtypeStruct((bsz, H, LANES), F32)],
        scratch_shapes=[pltpu.VMEM((H, DK, DV + LANES), F32)],
        compiler_params=_cparams("arbitrary", "arbitrary"),
    )(qkv, qkv, qkv, o, gates, _pad_lanes(b_gates.reshape(1, -1)), head_norm.reshape(1, hdv))


def _pad_lanes(x, width=LANES):
    return jnp.pad(x, ((0, 0), (0, width - x.shape[-1])))


def _to_col(row, n):
    eye = _iota((n, n), 0) == _iota((n, n), 1)
    return jnp.sum(jnp.where(eye, jnp.broadcast_to(row, (n, n)), 0.0), axis=-1, keepdims=True)


def _mlstm_step_kernel(x_ref, g_ref, bg_ref, hn_ref, c0_ref, n0_ref, m0_ref, hg_ref, c_ref, n_ref, m_ref,
                       *, H, DK, DV):
    x = x_ref[0]
    gpre = g_ref[0] + bg_ref[...]
    lf = _log_sigmoid(gpre)
    hdk = H * DK
    hdv = H * DV
    for h in range(H):
        f = FORGET_BIAS_COL + h
        ig = gpre[:, h:h + 1]
        m_prev = m0_ref[0, h:h + 1, :]
        log_inter = lf[:, f:f + 1] + m_prev
        m_t = jnp.maximum(log_inter, ig)
        d = jnp.exp(ig - m_t)
        inter = jnp.exp(log_inter - m_t)
        q = x[:, h * DK:(h + 1) * DK]
        k = x[:, hdk + h * DK:hdk + (h + 1) * DK] * (DK ** -0.5)
        v = x[:, 2 * hdk + h * DV:2 * hdk + (h + 1) * DV]
        og = x[:, 2 * hdk + hdv + h * DV:2 * hdk + hdv + (h + 1) * DV]
        c_old = c0_ref[0, h]
        n_old = n0_ref[0, h:h + 1, :]
        s = jnp.sum(q * k, axis=-1, keepdims=True) * d
        q_col = _to_col(q, DK)
        k_col = _to_col(k, DK)
        num = s * v + inter * jnp.sum(q_col * c_old, axis=0, keepdims=True)
        den = s + inter * jnp.sum(q * n_old, axis=-1, keepdims=True)
        hh = _rms(num / jnp.maximum(jnp.abs(den), jnp.exp(-m_t)))
        hg_ref[0, :, h * DV:(h + 1) * DV] = (hh * hn_ref[:, h * DV:(h + 1) * DV] * _sigmoid(og)).astype(BF16)
        c_ref[0, h] = inter * c_old + d * (k_col * v)
        n_ref[0, h:h + 1, :] = inter * n_old + d * k
        m_ref[0, h:h + 1, :] = m_t


def mlstm_step(qkvo, gates, b_gates, head_norm, c0, n0, m0):
    bsz, H, DK, DV = c0.shape
    wid = qkvo.shape[1]
    hdv = H * DV
    one = lambda b: (b, 0, 0)
    hg, c1, n1, m1 = pl.pallas_call(
        functools.partial(_mlstm_step_kernel, H=H, DK=DK, DV=DV),
        grid=(bsz,),
        in_specs=[pl.BlockSpec((1, 1, wid), one),
                  pl.BlockSpec((1, 1, LANES), one),
                  pl.BlockSpec((1, LANES), lambda b: (0, 0)),
                  pl.BlockSpec((1, hdv), lambda b: (0, 0)),
                  pl.BlockSpec((1, H, DK, DV), lambda b: (b, 0, 0, 0)),
                  pl.BlockSpec((1, H, DK), one),
                  pl.BlockSpec((1, H, 1), one)],
        out_specs=[pl.BlockSpec((1, 1, hdv), one),
                   pl.BlockSpec((1, H, DK, DV), lambda b: (b, 0, 0, 0)),
                   pl.BlockSpec((1, H, DK), one),
                   pl.BlockSpec((1, H, 1), one)],
        out_shape=[jax.ShapeDtypeStruct((bsz, 1, hdv), BF16),
                   jax.ShapeDtypeStruct((bsz, H, DK, DV), F32),
                   jax.ShapeDtypeStruct((bsz, H, DK), F32),
                   jax.ShapeDtypeStruct((bsz, H, 1), F32)],
        compiler_params=_cparams("parallel"),
    )(qkvo.reshape(bsz, 1, wid), gates.reshape(bsz, 1, LANES), _pad_lanes(b_gates.reshape(1, -1)),
      head_norm.reshape(1, hdv), c0, n0, m0.reshape(bsz, H, 1))
    return hg.reshape(bsz, hdv), c1, n1, m1.reshape(bsz, H)


def _conv_step_kernel(bg_ref, z_ref, buf_ref, cw_ref, o_ref):
    y = cw_ref[CONV_W - 1:CONV_W, :] * z_ref[...]
    for j in range(CONV_W - 1):
        y = y + cw_ref[j:j + 1, :] * buf_ref[j]
    o_ref[...] = bg_ref[...] * y


def conv_gate_step(bg, z, buf, conv_w):
    bsz, d = z.shape
    full = lambda: (0, 0)
    return pl.pallas_call(
        _conv_step_kernel,
        in_specs=[pl.BlockSpec((bsz, d), full), pl.BlockSpec((bsz, d), full),
                  pl.BlockSpec((CONV_W - 1, bsz, d), lambda: (0, 0, 0)), pl.BlockSpec((CONV_W, d), full)],
        out_specs=pl.BlockSpec((bsz, d), full),
        out_shape=jax.ShapeDtypeStruct((bsz, d), F32),
    )(bg, z, buf, conv_w)


def _gelu(x):
    return 0.5 * x * (1.0 + jnp.tanh(math.sqrt(2.0 / math.pi) * (x + 0.044715 * x * x * x)))


def _cmp_finish(a, posw, b1, w2, b2):
    hid = a.shape[1] // 2
    a1 = a[:, hid:]
    a1_next = jnp.concatenate([a1[1:], a1[:1]], axis=0)
    pre = a[:, :hid] + a1_next + b1 + posw[0:1, :hid] + posw[1:2, hid:]
    return _dot(_gelu(pre).astype(BF16), w2) + b2


def _cmp_prompt_kernel(k_ref, v_ref, w1_ref, pos_ref, b1_ref, w2_ref, b2_ref, ck_ref, cv_ref, *, nch):
    for idx, (x_ref, o_ref) in enumerate(((k_ref, ck_ref), (v_ref, cv_ref))):
        w1 = w1_ref[idx]
        lhs = jnp.concatenate([x_ref[pl.ds(p, nch, stride=CMP_STRIDE), :] for p in range(CMP_STRIDE)], axis=1)
        a = _dot(lhs.astype(BF16), w1)
        posw = _dot(pos_ref[idx], w1)
        o_ref[0, 0] = _cmp_finish(a, posw, b1_ref[idx], w2_ref[idx], b2_ref[idx])


def _cmp_weights(cmp_pos, cmp_w1, cmp_b1, cmp_w2, cmp_b2):
    two, ratio, stride, hd, hid = cmp_w1.shape
    w1 = cmp_w1.transpose(0, 2, 3, 1, 4).reshape(two, stride * hd, ratio * hid).astype(BF16)
    pos = cmp_pos.reshape(two, ratio, stride * hd)
    pos = jnp.pad(pos, ((0, 0), (0, SUBLANES - ratio), (0, 0))).astype(BF16)
    return w1, pos, cmp_b1.reshape(two, 1, hid), cmp_w2.astype(BF16), cmp_b2.reshape(two, 1, hd)


def cmp_prompt(kc, vc, bsz, T, cw):
    w1, pos, b1, w2, b2 = cw
    nch = T // CMP_STRIDE
    G = NSA_KV
    out = jax.ShapeDtypeStruct((bsz, G, nch, NSA_HD), F32)
    ospec = pl.BlockSpec((1, 1, nch, NSA_HD), lambda b, g: (b, g, 0, 0))
    whole = lambda a: pl.BlockSpec(a.shape, lambda b, g: (0,) * a.ndim)
    seq = pl.BlockSpec((T, NSA_HD), lambda b, g: (b, g))
    return pl.pallas_call(
        functools.partial(_cmp_prompt_kernel, nch=nch),
        grid=(bsz, G),
        in_specs=[seq, seq, whole(w1), whole(pos), whole(b1), whole(w2), whole(b2)],
        out_specs=[ospec, ospec],
        out_shape=[out, out],
        compiler_params=_cparams("parallel", "parallel"),
    )(kc, vc, w1, pos, b1, w2, b2)


def _overlap_t(nbs_pad, nbc_pad):
    j = _iota((nbs_pad, nbc_pad), 0)
    n = _iota((nbs_pad, nbc_pad), 1)
    lo = jnp.maximum(n * CMP_STRIDE, j * SEL_BLOCK)
    hi = jnp.minimum(n * CMP_STRIDE + CMP_BLOCK, j * SEL_BLOCK + SEL_BLOCK)
    return (jnp.maximum(hi - lo, 0) // CMP_STRIDE).astype(F32)


def _rank_select(score, n_sel, axis):
    n = score.shape[axis]
    idx = _iota(score.shape, axis)
    cnt = jnp.zeros(score.shape, F32)
    for jp in range(n):
        other = lax.slice_in_dim(score, jp, jp + 1, axis=axis)
        tie = jnp.where(idx > jp, 1.0, 0.0)
        cnt = cnt + jnp.where(other > score, 1.0, jnp.where(other == score, tie, 0.0))
    return jnp.where(cnt < n_sel, 1.0, 0.0)


def _attn_first(state, s, vt_b):
    m_ref, l_ref, acc_ref = state
    m = jnp.max(s, axis=0, keepdims=True)
    p = jnp.exp2(s - m)
    m_ref[...] = m
    l_ref[...] = jnp.sum(p, axis=0, keepdims=True)
    acc_ref[...] = _dot(vt_b, p.astype(BF16))


def _attn_more(state, s, vt_b):
    m_ref, l_ref, acc_ref = state
    m_old = m_ref[...]
    m_new = jnp.maximum(m_old, jnp.max(s, axis=0, keepdims=True))
    alpha = jnp.exp2(m_old - m_new)
    p = jnp.exp2(s - m_new)
    m_ref[...] = m_new
    l_ref[...] = alpha * l_ref[...] + jnp.sum(p, axis=0, keepdims=True)
    acc_ref[...] = alpha * acc_ref[...] + _dot(vt_b, p.astype(BF16))


def _attn_empty(state):
    m_ref, l_ref, acc_ref = state
    m_ref[...] = jnp.full(m_ref.shape, NEG, F32)
    l_ref[...] = jnp.zeros_like(l_ref)
    acc_ref[...] = jnp.zeros_like(acc_ref)


def _attn_merged(state_a, state_b):
    (ma_ref, la_ref, acca_ref), (mb_ref, lb_ref, accb_ref) = state_a, state_b
    m = jnp.maximum(ma_ref[...], mb_ref[...])
    wa = jnp.exp2(ma_ref[...] - m)
    wb = jnp.exp2(mb_ref[...] - m)
    return (wa * acca_ref[...] + wb * accb_ref[...]) / (wa * la_ref[...] + wb * lb_ref[...])


def _nsa_attn_kernel(q_ref, ck_ref, cv_ref, ks_ref, vs_ref, kw_ref, vw_ref, g_ref, bg_ref, o_ref,
                     ksb_ref, vst_ref, kwb_ref, vwt_ref, *states, tq, hpg, nbc, nbs, nbs_pad):
    qi = pl.program_id(2)
    t0 = qi * tq
    cols = hpg * tq
    tk = tq
    n_kt = ks_ref.shape[0] // tk
    nbc_pad = ck_ref.shape[2]

    @pl.when(qi == 0)
    def _():
        ksb_ref[...] = ks_ref[...].astype(BF16)
        kwb_ref[...] = kw_ref[...].astype(BF16)
        for kt in range(n_kt):
            vst_ref[kt] = vs_ref[kt * tk:(kt + 1) * tk, :].T.astype(BF16)
            vwt_ref[kt] = vw_ref[kt * tk:(kt + 1) * tk, :].T.astype(BF16)

    qb = (jnp.concatenate([q_ref[:, h * NSA_HD:(h + 1) * NSA_HD] for h in range(hpg)], axis=0)
          * (NSA_HD ** -0.5 * math.log2(math.e))).astype(BF16)

    n = _iota((nbc_pad, cols), 0)
    t_col = t0 + jnp.concatenate([_iota((nbc_pad, tq), 1)] * hpg, axis=1)
    cmask = jnp.where(n < nbc, jnp.where(n * CMP_STRIDE + (CMP_BLOCK - 1) <= t_col, 1.0, 0.0), 0.0)
    sc = jnp.where(cmask > 0.5, _dot_nt(ck_ref[0, 0].astype(BF16), qb), NEG)
    pe = jnp.exp2(sc - jnp.max(sc, axis=0, keepdims=True))
    p_cmp = pe / jnp.sum(pe, axis=0, keepdims=True) * cmask
    o_cmp = _dot(cv_ref[0, 0].T.astype(BF16), p_cmp.astype(BF16))

    p_sum = p_cmp[:, 0:tq]
    for h in range(1, hpg):
        p_sum = p_sum + p_cmp[:, h * tq:(h + 1) * tq]
    ov_t = _overlap_t(nbs_pad, nbc_pad).astype(BF16)
    p_hi, p_lo = _split_bf16(p_sum)
    imp_t = _dot(ov_t, p_hi) + _dot(ov_t, p_lo)
    j = _iota((nbs_pad, tq), 0)
    t = t0 + _iota((nbs_pad, tq), 1)
    t_blk = t // SEL_BLOCK
    forced = jnp.where(j == 0, 1.0, jnp.where(j == t_blk, 1.0, jnp.where(j == t_blk - 1, 1.0, 0.0)))
    visible = jnp.where(j < nbs, jnp.where(j * SEL_BLOCK <= t, 1.0, 0.0), 0.0)
    score = jnp.where(visible > 0.5, jnp.where(forced > 0.5, BIG, imp_t), NEG)
    sel_t = _rank_select(score, min(N_SELECT, nbs), axis=0).astype(BF16)

    k_ofs = _iota((tk, tq), 0)
    q_ofs = _iota((tk, tq), 1)
    causal = k_ofs <= q_ofs
    per_head = lambda bias: jnp.concatenate([bias] * hpg, axis=1)

    def scores(kb_ref, kt):
        k_b = kb_ref[pl.ds(pl.multiple_of(kt * tk, tk), tk), :]
        return _dot_nt(k_b, qb)

    def slc_bias(kt):
        blk = (kt * tk + _iota((tk, nbs_pad), 0)) // SEL_BLOCK
        expand = jnp.where(blk == _iota((tk, nbs_pad), 1), 1.0, 0.0).astype(BF16)
        return (_dot(expand, sel_t) - 1.0) * BIG

    slc_a, slc_b, win_a, win_b = states[0:3], states[3:6], states[6:9], states[9:12]

    def slc_tile(kt):
        return scores(ksb_ref, kt) + per_head(slc_bias(kt)), vst_ref[kt]

    def win_tile(back):
        s = scores(kwb_ref, qi - back)
        if back == n_back:
            s = s + per_head(jnp.where(k_ofs >= q_ofs, 0.0, NEG))
        return s, vwt_ref[qi - back]

    n_back = WINDOW // tk
    _attn_first(slc_a, scores(ksb_ref, qi) + per_head(jnp.where(causal, slc_bias(qi), NEG)), vst_ref[qi])
    _attn_first(win_a, scores(kwb_ref, qi) + per_head(jnp.where(causal, 0.0, NEG)), vwt_ref[qi])
    for st in (slc_b, win_b):
        _attn_empty(st)

    for back in range(1, n_back + 1, 2):
        both = back + 1 <= n_back

        @pl.when(qi >= back + 1 if both else qi >= back)
        def _():
            _attn_more(win_b, *win_tile(back))
            if both:
                _attn_more(win_a, *win_tile(back + 1))

        if both:
            @pl.when(qi == back)
            def _():
                _attn_more(win_b, *win_tile(back))

    def slc_pair(pair, _):
        _attn_more(slc_a, *slc_tile(2 * pair))
        _attn_more(slc_b, *slc_tile(2 * pair + 1))
        return 0

    lax.fori_loop(0, lax.shift_right_logical(qi, 1), slc_pair, 0)

    @pl.when(lax.bitwise_and(qi, 1) == 1)
    def _():
        _attn_more(slc_a, *slc_tile(qi - 1))

    o_slc = _attn_merged(slc_a, slc_b)
    o_win = _attn_merged(win_a, win_b)
    gate_t = _sigmoid(g_ref[...] + bg_ref[...]).T
    for h in range(hpg):
        sl = slice(h * tq, (h + 1) * tq)
        o_t = (gate_t[3 * h:3 * h + 1, :] * o_cmp[:, sl] + gate_t[3 * h + 1:3 * h + 2, :] * o_slc[:, sl]
               + gate_t[3 * h + 2:3 * h + 3, :] * o_win[:, sl])
        o_ref[:, h * NSA_HD:(h + 1) * NSA_HD] = o_t.T.astype(BF16)


def nsa_attn_prompt(q, ck, cv, ks, vs, kw, vw, gates, b_gate, bsz, T):
    G, HPG = NSA_KV, NSA_HPG
    tq = math.gcd(T, ATT_TILE)
    assert WINDOW % tq == 0
    nq = T // tq
    nch = T // CMP_STRIDE
    nbc = nch - CMP_BLOCK // CMP_STRIDE + 1
    nbs = -(-T // SEL_BLOCK)
    nbs_pad = -(-nbs // SUBLANES) * SUBLANES
    cols = HPG * tq
    kv_scratch = [pltpu.VMEM((T, NSA_HD), BF16), pltpu.VMEM((nq, NSA_HD, tq), BF16)] * 2
    stats = [pltpu.VMEM((1, cols), F32), pltpu.VMEM((1, cols), F32), pltpu.VMEM((NSA_HD, cols), F32)] * 4
    seq = pl.BlockSpec((T, NSA_HD), lambda b, g, i: (b, g))
    cspec = pl.BlockSpec((1, 1, nch, NSA_HD), lambda b, g, i: (b, g, 0, 0))
    qspec = pl.BlockSpec((tq, HPG * NSA_HD), lambda b, g, i: (b * nq + i, g))
    return pl.pallas_call(
        functools.partial(_nsa_attn_kernel, tq=tq, hpg=HPG, nbc=nbc, nbs=nbs, nbs_pad=nbs_pad),
        grid=(bsz, G, nq),
        in_specs=[qspec, cspec, cspec, seq, seq, seq, seq,
                  pl.BlockSpec((tq, LANES), lambda b, g, i: (b * nq + i, g)),
                  pl.BlockSpec((1, LANES), lambda b, g, i: (0, g))],
        out_specs=qspec,
        out_shape=jax.ShapeDtypeStruct((bsz * T, G * HPG * NSA_HD), BF16),
        scratch_shapes=kv_scratch + stats,
        compiler_params=_cparams("arbitrary", "arbitrary", "arbitrary"),
    )(q, ck, cv, ks, vs, kw, vw, gates, b_gate)


CHUNK_ROWS = CMP_STRIDE * NSA_KV
CHUNK_PITCH = CHUNK_ROWS + SUBLANES


def _dec_cmp_kernel(tbl_ref, *refs, P, nbc, nbs, nbs_pad, p0):
    k_pages, v_pages = refs[:P], refs[P:2 * P]
    w1_ref, q_ref, pos_ref, b1_ref, w2_ref, b2_ref, o_ref, sel_ref, pad_ref, ak_ref, av_ref = refs[2 * P:]
    cpp = LANES // CMP_STRIDE
    step = pl.program_id(1)
    for idx, (pages, a_ref) in enumerate(((k_pages, ak_ref), (v_pages, av_ref))):
        for i, pg in enumerate(pages):
            for c in range(cpp):
                pad_ref[idx * P + i, c * CHUNK_PITCH:c * CHUNK_PITCH + CHUNK_ROWS, :] = (
                    pg[0, c * CHUNK_ROWS:(c + 1) * CHUNK_ROWS, :])
        blocks = []
        for g in range(NSA_KV):
            for i in range(P):
                blocks.append(jnp.concatenate(
                    [pad_ref[idx * P + i, pl.ds(p * NSA_KV + g, cpp, stride=CHUNK_PITCH), :]
                     for p in range(CMP_STRIDE)], axis=1))
        a = _dot(jnp.concatenate(blocks, axis=0).astype(BF16), w1_ref[idx])
        per_g = P * cpp
        for g in range(NSA_KV):
            a_ref[g, pl.ds(pl.multiple_of(step * per_g, per_g), per_g), :] = a[g * per_g:(g + 1) * per_g]

    @pl.when(step == pl.num_programs(1) - 1)
    def _():
        _dec_cmp_finish(ak_ref, av_ref, q_ref, pos_ref, w1_ref, b1_ref, w2_ref, b2_ref, o_ref, sel_ref,
                        nbc=nbc, nbs=nbs, nbs_pad=nbs_pad, p0=p0)


def _dec_cmp_finish(ak_ref, av_ref, q_ref, pos_ref, w1_ref, b1_ref, w2_ref, b2_ref, o_ref, sel_ref,
                    *, nbc, nbs, nbs_pad, p0):
    nch = ak_ref.shape[1]
    for g in range(NSA_KV):
        cks = []
        for idx, a_ref in enumerate((ak_ref, av_ref)):
            posw = _dot(pos_ref[idx], w1_ref[idx])
            cks.append(_cmp_finish(a_ref[g], posw, b1_ref[idx], w2_ref[idx], b2_ref[idx]))
        ck, cv = cks
        qb = (q_ref[0, g] * (NSA_HD ** -0.5)).astype(BF16)
        n = _iota((SUBLANES, nch), 1)
        cmask = jnp.where(n < nbc, jnp.where(n * CMP_STRIDE + (CMP_BLOCK - 1) <= p0, 1.0, 0.0), 0.0)
        sc = jnp.where(cmask > 0.5, _dot_nt(qb, ck.astype(BF16)), NEG)
        pe = jnp.exp(sc - jnp.max(sc, axis=-1, keepdims=True))
        p_cmp = pe / jnp.sum(pe, axis=-1, keepdims=True) * cmask
        o_ref[0, g] = _dot(p_cmp.astype(BF16), cv.astype(BF16))

        head = jnp.where(_iota((SUBLANES, nch), 0) < NSA_HPG, p_cmp, 0.0)
        p_sum = jnp.broadcast_to(jnp.sum(head, axis=0, keepdims=True), (SUBLANES, nch))
        ov_t = _overlap_t(nbs_pad, nch).astype(BF16)
        p_hi, p_lo = _split_bf16(p_sum)
        imp = _dot_nt(p_hi, ov_t) + _dot_nt(p_lo, ov_t)
        j = _iota((SUBLANES, nbs_pad), 1)
        t_blk = p0 // SEL_BLOCK
        forced = jnp.where(j == 0, 1.0, jnp.where(j == t_blk, 1.0, jnp.where(j == t_blk - 1, 1.0, 0.0)))
        visible = jnp.where(j < nbs, jnp.where(j * SEL_BLOCK <= p0, 1.0, 0.0), 0.0)
        score = jnp.where(visible > 0.5, jnp.where(forced > 0.5, BIG, imp), NEG)
        s_row = jnp.broadcast_to(score[0:1], (nbs_pad, nbs_pad))
        s_col = jnp.concatenate([jnp.broadcast_to(score[0:1], (LANES, nbs_pad)).T] * (nbs_pad // LANES), axis=1)
        jr = _iota((nbs_pad, nbs_pad), 0)
        jc = _iota((nbs_pad, nbs_pad), 1)
        tie = jnp.where(jc < jr, 1.0, 0.0)
        beats = jnp.where(s_row > s_col, 1.0, jnp.where(s_row == s_col, tie, 0.0))
        rank = jnp.sum(beats, axis=-1, keepdims=True)
        slot = _iota((nbs_pad, LANES), 1).astype(F32)
        blk = _iota((nbs_pad, LANES), 0).astype(F32)
        picked = jnp.sum(jnp.where(rank == slot, blk, 0.0), axis=0, keepdims=True)
        sel_ref[0, g:g + 1, :] = picked.astype(jnp.int32)


def dec_cmp(pool_k, pool_v, table, q, cw, p0):
    w1, pos, b1, w2, b2 = cw
    n_pool, page, G, hd = pool_k.shape
    assert page == LANES and G == NSA_KV and hd == NSA_HD
    bsz, n_pages = table.shape
    assert p0 == n_pages * page
    P = math.gcd(n_pages, PAGES_PER_STEP)
    cpp = page // CMP_STRIDE
    nch = n_pages * cpp
    hid2 = w1.shape[2]
    nbc = nch - CMP_BLOCK // CMP_STRIDE + 1
    nbs = -(-(p0 + 1) // SEL_BLOCK)
    nbs_pad = -(-nbs // LANES) * LANES
    pk = pool_k.reshape(n_pool, page * G, hd)
    pv = pool_v.reshape(n_pool, page * G, hd)
    pspec = lambda i: pl.BlockSpec((1, page * G, hd),
                                   functools.partial(lambda b, s, tbl, i: (tbl[b * n_pages + s * P + i], 0, 0), i=i))
    whole = lambda a: pl.BlockSpec(a.shape, lambda b, s, tbl: (0,) * a.ndim)
    per_seq = lambda shape: pl.BlockSpec((1,) + shape, lambda b, s, tbl: (b,) + (0,) * len(shape))
    grid_spec = pltpu.PrefetchScalarGridSpec(
        num_scalar_prefetch=1,
        grid=(bsz, n_pages // P),
        in_specs=[pspec(i) for i in range(P)] * 2
        + [whole(w1), per_seq((G, SUBLANES, hd)), whole(pos), whole(b1), whole(w2), whole(b2)],
        out_specs=[per_seq((G, SUBLANES, hd)), per_seq((G, LANES))],
        scratch_shapes=[pltpu.VMEM((2 * P, cpp * CHUNK_PITCH, hd), F32),
                        pltpu.VMEM((G, nch, hid2), F32), pltpu.VMEM((G, nch, hid2), F32)],
    )
    return pl.pallas_call(
        functools.partial(_dec_cmp_kernel, P=P, nbc=nbc, nbs=nbs, nbs_pad=nbs_pad, p0=p0),
        grid_spec=grid_spec,
        out_shape=[jax.ShapeDtypeStruct((bsz, G, SUBLANES, hd), F32),
                   jax.ShapeDtypeStruct((bsz, G, LANES), jnp.int32)],
        compiler_params=_cparams("arbitrary", "arbitrary"),
    )(table.reshape(-1), *([pk] * P), *([pv] * P), w1, q, pos, b1, w2, b2)


def _softmax_with_new_key(qb, k_b, v_b, bias, s_new, v_new):
    s = _dot_nt(qb, k_b)
    if bias is not None:
        s = s + bias
    mx = jnp.maximum(jnp.max(s, axis=-1, keepdims=True), s_new)
    p = jnp.exp(s - mx)
    p_new = jnp.exp(s_new - mx)
    return (_dot(p.astype(BF16), v_b) + p_new * v_new) / (jnp.sum(p, axis=-1, keepdims=True) + p_new)


def _dec_attn_kernel(sel_ref, tbl_ref, *refs, n_cache_blocks, n_sel):
    n_blk = NSA_KV * n_sel
    ks_refs, vs_refs = refs[:n_blk], refs[n_blk:2 * n_blk]
    q_ref, new_ref, kw_ref, vw_ref, oc_ref, g_ref, bg_ref, o_ref = refs[2 * n_blk:]
    b = pl.program_id(0)
    slot = _iota((SUBLANES, n_sel * SEL_BLOCK), 1) // SEL_BLOCK
    head = _iota((SUBLANES, LANES), 0)
    lane = _iota((SUBLANES, LANES), 1)
    n_win = kw_ref.shape[1] // NSA_KV
    for g in range(NSA_KV):
        q = q_ref[0, g] * (NSA_HD ** -0.5)
        qb = q.astype(BF16)
        new = new_ref[0, g]

        mine = lambda r, n: r[pl.ds(g, n, stride=NSA_KV), :]
        picks = range(g * n_sel, (g + 1) * n_sel)
        k_all = jnp.concatenate([mine(ks_refs[i].at[0, 0], SEL_BLOCK) for i in picks], axis=0).astype(BF16)
        v_all = jnp.concatenate([mine(vs_refs[i].at[0, 0], SEL_BLOCK) for i in picks], axis=0).astype(BF16)
        bias = jnp.zeros((SUBLANES, n_sel * SEL_BLOCK), F32)
        for s in range(n_sel):
            blk = sel_ref[(b * NSA_KV + g) * n_sel + s]
            bias = jnp.where(slot == s, jnp.where(blk < n_cache_blocks, 0.0, NEG), bias)
        o_slc = _softmax_with_new_key(qb, k_all, v_all, bias,
                                      jnp.sum(q * new[0:1], axis=-1, keepdims=True), new[1:2])

        o_win = _softmax_with_new_key(qb, mine(kw_ref.at[0], n_win).astype(BF16),
                                      mine(vw_ref.at[0], n_win).astype(BF16), None,
                                      jnp.sum(q * new[2:3], axis=-1, keepdims=True), new[3:4])

        gate = jnp.broadcast_to(_sigmoid(g_ref[0, g] + bg_ref[:, g * LANES:(g + 1) * LANES]), (SUBLANES, LANES))
        o = jnp.zeros((SUBLANES, NSA_HD), F32)
        for br, ob in enumerate((oc_ref[0, g], o_slc, o_win)):
            o = o + jnp.sum(jnp.where(lane == 3 * head + br, gate, 0.0), axis=-1, keepdims=True) * ob
        o_ref[0, g] = o


def dec_attn(sel, table, pool_ks, pool_vs, q, new, kw_past, vw_past, o_cmp, gates, b_gate):
    n_pool, page, G, hd = pool_ks.shape
    bsz, n_pages = table.shape
    halves = page // SEL_BLOCK
    wb = kw_past.shape[1]
    assert wb <= WINDOW
    n_sel = sel.shape[-1]
    pk = pool_ks.reshape(n_pool, halves, SEL_BLOCK * G, hd)
    pv = pool_vs.reshape(n_pool, halves, SEL_BLOCK * G, hd)
    n_cache_blocks = n_pages * halves

    def page_map(b, sel_ref, tbl_ref, i):
        blk = jnp.minimum(sel_ref[b * G * n_sel + i], n_cache_blocks - 1)
        return (tbl_ref[b * n_pages + blk // halves], blk % halves, 0, 0)

    per_seq = lambda shape: pl.BlockSpec((1,) + shape, lambda b, sel_ref, tbl_ref: (b,) + (0,) * len(shape))
    small = per_seq((G, SUBLANES, hd))
    blocks = [pl.BlockSpec((1, 1, SEL_BLOCK * G, hd), functools.partial(page_map, i=i)) for i in range(G * n_sel)]
    grid_spec = pltpu.PrefetchScalarGridSpec(
        num_scalar_prefetch=2,
        grid=(bsz,),
        in_specs=blocks + blocks + [small, small, per_seq((wb * G, hd)), per_seq((wb * G, hd)), small,
                                    per_seq((G, 1, LANES)),
                                    pl.BlockSpec((1, G * LANES), lambda b, sel_ref, tbl_ref: (0, 0))],
        out_specs=small,
    )
    return pl.pallas_call(
        functools.partial(_dec_attn_kernel, n_cache_blocks=n_cache_blocks, n_sel=n_sel),
        grid_spec=grid_spec,
        out_shape=jax.ShapeDtypeStruct((bsz, G, SUBLANES, hd), F32),
        compiler_params=_cparams("arbitrary"),
    )(sel.reshape(-1), table.reshape(-1), *([pk] * (G * n_sel)), *([pv] * (G * n_sel)), q, new,
      kw_past.reshape(bsz, wb * G, hd), vw_past.reshape(bsz, wb * G, hd), o_cmp,
      gates.reshape(bsz, G, 1, LANES), b_gate)


def _gate_weights(wt, layer, row0, n):
    rows = lax.slice(wt, (layer, row0, 0), (layer + 1, row0 + n, wt.shape[2]))
    return jnp.pad(rows, ((0, 0), (0, LANES - n), (0, 0))), 0


def _nsa_gate_weights(wt, layer, b_gate, row0):
    per = 3 * NSA_HPG
    k = wt.shape[2]
    rows = lax.slice(wt, (layer, row0, 0), (layer + 1, row0 + NSA_KV * per, k)).reshape(NSA_KV, per, k)
    wg = jnp.pad(rows, ((0, 0), (0, LANES - per), (0, 0))).reshape(1, NSA_KV * LANES, k)
    bg = jnp.pad(b_gate.reshape(NSA_KV, per), ((0, 0), (0, LANES - per))).reshape(1, NSA_KV * LANES)
    return (wg, 0), bg


def kernel(x_prompt, x_sample, state_a_C, state_a_n, state_a_m, state_b_conv, cache_c_k_cmp, cache_c_v_cmp, cache_c_k_slc, cache_c_v_slc, cache_c_k_win, cache_c_v_win, page_table, norm_w, ffn_w_in, ffn_w_out, a_w_in, a_b_gates, a_head_norm, a_w_out, b_w_in, b_conv_w, b_w_out, c_w_in, c_b_gate, c_cmp_pos, c_cmp_w1, c_cmp_b1, c_cmp_w2, c_cmp_b2, c_w_out):
    bp, T, D = x_prompt.shape
    bs = x_sample.shape[0]
    assert x_sample.shape[1] == 1
    depth = norm_w.shape[0]
    G, hd = NSA_KV, NSA_HD
    past_len = page_table.shape[1] * cache_c_k_cmp.shape[2]
    xp = x_prompt.reshape(bp * T, D)
    xs = x_sample.reshape(bs, D)
    hp_in = norm_cast(xp, norm_w[0, 0])
    hs_in = norm_cast(xs, norm_w[0, 0])
    a_wt = jnp.swapaxes(a_w_in, 1, 2)
    c_wt = jnp.swapaxes(c_w_in, 1, 2)
    w_outs = {0: a_w_out, 1: b_w_out, 2: c_w_out}
    pa, sa, pb, sb, pc, sc = [], [], [], [], [], []
    for i in range(depth):
        kind, j = i % 3, i // 3
        nw = norm_w[i]
        w_out_f32 = (w_outs[kind], j)
        conv = None
        if kind == 0:
            hdv = a_head_norm.shape[1]
            n_main = 3 * hdv
            wg = _gate_weights(a_wt, j, n_main, 2 * MLSTM_HEADS)
            qkv_p, qkv_s, w_out = proj_plain(hp_in, hs_in, (a_wt, j), 0, 2 * hdv, 1024, transposed=True,
                                             cast=w_out_f32, prompt_dtype=BF16)
            o_p, o_s = proj_plain(hp_in, hs_in, (a_wt, j), 2 * hdv, hdv, 1024, transposed=True)
            qkvo_s = jnp.concatenate([qkv_s, o_s], axis=1)
            g_p, g_s = proj_plain(hp_in, hs_in, wg, 0, LANES, LANES, transposed=True)
            hp, c_p, n_p, m_p = mlstm_prompt(qkv_p, o_p, g_p, a_b_gates[j], a_head_norm[j], bp, T)
            hs, c_s, n_s, m_s = mlstm_step(qkvo_s, g_s, a_b_gates[j], a_head_norm[j],
                                           state_a_C[j], state_a_n[j], state_a_m[j])
            pa.append((c_p, n_p, m_p[:, :, 0]))
            sa.append((c_s, n_s, m_s))
        elif kind == 1:
            (bg_p, z_p), (bg_s, z_s), w_out = proj_conv_in(hp_in, hs_in, (b_w_in, j), D, w_out_f32)
            hp, conv = None, (bg_p, z_p, b_conv_w[j], T)
            buf = state_b_conv[j]
            hs = conv_gate_step(bg_s, z_s, buf.transpose(1, 0, 2), b_conv_w[j]).astype(BF16)
            pb.append(z_p.reshape(bp, T, D)[:, T - (CONV_W - 1):])
            sb.append(jnp.concatenate([buf, z_s[:, None]], axis=1)[:, -(CONV_W - 1):])
        else:
            nq_cols = NSA_HPG * G * hd
            nheads = G * NSA_HPG
            wg, bgate = _nsa_gate_weights(c_wt, j, c_b_gate[j], nq_cols + 6 * G * hd)
            cw = _cmp_weights(c_cmp_pos[j], c_cmp_w1[j], c_cmp_b1[j], c_cmp_w2[j], c_cmp_b2[j])
            q_p, q_s, w_out = proj_plain(hp_in, hs_in, (c_wt, j), 0, nq_cols, 1024, transposed=True, cast=w_out_f32)
            kv_p, kv_rows_p, kv_s = [], [], []
            for half in range(2):
                (tok_p, rows_p), (tok_s, _) = proj_groups(hp_in, hs_in, (c_wt, j), nq_cols + half * 3 * G * hd,
                                                         3, G * hd, transposed=True)
                kv_p, kv_rows_p, kv_s = kv_p + list(tok_p), kv_rows_p + list(rows_p), kv_s + list(tok_s)
            gt_p, gt_s = proj_plain(hp_in, hs_in, wg, 0, G * LANES, G * LANES, transposed=True)
            ck, cv = cmp_prompt(kv_p[0], kv_p[1], bp, T, cw)
            hp = nsa_attn_prompt(q_p, ck, cv, kv_p[2], kv_p[3], kv_p[4], kv_p[5], gt_p, bgate, bp, T)
            kv_rows_p = [a.reshape(bp, T, G, hd) for a in kv_rows_p]
            keep = min(WINDOW, T)
            pc.append(tuple(kv_rows_p[:4]) + tuple(a[:, T - keep:] for a in kv_rows_p[4:]))
            q_s = jnp.pad(q_s.reshape(bs, G, NSA_HPG, hd), ((0, 0), (0, 0), (0, SUBLANES - NSA_HPG), (0, 0)))
            kv_s = jnp.stack([a.reshape(bs, G, hd) for a in kv_s], axis=1)
            new = jnp.pad(kv_s[:, 2:6].transpose(0, 2, 1, 3), ((0, 0), (0, 0), (0, SUBLANES - 4), (0, 0)))
            o_cmp, ranked = dec_cmp(cache_c_k_cmp[j], cache_c_v_cmp[j], page_table, q_s, cw, past_len)
            n_sel = min(N_SELECT, -(-(past_len + 1) // SEL_BLOCK))
            o_s = dec_attn(ranked[:, :, :n_sel], page_table, cache_c_k_slc[j], cache_c_v_slc[j], q_s, new,
                           cache_c_k_win[j], cache_c_v_win[j], o_cmp, gt_s, bgate)
            hs = o_s[:, :, :NSA_HPG].reshape(bs, nheads * hd).astype(BF16)
            keep_s = min(WINDOW, cache_c_k_win.shape[2] + 1)
            win = [jnp.concatenate([c[j], kv_s[:, a][:, None]], axis=1)[:, -keep_s:]
                   for c, a in ((cache_c_k_win, 4), (cache_c_v_win, 5))]
            sc.append(tuple(kv_s[:, a][:, None] for a in range(4)) + tuple(win))
        last = i == depth - 1
        nw_next = norm_w[i + 1, 0] if not last else nw[0]
        (xp, xs), (hp_mid, hs_mid) = proj_out(hp, hs, w_out, xp, xs, nw[1], nw[2], conv=conv)
        act_p, act_s, f_out = proj_swiglu(hp_mid, hs_mid, (ffn_w_in, i), ffn_w_out.shape[1], (ffn_w_out, i))
        (xp, xs), (hp_in, hs_in) = proj_out(act_p, act_s, f_out, xp, xs, nw[3], nw_next, not last)

    stack = lambda items: [jnp.stack(a) for a in zip(*items)]
    p_a, s_a = stack(pa), stack(sa)
    p_c, s_c = stack(pc), stack(sc)
    return (xp.reshape(bp, T, D), xs.reshape(bs, 1, D), *p_a, *s_a, jnp.stack(pb), jnp.stack(sb), *p_c, *s_c)
```

```python
import functools
import math

import jax
import jax.numpy as jnp
from jax import lax
from jax.experimental import pallas as pl
from jax.experimental.pallas import tpu as pltpu

F32 = jnp.float32
BF16 = jnp.bfloat16

EPS = 1e-6
NEG = -1e30
BIG = 1e30

LANES = 128
SUBLANES = 8
VMEM_LIMIT = 56 * 1024 * 1024

MLSTM_HEADS = 8
MLSTM_CHUNK = 512
FORGET_BIAS_COL = MLSTM_HEADS
CONV_W = 3
NSA_HD = 128
NSA_KV = 4
NSA_HPG = 4
CMP_BLOCK = 32
CMP_STRIDE = 16
SEL_BLOCK = 64
N_SELECT = 16
WINDOW = 512
ATT_TILE = 512
PAGES_PER_STEP = 16
OUT_SUB_ROWS = 256
OUT_VMEM_SLACK = 6 * 1024 * 1024


def _cparams(*sem, vmem=VMEM_LIMIT):
    return pltpu.CompilerParams(dimension_semantics=sem, vmem_limit_bytes=vmem)


def _dot(a, b):
    return jnp.dot(a, b, preferred_element_type=F32)


def _dot_nt(a, b):
    return lax.dot_general(a, b, (((1,), (1,)), ((), ())), preferred_element_type=F32)


def _dot_tn(a, b):
    return lax.dot_general(a, b, (((0,), (0,)), ((), ())), preferred_element_type=F32)


def _split_bf16(x):
    hi = x.astype(BF16)
    lo = (x - hi.astype(F32)).astype(BF16)
    return hi, lo


def _iota(shape, dim):
    return lax.broadcasted_iota(jnp.int32, shape, dim)


def _sigmoid(x):
    return 1.0 / (1.0 + jnp.exp(-x))


def _rms(x):
    return x * lax.rsqrt(jnp.mean(x * x, axis=-1, keepdims=True) + EPS)


def _row_tile(m, want):
    t = min(m, want)
    assert m % t == 0, (m, t)
    return t


def _norm_kernel(x_ref, w_ref, o_ref):
    o_ref[...] = (_rms(x_ref[...]) * w_ref[...]).astype(BF16)


def norm_cast(x, w):
    m, d = x.shape
    tm = _row_tile(m, 512)
    return pl.pallas_call(
        _norm_kernel,
        grid=(m // tm,),
        in_specs=[pl.BlockSpec((tm, d), lambda i: (i, 0)), pl.BlockSpec((1, d), lambda i: (0, 0))],
        out_specs=pl.BlockSpec((tm, d), lambda i: (i, 0)),
        out_shape=jax.ShapeDtypeStruct((m, d), BF16),
        compiler_params=_cparams("parallel"),
    )(x, w.reshape(1, d))


def _proj_kernel(x_ref, xs_ref, *refs, n_groups, n_out, epilogue, transposed, with_cast):
    w_refs = refs[:n_groups]
    refs = refs[n_groups:]
    if with_cast:
        cast_in_ref, refs = refs[0], refs[1:]
        refs[2 * n_out][...] = cast_in_ref[...].astype(BF16)
    out_refs = refs[:n_out]
    sample_out_refs = refs[n_out:2 * n_out]
    wb_ref = refs[-1]
    mm = _dot_nt if transposed else _dot

    @pl.when(pl.program_id(1) == 0)
    def _():
        for g in range(n_groups):
            wb_ref[g] = w_refs[g][...].astype(BF16)
        xs = xs_ref[...]
        epilogue([mm(xs, wb_ref[g]) for g in range(n_groups)], sample_out_refs)

    x = x_ref[...]
    epilogue([mm(x, wb_ref[g]) for g in range(n_groups)], out_refs)


def _proj(x, xs, w, *, col0, tn, n_tiles, group_stride, n_groups, epilogue, outs, tm_want=1024, transposed=False,
          cast=None, sample_f32=False):
    w, layer = w
    m, k = x.shape
    s = xs.shape[0]
    tm = _row_tile(m, tm_want)
    assert col0 % tn == 0
    b0 = col0 // tn
    if transposed:
        w_block = (None, tn, k)
        w_map = lambda j, i, off: (layer, off + j, 0)
    else:
        w_block = (None, k, tn)
        w_map = lambda j, i, off: (layer, 0, off + j)
    w_specs = [pl.BlockSpec(w_block, functools.partial(w_map, off=b0 + g * group_stride)) for g in range(n_groups)]
    n_out = len(outs)

    def out_for(rows, row_tile, row_map, n, dt, by_rows):
        if not by_rows:
            return jax.ShapeDtypeStruct((rows, n), dt), pl.BlockSpec((row_tile, tn), row_map)
        assert n_tiles == 1 and n == tn
        per = n // LANES
        return (jax.ShapeDtypeStruct((rows * per, LANES), dt),
                pl.BlockSpec((row_tile * per, LANES), lambda j, i: (row_map(j, i)[0], 0)))

    prompt_outs = [out_for(m, tm, lambda j, i: (i, j), *o) for o in outs]
    sample_outs = [out_for(s, s, lambda j, i: (0, j), n, F32 if sample_f32 else dt, by_rows)
                   for n, dt, by_rows in outs]
    all_outs = prompt_outs + sample_outs
    operands = [x, xs] + [w] * n_groups
    in_specs = [pl.BlockSpec((tm, k), lambda j, i: (i, 0)), pl.BlockSpec((s, k), lambda j, i: (0, 0))] + w_specs
    n_i = m // tm
    if cast is not None:
        cw, c_layer = cast
        _, c_rows, c_cols = cw.shape
        bf16_rows = 2 * SUBLANES
        blk = next(r for r in range(bf16_rows, c_rows + 1, bf16_rows)
                   if c_rows % r == 0 and c_rows // r <= n_tiles * n_i)
        last = c_rows // blk - 1
        step = lambda j, i: jnp.minimum(j * n_i + i, last)
        in_specs.append(pl.BlockSpec((None, blk, c_cols), lambda j, i: (c_layer, step(j, i), 0)))
        operands.append(cw)
        all_outs.append((jax.ShapeDtypeStruct((c_rows, c_cols), BF16),
                         pl.BlockSpec((blk, c_cols), lambda j, i: (step(j, i), 0))))
    res = pl.pallas_call(
        functools.partial(_proj_kernel, n_groups=n_groups, n_out=n_out, epilogue=epilogue, transposed=transposed,
                          with_cast=cast is not None),
        grid=(n_tiles, n_i),
        in_specs=in_specs,
        out_specs=[spec for _, spec in all_outs],
        out_shape=[shape for shape, _ in all_outs],
        scratch_shapes=[pltpu.VMEM((n_groups,) + w_block[1:], BF16)],
        compiler_params=_cparams("arbitrary", "arbitrary"),
    )(*operands)
    return res[:n_out], res[n_out:2 * n_out], (res[2 * n_out] if cast is not None else None)


def _ep_plain(accs, outs):
    for acc, o in zip(accs, outs):
        o[...] = acc.astype(o.dtype)


def proj_plain(x, xs, w, col0, n_cols, tn, transposed=False, cast=None, prompt_dtype=F32):
    p, s, c = _proj(x, xs, w, col0=col0, tn=tn, n_tiles=n_cols // tn, group_stride=0, n_groups=1,
                    epilogue=_ep_plain, outs=[(n_cols, prompt_dtype, False)], transposed=transposed, cast=cast,
                    sample_f32=True)
    return (p[0], s[0]) if cast is None else (p[0], s[0], c)


def _ep_both_layouts(accs, outs):
    n = len(accs)
    for acc, o_tok, o_rows in zip(accs, outs[:n], outs[n:]):
        o_tok[...] = acc
        per = acc.shape[1] // LANES
        rows = acc.shape[0]
        for c in range(per):
            o_rows[pl.ds(c, rows, stride=per), :] = acc[:, c * LANES:(c + 1) * LANES]


def proj_groups(x, xs, w, col0, n_groups, group_cols, transposed=False):
    outs = [(group_cols, F32, False)] * n_groups + [(group_cols, F32, True)] * n_groups
    p, s, _ = _proj(x, xs, w, col0=col0, tn=group_cols, n_tiles=1, group_stride=1, n_groups=n_groups,
                    epilogue=_ep_both_layouts, outs=outs, tm_want=512, transposed=transposed)
    return (p[:n_groups], p[n_groups:]), (s[:n_groups], s[n_groups:])


def _ep_swiglu(accs, outs):
    g, u = accs
    outs[0][...] = (g * _sigmoid(g) * u).astype(BF16)


def proj_swiglu(x, xs, w, ff, cast):
    tn = 512
    p, s, c = _proj(x, xs, w, col0=0, tn=tn, n_tiles=ff // tn, group_stride=ff // tn, n_groups=2,
                    epilogue=_ep_swiglu, outs=[(ff, BF16, False)], cast=cast)
    return p[0], s[0], c


def _ep_conv_in(accs, outs):
    bg, cg, u = accs
    outs[0][...] = bg
    outs[1][...] = cg * u


def proj_conv_in(x, xs, w, d, cast):
    tn = 512
    return _proj(x, xs, w, col0=0, tn=tn, n_tiles=d // tn, group_stride=d // tn, n_groups=3,
                 epilogue=_ep_conv_in, outs=[(d, F32, False), (d, F32, False)], cast=cast)


def _conv_gated(bg_ref, z_ref, zp_ref, cw_ref, T):
    tm = z_ref.shape[0]
    z = z_ref[...]
    zc = jnp.concatenate([zp_ref[...], z], axis=0)
    tpos = (pl.program_id(0) * tm + _iota((tm, 1), 0)) % T
    y = cw_ref[CONV_W - 1:CONV_W, :] * z
    for back in range(1, CONV_W):
        zb = zc[SUBLANES - back:SUBLANES - back + tm, :]
        y = y + cw_ref[CONV_W - 1 - back:CONV_W - back, :] * jnp.where(tpos >= back, zb, 0.0)
    return (bg_ref[...] * y).astype(BF16)


def _out_kernel(*refs, emit_next, conv_T):
    if conv_T:
        h = _conv_gated(*refs[:4], conv_T)
        rows_of = lambda rows: h[rows, :]
        refs = refs[4:]
    else:
        h_ref = refs[0]
        rows_of = lambda rows: h_ref[rows, :]
        refs = refs[1:]
    hs_ref, w_ref, r_ref, rs_ref, nwa_ref, nwb_ref, *out_refs = refs

    def finalize(y, r, x_ref, xn_ref):
        x_new = r + _rms(y) * nwa_ref[...]
        x_ref[...] = x_new
        if emit_next:
            xn_ref[...] = (_rms(x_new) * nwb_ref[...]).astype(BF16)

    if emit_next:
        x_ref, xn_ref, xs_ref, xsn_ref = out_refs
    else:
        (x_ref, xs_ref), xn_ref, xsn_ref = out_refs, None, None

    @pl.when(pl.program_id(0) == 0)
    def _():
        finalize(_dot(hs_ref[...], w_ref[...]), rs_ref[...], xs_ref, xsn_ref)

    tm = r_ref.shape[0]
    sub = math.gcd(tm, OUT_SUB_ROWS)
    for r0 in range(0, tm, sub):
        rows = slice(r0, r0 + sub)
        finalize(_dot(rows_of(rows), w_ref[...]), r_ref[rows, :], x_ref.at[rows, :],
                 xn_ref.at[rows, :] if emit_next else None)


def proj_out(h, hs, w, resid, resid_s, nw_post, nw_next, emit_next=True, conv=None):
    m, d = resid.shape
    k = w.shape[0]
    s = hs.shape[0]
    tm = _row_tile(m, 2 * OUT_SUB_ROWS)
    h_bytes = tm * k * (2 if conv is None else 8)
    conv_tmp = 0 if conv is None else h_bytes
    vmem = k * d * 2 + 2 * (h_bytes + tm * d * (4 + 4 + (2 if emit_next else 0))) + conv_tmp + OUT_VMEM_SLACK
    row = lambda i: (i, 0)
    fixed = lambda i: (0, 0)
    if conv is None:
        h_ops, h_specs, conv_T = [h], [pl.BlockSpec((tm, k), row)], 0
    else:
        bg, z, conv_w, conv_T = conv
        assert conv_T % tm == 0
        per = tm // SUBLANES
        h_ops = [bg, z, z, conv_w]
        h_specs = [pl.BlockSpec((tm, k), row), pl.BlockSpec((tm, k), row),
                   pl.BlockSpec((SUBLANES, k), lambda i: (jnp.maximum(i * per - 1, 0), 0)),
                   pl.BlockSpec((CONV_W, k), fixed)]
    f32_out = [jax.ShapeDtypeStruct((m, d), F32), jax.ShapeDtypeStruct((s, d), F32)]
    bf16_out = [jax.ShapeDtypeStruct((m, d), BF16), jax.ShapeDtypeStruct((s, d), BF16)]
    specs = [pl.BlockSpec((tm, d), row), pl.BlockSpec((s, d), fixed)]
    if emit_next:
        out_shape = [f32_out[0], bf16_out[0], f32_out[1], bf16_out[1]]
        out_specs = [specs[0], specs[0], specs[1], specs[1]]
    else:
        out_shape, out_specs = f32_out, specs
    res = pl.pallas_call(
        functools.partial(_out_kernel, emit_next=emit_next, conv_T=conv_T),
        grid=(m // tm,),
        in_specs=h_specs + [pl.BlockSpec((s, k), fixed),
                            pl.BlockSpec((k, d), fixed, pipeline_mode=pl.Buffered(1)),
                            pl.BlockSpec((tm, d), row),
                            pl.BlockSpec((s, d), fixed),
                            pl.BlockSpec((1, d), fixed),
                            pl.BlockSpec((1, d), fixed)],
        out_specs=out_specs,
        out_shape=out_shape,
        compiler_params=_cparams("arbitrary", vmem=vmem),
    )(*h_ops, hs, w, resid, resid_s, nw_post.reshape(1, d), nw_next.reshape(1, d))
    if emit_next:
        return (res[0], res[2]), (res[1], res[3])
    return (res[0], res[1]), (None, None)


def _log_sigmoid(x):
    return jnp.minimum(x, 0.0) - jnp.log1p(jnp.exp(-jnp.abs(x)))


def _mlstm_kernel(q_ref, k_ref, v_ref, o_ref, g_ref, bg_ref, hn_ref, hg_ref, c_ref, n_ref, m_ref, cn_ref,
                  *, L, H, DK, DV):
    chunk = pl.program_id(1)

    @pl.when(chunk == 0)
    def _():
        cn_ref[...] = jnp.zeros_like(cn_ref)
        m_ref[...] = jnp.zeros_like(m_ref)

    gpre = g_ref[...] + bg_ref[...]
    lf = _log_sigmoid(gpre)
    rows = _iota((L, L), 0)
    cols = _iota((L, L), 1)
    causal = rows >= cols
    tril = jnp.where(causal, 1.0, 0.0).astype(BF16)
    lf_hi, lf_lo = _split_bf16(lf)
    bcum = _dot(tril, lf_hi) + _dot(tril, lf_lo)
    g_t = gpre.T
    b_t = bcum.T
    for h in range(H):
        f = FORGET_BIAS_COL + h
        a_col = bcum[:, f:f + 1]
        i_col = gpre[:, h:h + 1]
        b_row = b_t[f:f + 1, :]
        i_row = g_t[h:h + 1, :]
        m_prev = m_ref[0, h:h + 1, 0:1]
        log_d = jnp.where(causal, a_col - b_row + i_row, NEG)
        log_inter = a_col + m_prev
        m_t = jnp.maximum(log_inter, jnp.max(log_d, axis=-1, keepdims=True))
        d = jnp.exp(log_d - m_t)
        inter = jnp.exp(log_inter - m_t)
        kh = k_ref[:, h * DK:(h + 1) * DK].astype(F32) * (DK ** -0.5)
        v1 = jnp.concatenate([v_ref[:, h * DV:(h + 1) * DV], jnp.ones((L, LANES), BF16)], axis=1)
        qb = q_ref[:, h * DK:(h + 1) * DK]
        s = _dot_nt(qb, kh.astype(BF16)) * d
        cn_old = cn_ref[h]
        both = _dot(s.astype(BF16), v1) + inter * _dot(qb, cn_old.astype(BF16))
        num, den = both[:, :DV], both[:, DV:]
        scale = 1.0 / jnp.maximum(jnp.abs(den), jnp.exp(-m_t))
        hh = _rms(num * jnp.concatenate([scale] * (DV // LANES), axis=1))
        gate = _sigmoid(o_ref[:, h * DV:(h + 1) * DV])
        hg_ref[:, h * DV:(h + 1) * DV] = (hh * hn_ref[:, h * DV:(h + 1) * DV] * gate).astype(BF16)
        m_new = m_t[L - 1:L, :]
        w_col = jnp.exp(a_col[L - 1:L, :] - a_col + i_col - m_new)
        decay = jnp.exp(log_inter[L - 1:L, :] - m_new)
        cn_new = decay * cn_old + _dot_tn((kh * w_col).astype(BF16), v1)
        cn_ref[h] = cn_new
        m_ref[0, h:h + 1, :] = jnp.broadcast_to(m_new, (1, LANES))

        @pl.when(chunk == pl.num_programs(1) - 1)
        def _():
            c_ref[0, h] = cn_new[:, :DV]
            n_ref[0, h:h + 1, :] = cn_new[:, DV:].T[0:1, :]


def mlstm_prompt(qkv, o, gates, b_gates, head_norm, bsz, T):
    H = MLSTM_HEADS
    m, hdv = o.shape
    hdk = hdv // 2
    DK, DV = hdk // H, hdv // H
    L = math.gcd(T, MLSTM_CHUNK)
    nc = T // L
    row = lambda b, c: (b * nc + c, 0)
    state = lambda b, c: (b, 0, 0)
    return pl.pallas_call(
        functools.partial(_mlstm_kernel, L=L, H=H, DK=DK, DV=DV),
        grid=(bsz, nc),
        in_specs=[pl.BlockSpec((L, hdk), row),
                  pl.BlockSpec((L, hdk), lambda b, c: (b * nc + c, 1)),
                  pl.BlockSpec((L, hdv), lambda b, c: (b * nc + c, 1)),
                  pl.BlockSpec((L, hdv), row),
                  pl.BlockSpec((L, LANES), row),
                  pl.BlockSpec((1, LANES), lambda b, c: (0, 0)),
                  pl.BlockSpec((1, hdv), lambda b, c: (0, 0))],
        out_specs=[pl.BlockSpec((L, hdv), row),
                   pl.BlockSpec((1, H, DK, DV), lambda b, c: (b, 0, 0, 0)),
                   pl.BlockSpec((1, H, DK), state),
                   pl.BlockSpec((1, H, LANES), state)],
        out_shape=[jax.ShapeDtypeStruct((m, hdv), BF16),
                   jax.ShapeDtypeStruct((bsz, H, DK, DV), F32),
                   jax.ShapeDtypeStruct((bsz, H, DK), F32),
                   jax.ShapeDtypeStruct((bsz, H, LANES), F32)],
        scratch_shapes=[pltpu.VMEM((H, DK, DV + LANES), F32)],
        compiler_params=_cparams("arbitrary", "arbitrary"),
    )(qkv, qkv, qkv, o, gates, _pad_lanes(b_gates.reshape(1, -1)), head_norm.reshape(1, hdv))


def _pad_lanes(x, width=LANES):
    return jnp.pad(x, ((0, 0), (0, width - x.shape[-1])))


def _to_col(row, n):
    eye = _iota((n, n), 0) == _iota((n, n), 1)
    return jnp.sum(jnp.where(eye, jnp.broadcast_to(row, (n, n)), 0.0), axis=-1, keepdims=True)


def _mlstm_step_kernel(x_ref, g_ref, bg_ref, hn_ref, c0_ref, n0_ref, m0_ref, hg_ref, c_ref, n_ref, m_ref,
                       *, H, DK, DV):
    x = x_ref[0]
    gpre = g_ref[0] + bg_ref[...]
    lf = _log_sigmoid(gpre)
    hdk = H * DK
    hdv = H * DV
    for h in range(H):
        f = FORGET_BIAS_COL + h
        ig = gpre[:, h:h + 1]
        m_prev = m0_ref[0, h:h + 1, :]
        log_inter = lf[:, f:f + 1] + m_prev
        m_t = jnp.maximum(log_inter, ig)
        d = jnp.exp(ig - m_t)
        inter = jnp.exp(log_inter - m_t)
        q = x[:, h * DK:(h + 1) * DK]
        k = x[:, hdk + h * DK:hdk + (h + 1) * DK] * (DK ** -0.5)
        v = x[:, 2 * hdk + h * DV:2 * hdk + (h + 1) * DV]
        og = x[:, 2 * hdk + hdv + h * DV:2 * hdk + hdv + (h + 1) * DV]
        c_old = c0_ref[0, h]
        n_old = n0_ref[0, h:h + 1, :]
        s = jnp.sum(q * k, axis=-1, keepdims=True) * d
        q_col = _to_col(q, DK)
        k_col = _to_col(k, DK)
        num = s * v + inter * jnp.sum(q_col * c_old, axis=0, keepdims=True)
        den = s + inter * jnp.sum(q * n_old, axis=-1, keepdims=True)
        hh = _rms(num / jnp.maximum(jnp.abs(den), jnp.exp(-m_t)))
        hg_ref[0, :, h * DV:(h + 1) * DV] = (hh * hn_ref[:, h * DV:(h + 1) * DV] * _sigmoid(og)).astype(BF16)
        c_ref[0, h] = inter * c_old + d * (k_col * v)
        n_ref[0, h:h + 1, :] = inter * n_old + d * k
        m_ref[0, h:h + 1, :] = m_t


def mlstm_step(qkvo, gates, b_gates, head_norm, c0, n0, m0):
    bsz, H, DK, DV = c0.shape
    wid = qkvo.shape[1]
    hdv = H * DV
    one = lambda b: (b, 0, 0)
    hg, c1, n1, m1 = pl.pallas_call(
        functools.partial(_mlstm_step_kernel, H=H, DK=DK, DV=DV),
        grid=(bsz,),
        in_specs=[pl.BlockSpec((1, 1, wid), one),
                  pl.BlockSpec((1, 1, LANES), one),
                  pl.BlockSpec((1, LANES), lambda b: (0, 0)),
                  pl.BlockSpec((1, hdv), lambda b: (0, 0)),
                  pl.BlockSpec((1, H, DK, DV), lambda b: (b, 0, 0, 0)),
                  pl.BlockSpec((1, H, DK), one),
                  pl.BlockSpec((1, H, 1), one)],
        out_specs=[pl.BlockSpec((1, 1, hdv), one),
                   pl.BlockSpec((1, H, DK, DV), lambda b: (b, 0, 0, 0)),
                   pl.BlockSpec((1, H, DK), one),
                   pl.BlockSpec((1, H, 1), one)],
        out_shape=[jax.ShapeDtypeStruct((bsz, 1, hdv), BF16),
                   jax.ShapeDtypeStruct((bsz, H, DK, DV), F32),
                   jax.ShapeDtypeStruct((bsz, H, DK), F32),
                   jax.ShapeDtypeStruct((bsz, H, 1), F32)],
        compiler_params=_cparams("parallel"),
    )(qkvo.reshape(bsz, 1, wid), gates.reshape(bsz, 1, LANES), _pad_lanes(b_gates.reshape(1, -1)),
      head_norm.reshape(1, hdv), c0, n0, m0.reshape(bsz, H, 1))
    return hg.reshape(bsz, hdv), c1, n1, m1.reshape(bsz, H)


def _conv_step_kernel(bg_ref, z_ref, buf_ref, cw_ref, o_ref):
    y = cw_ref[CONV_W - 1:CONV_W, :] * z_ref[...]
    for j in range(CONV_W - 1):
        y = y + cw_ref[j:j + 1, :] * buf_ref[j]
    o_ref[...] = bg_ref[...] * y


def conv_gate_step(bg, z, buf, conv_w):
    bsz, d = z.shape
    full = lambda: (0, 0)
    return pl.pallas_call(
        _conv_step_kernel,
        in_specs=[pl.BlockSpec((bsz, d), full), pl.BlockSpec((bsz, d), full),
                  pl.BlockSpec((CONV_W - 1, bsz, d), lambda: (0, 0, 0)), pl.BlockSpec((CONV_W, d), full)],
        out_specs=pl.BlockSpec((bsz, d), full),
        out_shape=jax.ShapeDtypeStruct((bsz, d), F32),
    )(bg, z, buf, conv_w)


def _gelu(x):
    return 0.5 * x * (1.0 + jnp.tanh(math.sqrt(2.0 / math.pi) * (x + 0.044715 * x * x * x)))


def _cmp_finish(a, posw, b1, w2, b2):
    hid = a.shape[1] // 2
    a1 = a[:, hid:]
    a1_next = jnp.concatenate([a1[1:], a1[:1]], axis=0)
    pre = a[:, :hid] + a1_next + b1 + posw[0:1, :hid] + posw[1:2, hid:]
    return _dot(_gelu(pre).astype(BF16), w2) + b2


def _cmp_prompt_kernel(k_ref, v_ref, w1_ref, pos_ref, b1_ref, w2_ref, b2_ref, ck_ref, cv_ref, *, nch):
    for idx, (x_ref, o_ref) in enumerate(((k_ref, ck_ref), (v_ref, cv_ref))):
        w1 = w1_ref[idx]
        lhs = jnp.concatenate([x_ref[pl.ds(p, nch, stride=CMP_STRIDE), :] for p in range(CMP_STRIDE)], axis=1)
        a = _dot(lhs.astype(BF16), w1)
        posw = _dot(pos_ref[idx], w1)
        o_ref[0, 0] = _cmp_finish(a, posw, b1_ref[idx], w2_ref[idx], b2_ref[idx])


def _cmp_weights(cmp_pos, cmp_w1, cmp_b1, cmp_w2, cmp_b2):
    two, ratio, stride, hd, hid = cmp_w1.shape
    w1 = cmp_w1.transpose(0, 2, 3, 1, 4).reshape(two, stride * hd, ratio * hid).astype(BF16)
    pos = cmp_pos.reshape(two, ratio, stride * hd)
    pos = jnp.pad(pos, ((0, 0), (0, SUBLANES - ratio), (0, 0))).astype(BF16)
    return w1, pos, cmp_b1.reshape(two, 1, hid), cmp_w2.astype(BF16), cmp_b2.reshape(two, 1, hd)


def cmp_prompt(kc, vc, bsz, T, cw):
    w1, pos, b1, w2, b2 = cw
    nch = T // CMP_STRIDE
    G = NSA_KV
    out = jax.ShapeDtypeStruct((bsz, G, nch, NSA_HD), F32)
    ospec = pl.BlockSpec((1, 1, nch, NSA_HD), lambda b, g: (b, g, 0, 0))
    whole = lambda a: pl.BlockSpec(a.shape, lambda b, g: (0,) * a.ndim)
    seq = pl.BlockSpec((T, NSA_HD), lambda b, g: (b, g))
    return pl.pallas_call(
        functools.partial(_cmp_prompt_kernel, nch=nch),
        grid=(bsz, G),
        in_specs=[seq, seq, whole(w1), whole(pos), whole(b1), whole(w2), whole(b2)],
        out_specs=[ospec, ospec],
        out_shape=[out, out],
        compiler_params=_cparams("parallel", "parallel"),
    )(kc, vc, w1, pos, b1, w2, b2)


def _overlap_t(nbs_pad, nbc_pad):
    j = _iota((nbs_pad, nbc_pad), 0)
    n = _iota((nbs_pad, nbc_pad), 1)
    lo = jnp.maximum(n * CMP_STRIDE, j * SEL_BLOCK)
    hi = jnp.minimum(n * CMP_STRIDE + CMP_BLOCK, j * SEL_BLOCK + SEL_BLOCK)
    return (jnp.maximum(hi - lo, 0) // CMP_STRIDE).astype(F32)


def _rank_select(score, n_sel, axis):
    n = score.shape[axis]
    idx = _iota(score.shape, axis)
    cnt = jnp.zeros(score.shape, F32)
    for jp in range(n):
        other = lax.slice_in_dim(score, jp, jp + 1, axis=axis)
        tie = jnp.where(idx > jp, 1.0, 0.0)
        cnt = cnt + jnp.where(other > score, 1.0, jnp.where(other == score, tie, 0.0))
    return jnp.where(cnt < n_sel, 1.0, 0.0)


def _attn_first(state, s, vt_b):
    m_ref, l_ref, acc_ref = state
    m = jnp.max(s, axis=0, keepdims=True)
    p = jnp.exp2(s - m)
    m_ref[...] = m
    l_ref[...] = jnp.sum(p, axis=0, keepdims=True)
    acc_ref[...] = _dot(vt_b, p.astype(BF16))


def _attn_more(state, s, vt_b):
    m_ref, l_ref, acc_ref = state
    m_old = m_ref[...]
    m_new = jnp.maximum(m_old, jnp.max(s, axis=0, keepdims=True))
    alpha = jnp.exp2(m_old - m_new)
    p = jnp.exp2(s - m_new)
    m_ref[...] = m_new
    l_ref[...] = alpha * l_ref[...] + jnp.sum(p, axis=0, keepdims=True)
    acc_ref[...] = alpha * acc_ref[...] + _dot(vt_b, p.astype(BF16))


def _attn_empty(state):
    m_ref, l_ref, acc_ref = state
    m_ref[...] = jnp.full(m_ref.shape, NEG, F32)
    l_ref[...] = jnp.zeros_like(l_ref)
    acc_ref[...] = jnp.zeros_like(acc_ref)


def _attn_merged(state_a, state_b):
    (ma_ref, la_ref, acca_ref), (mb_ref, lb_ref, accb_ref) = state_a, state_b
    m = jnp.maximum(ma_ref[...], mb_ref[...])
    wa = jnp.exp2(ma_ref[...] - m)
    wb = jnp.exp2(mb_ref[...] - m)
    return (wa * acca_ref[...] + wb * accb_ref[...]) / (wa * la_ref[...] + wb * lb_ref[...])


def _nsa_attn_kernel(q_ref, ck_ref, cv_ref, ks_ref, vs_ref, kw_ref, vw_ref, g_ref, bg_ref, o_ref,
                     ksb_ref, vst_ref, kwb_ref, vwt_ref, *states, tq, hpg, nbc, nbs, nbs_pad):
    qi = pl.program_id(2)
    t0 = qi * tq
    cols = hpg * tq
    tk = tq
    n_kt = ks_ref.shape[0] // tk
    nbc_pad = ck_ref.shape[2]

    @pl.when(qi == 0)
    def _():
        ksb_ref[...] = ks_ref[...].astype(BF16)
        kwb_ref[...] = kw_ref[...].astype(BF16)
        for kt in range(n_kt):
            vst_ref[kt] = vs_ref[kt * tk:(kt + 1) * tk, :].T.astype(BF16)
            vwt_ref[kt] = vw_ref[kt * tk:(kt + 1) * tk, :].T.astype(BF16)

    qb = (jnp.concatenate([q_ref[:, h * NSA_HD:(h + 1) * NSA_HD] for h in range(hpg)], axis=0)
          * (NSA_HD ** -0.5 * math.log2(math.e))).astype(BF16)

    n = _iota((nbc_pad, cols), 0)
    t_col = t0 + jnp.concatenate([_iota((nbc_pad, tq), 1)] * hpg, axis=1)
    cmask = jnp.where(n < nbc, jnp.where(n * CMP_STRIDE + (CMP_BLOCK - 1) <= t_col, 1.0, 0.0), 0.0)
    sc = jnp.where(cmask > 0.5, _dot_nt(ck_ref[0, 0].astype(BF16), qb), NEG)
    pe = jnp.exp2(sc - jnp.max(sc, axis=0, keepdims=True))
    p_cmp = pe / jnp.sum(pe, axis=0, keepdims=True) * cmask
    o_cmp = _dot(cv_ref[0, 0].T.astype(BF16), p_cmp.astype(BF16))

    p_sum = p_cmp[:, 0:tq]
    for h in range(1, hpg):
        p_sum = p_sum + p_cmp[:, h * tq:(h + 1) * tq]
    ov_t = _overlap_t(nbs_pad, nbc_pad).astype(BF16)
    p_hi, p_lo = _split_bf16(p_sum)
    imp_t = _dot(ov_t, p_hi) + _dot(ov_t, p_lo)
    j = _iota((nbs_pad, tq), 0)
    t = t0 + _iota((nbs_pad, tq), 1)
    t_blk = t // SEL_BLOCK
    forced = jnp.where(j == 0, 1.0, jnp.where(j == t_blk, 1.0, jnp.where(j == t_blk - 1, 1.0, 0.0)))
    visible = jnp.where(j < nbs, jnp.where(j * SEL_BLOCK <= t, 1.0, 0.0), 0.0)
    score = jnp.where(visible > 0.5, jnp.where(forced > 0.5, BIG, imp_t), NEG)
    sel_t = _rank_select(score, min(N_SELECT, nbs), axis=0).astype(BF16)

    k_ofs = _iota((tk, tq), 0)
    q_ofs = _iota((tk, tq), 1)
    causal = k_ofs <= q_ofs
    per_head = lambda bias: jnp.concatenate([bias] * hpg, axis=1)

    def scores(kb_ref, kt):
        k_b = kb_ref[pl.ds(pl.multiple_of(kt * tk, tk), tk), :]
        return _dot_nt(k_b, qb)

    def slc_bias(kt):
        blk = (kt * tk + _iota((tk, nbs_pad), 0)) // SEL_BLOCK
        expand = jnp.where(blk == _iota((tk, nbs_pad), 1), 1.0, 0.0).astype(BF16)
        return (_dot(expand, sel_t) - 1.0) * BIG

    slc_a, slc_b, win_a, win_b = states[0:3], states[3:6], states[6:9], states[9:12]

    def slc_tile(kt):
        return scores(ksb_ref, kt) + per_head(slc_bias(kt)), vst_ref[kt]

    def win_tile(back):
        s = scores(kwb_ref, qi - back)
        if back == n_back:
            s = s + per_head(jnp.where(k_ofs >= q_ofs, 0.0, NEG))
        return s, vwt_ref[qi - back]

    n_back = WINDOW // tk
    _attn_first(slc_a, scores(ksb_ref, qi) + per_head(jnp.where(causal, slc_bias(qi), NEG)), vst_ref[qi])
    _attn_first(win_a, scores(kwb_ref, qi) + per_head(jnp.where(causal, 0.0, NEG)), vwt_ref[qi])
    for st in (slc_b, win_b):
        _attn_empty(st)

    for back in range(1, n_back + 1, 2):
        both = back + 1 <= n_back

        @pl.when(qi >= back + 1 if both else qi >= back)
        def _():
            _attn_more(win_b, *win_tile(back))
            if both:
                _attn_more(win_a, *win_tile(back + 1))

        if both:
            @pl.when(qi == back)
            def _():
                _attn_more(win_b, *win_tile(back))

    def slc_pair(pair, _):
        _attn_more(slc_a, *slc_tile(2 * pair))
        _attn_more(slc_b, *slc_tile(2 * pair + 1))
        return 0

    lax.fori_loop(0, lax.shift_right_logical(qi, 1), slc_pair, 0)

    @pl.when(lax.bitwise_and(qi, 1) == 1)
    def _():
        _attn_more(slc_a, *slc_tile(qi - 1))

    o_slc = _attn_merged(slc_a, slc_b)
    o_win = _attn_merged(win_a, win_b)
    gate_t = _sigmoid(g_ref[...] + bg_ref[...]).T
    for h in range(hpg):
        sl = slice(h * tq, (h + 1) * tq)
        o_t = (gate_t[3 * h:3 * h + 1, :] * o_cmp[:, sl] + gate_t[3 * h + 1:3 * h + 2, :] * o_slc[:, sl]
               + gate_t[3 * h + 2:3 * h + 3, :] * o_win[:, sl])
        o_ref[:, h * NSA_HD:(h + 1) * NSA_HD] = o_t.T.astype(BF16)


def nsa_attn_prompt(q, ck, cv, ks, vs, kw, vw, gates, b_gate, bsz, T):
    G, HPG = NSA_KV, NSA_HPG
    tq = math.gcd(T, ATT_TILE)
    assert WINDOW % tq == 0
    nq = T // tq
    nch = T // CMP_STRIDE
    nbc = nch - CMP_BLOCK // CMP_STRIDE + 1
    nbs = -(-T // SEL_BLOCK)
    nbs_pad = -(-nbs // SUBLANES) * SUBLANES
    cols = HPG * tq
    kv_scratch = [pltpu.VMEM((T, NSA_HD), BF16), pltpu.VMEM((nq, NSA_HD, tq), BF16)] * 2
    stats = [pltpu.VMEM((1, cols), F32), pltpu.VMEM((1, cols), F32), pltpu.VMEM((NSA_HD, cols), F32)] * 4
    seq = pl.BlockSpec((T, NSA_HD), lambda b, g, i: (b, g))
    cspec = pl.BlockSpec((1, 1, nch, NSA_HD), lambda b, g, i: (b, g, 0, 0))
    qspec = pl.BlockSpec((tq, HPG * NSA_HD), lambda b, g, i: (b * nq + i, g))
    return pl.pallas_call(
        functools.partial(_nsa_attn_kernel, tq=tq, hpg=HPG, nbc=nbc, nbs=nbs, nbs_pad=nbs_pad),
        grid=(bsz, G, nq),
        in_specs=[qspec, cspec, cspec, seq, seq, seq, seq,
                  pl.BlockSpec((tq, LANES), lambda b, g, i: (b * nq + i, g)),
                  pl.BlockSpec((1, LANES), lambda b, g, i: (0, g))],
        out_specs=qspec,
        out_shape=jax.ShapeDtypeStruct((bsz * T, G * HPG * NSA_HD), BF16),
        scratch_shapes=kv_scratch + stats,
        compiler_params=_cparams("arbitrary", "arbitrary", "arbitrary"),
    )(q, ck, cv, ks, vs, kw, vw, gates, b_gate)


CHUNK_ROWS = CMP_STRIDE * NSA_KV
CHUNK_PITCH = CHUNK_ROWS + SUBLANES


def _dec_cmp_kernel(tbl_ref, *refs, P, nbc, nbs, nbs_pad, p0):
    k_pages, v_pages = refs[:P], refs[P:2 * P]
    w1_ref, q_ref, pos_ref, b1_ref, w2_ref, b2_ref, o_ref, sel_ref, pad_ref, ak_ref, av_ref = refs[2 * P:]
    cpp = LANES // CMP_STRIDE
    step = pl.program_id(1)
    for idx, (pages, a_ref) in enumerate(((k_pages, ak_ref), (v_pages, av_ref))):
        for i, pg in enumerate(pages):
            for c in range(cpp):
                pad_ref[idx * P + i, c * CHUNK_PITCH:c * CHUNK_PITCH + CHUNK_ROWS, :] = (
                    pg[0, c * CHUNK_ROWS:(c + 1) * CHUNK_ROWS, :])
        blocks = []
        for g in range(NSA_KV):
            for i in range(P):
                blocks.append(jnp.concatenate(
                    [pad_ref[idx * P + i, pl.ds(p * NSA_KV + g, cpp, stride=CHUNK_PITCH), :]
                     for p in range(CMP_STRIDE)], axis=1))
        a = _dot(jnp.concatenate(blocks, axis=0).astype(BF16), w1_ref[idx])
        per_g = P * cpp
        for g in range(NSA_KV):
            a_ref[g, pl.ds(pl.multiple_of(step * per_g, per_g), per_g), :] = a[g * per_g:(g + 1) * per_g]

    @pl.when(step == pl.num_programs(1) - 1)
    def _():
        _dec_cmp_finish(ak_ref, av_ref, q_ref, pos_ref, w1_ref, b1_ref, w2_ref, b2_ref, o_ref, sel_ref,
                        nbc=nbc, nbs=nbs, nbs_pad=nbs_pad, p0=p0)


def _dec_cmp_finish(ak_ref, av_ref, q_ref, pos_ref, w1_ref, b1_ref, w2_ref, b2_ref, o_ref, sel_ref,
                    *, nbc, nbs, nbs_pad, p0):
    nch = ak_ref.shape[1]
    for g in range(NSA_KV):
        cks = []
        for idx, a_ref in enumerate((ak_ref, av_ref)):
            posw = _dot(pos_ref[idx], w1_ref[idx])
            cks.append(_cmp_finish(a_ref[g], posw, b1_ref[idx], w2_ref[idx], b2_ref[idx]))
        ck, cv = cks
        qb = (q_ref[0, g] * (NSA_HD ** -0.5)).astype(BF16)
        n = _iota((SUBLANES, nch), 1)
        cmask = jnp.where(n < nbc, jnp.where(n * CMP_STRIDE + (CMP_BLOCK - 1) <= p0, 1.0, 0.0), 0.0)
        sc = jnp.where(cmask > 0.5, _dot_nt(qb, ck.astype(BF16)), NEG)
        pe = jnp.exp(sc - jnp.max(sc, axis=-1, keepdims=True))
        p_cmp = pe / jnp.sum(pe, axis=-1, keepdims=True) * cmask
        o_ref[0, g] = _dot(p_cmp.astype(BF16), cv.astype(BF16))

        head = jnp.where(_iota((SUBLANES, nch), 0) < NSA_HPG, p_cmp, 0.0)
        p_sum = jnp.broadcast_to(jnp.sum(head, axis=0, keepdims=True), (SUBLANES, nch))
        ov_t = _overlap_t(nbs_pad, nch).astype(BF16)
        p_hi, p_lo = _split_bf16(p_sum)
        imp = _dot_nt(p_hi, ov_t) + _dot_nt(p_lo, ov_t)
        j = _iota((SUBLANES, nbs_pad), 1)
        t_blk = p0 // SEL_BLOCK
        forced = jnp.where(j == 0, 1.0, jnp.where(j == t_blk, 1.0, jnp.where(j == t_blk - 1, 1.0, 0.0)))
        visible = jnp.where(j < nbs, jnp.where(j * SEL_BLOCK <= p0, 1.0, 0.0), 0.0)
        score = jnp.where(visible > 0.5, jnp.where(forced > 0.5, BIG, imp), NEG)
        s_row = jnp.broadcast_to(score[0:1], (nbs_pad, nbs_pad))
        s_col = jnp.concatenate([jnp.broadcast_to(score[0:1], (LANES, nbs_pad)).T] * (nbs_pad // LANES), axis=1)
        jr = _iota((nbs_pad, nbs_pad), 0)
        jc = _iota((nbs_pad, nbs_pad), 1)
        tie = jnp.where(jc < jr, 1.0, 0.0)
        beats = jnp.where(s_row > s_col, 1.0, jnp.where(s_row == s_col, tie, 0.0))
        rank = jnp.sum(beats, axis=-1, keepdims=True)
        slot = _iota((nbs_pad, LANES), 1).astype(F32)
        blk = _iota((nbs_pad, LANES), 0).astype(F32)
        picked = jnp.sum(jnp.where(rank == slot, blk, 0.0), axis=0, keepdims=True)
        sel_ref[0, g:g + 1, :] = picked.astype(jnp.int32)


def dec_cmp(pool_k, pool_v, table, q, cw, p0):
    w1, pos, b1, w2, b2 = cw
    n_pool, page, G, hd = pool_k.shape
    assert page == LANES and G == NSA_KV and hd == NSA_HD
    bsz, n_pages = table.shape
    assert p0 == n_pages * page
    P = math.gcd(n_pages, PAGES_PER_STEP)
    cpp = page // CMP_STRIDE
    nch = n_pages * cpp
    hid2 = w1.shape[2]
    nbc = nch - CMP_BLOCK // CMP_STRIDE + 1
    nbs = -(-(p0 + 1) // SEL_BLOCK)
    nbs_pad = -(-nbs // LANES) * LANES
    pk = pool_k.reshape(n_pool, page * G, hd)
    pv = pool_v.reshape(n_pool, page * G, hd)
    pspec = lambda i: pl.BlockSpec((1, page * G, hd),
                                   functools.partial(lambda b, s, tbl, i: (tbl[b * n_pages + s * P + i], 0, 0), i=i))
    whole = lambda a: pl.BlockSpec(a.shape, lambda b, s, tbl: (0,) * a.ndim)
    per_seq = lambda shape: pl.BlockSpec((1,) + shape, lambda b, s, tbl: (b,) + (0,) * len(shape))
    grid_spec = pltpu.PrefetchScalarGridSpec(
        num_scalar_prefetch=1,
        grid=(bsz, n_pages // P),
        in_specs=[pspec(i) for i in range(P)] * 2
        + [whole(w1), per_seq((G, SUBLANES, hd)), whole(pos), whole(b1), whole(w2), whole(b2)],
        out_specs=[per_seq((G, SUBLANES, hd)), per_seq((G, LANES))],
        scratch_shapes=[pltpu.VMEM((2 * P, cpp * CHUNK_PITCH, hd), F32),
                        pltpu.VMEM((G, nch, hid2), F32), pltpu.VMEM((G, nch, hid2), F32)],
    )
    return pl.pallas_call(
        functools.partial(_dec_cmp_kernel, P=P, nbc=nbc, nbs=nbs, nbs_pad=nbs_pad, p0=p0),
        grid_spec=grid_spec,
        out_shape=[jax.ShapeDtypeStruct((bsz, G, SUBLANES, hd), F32),
                   jax.ShapeDtypeStruct((bsz, G, LANES), jnp.int32)],
        compiler_params=_cparams("arbitrary", "arbitrary"),
    )(table.reshape(-1), *([pk] * P), *([pv] * P), w1, q, pos, b1, w2, b2)


def _softmax_with_new_key(qb, k_b, v_b, bias, s_new, v_new):
    s = _dot_nt(qb, k_b)
    if bias is not None:
        s = s + bias
    mx = jnp.maximum(jnp.max(s, axis=-1, keepdims=True), s_new)
    p = jnp.exp(s - mx)
    p_new = jnp.exp(s_new - mx)
    return (_dot(p.astype(BF16), v_b) + p_new * v_new) / (jnp.sum(p, axis=-1, keepdims=True) + p_new)


def _dec_attn_kernel(sel_ref, tbl_ref, *refs, n_cache_blocks, n_sel):
    n_blk = NSA_KV * n_sel
    ks_refs, vs_refs = refs[:n_blk], refs[n_blk:2 * n_blk]
    q_ref, new_ref, kw_ref, vw_ref, oc_ref, g_ref, bg_ref, o_ref = refs[2 * n_blk:]
    b = pl.program_id(0)
    slot = _iota((SUBLANES, n_sel * SEL_BLOCK), 1) // SEL_BLOCK
    head = _iota((SUBLANES, LANES), 0)
    lane = _iota((SUBLANES, LANES), 1)
    n_win = kw_ref.shape[1] // NSA_KV
    for g in range(NSA_KV):
        q = q_ref[0, g] * (NSA_HD ** -0.5)
        qb = q.astype(BF16)
        new = new_ref[0, g]

        mine = lambda r, n: r[pl.ds(g, n, stride=NSA_KV), :]
        picks = range(g * n_sel, (g + 1) * n_sel)
        k_all = jnp.concatenate([mine(ks_refs[i].at[0, 0], SEL_BLOCK) for i in picks], axis=0).astype(BF16)
        v_all = jnp.concatenate([mine(vs_refs[i].at[0, 0], SEL_BLOCK) for i in picks], axis=0).astype(BF16)
        bias = jnp.zeros((SUBLANES, n_sel * SEL_BLOCK), F32)
        for s in range(n_sel):
            blk = sel_ref[(b * NSA_KV + g) * n_sel + s]
            bias = jnp.where(slot == s, jnp.where(blk < n_cache_blocks, 0.0, NEG), bias)
        o_slc = _softmax_with_new_key(qb, k_all, v_all, bias,
                                      jnp.sum(q * new[0:1], axis=-1, keepdims=True), new[1:2])

        o_win = _softmax_with_new_key(qb, mine(kw_ref.at[0], n_win).astype(BF16),
                                      mine(vw_ref.at[0], n_win).astype(BF16), None,
                                      jnp.sum(q * new[2:3], axis=-1, keepdims=True), new[3:4])

        gate = jnp.broadcast_to(_sigmoid(g_ref[0, g] + bg_ref[:, g * LANES:(g + 1) * LANES]), (SUBLANES, LANES))
        o = jnp.zeros((SUBLANES, NSA_HD), F32)
        for br, ob in enumerate((oc_ref[0, g], o_slc, o_win)):
            o = o + jnp.sum(jnp.where(lane == 3 * head + br, gate, 0.0), axis=-1, keepdims=True) * ob
        o_ref[0, g] = o


def dec_attn(sel, table, pool_ks, pool_vs, q, new, kw_past, vw_past, o_cmp, gates, b_gate):
    n_pool, page, G, hd = pool_ks.shape
    bsz, n_pages = table.shape
    halves = page // SEL_BLOCK
    wb = kw_past.shape[1]
    assert wb <= WINDOW
    n_sel = sel.shape[-1]
    pk = pool_ks.reshape(n_pool, halves, SEL_BLOCK * G, hd)
    pv = pool_vs.reshape(n_pool, halves, SEL_BLOCK * G, hd)
    n_cache_blocks = n_pages * halves

    def page_map(b, sel_ref, tbl_ref, i):
        blk = jnp.minimum(sel_ref[b * G * n_sel + i], n_cache_blocks - 1)
        return (tbl_ref[b * n_pages + blk // halves], blk % halves, 0, 0)

    per_seq = lambda shape: pl.BlockSpec((1,) + shape, lambda b, sel_ref, tbl_ref: (b,) + (0,) * len(shape))
    small = per_seq((G, SUBLANES, hd))
    blocks = [pl.BlockSpec((1, 1, SEL_BLOCK * G, hd), functools.partial(page_map, i=i)) for i in range(G * n_sel)]
    grid_spec = pltpu.PrefetchScalarGridSpec(
        num_scalar_prefetch=2,
        grid=(bsz,),
        in_specs=blocks + blocks + [small, small, per_seq((wb * G, hd)), per_seq((wb * G, hd)), small,
                                    per_seq((G, 1, LANES)),
                                    pl.BlockSpec((1, G * LANES), lambda b, sel_ref, tbl_ref: (0, 0))],
        out_specs=small,
    )
    return pl.pallas_call(
        functools.partial(_dec_attn_kernel, n_cache_blocks=n_cache_blocks, n_sel=n_sel),
        grid_spec=grid_spec,
        out_shape=jax.ShapeDtypeStruct((bsz, G, SUBLANES, hd), F32),
        compiler_params=_cparams("arbitrary"),
    )(sel.reshape(-1), table.reshape(-1), *([pk] * (G * n_sel)), *([pv] * (G * n_sel)), q, new,
      kw_past.reshape(bsz, wb * G, hd), vw_past.reshape(bsz, wb * G, hd), o_cmp,
      gates.reshape(bsz, G, 1, LANES), b_gate)


def _gate_weights(wt, layer, row0, n):
    rows = lax.slice(wt, (layer, row0, 0), (layer + 1, row0 + n, wt.shape[2]))
    return jnp.pad(rows, ((0, 0), (0, LANES - n), (0, 0))), 0


def _nsa_gate_weights(wt, layer, b_gate, row0):
    per = 3 * NSA_HPG
    k = wt.shape[2]
    rows = lax.slice(wt, (layer, row0, 0), (layer + 1, row0 + NSA_KV * per, k)).reshape(NSA_KV, per, k)
    wg = jnp.pad(rows, ((0, 0), (0, LANES - per), (0, 0))).reshape(1, NSA_KV * LANES, k)
    bg = jnp.pad(b_gate.reshape(NSA_KV, per), ((0, 0), (0, LANES - per))).reshape(1, NSA_KV * LANES)
    return (wg, 0), bg


def kernel(x_prompt, x_sample, state_a_C, state_a_n, state_a_m, state_b_conv, cache_c_k_cmp, cache_c_v_cmp, cache_c_k_slc, cache_c_v_slc, cache_c_k_win, cache_c_v_win, page_table, norm_w, ffn_w_in, ffn_w_out, a_w_in, a_b_gates, a_head_norm, a_w_out, b_w_in, b_conv_w, b_w_out, c_w_in, c_b_gate, c_cmp_pos, c_cmp_w1, c_cmp_b1, c_cmp_w2, c_cmp_b2, c_w_out):
    bp, T, D = x_prompt.shape
    bs = x_sample.shape[0]
    assert x_sample.shape[1] == 1
    depth = norm_w.shape[0]
    G, hd = NSA_KV, NSA_HD
    past_len = page_table.shape[1] * cache_c_k_cmp.shape[2]
    xp = x_prompt.reshape(bp * T, D)
    xs = x_sample.reshape(bs, D)
    hp_in = norm_cast(xp, norm_w[0, 0])
    hs_in = norm_cast(xs, norm_w[0, 0])
    a_wt = jnp.swapaxes(a_w_in, 1, 2)
    c_wt = jnp.swapaxes(c_w_in, 1, 2)
    w_outs = {0: a_w_out, 1: b_w_out, 2: c_w_out}
    pa, sa, pb, sb, pc, sc = [], [], [], [], [], []
    for i in range(depth):
        kind, j = i % 3, i // 3
        nw = norm_w[i]
        w_out_f32 = (w_outs[kind], j)
        conv = None
        if kind == 0:
            hdv = a_head_norm.shape[1]
            n_main = 3 * hdv
            wg = _gate_weights(a_wt, j, n_main, 2 * MLSTM_HEADS)
            qkv_p, qkv_s, w_out = proj_plain(hp_in, hs_in, (a_wt, j), 0, 2 * hdv, 1024, transposed=True,
                                             cast=w_out_f32, prompt_dtype=BF16)
            o_p, o_s = proj_plain(hp_in, hs_in, (a_wt, j), 2 * hdv, hdv, 1024, transposed=True)
            qkvo_s = jnp.concatenate([qkv_s, o_s], axis=1)
            g_p, g_s = proj_plain(hp_in, hs_in, wg, 0, LANES, LANES, transposed=True)
            hp, c_p, n_p, m_p = mlstm_prompt(qkv_p, o_p, g_p, a_b_gates[j], a_head_norm[j], bp, T)
            hs, c_s, n_s, m_s = mlstm_step(qkvo_s, g_s, a_b_gates[j], a_head_norm[j],
                                           state_a_C[j], state_a_n[j], state_a_m[j])
            pa.append((c_p, n_p, m_p[:, :, 0]))
            sa.append((c_s, n_s, m_s))
        elif kind == 1:
            (bg_p, z_p), (bg_s, z_s), w_out = proj_conv_in(hp_in, hs_in, (b_w_in, j), D, w_out_f32)
            hp, conv = None, (bg_p, z_p, b_conv_w[j], T)
            buf = state_b_conv[j]
            hs = conv_gate_step(bg_s, z_s, buf.transpose(1, 0, 2), b_conv_w[j]).astype(BF16)
            pb.append(z_p.reshape(bp, T, D)[:, T - (CONV_W - 1):])
            sb.append(jnp.concatenate([buf, z_s[:, None]], axis=1)[:, -(CONV_W - 1):])
        else:
            nq_cols = NSA_HPG * G * hd
            nheads = G * NSA_HPG
            wg, bgate = _nsa_gate_weights(c_wt, j, c_b_gate[j], nq_cols + 6 * G * hd)
            cw = _cmp_weights(c_cmp_pos[j], c_cmp_w1[j], c_cmp_b1[j], c_cmp_w2[j], c_cmp_b2[j])
            q_p, q_s, w_out = proj_plain(hp_in, hs_in, (c_wt, j), 0, nq_cols, 1024, transposed=True, cast=w_out_f32)
            kv_p, kv_rows_p, kv_s = [], [], []
            for half in range(2):
                (tok_p, rows_p), (tok_s, _) = proj_groups(hp_in, hs_in, (c_wt, j), nq_cols + half * 3 * G * hd,
                                                         3, G * hd, transposed=True)
                kv_p, kv_rows_p, kv_s = kv_p + list(tok_p), kv_rows_p + list(rows_p), kv_s + list(tok_s)
            gt_p, gt_s = proj_plain(hp_in, hs_in, wg, 0, G * LANES, G * LANES, transposed=True)
            ck, cv = cmp_prompt(kv_p[0], kv_p[1], bp, T, cw)
            hp = nsa_attn_prompt(q_p, ck, cv, kv_p[2], kv_p[3], kv_p[4], kv_p[5], gt_p, bgate, bp, T)
            kv_rows_p = [a.reshape(bp, T, G, hd) for a in kv_rows_p]
            keep = min(WINDOW, T)
            pc.append(tuple(kv_rows_p[:4]) + tuple(a[:, T - keep:] for a in kv_rows_p[4:]))
            q_s = jnp.pad(q_s.reshape(bs, G, NSA_HPG, hd), ((0, 0), (0, 0), (0, SUBLANES - NSA_HPG), (0, 0)))
            kv_s = jnp.stack([a.reshape(bs, G, hd) for a in kv_s], axis=1)
            new = jnp.pad(kv_s[:, 2:6].transpose(0, 2, 1, 3), ((0, 0), (0, 0), (0, SUBLANES - 4), (0, 0)))
            o_cmp, ranked = dec_cmp(cache_c_k_cmp[j], cache_c_v_cmp[j], page_table, q_s, cw, past_len)
            n_sel = min(N_SELECT, -(-(past_len + 1) // SEL_BLOCK))
            o_s = dec_attn(ranked[:, :, :n_sel], page_table, cache_c_k_slc[j], cache_c_v_slc[j], q_s, new,
                           cache_c_k_win[j], cache_c_v_win[j], o_cmp, gt_s, bgate)
            hs = o_s[:, :, :NSA_HPG].reshape(bs, nheads * hd).astype(BF16)
            keep_s = min(WINDOW, cache_c_k_win.shape[2] + 1)
            win = [jnp.concatenate([c[j], kv_s[:, a][:, None]], axis=1)[:, -keep_s:]
                   for c, a in ((cache_c_k_win, 4), (cache_c_v_win, 5))]
            sc.append(tuple(kv_s[:, a][:, None] for a in range(4)) + tuple(win))
        last = i == depth - 1
        nw_next = norm_w[i + 1, 0] if not last else nw[0]
        (xp, xs), (hp_mid, hs_mid) = proj_out(hp, hs, w_out, xp, xs, nw[1], nw[2], conv=conv)
        act_p, act_s, f_out = proj_swiglu(hp_mid, hs_mid, (ffn_w_in, i), ffn_w_out.shape[1], (ffn_w_out, i))
        (xp, xs), (hp_in, hs_in) = proj_out(act_p, act_s, f_out, xp, xs, nw[3], nw_next, not last)

    stack = lambda items: [jnp.stack(a) for a in zip(*items)]
    p_a, s_a = stack(pa), stack(sa)
    p_c, s_c = stack(pc), stack(sc)
    return (xp.reshape(bp, T, D), xs.reshape(bs, 1, D), *p_a, *s_a, jnp.stack(pb), jnp.stack(sb), *p_c, *s_c)
```

```python
import functools
import math

import jax
import jax.numpy as jnp
from jax import lax
from jax.experimental import pallas as pl
from jax.experimental.pallas import tpu as pltpu

F32 = jnp.float32
BF16 = jnp.bfloat16

EPS = 1e-6
NEG = -1e30
BIG = 1e30

LANES = 128
SUBLANES = 8
VMEM_LIMIT = 56 * 1024 * 1024

MLSTM_HEADS = 8
MLSTM_CHUNK = 512
FORGET_BIAS_COL = MLSTM_HEADS
CONV_W = 3
NSA_HD = 128
NSA_KV = 4
NSA_HPG = 4
CMP_BLOCK = 32
CMP_STRIDE = 16
SEL_BLOCK = 64
N_SELECT = 16
WINDOW = 512
ATT_TILE = 512
PAGES_PER_STEP = 16
OUT_SUB_ROWS = 256
OUT_VMEM_SLACK = 6 * 1024 * 1024


def _cparams(*sem, vmem=VMEM_LIMIT):
    return pltpu.CompilerParams(dimension_semantics=sem, vmem_limit_bytes=vmem)


def _dot(a, b):
    return jnp.dot(a, b, preferred_element_type=F32)


def _dot_nt(a, b):
    return lax.dot_general(a, b, (((1,), (1,)), ((), ())), preferred_element_type=F32)


def _dot_tn(a, b):
    return lax.dot_general(a, b, (((0,), (0,)), ((), ())), preferred_element_type=F32)


def _split_bf16(x):
    hi = x.astype(BF16)
    lo = (x - hi.astype(F32)).astype(BF16)
    return hi, lo


def _iota(shape, dim):
    return lax.broadcasted_iota(jnp.int32, shape, dim)


def _sigmoid(x):
    return 1.0 / (1.0 + jnp.exp(-x))


def _rms(x):
    return x * lax.rsqrt(jnp.mean(x * x, axis=-1, keepdims=True) + EPS)


def _row_tile(m, want):
    t = min(m, want)
    assert m % t == 0, (m, t)
    return t


def _norm_kernel(x_ref, w_ref, o_ref):
    o_ref[...] = (_rms(x_ref[...]) * w_ref[...]).astype(BF16)


def norm_cast(x, w):
    m, d = x.shape
    tm = _row_tile(m, 512)
    return pl.pallas_call(
        _norm_kernel,
        grid=(m // tm,),
        in_specs=[pl.BlockSpec((tm, d), lambda i: (i, 0)), pl.BlockSpec((1, d), lambda i: (0, 0))],
        out_specs=pl.BlockSpec((tm, d), lambda i: (i, 0)),
        out_shape=jax.ShapeDtypeStruct((m, d), BF16),
        compiler_params=_cparams("parallel"),
    )(x, w.reshape(1, d))


def _proj_kernel(x_ref, xs_ref, *refs, n_groups, n_out, epilogue, transposed, with_cast):
    w_refs = refs[:n_groups]
    refs = refs[n_groups:]
    if with_cast:
        cast_in_ref, refs = refs[0], refs[1:]
        refs[2 * n_out][...] = cast_in_ref[...].astype(BF16)
    out_refs = refs[:n_out]
    sample_out_refs = refs[n_out:2 * n_out]
    wb_ref = refs[-1]
    mm = _dot_nt if transposed else _dot

    @pl.when(pl.program_id(1) == 0)
    def _():
        for g in range(n_groups):
            wb_ref[g] = w_refs[g][...].astype(BF16)
        xs = xs_ref[...]
        epilogue([mm(xs, wb_ref[g]) for g in range(n_groups)], sample_out_refs)

    x = x_ref[...]
    epilogue([mm(x, wb_ref[g]) for g in range(n_groups)], out_refs)


def _proj(x, xs, w, *, col0, tn, n_tiles, group_stride, n_groups, epilogue, outs, tm_want=1024, transposed=False,
          cast=None, sample_f32=False):
    w, layer = w
    m, k = x.shape
    s = xs.shape[0]
    tm = _row_tile(m, tm_want)
    assert col0 % tn == 0
    b0 = col0 // tn
    if transposed:
        w_block = (None, tn, k)
        w_map = lambda j, i, off: (layer, off + j, 0)
    else:
        w_block = (None, k, tn)
        w_map = lambda j, i, off: (layer, 0, off + j)
    w_specs = [pl.BlockSpec(w_block, functools.partial(w_map, off=b0 + g * group_stride)) for g in range(n_groups)]
    n_out = len(outs)

    def out_for(rows, row_tile, row_map, n, dt, by_rows):
        if not by_rows:
            return jax.ShapeDtypeStruct((rows, n), dt), pl.BlockSpec((row_tile, tn), row_map)
        assert n_tiles == 1 and n == tn
        per = n // LANES
        return (jax.ShapeDtypeStruct((rows * per, LANES), dt),
                pl.BlockSpec((row_tile * per, LANES), lambda j, i: (row_map(j, i)[0], 0)))

    prompt_outs = [out_for(m, tm, lambda j, i: (i, j), *o) for o in outs]
    sample_outs = [out_for(s, s, lambda j, i: (0, j), n, F32 if sample_f32 else dt, by_rows)
                   for n, dt, by_rows in outs]
    all_outs = prompt_outs + sample_outs
    operands = [x, xs] + [w] * n_groups
    in_specs = [pl.BlockSpec((tm, k), lambda j, i: (i, 0)), pl.BlockSpec((s, k), lambda j, i: (0, 0))] + w_specs
    n_i = m // tm
    if cast is not None:
        cw, c_layer = cast
        _, c_rows, c_cols = cw.shape
        bf16_rows = 2 * SUBLANES
        blk = next(r for r in range(bf16_rows, c_rows + 1, bf16_rows)
                   if c_rows % r == 0 and c_rows // r <= n_tiles * n_i)
        last = c_rows // blk - 1
        step = lambda j, i: jnp.minimum(j * n_i + i, last)
        in_specs.append(pl.BlockSpec((None, blk, c_cols), lambda j, i: (c_layer, step(j, i), 0)))
        operands.append(cw)
        all_outs.append((jax.ShapeDtypeStruct((c_rows, c_cols), BF16),
                         pl.BlockSpec((blk, c_cols), lambda j, i: (step(j, i), 0))))
    res = pl.pallas_call(
        functools.partial(_proj_kernel, n_groups=n_groups, n_out=n_out, epilogue=epilogue, transposed=transposed,
                          with_cast=cast is not None),
        grid=(n_tiles, n_i),
        in_specs=in_specs,
        out_specs=[spec for _, spec in all_outs],
        out_shape=[shape for shape, _ in all_outs],
        scratch_shapes=[pltpu.VMEM((n_groups,) + w_block[1:], BF16)],
        compiler_params=_cparams("arbitrary", "arbitrary"),
    )(*operands)
    return res[:n_out], res[n_out:2 * n_out], (res[2 * n_out] if cast is not None else None)


def _ep_plain(accs, outs):
    for acc, o in zip(accs, outs):
        o[...] = acc.astype(o.dtype)


def proj_plain(x, xs, w, col0, n_cols, tn, transposed=False, cast=None, prompt_dtype=F32):
    p, s, c = _proj(x, xs, w, col0=col0, tn=tn, n_tiles=n_cols // tn, group_stride=0, n_groups=1,
                    epilogue=_ep_plain, outs=[(n_cols, prompt_dtype, False)], transposed=transposed, cast=cast,
                    sample_f32=True)
    return (p[0], s[0]) if cast is None else (p[0], s[0], c)


def _ep_both_layouts(accs, outs):
    n = len(accs)
    for acc, o_tok, o_rows in zip(accs, outs[:n], outs[n:]):
        o_tok[...] = acc
        per = acc.shape[1] // LANES
        rows = acc.shape[0]
        for c in range(per):
            o_rows[pl.ds(c, rows, stride=per), :] = acc[:, c * LANES:(c + 1) * LANES]


def proj_groups(x, xs, w, col0, n_groups, group_cols, transposed=False):
    outs = [(group_cols, F32, False)] * n_groups + [(group_cols, F32, True)] * n_groups
    p, s, _ = _proj(x, xs, w, col0=col0, tn=group_cols, n_tiles=1, group_stride=1, n_groups=n_groups,
                    epilogue=_ep_both_layouts, outs=outs, tm_want=512, transposed=transposed)
    return (p[:n_groups], p[n_groups:]), (s[:n_groups], s[n_groups:])


def _ep_swiglu(accs, outs):
    g, u = accs
    outs[0][...] = (g * _sigmoid(g) * u).astype(BF16)


def proj_swiglu(x, xs, w, ff, cast):
    tn = 512
    p, s, c = _proj(x, xs, w, col0=0, tn=tn, n_tiles=ff // tn, group_stride=ff // tn, n_groups=2,
                    epilogue=_ep_swiglu, outs=[(ff, BF16, False)], cast=cast)
    return p[0], s[0], c


def _ep_conv_in(accs, outs):
    bg, cg, u = accs
    outs[0][...] = bg
    outs[1][...] = cg * u


def proj_conv_in(x, xs, w, d, cast):
    tn = 512
    return _proj(x, xs, w, col0=0, tn=tn, n_tiles=d // tn, group_stride=d // tn, n_groups=3,
                 epilogue=_ep_conv_in, outs=[(d, F32, False), (d, F32, False)], cast=cast)


def _conv_gated(bg_ref, z_ref, zp_ref, cw_ref, T):
    tm = z_ref.shape[0]
    z = z_ref[...]
    zc = jnp.concatenate([zp_ref[...], z], axis=0)
    tpos = (pl.program_id(0) * tm + _iota((tm, 1), 0)) % T
    y = cw_ref[CONV_W - 1:CONV_W, :] * z
    for back in range(1, CONV_W):
        zb = zc[SUBLANES - back:SUBLANES - back + tm, :]
        y = y + cw_ref[CONV_W - 1 - back:CONV_W - back, :] * jnp.where(tpos >= back, zb, 0.0)
    return (bg_ref[...] * y).astype(BF16)


def _out_kernel(*refs, emit_next, conv_T):
    if conv_T:
        h = _conv_gated(*refs[:4], conv_T)
        rows_of = lambda rows: h[rows, :]
        refs = refs[4:]
    else:
        h_ref = refs[0]
        rows_of = lambda rows: h_ref[rows, :]
        refs = refs[1:]
    hs_ref, w_ref, r_ref, rs_ref, nwa_ref, nwb_ref, *out_refs = refs

    def finalize(y, r, x_ref, xn_ref):
        x_new = r + _rms(y) * nwa_ref[...]
        x_ref[...] = x_new
        if emit_next:
            xn_ref[...] = (_rms(x_new) * nwb_ref[...]).astype(BF16)

    if emit_next:
        x_ref, xn_ref, xs_ref, xsn_ref = out_refs
    else:
        (x_ref, xs_ref), xn_ref, xsn_ref = out_refs, None, None

    @pl.when(pl.program_id(0) == 0)
    def _():
        finalize(_dot(hs_ref[...], w_ref[...]), rs_ref[...], xs_ref, xsn_ref)

    tm = r_ref.shape[0]
    sub = math.gcd(tm, OUT_SUB_ROWS)
    for r0 in range(0, tm, sub):
        rows = slice(r0, r0 + sub)
        finalize(_dot(rows_of(rows), w_ref[...]), r_ref[rows, :], x_ref.at[rows, :],
                 xn_ref.at[rows, :] if emit_next else None)


def proj_out(h, hs, w, resid, resid_s, nw_post, nw_next, emit_next=True, conv=None):
    m, d = resid.shape
    k = w.shape[0]
    s = hs.shape[0]
    tm = _row_tile(m, 2 * OUT_SUB_ROWS)
    h_bytes = tm * k * (2 if conv is None else 8)
    conv_tmp = 0 if conv is None else h_bytes
    vmem = k * d * 2 + 2 * (h_bytes + tm * d * (4 + 4 + (2 if emit_next else 0))) + conv_tmp + OUT_VMEM_SLACK
    row = lambda i: (i, 0)
    fixed = lambda i: (0, 0)
    if conv is None:
        h_ops, h_specs, conv_T = [h], [pl.BlockSpec((tm, k), row)], 0
    else:
        bg, z, conv_w, conv_T = conv
        assert conv_T % tm == 0
        per = tm // SUBLANES
        h_ops = [bg, z, z, conv_w]
        h_specs = [pl.BlockSpec((tm, k), row), pl.BlockSpec((tm, k), row),
                   pl.BlockSpec((SUBLANES, k), lambda i: (jnp.maximum(i * per - 1, 0), 0)),
                   pl.BlockSpec((CONV_W, k), fixed)]
    f32_out = [jax.ShapeDtypeStruct((m, d), F32), jax.ShapeDtypeStruct((s, d), F32)]
    bf16_out = [jax.ShapeDtypeStruct((m, d), BF16), jax.ShapeDtypeStruct((s, d), BF16)]
    specs = [pl.BlockSpec((tm, d), row), pl.BlockSpec((s, d), fixed)]
    if emit_next:
        out_shape = [f32_out[0], bf16_out[0], f32_out[1], bf16_out[1]]
        out_specs = [specs[0], specs[0], specs[1], specs[1]]
    else:
        out_shape, out_specs = f32_out, specs
    res = pl.pallas_call(
        functools.partial(_out_kernel, emit_next=emit_next, conv_T=conv_T),
        grid=(m // tm,),
        in_specs=h_specs + [pl.BlockSpec((s, k), fixed),
                            pl.BlockSpec((k, d), fixed, pipeline_mode=pl.Buffered(1)),
                            pl.BlockSpec((tm, d), row),
                            pl.BlockSpec((s, d), fixed),
                            pl.BlockSpec((1, d), fixed),
                            pl.BlockSpec((1, d), fixed)],
        out_specs=out_specs,
        out_shape=out_shape,
        compiler_params=_cparams("arbitrary", vmem=vmem),
    )(*h_ops, hs, w, resid, resid_s, nw_post.reshape(1, d), nw_next.reshape(1, d))
    if emit_next:
        return (res[0], res[2]), (res[1], res[3])
    return (res[0], res[1]), (None, None)


def _log_sigmoid(x):
    return jnp.minimum(x, 0.0) - jnp.log1p(jnp.exp(-jnp.abs(x)))


def _mlstm_kernel(q_ref, k_ref, v_ref, o_ref, g_ref, bg_ref, hn_ref, hg_ref, c_ref, n_ref, m_ref, cn_ref,
                  *, L, H, DK, DV):
    chunk = pl.program_id(1)

    @pl.when(chunk == 0)
    def _():
        cn_ref[...] = jnp.zeros_like(cn_ref)
        m_ref[...] = jnp.zeros_like(m_ref)

    gpre = g_ref[...] + bg_ref[...]
    lf = _log_sigmoid(gpre)
    rows = _iota((L, L), 0)
    cols = _iota((L, L), 1)
    causal = rows >= cols
    tril = jnp.where(causal, 1.0, 0.0).astype(BF16)
    lf_hi, lf_lo = _split_bf16(lf)
    bcum = _dot(tril, lf_hi) + _dot(tril, lf_lo)
    g_t = gpre.T
    b_t = bcum.T
    for h in range(H):
        f = FORGET_BIAS_COL + h
        a_col = bcum[:, f:f + 1]
        i_col = gpre[:, h:h + 1]
        b_row = b_t[f:f + 1, :]
        i_row = g_t[h:h + 1, :]
        m_prev = m_ref[0, h:h + 1, 0:1]
        log_d = jnp.where(causal, a_col - b_row + i_row, NEG)
        log_inter = a_col + m_prev
        m_t = jnp.maximum(log_inter, jnp.max(log_d, axis=-1, keepdims=True))
        d = jnp.exp(log_d - m_t)
        inter = jnp.exp(log_inter - m_t)
        kh = k_ref[:, h * DK:(h + 1) * DK].astype(F32) * (DK ** -0.5)
        v1 = jnp.concatenate([v_ref[:, h * DV:(h + 1) * DV], jnp.ones((L, LANES), BF16)], axis=1)
        qb = q_ref[:, h * DK:(h + 1) * DK]
        s = _dot_nt(qb, kh.astype(BF16)) * d
        cn_old = cn_ref[h]
        both = _dot(s.astype(BF16), v1) + inter * _dot(qb, cn_old.astype(BF16))
        num, den = both[:, :DV], both[:, DV:]
        scale = 1.0 / jnp.maximum(jnp.abs(den), jnp.exp(-m_t))
        hh = _rms(num * jnp.concatenate([scale] * (DV // LANES), axis=1))
        gate = _sigmoid(o_ref[:, h * DV:(h + 1) * DV])
        hg_ref[:, h * DV:(h + 1) * DV] = (hh * hn_ref[:, h * DV:(h + 1) * DV] * gate).astype(BF16)
        m_new = m_t[L - 1:L, :]
        w_col = jnp.exp(a_col[L - 1:L, :] - a_col + i_col - m_new)
        decay = jnp.exp(log_inter[L - 1:L, :] - m_new)
        cn_new = decay * cn_old + _dot_tn((kh * w_col).astype(BF16), v1)
        cn_ref[h] = cn_new
        m_ref[0, h:h + 1, :] = jnp.broadcast_to(m_new, (1, LANES))

        @pl.when(chunk == pl.num_programs(1) - 1)
        def _():
            c_ref[0, h] = cn_new[:, :DV]
            n_ref[0, h:h + 1, :] = cn_new[:, DV:].T[0:1, :]


def mlstm_prompt(qkv, o, gates, b_gates, head_norm, bsz, T):
    H = MLSTM_HEADS
    m, hdv = o.shape
    hdk = hdv // 2
    DK, DV = hdk // H, hdv // H
    L = math.gcd(T, MLSTM_CHUNK)
    nc = T // L
    row = lambda b, c: (b * nc + c, 0)
    state = lambda b, c: (b, 0, 0)
    return pl.pallas_call(
        functools.partial(_mlstm_kernel, L=L, H=H, DK=DK, DV=DV),
        grid=(bsz, nc),
        in_specs=[pl.BlockSpec((L, hdk), row),
                  pl.BlockSpec((L, hdk), lambda b, c: (b * nc + c, 1)),
                  pl.BlockSpec((L, hdv), lambda b, c: (b * nc + c, 1)),
                  pl.BlockSpec((L, hdv), row),
                  pl.BlockSpec((L, LANES), row),
                  pl.BlockSpec((1, LANES), lambda b, c: (0, 0)),
                  pl.BlockSpec((1, hdv), lambda b, c: (0, 0))],
        out_specs=[pl.BlockSpec((L, hdv), row),
                   pl.BlockSpec((1, H, DK, DV), lambda b, c: (b, 0, 0, 0)),
                   pl.BlockSpec((1, H, DK), state),
                   pl.BlockSpec((1, H, LANES), state)],
        out_shape=[jax.ShapeDtypeStruct((m, hdv), BF16),
                   jax.ShapeDtypeStruct((bsz, H, DK, DV), F32),
                   jax.ShapeDtypeStruct((bsz, H, DK), F32),
                   jax.ShapeDtypeStruct((bsz, H, LANES), F32)],
        scratch_shapes=[pltpu.VMEM((H, DK, DV + LANES), F32)],
        compiler_params=_cparams("arbitrary", "arbitrary"),
    )(qkv, qkv, qkv, o, gates, _pad_lanes(b_gates.reshape(1, -1)), head_norm.reshape(1, hdv))


def _pad_lanes(x, width=LANES):
    return jnp.pad(x, ((0, 0), (0, width - x.shape[-1])))


def _to_col(row, n):
    eye = _iota((n, n), 0) == _iota((n, n), 1)
    return jnp.sum(jnp.where(eye, jnp.broadcast_to(row, (n, n)), 0.0), axis=-1, keepdims=True)


def _mlstm_step_kernel(x_ref, g_ref, bg_ref, hn_ref, c0_ref, n0_ref, m0_ref, hg_ref, c_ref, n_ref, m_ref,
                       *, H, DK, DV):
    x = x_ref[0]
    gpre = g_ref[0] + bg_ref[...]
    lf = _log_sigmoid(gpre)
    hdk = H * DK
    hdv = H * DV
    for h in range(H):
        f = FORGET_BIAS_COL + h
        ig = gpre[:, h:h + 1]
        m_prev = m0_ref[0, h:h + 1, :]
        log_inter = lf[:, f:f + 1] + m_prev
        m_t = jnp.maximum(log_inter, ig)
        d = jnp.exp(ig - m_t)
        inter = jnp.exp(log_inter - m_t)
        q = x[:, h * DK:(h + 1) * DK]
        k = x[:, hdk + h * DK:hdk + (h + 1) * DK] * (DK ** -0.5)
        v = x[:, 2 * hdk + h * DV:2 * hdk + (h + 1) * DV]
        og = x[:, 2 * hdk + hdv + h * DV:2 * hdk + hdv + (h + 1) * DV]
        c_old = c0_ref[0, h]
        n_old = n0_ref[0, h:h + 1, :]
        s = jnp.sum(q * k, axis=-1, keepdims=True) * d
        q_col = _to_col(q, DK)
        k_col = _to_col(k, DK)
        num = s * v + inter * jnp.sum(q_col * c_old, axis=0, keepdims=True)
        den = s + inter * jnp.sum(q * n_old, axis=-1, keepdims=True)
        hh = _rms(num / jnp.maximum(jnp.abs(den), jnp.exp(-m_t)))
        hg_ref[0, :, h * DV:(h + 1) * DV] = (hh * hn_ref[:, h * DV:(h + 1) * DV] * _sigmoid(og)).astype(BF16)
        c_ref[0, h] = inter * c_old + d * (k_col * v)
        n_ref[0, h:h + 1, :] = inter * n_old + d * k
        m_ref[0, h:h + 1, :] = m_t


def mlstm_step(qkvo, gates, b_gates, head_norm, c0, n0, m0):
    bsz, H, DK, DV = c0.shape
    wid = qkvo.shape[1]
    hdv = H * DV
    one = lambda b: (b, 0, 0)
    hg, c1, n1, m1 = pl.pallas_call(
        functools.partial(_mlstm_step_kernel, H=H, DK=DK, DV=DV),
        grid=(bsz,),
        in_specs=[pl.BlockSpec((1, 1, wid), one),
                  pl.BlockSpec((1, 1, LANES), one),
                  pl.BlockSpec((1, LANES), lambda b: (0, 0)),
                  pl.BlockSpec((1, hdv), lambda b: (0, 0)),
                  pl.BlockSpec((1, H, DK, DV), lambda b: (b, 0, 0, 0)),
                  pl.BlockSpec((1, H, DK), one),
                  pl.BlockSpec((1, H, 1), one)],
        out_specs=[pl.BlockSpec((1, 1, hdv), one),
                   pl.BlockSpec((1, H, DK, DV), lambda b: (b, 0, 0, 0)),
                   pl.BlockSpec((1, H, DK), one),
                   pl.BlockSpec((1, H, 1), one)],
        out_shape=[jax.ShapeDtypeStruct((bsz, 1, hdv), BF16),
                   jax.ShapeDtypeStruct((bsz, H, DK, DV), F32),
                   jax.ShapeDtypeStruct((bsz, H, DK), F32),
                   jax.ShapeDtypeStruct((bsz, H, 1), F32)],
        compiler_params=_cparams("parallel"),
    )(qkvo.reshape(bsz, 1, wid), gates.reshape(bsz, 1, LANES), _pad_lanes(b_gates.reshape(1, -1)),
      head_norm.reshape(1, hdv), c0, n0, m0.reshape(bsz, H, 1))
    return hg.reshape(bsz, hdv), c1, n1, m1.reshape(bsz, H)


def _conv_step_kernel(bg_ref, z_ref, buf_ref, cw_ref, o_ref):
    y = cw_ref[CONV_W - 1:CONV_W, :] * z_ref[...]
    for j in range(CONV_W - 1):
        y = y + cw_ref[j:j + 1, :] * buf_ref[j]
    o_ref[...] = bg_ref[...] * y


def conv_gate_step(bg, z, buf, conv_w):
    bsz, d = z.shape
    full = lambda: (0, 0)
    return pl.pallas_call(
        _conv_step_kernel,
        in_specs=[pl.BlockSpec((bsz, d), full), pl.BlockSpec((bsz, d), full),
                  pl.BlockSpec((CONV_W - 1, bsz, d), lambda: (0, 0, 0)), pl.BlockSpec((CONV_W, d), full)],
        out_specs=pl.BlockSpec((bsz, d), full),
        out_shape=jax.ShapeDtypeStruct((bsz, d), F32),
    )(bg, z, buf, conv_w)


def _gelu(x):
    return 0.5 * x * (1.0 + jnp.tanh(math.sqrt(2.0 / math.pi) * (x + 0.044715 * x * x * x)))


def _cmp_finish(a, posw, b1, w2, b2):
    hid = a.shape[1] // 2
    a1 = a[:, hid:]
    a1_next = jnp.concatenate([a1[1:], a1[:1]], axis=0)
    pre = a[:, :hid] + a1_next + b1 + posw[0:1, :hid] + posw[1:2, hid:]
    return _dot(_gelu(pre).astype(BF16), w2) + b2


def _cmp_prompt_kernel(k_ref, v_ref, w1_ref, pos_ref, b1_ref, w2_ref, b2_ref, ck_ref, cv_ref, *, nch):
    for idx, (x_ref, o_ref) in enumerate(((k_ref, ck_ref), (v_ref, cv_ref))):
        w1 = w1_ref[idx]
        lhs = jnp.concatenate([x_ref[pl.ds(p, nch, stride=CMP_STRIDE), :] for p in range(CMP_STRIDE)], axis=1)
        a = _dot(lhs.astype(BF16), w1)
        posw = _dot(pos_ref[idx], w1)
        o_ref[0, 0] = _cmp_finish(a, posw, b1_ref[idx], w2_ref[idx], b2_ref[idx])


def _cmp_weights(cmp_pos, cmp_w1, cmp_b1, cmp_w2, cmp_b2):
    two, ratio, stride, hd, hid = cmp_w1.shape
    w1 = cmp_w1.transpose(0, 2, 3, 1, 4).reshape(two, stride * hd, ratio * hid).astype(BF16)
    pos = cmp_pos.reshape(two, ratio, stride * hd)
    pos = jnp.pad(pos, ((0, 0), (0, SUBLANES - ratio), (0, 0))).astype(BF16)
    return w1, pos, cmp_b1.reshape(two, 1, hid), cmp_w2.astype(BF16), cmp_b2.reshape(two, 1, hd)


def cmp_prompt(kc, vc, bsz, T, cw):
    w1, pos, b1, w2, b2 = cw
    nch = T // CMP_STRIDE
    G = NSA_KV
    out = jax.ShapeDtypeStruct((bsz, G, nch, NSA_HD), F32)
    ospec = pl.BlockSpec((1, 1, nch, NSA_HD), lambda b, g: (b, g, 0, 0))
    whole = lambda a: pl.BlockSpec(a.shape, lambda b, g: (0,) * a.ndim)
    seq = pl.BlockSpec((T, NSA_HD), lambda b, g: (b, g))
    return pl.pallas_call(
        functools.partial(_cmp_prompt_kernel, nch=nch),
        grid=(bsz, G),
        in_specs=[seq, seq, whole(w1), whole(pos), whole(b1), whole(w2), whole(b2)],
        out_specs=[ospec, ospec],
        out_shape=[out, out],
        compiler_params=_cparams("parallel", "parallel"),
    )(kc, vc, w1, pos, b1, w2, b2)


def _overlap_t(nbs_pad, nbc_pad):
    j = _iota((nbs_pad, nbc_pad), 0)
    n = _iota((nbs_pad, nbc_pad), 1)
    lo = jnp.maximum(n * CMP_STRIDE, j * SEL_BLOCK)
    hi = jnp.minimum(n * CMP_STRIDE + CMP_BLOCK, j * SEL_BLOCK + SEL_BLOCK)
    return (jnp.maximum(hi - lo, 0) // CMP_STRIDE).astype(F32)


def _rank_select(score, n_sel, axis):
    n = score.shape[axis]
    idx = _iota(score.shape, axis)
    cnt = jnp.zeros(score.shape, F32)
    for jp in range(n):
        other = lax.slice_in_dim(score, jp, jp + 1, axis=axis)
        tie = jnp.where(idx > jp, 1.0, 0.0)
        cnt = cnt + jnp.where(other > score, 1.0, jnp.where(other == score, tie, 0.0))
    return jnp.where(cnt < n_sel, 1.0, 0.0)


def _attn_first(state, s, vt_b):
    m_ref, l_ref, acc_ref = state
    m = jnp.max(s, axis=0, keepdims=True)
    p = jnp.exp2(s - m)
    m_ref[...] = m
    l_ref[...] = jnp.sum(p, axis=0, keepdims=True)
    acc_ref[...] = _dot(vt_b, p.astype(BF16))


def _get_cols(ref, ranges):
    return ref[...] if ranges is None else jnp.concatenate([ref[:, a:b] for a, b in ranges], axis=1)


def _set_cols(ref, ranges, val):
    if ranges is None:
        ref[...] = val
        return
    at = 0
    for a, b in ranges:
        ref[:, a:b] = val[:, at:at + b - a]
        at += b - a


def _attn_more(state, s, vt_b, ranges=None):
    m_ref, l_ref, acc_ref = state
    m_old = _get_cols(m_ref, ranges)
    m_new = jnp.maximum(m_old, jnp.max(s, axis=0, keepdims=True))
    alpha = jnp.exp2(m_old - m_new)
    p = jnp.exp2(s - m_new)
    _set_cols(m_ref, ranges, m_new)
    _set_cols(l_ref, ranges, alpha * _get_cols(l_ref, ranges) + jnp.sum(p, axis=0, keepdims=True))
    _set_cols(acc_ref, ranges, alpha * _get_cols(acc_ref, ranges) + _dot(vt_b, p.astype(BF16)))


def _attn_empty(state):
    m_ref, l_ref, acc_ref = state
    m_ref[...] = jnp.full(m_ref.shape, NEG, F32)
    l_ref[...] = jnp.zeros_like(l_ref)
    acc_ref[...] = jnp.zeros_like(acc_ref)


def _attn_merged(state_a, state_b):
    (ma_ref, la_ref, acca_ref), (mb_ref, lb_ref, accb_ref) = state_a, state_b
    m = jnp.maximum(ma_ref[...], mb_ref[...])
    wa = jnp.exp2(ma_ref[...] - m)
    wb = jnp.exp2(mb_ref[...] - m)
    return (wa * acca_ref[...] + wb * accb_ref[...]) / (wa * la_ref[...] + wb * lb_ref[...])


def _nsa_attn_kernel(q_ref, ck_ref, cv_ref, ks_ref, vs_ref, kw_ref, vw_ref, g_ref, bg_ref, o_ref,
                     ksb_ref, vst_ref, kwb_ref, vwt_ref, *states, tq, hpg, nbc, nbs, nbs_pad):
    qi = pl.program_id(2)
    t0 = qi * tq
    cols = hpg * tq
    tk = tq
    n_kt = ks_ref.shape[0] // tk
    nbc_pad = ck_ref.shape[2]

    @pl.when(qi == 0)
    def _():
        ksb_ref[...] = ks_ref[...].astype(BF16)
        kwb_ref[...] = kw_ref[...].astype(BF16)
        for kt in range(n_kt):
            vst_ref[kt] = vs_ref[kt * tk:(kt + 1) * tk, :].T.astype(BF16)
            vwt_ref[kt] = vw_ref[kt * tk:(kt + 1) * tk, :].T.astype(BF16)

    qb = (jnp.concatenate([q_ref[:, h * NSA_HD:(h + 1) * NSA_HD] for h in range(hpg)], axis=0)
          * (NSA_HD ** -0.5 * math.log2(math.e))).astype(BF16)

    n = _iota((nbc_pad, cols), 0)
    t_col = t0 + jnp.concatenate([_iota((nbc_pad, tq), 1)] * hpg, axis=1)
    cmask = jnp.where(n < nbc, jnp.where(n * CMP_STRIDE + (CMP_BLOCK - 1) <= t_col, 1.0, 0.0), 0.0)
    sc = jnp.where(cmask > 0.5, _dot_nt(ck_ref[0, 0].astype(BF16), qb), NEG)
    pe = jnp.exp2(sc - jnp.max(sc, axis=0, keepdims=True))
    p_cmp = pe / jnp.sum(pe, axis=0, keepdims=True) * cmask
    o_cmp = _dot(cv_ref[0, 0].T.astype(BF16), p_cmp.astype(BF16))

    p_sum = p_cmp[:, 0:tq]
    for h in range(1, hpg):
        p_sum = p_sum + p_cmp[:, h * tq:(h + 1) * tq]
    ov_t = _overlap_t(nbs_pad, nbc_pad).astype(BF16)
    p_hi, p_lo = _split_bf16(p_sum)
    imp_t = _dot(ov_t, p_hi) + _dot(ov_t, p_lo)
    j = _iota((nbs_pad, tq), 0)
    t = t0 + _iota((nbs_pad, tq), 1)
    t_blk = t // SEL_BLOCK
    forced = jnp.where(j == 0, 1.0, jnp.where(j == t_blk, 1.0, jnp.where(j == t_blk - 1, 1.0, 0.0)))
    visible = jnp.where(j < nbs, jnp.where(j * SEL_BLOCK <= t, 1.0, 0.0), 0.0)
    score = jnp.where(visible > 0.5, jnp.where(forced > 0.5, BIG, imp_t), NEG)
    sel_t = _rank_select(score, min(N_SELECT, nbs), axis=0).astype(BF16)

    k_ofs = _iota((tk, tq), 0)
    q_ofs = _iota((tk, tq), 1)
    causal = k_ofs <= q_ofs
    per_head = lambda bias: jnp.concatenate([bias] * hpg, axis=1)

    def scores(kb_ref, kt):
        k_b = kb_ref[pl.ds(pl.multiple_of(kt * tk, tk), tk), :]
        return _dot_nt(k_b, qb)

    half = tk // 2
    early = [(h * tq, h * tq + half) for h in range(hpg)]
    late = [(h * tq + half, (h + 1) * tq) for h in range(hpg)]
    q_of = lambda ranges: jnp.concatenate([qb[a:b] for a, b in ranges], axis=0)

    def half_scores(kb_ref, kt, key_half, q_part):
        k_b = kb_ref[pl.ds(pl.multiple_of(kt * tk + key_half * half, half), half), :]
        return _dot_nt(k_b, q_part)

    def slc_bias(kt):
        blk = (kt * tk + _iota((tk, nbs_pad), 0)) // SEL_BLOCK
        expand = jnp.where(blk == _iota((tk, nbs_pad), 1), 1.0, 0.0).astype(BF16)
        return (_dot(expand, sel_t) - 1.0) * BIG

    slc_a, slc_b, win_a, win_b = states[0:3], states[3:6], states[6:9], states[9:12]

    def slc_tile(kt):
        return scores(ksb_ref, kt) + per_head(slc_bias(kt)), vst_ref[kt]

    n_back = WINDOW // tk

    def win_more(chain, other, back):
        kt = qi - back
        if back < n_back:
            _attn_more(chain, scores(kwb_ref, kt), vwt_ref[kt])
            return
        seen = jnp.where(k_ofs >= q_ofs, 0.0, NEG)
        _attn_more(chain, half_scores(kwb_ref, kt, 1, qb) + per_head(seen[half:, :]), vwt_ref[kt][:, half:])
        _attn_more(other, half_scores(kwb_ref, kt, 0, q_of(early)) + per_head(seen[:half, :half]),
                   vwt_ref[kt][:, :half], early)

    slc_diag = jnp.where(causal, slc_bias(qi), NEG)
    win_diag = jnp.where(causal, 0.0, NEG)
    _attn_first(slc_a, half_scores(ksb_ref, qi, 0, qb) + per_head(slc_diag[:half, :]), vst_ref[qi][:, :half])
    _attn_first(win_a, half_scores(kwb_ref, qi, 0, qb) + per_head(win_diag[:half, :]), vwt_ref[qi][:, :half])
    for st in (slc_b, win_b):
        _attn_empty(st)
    q_late = q_of(late)
    _attn_more(slc_b, half_scores(ksb_ref, qi, 1, q_late) + per_head(slc_diag[half:, half:]),
               vst_ref[qi][:, half:], late)
    _attn_more(win_b, half_scores(kwb_ref, qi, 1, q_late) + per_head(win_diag[half:, half:]),
               vwt_ref[qi][:, half:], late)

    for back in range(1, n_back + 1, 2):
        both = back + 1 <= n_back

        @pl.when(qi >= back + 1 if both else qi >= back)
        def _():
            win_more(win_b, win_a, back)
            if both:
                win_more(win_a, win_b, back + 1)

        if both:
            @pl.when(qi == back)
            def _():
                win_more(win_b, win_a, back)

    def slc_pair(pair, _):
        _attn_more(slc_a, *slc_tile(2 * pair))
        _attn_more(slc_b, *slc_tile(2 * pair + 1))
        return 0

    lax.fori_loop(0, lax.shift_right_logical(qi, 1), slc_pair, 0)

    @pl.when(lax.bitwise_and(qi, 1) == 1)
    def _():
        _attn_more(slc_a, *slc_tile(qi - 1))

    o_slc = _attn_merged(slc_a, slc_b)
    o_win = _attn_merged(win_a, win_b)
    gate_t = _sigmoid(g_ref[...] + bg_ref[...]).T
    for h in range(hpg):
        sl = slice(h * tq, (h + 1) * tq)
        o_t = (gate_t[3 * h:3 * h + 1, :] * o_cmp[:, sl] + gate_t[3 * h + 1:3 * h + 2, :] * o_slc[:, sl]
               + gate_t[3 * h + 2:3 * h + 3, :] * o_win[:, sl])
        o_ref[:, h * NSA_HD:(h + 1) * NSA_HD] = o_t.T.astype(BF16)


def nsa_attn_prompt(q, ck, cv, ks, vs, kw, vw, gates, b_gate, bsz, T):
    G, HPG = NSA_KV, NSA_HPG
    tq = math.gcd(T, ATT_TILE)
    assert WINDOW % tq == 0
    nq = T // tq
    nch = T // CMP_STRIDE
    nbc = nch - CMP_BLOCK // CMP_STRIDE + 1
    nbs = -(-T // SEL_BLOCK)
    nbs_pad = -(-nbs // SUBLANES) * SUBLANES
    cols = HPG * tq
    kv_scratch = [pltpu.VMEM((T, NSA_HD), BF16), pltpu.VMEM((nq, NSA_HD, tq), BF16)] * 2
    stats = [pltpu.VMEM((1, cols), F32), pltpu.VMEM((1, cols), F32), pltpu.VMEM((NSA_HD, cols), F32)] * 4
    seq = pl.BlockSpec((T, NSA_HD), lambda b, g, i: (b, g))
    cspec = pl.BlockSpec((1, 1, nch, NSA_HD), lambda b, g, i: (b, g, 0, 0))
    qspec = pl.BlockSpec((tq, HPG * NSA_HD), lambda b, g, i: (b * nq + i, g))
    return pl.pallas_call(
        functools.partial(_nsa_attn_kernel, tq=tq, hpg=HPG, nbc=nbc, nbs=nbs, nbs_pad=nbs_pad),
        grid=(bsz, G, nq),
        in_specs=[qspec, cspec, cspec, seq, seq, seq, seq,
                  pl.BlockSpec((tq, LANES), lambda b, g, i: (b * nq + i, g)),
                  pl.BlockSpec((1, LANES), lambda b, g, i: (0, g))],
        out_specs=qspec,
        out_shape=jax.ShapeDtypeStruct((bsz * T, G * HPG * NSA_HD), BF16),
        scratch_shapes=kv_scratch + stats,
        compiler_params=_cparams("arbitrary", "arbitrary", "arbitrary"),
    )(q, ck, cv, ks, vs, kw, vw, gates, b_gate)


CHUNK_ROWS = CMP_STRIDE * NSA_KV
CHUNK_PITCH = CHUNK_ROWS + SUBLANES


def _dec_cmp_kernel(tbl_ref, *refs, P, nbc, nbs, nbs_pad, p0):
    k_pages, v_pages = refs[:P], refs[P:2 * P]
    w1_ref, q_ref, pos_ref, b1_ref, w2_ref, b2_ref, o_ref, sel_ref, pad_ref, ak_ref, av_ref = refs[2 * P:]
    cpp = LANES // CMP_STRIDE
    step = pl.program_id(1)
    for idx, (pages, a_ref) in enumerate(((k_pages, ak_ref), (v_pages, av_ref))):
        for i, pg in enumerate(pages):
            for c in range(cpp):
                pad_ref[idx * P + i, c * CHUNK_PITCH:c * CHUNK_PITCH + CHUNK_ROWS, :] = (
                    pg[0, c * CHUNK_ROWS:(c + 1) * CHUNK_ROWS, :])
        blocks = []
        for g in range(NSA_KV):
            for i in range(P):
                blocks.append(jnp.concatenate(
                    [pad_ref[idx * P + i, pl.ds(p * NSA_KV + g, cpp, stride=CHUNK_PITCH), :]
                     for p in range(CMP_STRIDE)], axis=1))
        a = _dot(jnp.concatenate(blocks, axis=0).astype(BF16), w1_ref[idx])
        per_g = P * cpp
        for g in range(NSA_KV):
            a_ref[g, pl.ds(pl.multiple_of(step * per_g, per_g), per_g), :] = a[g * per_g:(g + 1) * per_g]

    @pl.when(step == pl.num_programs(1) - 1)
    def _():
        _dec_cmp_finish(ak_ref, av_ref, q_ref, pos_ref, w1_ref, b1_ref, w2_ref, b2_ref, o_ref, sel_ref,
                        nbc=nbc, nbs=nbs, nbs_pad=nbs_pad, p0=p0)


def _dec_cmp_finish(ak_ref, av_ref, q_ref, pos_ref, w1_ref, b1_ref, w2_ref, b2_ref, o_ref, sel_ref,
                    *, nbc, nbs, nbs_pad, p0):
    nch = ak_ref.shape[1]
    for g in range(NSA_KV):
        cks = []
        for idx, a_ref in enumerate((ak_ref, av_ref)):
            posw = _dot(pos_ref[idx], w1_ref[idx])
            cks.append(_cmp_finish(a_ref[g], posw, b1_ref[idx], w2_ref[idx], b2_ref[idx]))
        ck, cv = cks
        qb = (q_ref[0, g] * (NSA_HD ** -0.5)).astype(BF16)
        n = _iota((SUBLANES, nch), 1)
        cmask = jnp.where(n < nbc, jnp.where(n * CMP_STRIDE + (CMP_BLOCK - 1) <= p0, 1.0, 0.0), 0.0)
        sc = jnp.where(cmask > 0.5, _dot_nt(qb, ck.astype(BF16)), NEG)
        pe = jnp.exp(sc - jnp.max(sc, axis=-1, keepdims=True))
        p_cmp = pe / jnp.sum(pe, axis=-1, keepdims=True) * cmask
        o_ref[0, g] = _dot(p_cmp.astype(BF16), cv.astype(BF16))

        head = jnp.where(_iota((SUBLANES, nch), 0) < NSA_HPG, p_cmp, 0.0)
        p_sum = jnp.broadcast_to(jnp.sum(head, axis=0, keepdims=True), (SUBLANES, nch))
        ov_t = _overlap_t(nbs_pad, nch).astype(BF16)
        p_hi, p_lo = _split_bf16(p_sum)
        imp = _dot_nt(p_hi, ov_t) + _dot_nt(p_lo, ov_t)
        j = _iota((SUBLANES, nbs_pad), 1)
        t_blk = p0 // SEL_BLOCK
        forced = jnp.where(j == 0, 1.0, jnp.where(j == t_blk, 1.0, jnp.where(j == t_blk - 1, 1.0, 0.0)))
        visible = jnp.where(j < nbs, jnp.where(j * SEL_BLOCK <= p0, 1.0, 0.0), 0.0)
        score = jnp.where(visible > 0.5, jnp.where(forced > 0.5, BIG, imp), NEG)
        s_row = jnp.broadcast_to(score[0:1], (nbs_pad, nbs_pad))
        s_col = jnp.concatenate([jnp.broadcast_to(score[0:1], (LANES, nbs_pad)).T] * (nbs_pad // LANES), axis=1)
        jr = _iota((nbs_pad, nbs_pad), 0)
        jc = _iota((nbs_pad, nbs_pad), 1)
        tie = jnp.where(jc < jr, 1.0, 0.0)
        beats = jnp.where(s_row > s_col, 1.0, jnp.where(s_row == s_col, tie, 0.0))
        rank = jnp.sum(beats, axis=-1, keepdims=True)
        slot = _iota((nbs_pad, LANES), 1).astype(F32)
        blk = _iota((nbs_pad, LANES), 0).astype(F32)
        picked = jnp.sum(jnp.where(rank == slot, blk, 0.0), axis=0, keepdims=True)
        sel_ref[0, g:g + 1, :] = picked.astype(jnp.int32)


def dec_cmp(pool_k, pool_v, table, q, cw, p0):
    w1, pos, b1, w2, b2 = cw
    n_pool, page, G, hd = pool_k.shape
    assert page == LANES and G == NSA_KV and hd == NSA_HD
    bsz, n_pages = table.shape
    assert p0 == n_pages * page
    P = math.gcd(n_pages, PAGES_PER_STEP)
    cpp = page // CMP_STRIDE
    nch = n_pages * cpp
    hid2 = w1.shape[2]
    nbc = nch - CMP_BLOCK // CMP_STRIDE + 1
    nbs = -(-(p0 + 1) // SEL_BLOCK)
    nbs_pad = -(-nbs // LANES) * LANES
    pk = pool_k.reshape(n_pool, page * G, hd)
    pv = pool_v.reshape(n_pool, page * G, hd)
    pspec = lambda i: pl.BlockSpec((1, page * G, hd),
                                   functools.partial(lambda b, s, tbl, i: (tbl[b * n_pages + s * P + i], 0, 0), i=i))
    whole = lambda a: pl.BlockSpec(a.shape, lambda b, s, tbl: (0,) * a.ndim)
    per_seq = lambda shape: pl.BlockSpec((1,) + shape, lambda b, s, tbl: (b,) + (0,) * len(shape))
    grid_spec = pltpu.PrefetchScalarGridSpec(
        num_scalar_prefetch=1,
        grid=(bsz, n_pages // P),
        in_specs=[pspec(i) for i in range(P)] * 2
        + [whole(w1), per_seq((G, SUBLANES, hd)), whole(pos), whole(b1), whole(w2), whole(b2)],
        out_specs=[per_seq((G, SUBLANES, hd)), per_seq((G, LANES))],
        scratch_shapes=[pltpu.VMEM((2 * P, cpp * CHUNK_PITCH, hd), F32),
                        pltpu.VMEM((G, nch, hid2), F32), pltpu.VMEM((G, nch, hid2), F32)],
    )
    return pl.pallas_call(
        functools.partial(_dec_cmp_kernel, P=P, nbc=nbc, nbs=nbs, nbs_pad=nbs_pad, p0=p0),
        grid_spec=grid_spec,
        out_shape=[jax.ShapeDtypeStruct((bsz, G, SUBLANES, hd), F32),
                   jax.ShapeDtypeStruct((bsz, G, LANES), jnp.int32)],
        compiler_params=_cparams("arbitrary", "arbitrary"),
    )(table.reshape(-1), *([pk] * P), *([pv] * P), w1, q, pos, b1, w2, b2)


def _softmax_with_new_key(qb, k_b, v_b, bias, s_new, v_new):
    s = _dot_nt(qb, k_b)
    if bias is not None:
        s = s + bias
    mx = jnp.maximum(jnp.max(s, axis=-1, keepdims=True), s_new)
    p = jnp.exp(s - mx)
    p_new = jnp.exp(s_new - mx)
    return (_dot(p.astype(BF16), v_b) + p_new * v_new) / (jnp.sum(p, axis=-1, keepdims=True) + p_new)


def _dec_attn_kernel(sel_ref, tbl_ref, *refs, n_cache_blocks, n_sel):
    n_blk = NSA_KV * n_sel
    ks_refs, vs_refs = refs[:n_blk], refs[n_blk:2 * n_blk]
    q_ref, new_ref, kw_ref, vw_ref, oc_ref, g_ref, bg_ref, o_ref = refs[2 * n_blk:]
    b = pl.program_id(0)
    slot = _iota((SUBLANES, n_sel * SEL_BLOCK), 1) // SEL_BLOCK
    head = _iota((SUBLANES, LANES), 0)
    lane = _iota((SUBLANES, LANES), 1)
    n_win = kw_ref.shape[1] // NSA_KV
    for g in range(NSA_KV):
        q = q_ref[0, g] * (NSA_HD ** -0.5)
        qb = q.astype(BF16)
        new = new_ref[0, g]

        mine = lambda r, n: r[pl.ds(g, n, stride=NSA_KV), :]
        picks = range(g * n_sel, (g + 1) * n_sel)
        k_all = jnp.concatenate([mine(ks_refs[i].at[0, 0], SEL_BLOCK) for i in picks], axis=0).astype(BF16)
        v_all = jnp.concatenate([mine(vs_refs[i].at[0, 0], SEL_BLOCK) for i in picks], axis=0).astype(BF16)
        bias = jnp.zeros((SUBLANES, n_sel * SEL_BLOCK), F32)
        for s in range(n_sel):
            blk = sel_ref[(b * NSA_KV + g) * n_sel + s]
            bias = jnp.where(slot == s, jnp.where(blk < n_cache_blocks, 0.0, NEG), bias)
        o_slc = _softmax_with_new_key(qb, k_all, v_all, bias,
                                      jnp.sum(q * new[0:1], axis=-1, keepdims=True), new[1:2])

        o_win = _softmax_with_new_key(qb, mine(kw_ref.at[0], n_win).astype(BF16),
                                      mine(vw_ref.at[0], n_win).astype(BF16), None,
                                      jnp.sum(q * new[2:3], axis=-1, keepdims=True), new[3:4])

        gate = jnp.broadcast_to(_sigmoid(g_ref[0, g] + bg_ref[:, g * LANES:(g + 1) * LANES]), (SUBLANES, LANES))
        o = jnp.zeros((SUBLANES, NSA_HD), F32)
        for br, ob in enumerate((oc_ref[0, g], o_slc, o_win)):
            o = o + jnp.sum(jnp.where(lane == 3 * head + br, gate, 0.0), axis=-1, keepdims=True) * ob
        o_ref[0, g] = o


def dec_attn(sel, table, pool_ks, pool_vs, q, new, kw_past, vw_past, o_cmp, gates, b_gate):
    n_pool, page, G, hd = pool_ks.shape
    bsz, n_pages = table.shape
    halves = page // SEL_BLOCK
    wb = kw_past.shape[1]
    assert wb <= WINDOW
    n_sel = sel.shape[-1]
    pk = pool_ks.reshape(n_pool, halves, SEL_BLOCK * G, hd)
    pv = pool_vs.reshape(n_pool, halves, SEL_BLOCK * G, hd)
    n_cache_blocks = n_pages * halves

    def page_map(b, sel_ref, tbl_ref, i):
        blk = jnp.minimum(sel_ref[b * G * n_sel + i], n_cache_blocks - 1)
        return (tbl_ref[b * n_pages + blk // halves], blk % halves, 0, 0)

    per_seq = lambda shape: pl.BlockSpec((1,) + shape, lambda b, sel_ref, tbl_ref: (b,) + (0,) * len(shape))
    small = per_seq((G, SUBLANES, hd))
    blocks = [pl.BlockSpec((1, 1, SEL_BLOCK * G, hd), functools.partial(page_map, i=i)) for i in range(G * n_sel)]
    grid_spec = pltpu.PrefetchScalarGridSpec(
        num_scalar_prefetch=2,
        grid=(bsz,),
        in_specs=blocks + blocks + [small, small, per_seq((wb * G, hd)), per_seq((wb * G, hd)), small,
                                    per_seq((G, 1, LANES)),
                                    pl.BlockSpec((1, G * LANES), lambda b, sel_ref, tbl_ref: (0, 0))],
        out_specs=small,
    )
    return pl.pallas_call(
        functools.partial(_dec_attn_kernel, n_cache_blocks=n_cache_blocks, n_sel=n_sel),
        grid_spec=grid_spec,
        out_shape=jax.ShapeDtypeStruct((bsz, G, SUBLANES, hd), F32),
        compiler_params=_cparams("arbitrary"),
    )(sel.reshape(-1), table.reshape(-1), *([pk] * (G * n_sel)), *([pv] * (G * n_sel)), q, new,
      kw_past.reshape(bsz, wb * G, hd), vw_past.reshape(bsz, wb * G, hd), o_cmp,
      gates.reshape(bsz, G, 1, LANES), b_gate)


def _gate_weights(wt, layer, row0, n):
    rows = lax.slice(wt, (layer, row0, 0), (layer + 1, row0 + n, wt.shape[2]))
    return jnp.pad(rows, ((0, 0), (0, LANES - n), (0, 0))), 0


def _nsa_gate_weights(wt, layer, b_gate, row0):
    per = 3 * NSA_HPG
    k = wt.shape[2]
    rows = lax.slice(wt, (layer, row0, 0), (layer + 1, row0 + NSA_KV * per, k)).reshape(NSA_KV, per, k)
    wg = jnp.pad(rows, ((0, 0), (0, LANES - per), (0, 0))).reshape(1, NSA_KV * LANES, k)
    bg = jnp.pad(b_gate.reshape(NSA_KV, per), ((0, 0), (0, LANES - per))).reshape(1, NSA_KV * LANES)
    return (wg, 0), bg


def kernel(x_prompt, x_sample, state_a_C, state_a_n, state_a_m, state_b_conv, cache_c_k_cmp, cache_c_v_cmp, cache_c_k_slc, cache_c_v_slc, cache_c_k_win, cache_c_v_win, page_table, norm_w, ffn_w_in, ffn_w_out, a_w_in, a_b_gates, a_head_norm, a_w_out, b_w_in, b_conv_w, b_w_out, c_w_in, c_b_gate, c_cmp_pos, c_cmp_w1, c_cmp_b1, c_cmp_w2, c_cmp_b2, c_w_out):
    bp, T, D = x_prompt.shape
    bs = x_sample.shape[0]
    assert x_sample.shape[1] == 1
    depth = norm_w.shape[0]
    G, hd = NSA_KV, NSA_HD
    past_len = page_table.shape[1] * cache_c_k_cmp.shape[2]
    xp = x_prompt.reshape(bp * T, D)
    xs = x_sample.reshape(bs, D)
    hp_in = norm_cast(xp, norm_w[0, 0])
    hs_in = norm_cast(xs, norm_w[0, 0])
    a_wt = jnp.swapaxes(a_w_in, 1, 2)
    c_wt = jnp.swapaxes(c_w_in, 1, 2)
    w_outs = {0: a_w_out, 1: b_w_out, 2: c_w_out}
    pa, sa, pb, sb, pc, sc = [], [], [], [], [], []
    for i in range(depth):
        kind, j = i % 3, i // 3
        nw = norm_w[i]
        w_out_f32 = (w_outs[kind], j)
        conv = None
        if kind == 0:
            hdv = a_head_norm.shape[1]
            n_main = 3 * hdv
            wg = _gate_weights(a_wt, j, n_main, 2 * MLSTM_HEADS)
            qkv_p, qkv_s, w_out = proj_plain(hp_in, hs_in, (a_wt, j), 0, 2 * hdv, 1024, transposed=True,
                                             cast=w_out_f32, prompt_dtype=BF16)
            o_p, o_s = proj_plain(hp_in, hs_in, (a_wt, j), 2 * hdv, hdv, 1024, transposed=True)
            qkvo_s = jnp.concatenate([qkv_s, o_s], axis=1)
            g_p, g_s = proj_plain(hp_in, hs_in, wg, 0, LANES, LANES, transposed=True)
            hp, c_p, n_p, m_p = mlstm_prompt(qkv_p, o_p, g_p, a_b_gates[j], a_head_norm[j], bp, T)
            hs, c_s, n_s, m_s = mlstm_step(qkvo_s, g_s, a_b_gates[j], a_head_norm[j],
                                           state_a_C[j], state_a_n[j], state_a_m[j])
            pa.append((c_p, n_p, m_p[:, :, 0]))
            sa.append((c_s, n_s, m_s))
        elif kind == 1:
            (bg_p, z_p), (bg_s, z_s), w_out = proj_conv_in(hp_in, hs_in, (b_w_in, j), D, w_out_f32)
            hp, conv = None, (bg_p, z_p, b_conv_w[j], T)
            buf = state_b_conv[j]
            hs = conv_gate_step(bg_s, z_s, buf.transpose(1, 0, 2), b_conv_w[j]).astype(BF16)
            pb.append(z_p.reshape(bp, T, D)[:, T - (CONV_W - 1):])
            sb.append(jnp.concatenate([buf, z_s[:, None]], axis=1)[:, -(CONV_W - 1):])
        else:
            nq_cols = NSA_HPG * G * hd
            nheads = G * NSA_HPG
            wg, bgate = _nsa_gate_weights(c_wt, j, c_b_gate[j], nq_cols + 6 * G * hd)
            cw = _cmp_weights(c_cmp_pos[j], c_cmp_w1[j], c_cmp_b1[j], c_cmp_w2[j], c_cmp_b2[j])
            q_p, q_s, w_out = proj_plain(hp_in, hs_in, (c_wt, j), 0, nq_cols, 1024, transposed=True, cast=w_out_f32)
            kv_p, kv_rows_p, kv_s = [], [], []
            for half in range(2):
                (tok_p, rows_p), (tok_s, _) = proj_groups(hp_in, hs_in, (c_wt, j), nq_cols + half * 3 * G * hd,
                                                         3, G * hd, transposed=True)
                kv_p, kv_rows_p, kv_s = kv_p + list(tok_p), kv_rows_p + list(rows_p), kv_s + list(tok_s)
            gt_p, gt_s = proj_plain(hp_in, hs_in, wg, 0, G * LANES, G * LANES, transposed=True)
            ck, cv = cmp_prompt(kv_p[0], kv_p[1], bp, T, cw)
            hp = nsa_attn_prompt(q_p, ck, cv, kv_p[2], kv_p[3], kv_p[4], kv_p[5], gt_p, bgate, bp, T)
            kv_rows_p = [a.reshape(bp, T, G, hd) for a in kv_rows_p]
            keep = min(WINDOW, T)
            pc.append(tuple(kv_rows_p[:4]) + tuple(a[:, T - keep:] for a in kv_rows_p[4:]))
            q_s = jnp.pad(q_s.reshape(bs, G, NSA_HPG, hd), ((0, 0), (0, 0), (0, SUBLANES - NSA_HPG), (0, 0)))
            kv_s = jnp.stack([a.reshape(bs, G, hd) for a in kv_s], axis=1)
            new = jnp.pad(kv_s[:, 2:6].transpose(0, 2, 1, 3), ((0, 0), (0, 0), (0, SUBLANES - 4), (0, 0)))
            o_cmp, ranked = dec_cmp(cache_c_k_cmp[j], cache_c_v_cmp[j], page_table, q_s, cw, past_len)
            n_sel = min(N_SELECT, -(-(past_len + 1) // SEL_BLOCK))
            o_s = dec_attn(ranked[:, :, :n_sel], page_table, cache_c_k_slc[j], cache_c_v_slc[j], q_s, new,
                           cache_c_k_win[j], cache_c_v_win[j], o_cmp, gt_s, bgate)
            hs = o_s[:, :, :NSA_HPG].reshape(bs, nheads * hd).astype(BF16)
            keep_s = min(WINDOW, cache_c_k_win.shape[2] + 1)
            win = [jnp.concatenate([c[j], kv_s[:, a][:, None]], axis=1)[:, -keep_s:]
                   for c, a in ((cache_c_k_win, 4), (cache_c_v_win, 5))]
            sc.append(tuple(kv_s[:, a][:, None] for a in range(4)) + tuple(win))
        last = i == depth - 1
        nw_next = norm_w[i + 1, 0] if not last else nw[0]
        (xp, xs), (hp_mid, hs_mid) = proj_out(hp, hs, w_out, xp, xs, nw[1], nw[2], conv=conv)
        act_p, act_s, f_out = proj_swiglu(hp_mid, hs_mid, (ffn_w_in, i), ffn_w_out.shape[1], (ffn_w_out, i))
        (xp, xs), (hp_in, hs_in) = proj_out(act_p, act_s, f_out, xp, xs, nw[3], nw_next, not last)

    stack = lambda items: [jnp.stack(a) for a in zip(*items)]
    p_a, s_a = stack(pa), stack(sa)
    p_c, s_c = stack(pc), stack(sc)
    return (xp.reshape(bp, T, D), xs.reshape(bs, 1, D), *p_a, *s_a, jnp.stack(pb), jnp.stack(sb), *p_c, *s_c)
```

```python
import functools
import math

import jax
import jax.numpy as jnp
from jax import lax
from jax.experimental import pallas as pl
from jax.experimental.pallas import tpu as pltpu

F32 = jnp.float32
BF16 = jnp.bfloat16

EPS = 1e-6
NEG = -1e30
BIG = 1e30

LANES = 128
SUBLANES = 8
VMEM_LIMIT = 56 * 1024 * 1024

MLSTM_HEADS = 8
MLSTM_CHUNK = 512
FORGET_BIAS_COL = MLSTM_HEADS
CONV_W = 3
NSA_HD = 128
NSA_KV = 4
NSA_HPG = 4
CMP_BLOCK = 32
CMP_STRIDE = 16
SEL_BLOCK = 64
N_SELECT = 16
WINDOW = 512
ATT_TILE = 512
PAGES_PER_STEP = 16
MXU_TILE = 1024
HALF_TILE = 512
OUT_SUB_ROWS = 256
OUT_VMEM_SLACK = 6 * 1024 * 1024


def _cparams(*sem, vmem=VMEM_LIMIT):
    return pltpu.CompilerParams(dimension_semantics=sem, vmem_limit_bytes=vmem)


def _dot(a, b):
    return jnp.dot(a, b, preferred_element_type=F32)


def _dot_nt(a, b):
    return lax.dot_general(a, b, (((1,), (1,)), ((), ())), preferred_element_type=F32)


def _dot_tn(a, b):
    return lax.dot_general(a, b, (((0,), (0,)), ((), ())), preferred_element_type=F32)


def _split_bf16(x):
    hi = x.astype(BF16)
    lo = (x - hi.astype(F32)).astype(BF16)
    return hi, lo


def _iota(shape, dim):
    return lax.broadcasted_iota(jnp.int32, shape, dim)


def _sigmoid(x):
    return 1.0 / (1.0 + jnp.exp(-x))


def _rms(x):
    return x * lax.rsqrt(jnp.mean(x * x, axis=-1, keepdims=True) + EPS)


def _row_tile(m, want):
    t = min(m, want)
    assert m % t == 0, (m, t)
    return t


def _norm_kernel(x_ref, w_ref, o_ref):
    o_ref[...] = (_rms(x_ref[...]) * w_ref[...]).astype(BF16)


def norm_cast(x, w):
    m, d = x.shape
    tm = _row_tile(m, HALF_TILE)
    return pl.pallas_call(
        _norm_kernel,
        grid=(m // tm,),
        in_specs=[pl.BlockSpec((tm, d), lambda i: (i, 0)), pl.BlockSpec((1, d), lambda i: (0, 0))],
        out_specs=pl.BlockSpec((tm, d), lambda i: (i, 0)),
        out_shape=jax.ShapeDtypeStruct((m, d), BF16),
        compiler_params=_cparams("parallel"),
    )(x, w.reshape(1, d))


def _proj_kernel(x_ref, xs_ref, *refs, n_groups, n_out, epilogue, transposed, with_cast):
    w_refs = refs[:n_groups]
    refs = refs[n_groups:]
    if with_cast:
        cast_in_ref, refs = refs[0], refs[1:]
        refs[2 * n_out][...] = cast_in_ref[...].astype(BF16)
    out_refs = refs[:n_out]
    sample_out_refs = refs[n_out:2 * n_out]
    wb_ref = refs[-1]
    mm = _dot_nt if transposed else _dot

    @pl.when(pl.program_id(1) == 0)
    def _():
        for g in range(n_groups):
            wb_ref[g] = w_refs[g][...].astype(BF16)
        xs = xs_ref[...]
        epilogue([mm(xs, wb_ref[g]) for g in range(n_groups)], sample_out_refs)

    x = x_ref[...]
    epilogue([mm(x, wb_ref[g]) for g in range(n_groups)], out_refs)


def _proj(x, xs, w, *, col0, tn, n_tiles, group_stride, n_groups, epilogue, outs, tm_want=MXU_TILE, transposed=False,
          cast=None, sample_f32=False):
    w, layer = w
    m, k = x.shape
    s = xs.shape[0]
    tm = _row_tile(m, tm_want)
    assert col0 % tn == 0
    b0 = col0 // tn
    if transposed:
        w_block = (None, tn, k)
        w_map = lambda j, i, off: (layer, off + j, 0)
    else:
        w_block = (None, k, tn)
        w_map = lambda j, i, off: (layer, 0, off + j)
    w_specs = [pl.BlockSpec(w_block, functools.partial(w_map, off=b0 + g * group_stride)) for g in range(n_groups)]
    n_out = len(outs)

    def out_for(rows, row_tile, row_map, n, dt, by_rows):
        if not by_rows:
            return jax.ShapeDtypeStruct((rows, n), dt), pl.BlockSpec((row_tile, tn), row_map)
        assert n_tiles == 1 and n == tn
        per = n // LANES
        return (jax.ShapeDtypeStruct((rows * per, LANES), dt),
                pl.BlockSpec((row_tile * per, LANES), lambda j, i: (row_map(j, i)[0], 0)))

    prompt_outs = [out_for(m, tm, lambda j, i: (i, j), *o) for o in outs]
    sample_outs = [out_for(s, s, lambda j, i: (0, j), n, F32 if sample_f32 else dt, by_rows)
                   for n, dt, by_rows in outs]
    all_outs = prompt_outs + sample_outs
    operands = [x, xs] + [w] * n_groups
    in_specs = [pl.BlockSpec((tm, k), lambda j, i: (i, 0)), pl.BlockSpec((s, k), lambda j, i: (0, 0))] + w_specs
    n_i = m // tm
    if cast is not None:
        cw, c_layer = cast
        _, c_rows, c_cols = cw.shape
        bf16_rows = 2 * SUBLANES
        blk = next(r for r in range(bf16_rows, c_rows + 1, bf16_rows)
                   if c_rows % r == 0 and c_rows // r <= n_tiles * n_i)
        last = c_rows // blk - 1
        step = lambda j, i: jnp.minimum(j * n_i + i, last)
        in_specs.append(pl.BlockSpec((None, blk, c_cols), lambda j, i: (c_layer, step(j, i), 0)))
        operands.append(cw)
        all_outs.append((jax.ShapeDtypeStruct((c_rows, c_cols), BF16),
                         pl.BlockSpec((blk, c_cols), lambda j, i: (step(j, i), 0))))
    res = pl.pallas_call(
        functools.partial(_proj_kernel, n_groups=n_groups, n_out=n_out, epilogue=epilogue, transposed=transposed,
                          with_cast=cast is not None),
        grid=(n_tiles, n_i),
        in_specs=in_specs,
        out_specs=[spec for _, spec in all_outs],
        out_shape=[shape for shape, _ in all_outs],
        scratch_shapes=[pltpu.VMEM((n_groups,) + w_block[1:], BF16)],
        compiler_params=_cparams("arbitrary", "arbitrary"),
    )(*operands)
    return res[:n_out], res[n_out:2 * n_out], (res[2 * n_out] if cast is not None else None)


def _ep_plain(accs, outs):
    for acc, o in zip(accs, outs):
        o[...] = acc.astype(o.dtype)


def proj_plain(x, xs, w, col0, n_cols, tn, transposed=False, cast=None, prompt_dtype=F32):
    p, s, c = _proj(x, xs, w, col0=col0, tn=tn, n_tiles=n_cols // tn, group_stride=0, n_groups=1,
                    epilogue=_ep_plain, outs=[(n_cols, prompt_dtype, False)], transposed=transposed, cast=cast,
                    sample_f32=True)
    return (p[0], s[0]) if cast is None else (p[0], s[0], c)


def _ep_both_layouts(accs, outs):
    n = len(accs)
    for acc, o_tok, o_rows in zip(accs, outs[:n], outs[n:]):
        o_tok[...] = acc
        per = acc.shape[1] // LANES
        rows = acc.shape[0]
        for c in range(per):
            o_rows[pl.ds(c, rows, stride=per), :] = acc[:, c * LANES:(c + 1) * LANES]


def proj_groups(x, xs, w, col0, n_groups, group_cols, transposed=False):
    outs = [(group_cols, F32, False)] * n_groups + [(group_cols, F32, True)] * n_groups
    p, s, _ = _proj(x, xs, w, col0=col0, tn=group_cols, n_tiles=1, group_stride=1, n_groups=n_groups,
                    epilogue=_ep_both_layouts, outs=outs, tm_want=HALF_TILE, transposed=transposed)
    return (p[:n_groups], p[n_groups:]), (s[:n_groups], s[n_groups:])


def _ep_swiglu(accs, outs):
    g, u = accs
    outs[0][...] = (g * _sigmoid(g) * u).astype(BF16)


def proj_swiglu(x, xs, w, ff, cast):
    tn = HALF_TILE
    p, s, c = _proj(x, xs, w, col0=0, tn=tn, n_tiles=ff // tn, group_stride=ff // tn, n_groups=2,
                    epilogue=_ep_swiglu, outs=[(ff, BF16, False)], cast=cast)
    return p[0], s[0], c


def _ep_conv_in(accs, outs):
    bg, cg, u = accs
    outs[0][...] = bg
    outs[1][...] = cg * u


def proj_conv_in(x, xs, w, d, cast):
    tn = HALF_TILE
    return _proj(x, xs, w, col0=0, tn=tn, n_tiles=d // tn, group_stride=d // tn, n_groups=3,
                 epilogue=_ep_conv_in, outs=[(d, F32, False), (d, F32, False)], cast=cast)


def _conv_gated(bg_ref, z_ref, zp_ref, cw_ref, T):
    tm = z_ref.shape[0]
    z = z_ref[...]
    zc = jnp.concatenate([zp_ref[...], z], axis=0)
    tpos = (pl.program_id(0) * tm + _iota((tm, 1), 0)) % T
    y = cw_ref[CONV_W - 1:CONV_W, :] * z
    for back in range(1, CONV_W):
        zb = zc[SUBLANES - back:SUBLANES - back + tm, :]
        y = y + cw_ref[CONV_W - 1 - back:CONV_W - back, :] * jnp.where(tpos >= back, zb, 0.0)
    return (bg_ref[...] * y).astype(BF16)


def _out_kernel(*refs, emit_next, conv_T):
    if conv_T:
        h = _conv_gated(*refs[:4], conv_T)
        rows_of = lambda rows: h[rows, :]
        refs = refs[4:]
    else:
        h_ref = refs[0]
        rows_of = lambda rows: h_ref[rows, :]
        refs = refs[1:]
    hs_ref, w_ref, r_ref, rs_ref, nwa_ref, nwb_ref, *out_refs = refs

    def finalize(y, r, x_ref, xn_ref):
        x_new = r + _rms(y) * nwa_ref[...]
        x_ref[...] = x_new
        if emit_next:
            xn_ref[...] = (_rms(x_new) * nwb_ref[...]).astype(BF16)

    if emit_next:
        x_ref, xn_ref, xs_ref, xsn_ref = out_refs
    else:
        (x_ref, xs_ref), xn_ref, xsn_ref = out_refs, None, None

    @pl.when(pl.program_id(0) == 0)
    def _():
        finalize(_dot(hs_ref[...], w_ref[...]), rs_ref[...], xs_ref, xsn_ref)

    tm = r_ref.shape[0]
    sub = math.gcd(tm, OUT_SUB_ROWS)
    for r0 in range(0, tm, sub):
        rows = slice(r0, r0 + sub)
        finalize(_dot(rows_of(rows), w_ref[...]), r_ref[rows, :], x_ref.at[rows, :],
                 xn_ref.at[rows, :] if emit_next else None)


def proj_out(h, hs, w, resid, resid_s, nw_post, nw_next, emit_next=True, conv=None):
    m, d = resid.shape
    k = w.shape[0]
    s = hs.shape[0]
    tm = _row_tile(m, 2 * OUT_SUB_ROWS)
    h_bytes = tm * k * (2 if conv is None else 8)
    conv_tmp = 0 if conv is None else h_bytes
    vmem = k * d * 2 + 2 * (h_bytes + tm * d * (4 + 4 + (2 if emit_next else 0))) + conv_tmp + OUT_VMEM_SLACK
    row = lambda i: (i, 0)
    fixed = lambda i: (0, 0)
    if conv is None:
        h_ops, h_specs, conv_T = [h], [pl.BlockSpec((tm, k), row)], 0
    else:
        bg, z, conv_w, conv_T = conv
        assert conv_T % tm == 0
        per = tm // SUBLANES
        h_ops = [bg, z, z, conv_w]
        h_specs = [pl.BlockSpec((tm, k), row), pl.BlockSpec((tm, k), row),
                   pl.BlockSpec((SUBLANES, k), lambda i: (jnp.maximum(i * per - 1, 0), 0)),
                   pl.BlockSpec((CONV_W, k), fixed)]
    f32_out = [jax.ShapeDtypeStruct((m, d), F32), jax.ShapeDtypeStruct((s, d), F32)]
    bf16_out = [jax.ShapeDtypeStruct((m, d), BF16), jax.ShapeDtypeStruct((s, d), BF16)]
    specs = [pl.BlockSpec((tm, d), row), pl.BlockSpec((s, d), fixed)]
    if emit_next:
        out_shape = [f32_out[0], bf16_out[0], f32_out[1], bf16_out[1]]
        out_specs = [specs[0], specs[0], specs[1], specs[1]]
    else:
        out_shape, out_specs = f32_out, specs
    res = pl.pallas_call(
        functools.partial(_out_kernel, emit_next=emit_next, conv_T=conv_T),
        grid=(m // tm,),
        in_specs=h_specs + [pl.BlockSpec((s, k), fixed),
                            pl.BlockSpec((k, d), fixed, pipeline_mode=pl.Buffered(1)),
                            pl.BlockSpec((tm, d), row),
                            pl.BlockSpec((s, d), fixed),
                            pl.BlockSpec((1, d), fixed),
                            pl.BlockSpec((1, d), fixed)],
        out_specs=out_specs,
        out_shape=out_shape,
        compiler_params=_cparams("arbitrary", vmem=vmem),
    )(*h_ops, hs, w, resid, resid_s, nw_post.reshape(1, d), nw_next.reshape(1, d))
    if emit_next:
        return (res[0], res[2]), (res[1], res[3])
    return (res[0], res[1]), (None, None)


def _log_sigmoid(x):
    return jnp.minimum(x, 0.0) - jnp.log1p(jnp.exp(-jnp.abs(x)))


def _mlstm_kernel(q_ref, k_ref, v_ref, o_ref, g_ref, bg_ref, hn_ref, hg_ref, c_ref, n_ref, m_ref, cn_ref,
                  *, L, H, DK, DV):
    chunk = pl.program_id(1)

    @pl.when(chunk == 0)
    def _():
        cn_ref[...] = jnp.zeros_like(cn_ref)
        m_ref[...] = jnp.zeros_like(m_ref)

    gpre = g_ref[...] + bg_ref[...]
    lf = _log_sigmoid(gpre)
    rows = _iota((L, L), 0)
    cols = _iota((L, L), 1)
    causal = rows >= cols
    tril = jnp.where(causal, 1.0, 0.0).astype(BF16)
    lf_hi, lf_lo = _split_bf16(lf)
    bcum = _dot(tril, lf_hi) + _dot(tril, lf_lo)
    g_t = gpre.T
    b_t = bcum.T
    for h in range(H):
        f = FORGET_BIAS_COL + h
        a_col = bcum[:, f:f + 1]
        i_col = gpre[:, h:h + 1]
        b_row = b_t[f:f + 1, :]
        i_row = g_t[h:h + 1, :]
        m_prev = m_ref[0, h:h + 1, 0:1]
        log_d = jnp.where(causal, a_col - b_row + i_row, NEG)
        log_inter = a_col + m_prev
        m_t = jnp.maximum(log_inter, jnp.max(log_d, axis=-1, keepdims=True))
        d = jnp.exp(log_d - m_t)
        inter = jnp.exp(log_inter - m_t)
        kh = k_ref[:, h * DK:(h + 1) * DK].astype(F32) * (DK ** -0.5)
        v1 = jnp.concatenate([v_ref[:, h * DV:(h + 1) * DV], jnp.ones((L, LANES), BF16)], axis=1)
        qb = q_ref[:, h * DK:(h + 1) * DK]
        s = _dot_nt(qb, kh.astype(BF16)) * d
        cn_old = cn_ref[h]
        both = _dot(s.astype(BF16), v1) + inter * _dot(qb, cn_old.astype(BF16))
        num, den = both[:, :DV], both[:, DV:]
        scale = 1.0 / jnp.maximum(jnp.abs(den), jnp.exp(-m_t))
        hh = _rms(num * jnp.concatenate([scale] * (DV // LANES), axis=1))
        gate = _sigmoid(o_ref[:, h * DV:(h + 1) * DV])
        hg_ref[:, h * DV:(h + 1) * DV] = (hh * hn_ref[:, h * DV:(h + 1) * DV] * gate).astype(BF16)
        m_new = m_t[L - 1:L, :]
        w_col = jnp.exp(a_col[L - 1:L, :] - a_col + i_col - m_new)
        decay = jnp.exp(log_inter[L - 1:L, :] - m_new)
        cn_new = decay * cn_old + _dot_tn((kh * w_col).astype(BF16), v1)
        cn_ref[h] = cn_new
        m_ref[0, h:h + 1, :] = jnp.broadcast_to(m_new, (1, LANES))

        @pl.when(chunk == pl.num_programs(1) - 1)
        def _():
            c_ref[0, h] = cn_new[:, :DV]
            n_ref[0, h:h + 1, :] = cn_new[:, DV:].T[0:1, :]


def mlstm_prompt(qkv, o, gates, b_gates, head_norm, bsz, T):
    H = MLSTM_HEADS
    m, hdv = o.shape
    hdk = hdv // 2
    DK, DV = hdk // H, hdv // H
    L = math.gcd(T, MLSTM_CHUNK)
    nc = T // L
    row = lambda b, c: (b * nc + c, 0)
    state = lambda b, c: (b, 0, 0)
    return pl.pallas_call(
        functools.partial(_mlstm_kernel, L=L, H=H, DK=DK, DV=DV),
        grid=(bsz, nc),
        in_specs=[pl.BlockSpec((L, hdk), row),
                  pl.BlockSpec((L, hdk), lambda b, c: (b * nc + c, 1)),
                  pl.BlockSpec((L, hdv), lambda b, c: (b * nc + c, 1)),
                  pl.BlockSpec((L, hdv), row),
                  pl.BlockSpec((L, LANES), row),
                  pl.BlockSpec((1, LANES), lambda b, c: (0, 0)),
                  pl.BlockSpec((1, hdv), lambda b, c: (0, 0))],
        out_specs=[pl.BlockSpec((L, hdv), row),
                   pl.BlockSpec((1, H, DK, DV), lambda b, c: (b, 0, 0, 0)),
                   pl.BlockSpec((1, H, DK), state),
                   pl.BlockSpec((1, H, LANES), state)],
        out_shape=[jax.ShapeDtypeStruct((m, hdv), BF16),
                   jax.ShapeDtypeStruct((bsz, H, DK, DV), F32),
                   jax.ShapeDtypeStruct((bsz, H, DK), F32),
                   jax.ShapeDtypeStruct((bsz, H, LANES), F32)],
        scratch_shapes=[pltpu.VMEM((H, DK, DV + LANES), F32)],
        compiler_params=_cparams("arbitrary", "arbitrary"),
    )(qkv, qkv, qkv, o, gates, _pad_lanes(b_gates.reshape(1, -1)), head_norm.reshape(1, hdv))


def _pad_lanes(x, width=LANES):
    return jnp.pad(x, ((0, 0), (0, width - x.shape[-1])))


def _to_col(row, n):
    eye = _iota((n, n), 0) == _iota((n, n), 1)
    return jnp.sum(jnp.where(eye, jnp.broadcast_to(row, (n, n)), 0.0), axis=-1, keepdims=True)


def _mlstm_step_kernel(x_ref, g_ref, bg_ref, hn_ref, c0_ref, n0_ref, m0_ref, hg_ref, c_ref, n_ref, m_ref,
                       *, H, DK, DV):
    x = x_ref[0]
    gpre = g_ref[0] + bg_ref[...]
    lf = _log_sigmoid(gpre)
    hdk = H * DK
    hdv = H * DV
    for h in range(H):
        f = FORGET_BIAS_COL + h
        ig = gpre[:, h:h + 1]
        m_prev = m0_ref[0, h:h + 1, :]
        log_inter = lf[:, f:f + 1] + m_prev
        m_t = jnp.maximum(log_inter, ig)
        d = jnp.exp(ig - m_t)
        inter = jnp.exp(log_inter - m_t)
        q = x[:, h * DK:(h + 1) * DK]
        k = x[:, hdk + h * DK:hdk + (h + 1) * DK] * (DK ** -0.5)
        v = x[:, 2 * hdk + h * DV:2 * hdk + (h + 1) * DV]
        og = x[:, 2 * hdk + hdv + h * DV:2 * hdk + hdv + (h + 1) * DV]
        c_old = c0_ref[0, h]
        n_old = n0_ref[0, h:h + 1, :]
        s = jnp.sum(q * k, axis=-1, keepdims=True) * d
        q_col = _to_col(q, DK)
        k_col = _to_col(k, DK)
        num = s * v + inter * jnp.sum(q_col * c_old, axis=0, keepdims=True)
        den = s + inter * jnp.sum(q * n_old, axis=-1, keepdims=True)
        hh = _rms(num / jnp.maximum(jnp.abs(den), jnp.exp(-m_t)))
        hg_ref[0, :, h * DV:(h + 1) * DV] = (hh * hn_ref[:, h * DV:(h + 1) * DV] * _sigmoid(og)).astype(BF16)
        c_ref[0, h] = inter * c_old + d * (k_col * v)
        n_ref[0, h:h + 1, :] = inter * n_old + d * k
        m_ref[0, h:h + 1, :] = m_t


def mlstm_step(qkvo, gates, b_gates, head_norm, c0, n0, m0):
    bsz, H, DK, DV = c0.shape
    wid = qkvo.shape[1]
    hdv = H * DV
    one = lambda b: (b, 0, 0)
    hg, c1, n1, m1 = pl.pallas_call(
        functools.partial(_mlstm_step_kernel, H=H, DK=DK, DV=DV),
        grid=(bsz,),
        in_specs=[pl.BlockSpec((1, 1, wid), one),
                  pl.BlockSpec((1, 1, LANES), one),
                  pl.BlockSpec((1, LANES), lambda b: (0, 0)),
                  pl.BlockSpec((1, hdv), lambda b: (0, 0)),
                  pl.BlockSpec((1, H, DK, DV), lambda b: (b, 0, 0, 0)),
                  pl.BlockSpec((1, H, DK), one),
                  pl.BlockSpec((1, H, 1), one)],
        out_specs=[pl.BlockSpec((1, 1, hdv), one),
                   pl.BlockSpec((1, H, DK, DV), lambda b: (b, 0, 0, 0)),
                   pl.BlockSpec((1, H, DK), one),
                   pl.BlockSpec((1, H, 1), one)],
        out_shape=[jax.ShapeDtypeStruct((bsz, 1, hdv), BF16),
                   jax.ShapeDtypeStruct((bsz, H, DK, DV), F32),
                   jax.ShapeDtypeStruct((bsz, H, DK), F32),
                   jax.ShapeDtypeStruct((bsz, H, 1), F32)],
        compiler_params=_cparams("parallel"),
    )(qkvo.reshape(bsz, 1, wid), gates.reshape(bsz, 1, LANES), _pad_lanes(b_gates.reshape(1, -1)),
      head_norm.reshape(1, hdv), c0, n0, m0.reshape(bsz, H, 1))
    return hg.reshape(bsz, hdv), c1, n1, m1.reshape(bsz, H)


def _conv_step_kernel(bg_ref, z_ref, buf_ref, cw_ref, o_ref):
    y = cw_ref[CONV_W - 1:CONV_W, :] * z_ref[...]
    for j in range(CONV_W - 1):
        y = y + cw_ref[j:j + 1, :] * buf_ref[j]
    o_ref[...] = bg_ref[...] * y


def conv_gate_step(bg, z, buf, conv_w):
    bsz, d = z.shape
    full = lambda: (0, 0)
    return pl.pallas_call(
        _conv_step_kernel,
        in_specs=[pl.BlockSpec((bsz, d), full), pl.BlockSpec((bsz, d), full),
                  pl.BlockSpec((CONV_W - 1, bsz, d), lambda: (0, 0, 0)), pl.BlockSpec((CONV_W, d), full)],
        out_specs=pl.BlockSpec((bsz, d), full),
        out_shape=jax.ShapeDtypeStruct((bsz, d), F32),
    )(bg, z, buf, conv_w)


def _gelu(x):
    return 0.5 * x * (1.0 + jnp.tanh(math.sqrt(2.0 / math.pi) * (x + 0.044715 * x * x * x)))


def _cmp_finish(a, posw, b1, w2, b2):
    hid = a.shape[1] // 2
    a1 = a[:, hid:]
    a1_next = jnp.concatenate([a1[1:], a1[:1]], axis=0)
    pre = a[:, :hid] + a1_next + b1 + posw[0:1, :hid] + posw[1:2, hid:]
    return _dot(_gelu(pre).astype(BF16), w2) + b2


def _cmp_prompt_kernel(k_ref, v_ref, w1_ref, pos_ref, b1_ref, w2_ref, b2_ref, ck_ref, cv_ref, *, nch):
    for idx, (x_ref, o_ref) in enumerate(((k_ref, ck_ref), (v_ref, cv_ref))):
        w1 = w1_ref[idx]
        lhs = jnp.concatenate([x_ref[pl.ds(p, nch, stride=CMP_STRIDE), :] for p in range(CMP_STRIDE)], axis=1)
        a = _dot(lhs.astype(BF16), w1)
        posw = _dot(pos_ref[idx], w1)
        o_ref[0, 0] = _cmp_finish(a, posw, b1_ref[idx], w2_ref[idx], b2_ref[idx])


def _cmp_weights(cmp_pos, cmp_w1, cmp_b1, cmp_w2, cmp_b2):
    two, ratio, stride, hd, hid = cmp_w1.shape
    w1 = cmp_w1.transpose(0, 2, 3, 1, 4).reshape(two, stride * hd, ratio * hid).astype(BF16)
    pos = cmp_pos.reshape(two, ratio, stride * hd)
    pos = jnp.pad(pos, ((0, 0), (0, SUBLANES - ratio), (0, 0))).astype(BF16)
    return w1, pos, cmp_b1.reshape(two, 1, hid), cmp_w2.astype(BF16), cmp_b2.reshape(two, 1, hd)


def cmp_prompt(kc, vc, bsz, T, cw):
    w1, pos, b1, w2, b2 = cw
    nch = T // CMP_STRIDE
    G = NSA_KV
    out = jax.ShapeDtypeStruct((bsz, G, nch, NSA_HD), F32)
    ospec = pl.BlockSpec((1, 1, nch, NSA_HD), lambda b, g: (b, g, 0, 0))
    whole = lambda a: pl.BlockSpec(a.shape, lambda b, g: (0,) * a.ndim)
    seq = pl.BlockSpec((T, NSA_HD), lambda b, g: (b, g))
    return pl.pallas_call(
        functools.partial(_cmp_prompt_kernel, nch=nch),
        grid=(bsz, G),
        in_specs=[seq, seq, whole(w1), whole(pos), whole(b1), whole(w2), whole(b2)],
        out_specs=[ospec, ospec],
        out_shape=[out, out],
        compiler_params=_cparams("parallel", "parallel"),
    )(kc, vc, w1, pos, b1, w2, b2)


def _overlap_t(nbs_pad, nbc_pad):
    j = _iota((nbs_pad, nbc_pad), 0)
    n = _iota((nbs_pad, nbc_pad), 1)
    lo = jnp.maximum(n * CMP_STRIDE, j * SEL_BLOCK)
    hi = jnp.minimum(n * CMP_STRIDE + CMP_BLOCK, j * SEL_BLOCK + SEL_BLOCK)
    return (jnp.maximum(hi - lo, 0) // CMP_STRIDE).astype(F32)


def _rank_select(score, n_sel, axis):
    n = score.shape[axis]
    idx = _iota(score.shape, axis)
    cnt = jnp.zeros(score.shape, F32)
    for jp in range(n):
        other = lax.slice_in_dim(score, jp, jp + 1, axis=axis)
        tie = jnp.where(idx > jp, 1.0, 0.0)
        cnt = cnt + jnp.where(other > score, 1.0, jnp.where(other == score, tie, 0.0))
    return jnp.where(cnt < n_sel, 1.0, 0.0)


def _attn_first(state, s, vt_b):
    m_ref, l_ref, acc_ref = state
    m = jnp.max(s, axis=0, keepdims=True)
    p = jnp.exp2(s - m)
    m_ref[...] = m
    l_ref[...] = jnp.sum(p, axis=0, keepdims=True)
    acc_ref[...] = _dot(vt_b, p.astype(BF16))


def _get_cols(ref, ranges):
    return ref[...] if ranges is None else jnp.concatenate([ref[:, a:b] for a, b in ranges], axis=1)


def _set_cols(ref, ranges, val):
    if ranges is None:
        ref[...] = val
        return
    at = 0
    for a, b in ranges:
        ref[:, a:b] = val[:, at:at + b - a]
        at += b - a


def _attn_more(state, s, vt_b, ranges=None):
    m_ref, l_ref, acc_ref = state
    m_old = _get_cols(m_ref, ranges)
    m_new = jnp.maximum(m_old, jnp.max(s, axis=0, keepdims=True))
    alpha = jnp.exp2(m_old - m_new)
    p = jnp.exp2(s - m_new)
    _set_cols(m_ref, ranges, m_new)
    _set_cols(l_ref, ranges, alpha * _get_cols(l_ref, ranges) + jnp.sum(p, axis=0, keepdims=True))
    _set_cols(acc_ref, ranges, alpha * _get_cols(acc_ref, ranges) + _dot(vt_b, p.astype(BF16)))


def _attn_empty(state):
    m_ref, l_ref, acc_ref = state
    m_ref[...] = jnp.full(m_ref.shape, NEG, F32)
    l_ref[...] = jnp.zeros_like(l_ref)
    acc_ref[...] = jnp.zeros_like(acc_ref)


def _attn_merged(state_a, state_b):
    (ma_ref, la_ref, acca_ref), (mb_ref, lb_ref, accb_ref) = state_a, state_b
    m = jnp.maximum(ma_ref[...], mb_ref[...])
    wa = jnp.exp2(ma_ref[...] - m)
    wb = jnp.exp2(mb_ref[...] - m)
    return (wa * acca_ref[...] + wb * accb_ref[...]) / (wa * la_ref[...] + wb * lb_ref[...])


def _nsa_attn_kernel(q_ref, ck_ref, cv_ref, ks_ref, vs_ref, kw_ref, vw_ref, g_ref, bg_ref, o_ref,
                     ksb_ref, vst_ref, kwb_ref, vwt_ref, *states, tq, hpg, nbc, nbs, nbs_pad):
    qi = pl.program_id(2)
    t0 = qi * tq
    cols = hpg * tq
    tk = tq
    n_kt = ks_ref.shape[0] // tk
    nbc_pad = ck_ref.shape[2]

    @pl.when(qi == 0)
    def _():
        ksb_ref[...] = ks_ref[...].astype(BF16)
        kwb_ref[...] = kw_ref[...].astype(BF16)
        for kt in range(n_kt):
            vst_ref[kt] = vs_ref[kt * tk:(kt + 1) * tk, :].T.astype(BF16)
            vwt_ref[kt] = vw_ref[kt * tk:(kt + 1) * tk, :].T.astype(BF16)

    qb = (jnp.concatenate([q_ref[:, h * NSA_HD:(h + 1) * NSA_HD] for h in range(hpg)], axis=0)
          * (NSA_HD ** -0.5 * math.log2(math.e))).astype(BF16)

    n = _iota((nbc_pad, cols), 0)
    t_col = t0 + jnp.concatenate([_iota((nbc_pad, tq), 1)] * hpg, axis=1)
    cmask = jnp.where(n < nbc, jnp.where(n * CMP_STRIDE + (CMP_BLOCK - 1) <= t_col, 1.0, 0.0), 0.0)
    sc = jnp.where(cmask > 0.5, _dot_nt(ck_ref[0, 0].astype(BF16), qb), NEG)
    pe = jnp.exp2(sc - jnp.max(sc, axis=0, keepdims=True))
    p_cmp = pe / jnp.sum(pe, axis=0, keepdims=True) * cmask
    o_cmp = _dot(cv_ref[0, 0].T.astype(BF16), p_cmp.astype(BF16))

    p_sum = p_cmp[:, 0:tq]
    for h in range(1, hpg):
        p_sum = p_sum + p_cmp[:, h * tq:(h + 1) * tq]
    ov_t = _overlap_t(nbs_pad, nbc_pad).astype(BF16)
    p_hi, p_lo = _split_bf16(p_sum)
    imp_t = _dot(ov_t, p_hi) + _dot(ov_t, p_lo)
    j = _iota((nbs_pad, tq), 0)
    t = t0 + _iota((nbs_pad, tq), 1)
    t_blk = t // SEL_BLOCK
    forced = jnp.where(j == 0, 1.0, jnp.where(j == t_blk, 1.0, jnp.where(j == t_blk - 1, 1.0, 0.0)))
    visible = jnp.where(j < nbs, jnp.where(j * SEL_BLOCK <= t, 1.0, 0.0), 0.0)
    score = jnp.where(visible > 0.5, jnp.where(forced > 0.5, BIG, imp_t), NEG)
    sel_t = _rank_select(score, min(N_SELECT, nbs), axis=0).astype(BF16)

    k_ofs = _iota((tk, tq), 0)
    q_ofs = _iota((tk, tq), 1)
    causal = k_ofs <= q_ofs
    per_head = lambda bias: jnp.concatenate([bias] * hpg, axis=1)

    def scores(kb_ref, kt):
        k_b = kb_ref[pl.ds(pl.multiple_of(kt * tk, tk), tk), :]
        return _dot_nt(k_b, qb)

    half = tk // 2
    early = [(h * tq, h * tq + half) for h in range(hpg)]
    late = [(h * tq + half, (h + 1) * tq) for h in range(hpg)]
    q_of = lambda ranges: jnp.concatenate([qb[a:b] for a, b in ranges], axis=0)

    def half_scores(kb_ref, kt, key_half, q_part):
        k_b = kb_ref[pl.ds(pl.multiple_of(kt * tk + key_half * half, half), half), :]
        return _dot_nt(k_b, q_part)

    def slc_bias(kt):
        blk = (kt * tk + _iota((tk, nbs_pad), 0)) // SEL_BLOCK
        expand = jnp.where(blk == _iota((tk, nbs_pad), 1), 1.0, 0.0).astype(BF16)
        return (_dot(expand, sel_t) - 1.0) * BIG

    slc_a, slc_b, win_a, win_b = states[0:3], states[3:6], states[6:9], states[9:12]

    def slc_tile(kt):
        return scores(ksb_ref, kt) + per_head(slc_bias(kt)), vst_ref[kt]

    n_back = WINDOW // tk

    def win_more(chain, other, back):
        kt = qi - back
        if back < n_back:
            _attn_more(chain, scores(kwb_ref, kt), vwt_ref[kt])
            return
        seen = jnp.where(k_ofs >= q_ofs, 0.0, NEG)
        _attn_more(chain, half_scores(kwb_ref, kt, 1, qb) + per_head(seen[half:, :]), vwt_ref[kt][:, half:])
        _attn_more(other, half_scores(kwb_ref, kt, 0, q_of(early)) + per_head(seen[:half, :half]),
                   vwt_ref[kt][:, :half], early)

    slc_diag = jnp.where(causal, slc_bias(qi), NEG)
    win_diag = jnp.where(causal, 0.0, NEG)
    _attn_first(slc_a, half_scores(ksb_ref, qi, 0, qb) + per_head(slc_diag[:half, :]), vst_ref[qi][:, :half])
    _attn_first(win_a, half_scores(kwb_ref, qi, 0, qb) + per_head(win_diag[:half, :]), vwt_ref[qi][:, :half])
    for st in (slc_b, win_b):
        _attn_empty(st)
    q_late = q_of(late)
    _attn_more(slc_b, half_scores(ksb_ref, qi, 1, q_late) + per_head(slc_diag[half:, half:]),
               vst_ref[qi][:, half:], late)
    _attn_more(win_b, half_scores(kwb_ref, qi, 1, q_late) + per_head(win_diag[half:, half:]),
               vwt_ref[qi][:, half:], late)

    for back in range(1, n_back + 1, 2):
        both = back + 1 <= n_back

        @pl.when(qi >= back + 1 if both else qi >= back)
        def _():
            win_more(win_b, win_a, back)
            if both:
                win_more(win_a, win_b, back + 1)

        if both:
            @pl.when(qi == back)
            def _():
                win_more(win_b, win_a, back)

    def slc_pair(pair, _):
        _attn_more(slc_a, *slc_tile(2 * pair))
        _attn_more(slc_b, *slc_tile(2 * pair + 1))
        return 0

    lax.fori_loop(0, lax.shift_right_logical(qi, 1), slc_pair, 0)

    @pl.when(lax.bitwise_and(qi, 1) == 1)
    def _():
        _attn_more(slc_a, *slc_tile(qi - 1))

    o_slc = _attn_merged(slc_a, slc_b)
    o_win = _attn_merged(win_a, win_b)
    gate_t = _sigmoid(g_ref[...] + bg_ref[...]).T
    for h in range(hpg):
        sl = slice(h * tq, (h + 1) * tq)
        o_t = (gate_t[3 * h:3 * h + 1, :] * o_cmp[:, sl] + gate_t[3 * h + 1:3 * h + 2, :] * o_slc[:, sl]
               + gate_t[3 * h + 2:3 * h + 3, :] * o_win[:, sl])
        o_ref[:, h * NSA_HD:(h + 1) * NSA_HD] = o_t.T.astype(BF16)


def nsa_attn_prompt(q, ck, cv, ks, vs, kw, vw, gates, b_gate, bsz, T):
    G, HPG = NSA_KV, NSA_HPG
    tq = math.gcd(T, ATT_TILE)
    assert WINDOW % tq == 0
    nq = T // tq
    nch = T // CMP_STRIDE
    nbc = nch - CMP_BLOCK // CMP_STRIDE + 1
    nbs = -(-T // SEL_BLOCK)
    nbs_pad = -(-nbs // SUBLANES) * SUBLANES
    cols = HPG * tq
    kv_scratch = [pltpu.VMEM((T, NSA_HD), BF16), pltpu.VMEM((nq, NSA_HD, tq), BF16)] * 2
    stats = [pltpu.VMEM((1, cols), F32), pltpu.VMEM((1, cols), F32), pltpu.VMEM((NSA_HD, cols), F32)] * 4
    seq = pl.BlockSpec((T, NSA_HD), lambda b, g, i: (b, g))
    cspec = pl.BlockSpec((1, 1, nch, NSA_HD), lambda b, g, i: (b, g, 0, 0))
    qspec = pl.BlockSpec((tq, HPG * NSA_HD), lambda b, g, i: (b * nq + i, g))
    return pl.pallas_call(
        functools.partial(_nsa_attn_kernel, tq=tq, hpg=HPG, nbc=nbc, nbs=nbs, nbs_pad=nbs_pad),
        grid=(bsz, G, nq),
        in_specs=[qspec, cspec, cspec, seq, seq, seq, seq,
                  pl.BlockSpec((tq, LANES), lambda b, g, i: (b * nq + i, g)),
                  pl.BlockSpec((1, LANES), lambda b, g, i: (0, g))],
        out_specs=qspec,
        out_shape=jax.ShapeDtypeStruct((bsz * T, G * HPG * NSA_HD), BF16),
        scratch_shapes=kv_scratch + stats,
        compiler_params=_cparams("arbitrary", "arbitrary", "arbitrary"),
    )(q, ck, cv, ks, vs, kw, vw, gates, b_gate)


CHUNK_ROWS = CMP_STRIDE * NSA_KV
CHUNK_PITCH = CHUNK_ROWS + SUBLANES


def _dec_cmp_kernel(tbl_ref, *refs, P, nbc, nbs, nbs_pad, p0):
    k_pages, v_pages = refs[:P], refs[P:2 * P]
    w1_ref, q_ref, pos_ref, b1_ref, w2_ref, b2_ref, o_ref, sel_ref, pad_ref, ak_ref, av_ref = refs[2 * P:]
    cpp = LANES // CMP_STRIDE
    step = pl.program_id(1)
    for idx, (pages, a_ref) in enumerate(((k_pages, ak_ref), (v_pages, av_ref))):
        for i, pg in enumerate(pages):
            for c in range(cpp):
                pad_ref[idx * P + i, c * CHUNK_PITCH:c * CHUNK_PITCH + CHUNK_ROWS, :] = (
                    pg[0, c * CHUNK_ROWS:(c + 1) * CHUNK_ROWS, :])
        blocks = []
        for g in range(NSA_KV):
            for i in range(P):
                blocks.append(jnp.concatenate(
                    [pad_ref[idx * P + i, pl.ds(p * NSA_KV + g, cpp, stride=CHUNK_PITCH), :]
                     for p in range(CMP_STRIDE)], axis=1))
        a = _dot(jnp.concatenate(blocks, axis=0).astype(BF16), w1_ref[idx])
        per_g = P * cpp
        for g in range(NSA_KV):
            a_ref[g, pl.ds(pl.multiple_of(step * per_g, per_g), per_g), :] = a[g * per_g:(g + 1) * per_g]

    @pl.when(step == pl.num_programs(1) - 1)
    def _():
        _dec_cmp_finish(ak_ref, av_ref, q_ref, pos_ref, w1_ref, b1_ref, w2_ref, b2_ref, o_ref, sel_ref,
                        nbc=nbc, nbs=nbs, nbs_pad=nbs_pad, p0=p0)


def _dec_cmp_finish(ak_ref, av_ref, q_ref, pos_ref, w1_ref, b1_ref, w2_ref, b2_ref, o_ref, sel_ref,
                    *, nbc, nbs, nbs_pad, p0):
    nch = ak_ref.shape[1]
    for g in range(NSA_KV):
        cks = []
        for idx, a_ref in enumerate((ak_ref, av_ref)):
            posw = _dot(pos_ref[idx], w1_ref[idx])
            cks.append(_cmp_finish(a_ref[g], posw, b1_ref[idx], w2_ref[idx], b2_ref[idx]))
        ck, cv = cks
        qb = (q_ref[0, g] * (NSA_HD ** -0.5)).astype(BF16)
        n = _iota((SUBLANES, nch), 1)
        cmask = jnp.where(n < nbc, jnp.where(n * CMP_STRIDE + (CMP_BLOCK - 1) <= p0, 1.0, 0.0), 0.0)
        sc = jnp.where(cmask > 0.5, _dot_nt(qb, ck.astype(BF16)), NEG)
        pe = jnp.exp(sc - jnp.max(sc, axis=-1, keepdims=True))
        p_cmp = pe / jnp.sum(pe, axis=-1, keepdims=True) * cmask
        o_ref[0, g] = _dot(p_cmp.astype(BF16), cv.astype(BF16))

        head = jnp.where(_iota((SUBLANES, nch), 0) < NSA_HPG, p_cmp, 0.0)
        p_sum = jnp.broadcast_to(jnp.sum(head, axis=0, keepdims=True), (SUBLANES, nch))
        ov_t = _overlap_t(nbs_pad, nch).astype(BF16)
        p_hi, p_lo = _split_bf16(p_sum)
        imp = _dot_nt(p_hi, ov_t) + _dot_nt(p_lo, ov_t)
        j = _iota((SUBLANES, nbs_pad), 1)
        t_blk = p0 // SEL_BLOCK
        forced = jnp.where(j == 0, 1.0, jnp.where(j == t_blk, 1.0, jnp.where(j == t_blk - 1, 1.0, 0.0)))
        visible = jnp.where(j < nbs, jnp.where(j * SEL_BLOCK <= p0, 1.0, 0.0), 0.0)
        score = jnp.where(visible > 0.5, jnp.where(forced > 0.5, BIG, imp), NEG)
        s_row = jnp.broadcast_to(score[0:1], (nbs_pad, nbs_pad))
        s_col = jnp.concatenate([jnp.broadcast_to(score[0:1], (LANES, nbs_pad)).T] * (nbs_pad // LANES), axis=1)
        jr = _iota((nbs_pad, nbs_pad), 0)
        jc = _iota((nbs_pad, nbs_pad), 1)
        tie = jnp.where(jc < jr, 1.0, 0.0)
        beats = jnp.where(s_row > s_col, 1.0, jnp.where(s_row == s_col, tie, 0.0))
        rank = jnp.sum(beats, axis=-1, keepdims=True)
        slot = _iota((nbs_pad, LANES), 1).astype(F32)
        blk = _iota((nbs_pad, LANES), 0).astype(F32)
        picked = jnp.sum(jnp.where(rank == slot, blk, 0.0), axis=0, keepdims=True)
        sel_ref[0, g:g + 1, :] = picked.astype(jnp.int32)


def dec_cmp(pool_k, pool_v, table, q, cw, p0):
    w1, pos, b1, w2, b2 = cw
    n_pool, page, G, hd = pool_k.shape
    assert page == LANES and G == NSA_KV and hd == NSA_HD
    bsz, n_pages = table.shape
    assert p0 == n_pages * page
    P = math.gcd(n_pages, PAGES_PER_STEP)
    cpp = page // CMP_STRIDE
    nch = n_pages * cpp
    hid2 = w1.shape[2]
    nbc = nch - CMP_BLOCK // CMP_STRIDE + 1
    nbs = -(-(p0 + 1) // SEL_BLOCK)
    nbs_pad = -(-nbs // LANES) * LANES
    pk = pool_k.reshape(n_pool, page * G, hd)
    pv = pool_v.reshape(n_pool, page * G, hd)
    pspec = lambda i: pl.BlockSpec((1, page * G, hd),
                                   functools.partial(lambda b, s, tbl, i: (tbl[b * n_pages + s * P + i], 0, 0), i=i))
    whole = lambda a: pl.BlockSpec(a.shape, lambda b, s, tbl: (0,) * a.ndim)
    per_seq = lambda shape: pl.BlockSpec((1,) + shape, lambda b, s, tbl: (b,) + (0,) * len(shape))
    grid_spec = pltpu.PrefetchScalarGridSpec(
        num_scalar_prefetch=1,
        grid=(bsz, n_pages // P),
        in_specs=[pspec(i) for i in range(P)] * 2
        + [whole(w1), per_seq((G, SUBLANES, hd)), whole(pos), whole(b1), whole(w2), whole(b2)],
        out_specs=[per_seq((G, SUBLANES, hd)), per_seq((G, LANES))],
        scratch_shapes=[pltpu.VMEM((2 * P, cpp * CHUNK_PITCH, hd), F32),
                        pltpu.VMEM((G, nch, hid2), F32), pltpu.VMEM((G, nch, hid2), F32)],
    )
    return pl.pallas_call(
        functools.partial(_dec_cmp_kernel, P=P, nbc=nbc, nbs=nbs, nbs_pad=nbs_pad, p0=p0),
        grid_spec=grid_spec,
        out_shape=[jax.ShapeDtypeStruct((bsz, G, SUBLANES, hd), F32),
                   jax.ShapeDtypeStruct((bsz, G, LANES), jnp.int32)],
        compiler_params=_cparams("arbitrary", "arbitrary"),
    )(table.reshape(-1), *([pk] * P), *([pv] * P), w1, q, pos, b1, w2, b2)


def _softmax_with_new_key(qb, k_b, v_b, bias, s_new, v_new):
    s = _dot_nt(qb, k_b)
    if bias is not None:
        s = s + bias
    mx = jnp.maximum(jnp.max(s, axis=-1, keepdims=True), s_new)
    p = jnp.exp(s - mx)
    p_new = jnp.exp(s_new - mx)
    return (_dot(p.astype(BF16), v_b) + p_new * v_new) / (jnp.sum(p, axis=-1, keepdims=True) + p_new)


def _dec_attn_kernel(sel_ref, tbl_ref, *refs, n_cache_blocks, n_sel):
    n_blk = NSA_KV * n_sel
    ks_refs, vs_refs = refs[:n_blk], refs[n_blk:2 * n_blk]
    q_ref, new_ref, kw_ref, vw_ref, oc_ref, g_ref, bg_ref, o_ref = refs[2 * n_blk:]
    b = pl.program_id(0)
    slot = _iota((SUBLANES, n_sel * SEL_BLOCK), 1) // SEL_BLOCK
    head = _iota((SUBLANES, LANES), 0)
    lane = _iota((SUBLANES, LANES), 1)
    n_win = kw_ref.shape[1] // NSA_KV
    for g in range(NSA_KV):
        q = q_ref[0, g] * (NSA_HD ** -0.5)
        qb = q.astype(BF16)
        new = new_ref[0, g]

        mine = lambda r, n: r[pl.ds(g, n, stride=NSA_KV), :]
        picks = range(g * n_sel, (g + 1) * n_sel)
        k_all = jnp.concatenate([mine(ks_refs[i].at[0, 0], SEL_BLOCK) for i in picks], axis=0).astype(BF16)
        v_all = jnp.concatenate([mine(vs_refs[i].at[0, 0], SEL_BLOCK) for i in picks], axis=0).astype(BF16)
        bias = jnp.zeros((SUBLANES, n_sel * SEL_BLOCK), F32)
        for s in range(n_sel):
            blk = sel_ref[(b * NSA_KV + g) * n_sel + s]
            bias = jnp.where(slot == s, jnp.where(blk < n_cache_blocks, 0.0, NEG), bias)
        o_slc = _softmax_with_new_key(qb, k_all, v_all, bias,
                                      jnp.sum(q * new[0:1], axis=-1, keepdims=True), new[1:2])

        o_win = _softmax_with_new_key(qb, mine(kw_ref.at[0], n_win).astype(BF16),
                                      mine(vw_ref.at[0], n_win).astype(BF16), None,
                                      jnp.sum(q * new[2:3], axis=-1, keepdims=True), new[3:4])

        gate = jnp.broadcast_to(_sigmoid(g_ref[0, g] + bg_ref[:, g * LANES:(g + 1) * LANES]), (SUBLANES, LANES))
        o = jnp.zeros((SUBLANES, NSA_HD), F32)
        for br, ob in enumerate((oc_ref[0, g], o_slc, o_win)):
            o = o + jnp.sum(jnp.where(lane == 3 * head + br, gate, 0.0), axis=-1, keepdims=True) * ob
        o_ref[0, g] = o


def dec_attn(sel, table, pool_ks, pool_vs, q, new, kw_past, vw_past, o_cmp, gates, b_gate):
    n_pool, page, G, hd = pool_ks.shape
    bsz, n_pages = table.shape
    halves = page // SEL_BLOCK
    wb = kw_past.shape[1]
    assert wb <= WINDOW
    n_sel = sel.shape[-1]
    pk = pool_ks.reshape(n_pool, halves, SEL_BLOCK * G, hd)
    pv = pool_vs.reshape(n_pool, halves, SEL_BLOCK * G, hd)
    n_cache_blocks = n_pages * halves

    def page_map(b, sel_ref, tbl_ref, i):
        blk = jnp.minimum(sel_ref[b * G * n_sel + i], n_cache_blocks - 1)
        return (tbl_ref[b * n_pages + blk // halves], blk % halves, 0, 0)

    per_seq = lambda shape: pl.BlockSpec((1,) + shape, lambda b, sel_ref, tbl_ref: (b,) + (0,) * len(shape))
    small = per_seq((G, SUBLANES, hd))
    blocks = [pl.BlockSpec((1, 1, SEL_BLOCK * G, hd), functools.partial(page_map, i=i)) for i in range(G * n_sel)]
    grid_spec = pltpu.PrefetchScalarGridSpec(
        num_scalar_prefetch=2,
        grid=(bsz,),
        in_specs=blocks + blocks + [small, small, per_seq((wb * G, hd)), per_seq((wb * G, hd)), small,
                                    per_seq((G, 1, LANES)),
                                    pl.BlockSpec((1, G * LANES), lambda b, sel_ref, tbl_ref: (0, 0))],
        out_specs=small,
    )
    return pl.pallas_call(
        functools.partial(_dec_attn_kernel, n_cache_blocks=n_cache_blocks, n_sel=n_sel),
        grid_spec=grid_spec,
        out_shape=jax.ShapeDtypeStruct((bsz, G, SUBLANES, hd), F32),
        compiler_params=_cparams("arbitrary"),
    )(sel.reshape(-1), table.reshape(-1), *([pk] * (G * n_sel)), *([pv] * (G * n_sel)), q, new,
      kw_past.reshape(bsz, wb * G, hd), vw_past.reshape(bsz, wb * G, hd), o_cmp,
      gates.reshape(bsz, G, 1, LANES), b_gate)


def _gate_weights(wt, layer, row0, n):
    rows = lax.slice(wt, (layer, row0, 0), (layer + 1, row0 + n, wt.shape[2]))
    return jnp.pad(rows, ((0, 0), (0, LANES - n), (0, 0))), 0


def _nsa_gate_weights(wt, layer, b_gate, row0):
    per = 3 * NSA_HPG
    k = wt.shape[2]
    rows = lax.slice(wt, (layer, row0, 0), (layer + 1, row0 + NSA_KV * per, k)).reshape(NSA_KV, per, k)
    wg = jnp.pad(rows, ((0, 0), (0, LANES - per), (0, 0))).reshape(1, NSA_KV * LANES, k)
    bg = jnp.pad(b_gate.reshape(NSA_KV, per), ((0, 0), (0, LANES - per))).reshape(1, NSA_KV * LANES)
    return (wg, 0), bg


def kernel(x_prompt, x_sample, state_a_C, state_a_n, state_a_m, state_b_conv, cache_c_k_cmp, cache_c_v_cmp, cache_c_k_slc, cache_c_v_slc, cache_c_k_win, cache_c_v_win, page_table, norm_w, ffn_w_in, ffn_w_out, a_w_in, a_b_gates, a_head_norm, a_w_out, b_w_in, b_conv_w, b_w_out, c_w_in, c_b_gate, c_cmp_pos, c_cmp_w1, c_cmp_b1, c_cmp_w2, c_cmp_b2, c_w_out):
    bp, T, D = x_prompt.shape
    bs = x_sample.shape[0]
    assert x_sample.shape[1] == 1
    depth = norm_w.shape[0]
    G, hd = NSA_KV, NSA_HD
    past_len = page_table.shape[1] * cache_c_k_cmp.shape[2]
    xp = x_prompt.reshape(bp * T, D)
    xs = x_sample.reshape(bs, D)
    hp_in = norm_cast(xp, norm_w[0, 0])
    hs_in = norm_cast(xs, norm_w[0, 0])
    a_wt = jnp.swapaxes(a_w_in, 1, 2)
    c_wt = jnp.swapaxes(c_w_in, 1, 2)
    w_outs = {0: a_w_out, 1: b_w_out, 2: c_w_out}
    pa, sa, pb, sb, pc, sc = [], [], [], [], [], []
    for i in range(depth):
        kind, j = i % 3, i // 3
        nw = norm_w[i]
        w_out_f32 = (w_outs[kind], j)
        conv = None
        if kind == 0:
            hdv = a_head_norm.shape[1]
            n_main = 3 * hdv
            wg = _gate_weights(a_wt, j, n_main, 2 * MLSTM_HEADS)
            qkv_p, qkv_s, w_out = proj_plain(hp_in, hs_in, (a_wt, j), 0, 2 * hdv, MXU_TILE, transposed=True,
                                             cast=w_out_f32, prompt_dtype=BF16)
            o_p, o_s = proj_plain(hp_in, hs_in, (a_wt, j), 2 * hdv, hdv, MXU_TILE, transposed=True)
            qkvo_s = jnp.concatenate([qkv_s, o_s], axis=1)
            g_p, g_s = proj_plain(hp_in, hs_in, wg, 0, LANES, LANES, transposed=True)
            hp, c_p, n_p, m_p = mlstm_prompt(qkv_p, o_p, g_p, a_b_gates[j], a_head_norm[j], bp, T)
            hs, c_s, n_s, m_s = mlstm_step(qkvo_s, g_s, a_b_gates[j], a_head_norm[j],
                                           state_a_C[j], state_a_n[j], state_a_m[j])
            pa.append((c_p, n_p, m_p[:, :, 0]))
            sa.append((c_s, n_s, m_s))
        elif kind == 1:
            (bg_p, z_p), (bg_s, z_s), w_out = proj_conv_in(hp_in, hs_in, (b_w_in, j), D, w_out_f32)
            hp, conv = None, (bg_p, z_p, b_conv_w[j], T)
            buf = state_b_conv[j]
            hs = conv_gate_step(bg_s, z_s, buf.transpose(1, 0, 2), b_conv_w[j]).astype(BF16)
            pb.append(z_p.reshape(bp, T, D)[:, T - (CONV_W - 1):])
            sb.append(jnp.concatenate([buf, z_s[:, None]], axis=1)[:, -(CONV_W - 1):])
        else:
            nq_cols = NSA_HPG * G * hd
            nheads = G * NSA_HPG
            wg, bgate = _nsa_gate_weights(c_wt, j, c_b_gate[j], nq_cols + 6 * G * hd)
            cw = _cmp_weights(c_cmp_pos[j], c_cmp_w1[j], c_cmp_b1[j], c_cmp_w2[j], c_cmp_b2[j])
            q_p, q_s, w_out = proj_plain(hp_in, hs_in, (c_wt, j), 0, nq_cols, MXU_TILE, transposed=True, cast=w_out_f32)
            kv_p, kv_rows_p, kv_s = [], [], []
            for half in range(2):
                (tok_p, rows_p), (tok_s, _) = proj_groups(hp_in, hs_in, (c_wt, j), nq_cols + half * 3 * G * hd,
                                                         3, G * hd, transposed=True)
                kv_p, kv_rows_p, kv_s = kv_p + list(tok_p), kv_rows_p + list(rows_p), kv_s + list(tok_s)
            gt_p, gt_s = proj_plain(hp_in, hs_in, wg, 0, G * LANES, G * LANES, transposed=True)
            ck, cv = cmp_prompt(kv_p[0], kv_p[1], bp, T, cw)
            hp = nsa_attn_prompt(q_p, ck, cv, kv_p[2], kv_p[3], kv_p[4], kv_p[5], gt_p, bgate, bp, T)
            kv_rows_p = [a.reshape(bp, T, G, hd) for a in kv_rows_p]
            keep = min(WINDOW, T)
            pc.append(tuple(kv_rows_p[:4]) + tuple(a[:, T - keep:] for a in kv_rows_p[4:]))
            q_s = jnp.pad(q_s.reshape(bs, G, NSA_HPG, hd), ((0, 0), (0, 0), (0, SUBLANES - NSA_HPG), (0, 0)))
            kv_s = jnp.stack([a.reshape(bs, G, hd) for a in kv_s], axis=1)
            new = jnp.pad(kv_s[:, 2:6].transpose(0, 2, 1, 3), ((0, 0), (0, 0), (0, SUBLANES - 4), (0, 0)))
            o_cmp, ranked = dec_cmp(cache_c_k_cmp[j], cache_c_v_cmp[j], page_table, q_s, cw, past_len)
            n_sel = min(N_SELECT, -(-(past_len + 1) // SEL_BLOCK))
            o_s = dec_attn(ranked[:, :, :n_sel], page_table, cache_c_k_slc[j], cache_c_v_slc[j], q_s, new,
                           cache_c_k_win[j], cache_c_v_win[j], o_cmp, gt_s, bgate)
            hs = o_s[:, :, :NSA_HPG].reshape(bs, nheads * hd).astype(BF16)
            keep_s = min(WINDOW, cache_c_k_win.shape[2] + 1)
            win = [jnp.concatenate([c[j], kv_s[:, a][:, None]], axis=1)[:, -keep_s:]
                   for c, a in ((cache_c_k_win, 4), (cache_c_v_win, 5))]
            sc.append(tuple(kv_s[:, a][:, None] for a in range(4)) + tuple(win))
        last = i == depth - 1
        nw_next = norm_w[i + 1, 0] if not last else nw[0]
        (xp, xs), (hp_mid, hs_mid) = proj_out(hp, hs, w_out, xp, xs, nw[1], nw[2], conv=conv)
        act_p, act_s, f_out = proj_swiglu(hp_mid, hs_mid, (ffn_w_in, i), ffn_w_out.shape[1], (ffn_w_out, i))
        (xp, xs), (hp_in, hs_in) = proj_out(act_p, act_s, f_out, xp, xs, nw[3], nw_next, not last)

    stack = lambda items: [jnp.stack(a) for a in zip(*items)]
    p_a, s_a = stack(pa), stack(sa)
    p_c, s_c = stack(pc), stack(sc)
    return (xp.reshape(bp, T, D), xs.reshape(bs, 1, D), *p_a, *s_a, jnp.stack(pb), jnp.stack(sb), *p_c, *s_c)
```

```python
import functools
import math

import jax
import jax.numpy as jnp
from jax import lax
from jax.experimental import pallas as pl
from jax.experimental.pallas import tpu as pltpu

F32 = jnp.float32
BF16 = jnp.bfloat16

EPS = 1e-6
NEG = -1e30
BIG = 1e30

LANES = 128
SUBLANES = 8
VMEM_LIMIT = 56 * 1024 * 1024

MLSTM_HEADS = 8
MLSTM_CHUNK = 512
FORGET_BIAS_COL = MLSTM_HEADS
CONV_W = 3
NSA_HD = 128
NSA_KV = 4
NSA_HPG = 4
CMP_BLOCK = 32
CMP_STRIDE = 16
SEL_BLOCK = 64
N_SELECT = 16
WINDOW = 512
ATT_TILE = 512
PAGES_PER_STEP = 16
MXU_TILE = 1024
HALF_TILE = 512
OUT_SUB_ROWS = 256
OUT_VMEM_SLACK = 6 * 1024 * 1024


def _cparams(*sem, vmem=VMEM_LIMIT):
    return pltpu.CompilerParams(dimension_semantics=sem, vmem_limit_bytes=vmem)


def _dot(a, b):
    return jnp.dot(a, b, preferred_element_type=F32)


def _dot_nt(a, b):
    return lax.dot_general(a, b, (((1,), (1,)), ((), ())), preferred_element_type=F32)


def _dot_tn(a, b):
    return lax.dot_general(a, b, (((0,), (0,)), ((), ())), preferred_element_type=F32)


def _split_bf16(x):
    hi = x.astype(BF16)
    lo = (x - hi.astype(F32)).astype(BF16)
    return hi, lo


def _iota(shape, dim):
    return lax.broadcasted_iota(jnp.int32, shape, dim)


def _sigmoid(x):
    return 1.0 / (1.0 + jnp.exp(-x))


def _rms(x):
    return x * lax.rsqrt(jnp.mean(x * x, axis=-1, keepdims=True) + EPS)


def _row_tile(m, want):
    t = min(m, want)
    assert m % t == 0, (m, t)
    return t


def _norm_kernel(x_ref, w_ref, o_ref):
    o_ref[...] = (_rms(x_ref[...]) * w_ref[...]).astype(BF16)


def norm_cast(x, w):
    m, d = x.shape
    tm = _row_tile(m, HALF_TILE)
    return pl.pallas_call(
        _norm_kernel,
        grid=(m // tm,),
        in_specs=[pl.BlockSpec((tm, d), lambda i: (i, 0)), pl.BlockSpec((1, d), lambda i: (0, 0))],
        out_specs=pl.BlockSpec((tm, d), lambda i: (i, 0)),
        out_shape=jax.ShapeDtypeStruct((m, d), BF16),
        compiler_params=_cparams("parallel"),
    )(x, w.reshape(1, d))


def _proj_kernel(x_ref, xs_ref, *refs, n_groups, n_out, epilogue, transposed, with_cast):
    w_refs = refs[:n_groups]
    refs = refs[n_groups:]
    if with_cast:
        cast_in_ref, refs = refs[0], refs[1:]
        refs[2 * n_out][...] = cast_in_ref[...].astype(BF16)
    out_refs = refs[:n_out]
    sample_out_refs = refs[n_out:2 * n_out]
    wb_ref = refs[-1]
    mm = _dot_nt if transposed else _dot

    @pl.when(pl.program_id(1) == 0)
    def _():
        for g in range(n_groups):
            wb_ref[g] = w_refs[g][...].astype(BF16)
        xs = xs_ref[...]
        epilogue([mm(xs, wb_ref[g]) for g in range(n_groups)], sample_out_refs)

    x = x_ref[...]
    epilogue([mm(x, wb_ref[g]) for g in range(n_groups)], out_refs)


def _proj(x, xs, w, *, col0, tn, n_tiles, group_stride, n_groups, epilogue, outs, tm_want=MXU_TILE, transposed=False,
          cast=None, sample_f32=False):
    w, layer = w
    m, k = x.shape
    s = xs.shape[0]
    tm = _row_tile(m, tm_want)
    assert col0 % tn == 0
    b0 = col0 // tn
    if transposed:
        w_block = (None, tn, k)
        w_map = lambda j, i, off: (layer, off + j, 0)
    else:
        w_block = (None, k, tn)
        w_map = lambda j, i, off: (layer, 0, off + j)
    w_specs = [pl.BlockSpec(w_block, functools.partial(w_map, off=b0 + g * group_stride)) for g in range(n_groups)]
    n_out = len(outs)

    def out_for(rows, row_tile, row_map, n, dt, by_rows):
        if not by_rows:
            return jax.ShapeDtypeStruct((rows, n), dt), pl.BlockSpec((row_tile, tn), row_map)
        assert n_tiles == 1 and n == tn
        per = n // LANES
        return (jax.ShapeDtypeStruct((rows * per, LANES), dt),
                pl.BlockSpec((row_tile * per, LANES), lambda j, i: (row_map(j, i)[0], 0)))

    prompt_outs = [out_for(m, tm, lambda j, i: (i, j), *o) for o in outs]
    sample_outs = [out_for(s, s, lambda j, i: (0, j), n, F32 if sample_f32 else dt, by_rows)
                   for n, dt, by_rows in outs]
    all_outs = prompt_outs + sample_outs
    operands = [x, xs] + [w] * n_groups
    in_specs = [pl.BlockSpec((tm, k), lambda j, i: (i, 0)), pl.BlockSpec((s, k), lambda j, i: (0, 0))] + w_specs
    n_i = m // tm
    if cast is not None:
        cw, c_layer = cast
        _, c_rows, c_cols = cw.shape
        bf16_rows = 2 * SUBLANES
        blk = next(r for r in range(bf16_rows, c_rows + 1, bf16_rows)
                   if c_rows % r == 0 and c_rows // r <= n_tiles * n_i)
        last = c_rows // blk - 1
        step = lambda j, i: jnp.minimum(j * n_i + i, last)
        in_specs.append(pl.BlockSpec((None, blk, c_cols), lambda j, i: (c_layer, step(j, i), 0)))
        operands.append(cw)
        all_outs.append((jax.ShapeDtypeStruct((c_rows, c_cols), BF16),
                         pl.BlockSpec((blk, c_cols), lambda j, i: (step(j, i), 0))))
    res = pl.pallas_call(
        functools.partial(_proj_kernel, n_groups=n_groups, n_out=n_out, epilogue=epilogue, transposed=transposed,
                          with_cast=cast is not None),
        grid=(n_tiles, n_i),
        in_specs=in_specs,
        out_specs=[spec for _, spec in all_outs],
        out_shape=[shape for shape, _ in all_outs],
        scratch_shapes=[pltpu.VMEM((n_groups,) + w_block[1:], BF16)],
        compiler_params=_cparams("arbitrary", "arbitrary"),
    )(*operands)
    return res[:n_out], res[n_out:2 * n_out], (res[2 * n_out] if cast is not None else None)


def _ep_plain(accs, outs):
    for acc, o in zip(accs, outs):
        o[...] = acc.astype(o.dtype)


def proj_plain(x, xs, w, col0, n_cols, tn, transposed=False, cast=None, prompt_dtype=F32):
    p, s, c = _proj(x, xs, w, col0=col0, tn=tn, n_tiles=n_cols // tn, group_stride=0, n_groups=1,
                    epilogue=_ep_plain, outs=[(n_cols, prompt_dtype, False)], transposed=transposed, cast=cast,
                    sample_f32=True)
    return (p[0], s[0]) if cast is None else (p[0], s[0], c)


def _ep_both_layouts(accs, outs):
    n = len(accs)
    for acc, o_tok, o_rows in zip(accs, outs[:n], outs[n:]):
        o_tok[...] = acc
        per = acc.shape[1] // LANES
        rows = acc.shape[0]
        for c in range(per):
            o_rows[pl.ds(c, rows, stride=per), :] = acc[:, c * LANES:(c + 1) * LANES]


def proj_groups(x, xs, w, col0, n_groups, group_cols, transposed=False):
    outs = [(group_cols, F32, False)] * n_groups + [(group_cols, F32, True)] * n_groups
    p, s, _ = _proj(x, xs, w, col0=col0, tn=group_cols, n_tiles=1, group_stride=1, n_groups=n_groups,
                    epilogue=_ep_both_layouts, outs=outs, tm_want=HALF_TILE, transposed=transposed)
    return (p[:n_groups], p[n_groups:]), (s[:n_groups], s[n_groups:])


def _ep_swiglu(accs, outs):
    g, u = accs
    outs[0][...] = (g * _sigmoid(g) * u).astype(BF16)


def proj_swiglu(x, xs, w, ff, cast):
    tn = HALF_TILE
    p, s, c = _proj(x, xs, w, col0=0, tn=tn, n_tiles=ff // tn, group_stride=ff // tn, n_groups=2,
                    epilogue=_ep_swiglu, outs=[(ff, BF16, False)], cast=cast)
    return p[0], s[0], c


def _ep_conv_in(accs, outs):
    bg, cg, u = accs
    outs[0][...] = bg
    outs[1][...] = cg * u


def proj_conv_in(x, xs, w, d, cast):
    tn = HALF_TILE
    return _proj(x, xs, w, col0=0, tn=tn, n_tiles=d // tn, group_stride=d // tn, n_groups=3,
                 epilogue=_ep_conv_in, outs=[(d, F32, False), (d, F32, False)], cast=cast)


def _conv_gated(bg_ref, z_ref, zp_ref, cw_ref, T):
    tm = z_ref.shape[0]
    z = z_ref[...]
    zc = jnp.concatenate([zp_ref[...], z], axis=0)
    tpos = (pl.program_id(0) * tm + _iota((tm, 1), 0)) % T
    y = cw_ref[CONV_W - 1:CONV_W, :] * z
    for back in range(1, CONV_W):
        zb = zc[SUBLANES - back:SUBLANES - back + tm, :]
        y = y + cw_ref[CONV_W - 1 - back:CONV_W - back, :] * jnp.where(tpos >= back, zb, 0.0)
    return (bg_ref[...] * y).astype(BF16)


def _out_kernel(*refs, emit_next, conv_T):
    if conv_T:
        h = _conv_gated(*refs[:4], conv_T)
        rows_of = lambda rows: h[rows, :]
        refs = refs[4:]
    else:
        h_ref = refs[0]
        rows_of = lambda rows: h_ref[rows, :]
        refs = refs[1:]
    hs_ref, w_ref, r_ref, rs_ref, nwa_ref, nwb_ref, *out_refs = refs

    def finalize(y, r, x_ref, xn_ref):
        x_new = r + _rms(y) * nwa_ref[...]
        x_ref[...] = x_new
        if emit_next:
            xn_ref[...] = (_rms(x_new) * nwb_ref[...]).astype(BF16)

    if emit_next:
        x_ref, xn_ref, xs_ref, xsn_ref = out_refs
    else:
        (x_ref, xs_ref), xn_ref, xsn_ref = out_refs, None, None

    @pl.when(pl.program_id(0) == 0)
    def _():
        finalize(_dot(hs_ref[...], w_ref[...]), rs_ref[...], xs_ref, xsn_ref)

    tm = r_ref.shape[0]
    sub = math.gcd(tm, OUT_SUB_ROWS)
    for r0 in range(0, tm, sub):
        rows = slice(r0, r0 + sub)
        finalize(_dot(rows_of(rows), w_ref[...]), r_ref[rows, :], x_ref.at[rows, :],
                 xn_ref.at[rows, :] if emit_next else None)


def proj_out(h, hs, w, resid, resid_s, nw_post, nw_next, emit_next=True, conv=None):
    m, d = resid.shape
    k = w.shape[0]
    s = hs.shape[0]
    tm = _row_tile(m, 2 * OUT_SUB_ROWS)
    h_bytes = tm * k * (2 if conv is None else 8)
    conv_tmp = 0 if conv is None else h_bytes
    vmem = k * d * 2 + 2 * (h_bytes + tm * d * (4 + 4 + (2 if emit_next else 0))) + conv_tmp + OUT_VMEM_SLACK
    row = lambda i: (i, 0)
    fixed = lambda i: (0, 0)
    if conv is None:
        h_ops, h_specs, conv_T = [h], [pl.BlockSpec((tm, k), row)], 0
    else:
        bg, z, conv_w, conv_T = conv
        assert conv_T % tm == 0
        per = tm // SUBLANES
        h_ops = [bg, z, z, conv_w]
        h_specs = [pl.BlockSpec((tm, k), row), pl.BlockSpec((tm, k), row),
                   pl.BlockSpec((SUBLANES, k), lambda i: (jnp.maximum(i * per - 1, 0), 0)),
                   pl.BlockSpec((CONV_W, k), fixed)]
    f32_out = [jax.ShapeDtypeStruct((m, d), F32), jax.ShapeDtypeStruct((s, d), F32)]
    bf16_out = [jax.ShapeDtypeStruct((m, d), BF16), jax.ShapeDtypeStruct((s, d), BF16)]
    specs = [pl.BlockSpec((tm, d), row), pl.BlockSpec((s, d), fixed)]
    if emit_next:
        out_shape = [f32_out[0], bf16_out[0], f32_out[1], bf16_out[1]]
        out_specs = [specs[0], specs[0], specs[1], specs[1]]
    else:
        out_shape, out_specs = f32_out, specs
    res = pl.pallas_call(
        functools.partial(_out_kernel, emit_next=emit_next, conv_T=conv_T),
        grid=(m // tm,),
        in_specs=h_specs + [pl.BlockSpec((s, k), fixed),
                            pl.BlockSpec((k, d), fixed, pipeline_mode=pl.Buffered(1)),
                            pl.BlockSpec((tm, d), row),
                            pl.BlockSpec((s, d), fixed),
                            pl.BlockSpec((1, d), fixed),
                            pl.BlockSpec((1, d), fixed)],
        out_specs=out_specs,
        out_shape=out_shape,
        compiler_params=_cparams("arbitrary", vmem=vmem),
    )(*h_ops, hs, w, resid, resid_s, nw_post.reshape(1, d), nw_next.reshape(1, d))
    if emit_next:
        return (res[0], res[2]), (res[1], res[3])
    return (res[0], res[1]), (None, None)


def _log_sigmoid(x):
    return jnp.minimum(x, 0.0) - jnp.log1p(jnp.exp(-jnp.abs(x)))


def _mlstm_kernel(q_ref, k_ref, v_ref, o_ref, g_ref, bg_ref, hn_ref, hg_ref, c_ref, n_ref, m_ref, cn_ref,
                  *, L, H, DK, DV):
    chunk = pl.program_id(1)

    @pl.when(chunk == 0)
    def _():
        cn_ref[...] = jnp.zeros_like(cn_ref)
        m_ref[...] = jnp.zeros_like(m_ref)

    gpre = g_ref[...] + bg_ref[...]
    lf = _log_sigmoid(gpre)
    rows = _iota((L, L), 0)
    cols = _iota((L, L), 1)
    causal = rows >= cols
    tril = jnp.where(causal, 1.0, 0.0).astype(BF16)
    lf_hi, lf_lo = _split_bf16(lf)
    bcum = _dot(tril, lf_hi) + _dot(tril, lf_lo)
    g_t = gpre.T
    b_t = bcum.T
    for h in range(H):
        f = FORGET_BIAS_COL + h
        a_col = bcum[:, f:f + 1]
        i_col = gpre[:, h:h + 1]
        b_row = b_t[f:f + 1, :]
        i_row = g_t[h:h + 1, :]
        m_prev = m_ref[0, h:h + 1, 0:1]
        log_d = jnp.where(causal, a_col - b_row + i_row, NEG)
        log_inter = a_col + m_prev
        m_t = jnp.maximum(log_inter, jnp.max(log_d, axis=-1, keepdims=True))
        d = jnp.exp(log_d - m_t)
        inter = jnp.exp(log_inter - m_t)
        kh = k_ref[:, h * DK:(h + 1) * DK].astype(F32) * (DK ** -0.5)
        v1 = jnp.concatenate([v_ref[:, h * DV:(h + 1) * DV], jnp.ones((L, LANES), BF16)], axis=1)
        qb = q_ref[:, h * DK:(h + 1) * DK]
        s = _dot_nt(qb, kh.astype(BF16)) * d
        cn_old = cn_ref[h]
        both = _dot(s.astype(BF16), v1) + inter * _dot(qb, cn_old.astype(BF16))
        num, den = both[:, :DV], both[:, DV:]
        scale = 1.0 / jnp.maximum(jnp.abs(den), jnp.exp(-m_t))
        hh = _rms(num * jnp.concatenate([scale] * (DV // LANES), axis=1))
        gate = _sigmoid(o_ref[:, h * DV:(h + 1) * DV])
        hg_ref[:, h * DV:(h + 1) * DV] = (hh * hn_ref[:, h * DV:(h + 1) * DV] * gate).astype(BF16)
        m_new = m_t[L - 1:L, :]
        w_col = jnp.exp(a_col[L - 1:L, :] - a_col + i_col - m_new)
        decay = jnp.exp(log_inter[L - 1:L, :] - m_new)
        cn_new = decay * cn_old + _dot_tn((kh * w_col).astype(BF16), v1)
        cn_ref[h] = cn_new
        m_ref[0, h:h + 1, :] = jnp.broadcast_to(m_new, (1, LANES))

        @pl.when(chunk == pl.num_programs(1) - 1)
        def _():
            c_ref[0, h] = cn_new[:, :DV]
            n_ref[0, h:h + 1, :] = cn_new[:, DV:].T[0:1, :]


def mlstm_prompt(qkv, o, gates, b_gates, head_norm, bsz, T):
    H = MLSTM_HEADS
    m, hdv = o.shape
    hdk = hdv // 2
    DK, DV = hdk // H, hdv // H
    L = math.gcd(T, MLSTM_CHUNK)
    nc = T // L
    row = lambda b, c: (b * nc + c, 0)
    state = lambda b, c: (b, 0, 0)
    return pl.pallas_call(
        functools.partial(_mlstm_kernel, L=L, H=H, DK=DK, DV=DV),
        grid=(bsz, nc),
        in_specs=[pl.BlockSpec((L, hdk), row),
                  pl.BlockSpec((L, hdk), lambda b, c: (b * nc + c, 1)),
                  pl.BlockSpec((L, hdv), lambda b, c: (b * nc + c, 1)),
                  pl.BlockSpec((L, hdv), row),
                  pl.BlockSpec((L, LANES), row),
                  pl.BlockSpec((1, LANES), lambda b, c: (0, 0)),
                  pl.BlockSpec((1, hdv), lambda b, c: (0, 0))],
        out_specs=[pl.BlockSpec((L, hdv), row),
                   pl.BlockSpec((1, H, DK, DV), lambda b, c: (b, 0, 0, 0)),
                   pl.BlockSpec((1, H, DK), state),
                   pl.BlockSpec((1, H, LANES), state)],
        out_shape=[jax.ShapeDtypeStruct((m, hdv), BF16),
                   jax.ShapeDtypeStruct((bsz, H, DK, DV), F32),
                   jax.ShapeDtypeStruct((bsz, H, DK), F32),
                   jax.ShapeDtypeStruct((bsz, H, LANES), F32)],
        scratch_shapes=[pltpu.VMEM((H, DK, DV + LANES), F32)],
        compiler_params=_cparams("arbitrary", "arbitrary"),
    )(qkv, qkv, qkv, o, gates, _pad_lanes(b_gates.reshape(1, -1)), head_norm.reshape(1, hdv))


def _pad_lanes(x, width=LANES):
    return jnp.pad(x, ((0, 0), (0, width - x.shape[-1])))


def _to_col(row, n):
    eye = _iota((n, n), 0) == _iota((n, n), 1)
    return jnp.sum(jnp.where(eye, jnp.broadcast_to(row, (n, n)), 0.0), axis=-1, keepdims=True)


def _mlstm_step_kernel(x_ref, g_ref, bg_ref, hn_ref, c0_ref, n0_ref, m0_ref, hg_ref, c_ref, n_ref, m_ref,
                       *, H, DK, DV):
    x = x_ref[0]
    gpre = g_ref[0] + bg_ref[...]
    lf = _log_sigmoid(gpre)
    hdk = H * DK
    hdv = H * DV
    for h in range(H):
        f = FORGET_BIAS_COL + h
        ig = gpre[:, h:h + 1]
        m_prev = m0_ref[0, h:h + 1, :]
        log_inter = lf[:, f:f + 1] + m_prev
        m_t = jnp.maximum(log_inter, ig)
        d = jnp.exp(ig - m_t)
        inter = jnp.exp(log_inter - m_t)
        q = x[:, h * DK:(h + 1) * DK]
        k = x[:, hdk + h * DK:hdk + (h + 1) * DK] * (DK ** -0.5)
        v = x[:, 2 * hdk + h * DV:2 * hdk + (h + 1) * DV]
        og = x[:, 2 * hdk + hdv + h * DV:2 * hdk + hdv + (h + 1) * DV]
        c_old = c0_ref[0, h]
        n_old = n0_ref[0, h:h + 1, :]
        s = jnp.sum(q * k, axis=-1, keepdims=True) * d
        q_col = _to_col(q, DK)
        k_col = _to_col(k, DK)
        num = s * v + inter * jnp.sum(q_col * c_old, axis=0, keepdims=True)
        den = s + inter * jnp.sum(q * n_old, axis=-1, keepdims=True)
        hh = _rms(num / jnp.maximum(jnp.abs(den), jnp.exp(-m_t)))
        hg_ref[0, :, h * DV:(h + 1) * DV] = (hh * hn_ref[:, h * DV:(h + 1) * DV] * _sigmoid(og)).astype(BF16)
        c_ref[0, h] = inter * c_old + d * (k_col * v)
        n_ref[0, h:h + 1, :] = inter * n_old + d * k
        m_ref[0, h:h + 1, :] = m_t


def mlstm_step(qkvo, gates, b_gates, head_norm, c0, n0, m0):
    bsz, H, DK, DV = c0.shape
    wid = qkvo.shape[1]
    hdv = H * DV
    one = lambda b: (b, 0, 0)
    hg, c1, n1, m1 = pl.pallas_call(
        functools.partial(_mlstm_step_kernel, H=H, DK=DK, DV=DV),
        grid=(bsz,),
        in_specs=[pl.BlockSpec((1, 1, wid), one),
                  pl.BlockSpec((1, 1, LANES), one),
                  pl.BlockSpec((1, LANES), lambda b: (0, 0)),
                  pl.BlockSpec((1, hdv), lambda b: (0, 0)),
                  pl.BlockSpec((1, H, DK, DV), lambda b: (b, 0, 0, 0)),
                  pl.BlockSpec((1, H, DK), one),
                  pl.BlockSpec((1, H, 1), one)],
        out_specs=[pl.BlockSpec((1, 1, hdv), one),
                   pl.BlockSpec((1, H, DK, DV), lambda b: (b, 0, 0, 0)),
                   pl.BlockSpec((1, H, DK), one),
                   pl.BlockSpec((1, H, 1), one)],
        out_shape=[jax.ShapeDtypeStruct((bsz, 1, hdv), BF16),
                   jax.ShapeDtypeStruct((bsz, H, DK, DV), F32),
                   jax.ShapeDtypeStruct((bsz, H, DK), F32),
                   jax.ShapeDtypeStruct((bsz, H, 1), F32)],
        compiler_params=_cparams("parallel"),
    )(qkvo.reshape(bsz, 1, wid), gates.reshape(bsz, 1, LANES), _pad_lanes(b_gates.reshape(1, -1)),
      head_norm.reshape(1, hdv), c0, n0, m0.reshape(bsz, H, 1))
    return hg.reshape(bsz, hdv), c1, n1, m1.reshape(bsz, H)


def _conv_step_kernel(bg_ref, z_ref, buf_ref, cw_ref, o_ref):
    y = cw_ref[CONV_W - 1:CONV_W, :] * z_ref[...]
    for j in range(CONV_W - 1):
        y = y + cw_ref[j:j + 1, :] * buf_ref[j]
    o_ref[...] = bg_ref[...] * y


def conv_gate_step(bg, z, buf, conv_w):
    bsz, d = z.shape
    full = lambda: (0, 0)
    return pl.pallas_call(
        _conv_step_kernel,
        in_specs=[pl.BlockSpec((bsz, d), full), pl.BlockSpec((bsz, d), full),
                  pl.BlockSpec((CONV_W - 1, bsz, d), lambda: (0, 0, 0)), pl.BlockSpec((CONV_W, d), full)],
        out_specs=pl.BlockSpec((bsz, d), full),
        out_shape=jax.ShapeDtypeStruct((bsz, d), F32),
    )(bg, z, buf, conv_w)


def _gelu(x):
    return 0.5 * x * (1.0 + jnp.tanh(math.sqrt(2.0 / math.pi) * (x + 0.044715 * x * x * x)))


def _cmp_finish(a, posw, b1, w2, b2):
    hid = a.shape[1] // 2
    a1 = a[:, hid:]
    a1_next = jnp.concatenate([a1[1:], a1[:1]], axis=0)
    pre = a[:, :hid] + a1_next + b1 + posw[0:1, :hid] + posw[1:2, hid:]
    return _dot(_gelu(pre).astype(BF16), w2) + b2


def _cmp_prompt_kernel(k_ref, v_ref, w1_ref, pos_ref, b1_ref, w2_ref, b2_ref, ck_ref, cv_ref, *, nch):
    for idx, (x_ref, o_ref) in enumerate(((k_ref, ck_ref), (v_ref, cv_ref))):
        w1 = w1_ref[idx]
        lhs = jnp.concatenate([x_ref[pl.ds(p, nch, stride=CMP_STRIDE), :] for p in range(CMP_STRIDE)], axis=1)
        a = _dot(lhs.astype(BF16), w1)
        posw = _dot(pos_ref[idx], w1)
        o_ref[0, 0] = _cmp_finish(a, posw, b1_ref[idx], w2_ref[idx], b2_ref[idx])


def _cmp_weights(cmp_pos, cmp_w1, cmp_b1, cmp_w2, cmp_b2):
    two, ratio, stride, hd, hid = cmp_w1.shape
    w1 = cmp_w1.transpose(0, 2, 3, 1, 4).reshape(two, stride * hd, ratio * hid).astype(BF16)
    pos = cmp_pos.reshape(two, ratio, stride * hd)
    pos = jnp.pad(pos, ((0, 0), (0, SUBLANES - ratio), (0, 0))).astype(BF16)
    return w1, pos, cmp_b1.reshape(two, 1, hid), cmp_w2.astype(BF16), cmp_b2.reshape(two, 1, hd)


def cmp_prompt(kc, vc, bsz, T, cw):
    w1, pos, b1, w2, b2 = cw
    nch = T // CMP_STRIDE
    G = NSA_KV
    out = jax.ShapeDtypeStruct((bsz, G, nch, NSA_HD), F32)
    ospec = pl.BlockSpec((1, 1, nch, NSA_HD), lambda b, g: (b, g, 0, 0))
    whole = lambda a: pl.BlockSpec(a.shape, lambda b, g: (0,) * a.ndim)
    seq = pl.BlockSpec((T, NSA_HD), lambda b, g: (b, g))
    return pl.pallas_call(
        functools.partial(_cmp_prompt_kernel, nch=nch),
        grid=(bsz, G),
        in_specs=[seq, seq, whole(w1), whole(pos), whole(b1), whole(w2), whole(b2)],
        out_specs=[ospec, ospec],
        out_shape=[out, out],
        compiler_params=_cparams("parallel", "parallel"),
    )(kc, vc, w1, pos, b1, w2, b2)


def _overlap_t(nbs_pad, nbc_pad):
    j = _iota((nbs_pad, nbc_pad), 0)
    n = _iota((nbs_pad, nbc_pad), 1)
    lo = jnp.maximum(n * CMP_STRIDE, j * SEL_BLOCK)
    hi = jnp.minimum(n * CMP_STRIDE + CMP_BLOCK, j * SEL_BLOCK + SEL_BLOCK)
    return (jnp.maximum(hi - lo, 0) // CMP_STRIDE).astype(F32)


def _rank_select(score, n_sel, axis):
    n = score.shape[axis]
    idx = _iota(score.shape, axis)
    cnt = jnp.zeros(score.shape, F32)
    for jp in range(n):
        other = lax.slice_in_dim(score, jp, jp + 1, axis=axis)
        tie = jnp.where(idx > jp, 1.0, 0.0)
        cnt = cnt + jnp.where(other > score, 1.0, jnp.where(other == score, tie, 0.0))
    return jnp.where(cnt < n_sel, 1.0, 0.0)


def _attn_first(state, s, vt_b):
    m_ref, l_ref, acc_ref = state
    m = jnp.max(s, axis=0, keepdims=True)
    p = jnp.exp2(s - m)
    m_ref[...] = m
    l_ref[...] = jnp.sum(p, axis=0, keepdims=True)
    acc_ref[...] = _dot(vt_b, p.astype(BF16))


def _get_cols(ref, ranges):
    return ref[...] if ranges is None else jnp.concatenate([ref[:, a:b] for a, b in ranges], axis=1)


def _set_cols(ref, ranges, val):
    if ranges is None:
        ref[...] = val
        return
    at = 0
    for a, b in ranges:
        ref[:, a:b] = val[:, at:at + b - a]
        at += b - a


def _attn_more(state, s, vt_b, ranges=None):
    m_ref, l_ref, acc_ref = state
    m_old = _get_cols(m_ref, ranges)
    m_new = jnp.maximum(m_old, jnp.max(s, axis=0, keepdims=True))
    alpha = jnp.exp2(m_old - m_new)
    p = jnp.exp2(s - m_new)
    _set_cols(m_ref, ranges, m_new)
    _set_cols(l_ref, ranges, alpha * _get_cols(l_ref, ranges) + jnp.sum(p, axis=0, keepdims=True))
    _set_cols(acc_ref, ranges, alpha * _get_cols(acc_ref, ranges) + _dot(vt_b, p.astype(BF16)))


def _attn_empty(state):
    m_ref, l_ref, acc_ref = state
    m_ref[...] = jnp.full(m_ref.shape, NEG, F32)
    l_ref[...] = jnp.zeros_like(l_ref)
    acc_ref[...] = jnp.zeros_like(acc_ref)


def _attn_merged(state_a, state_b):
    (ma_ref, la_ref, acca_ref), (mb_ref, lb_ref, accb_ref) = state_a, state_b
    m = jnp.maximum(ma_ref[...], mb_ref[...])
    wa = jnp.exp2(ma_ref[...] - m)
    wb = jnp.exp2(mb_ref[...] - m)
    return (wa * acca_ref[...] + wb * accb_ref[...]) / (wa * la_ref[...] + wb * lb_ref[...])


def _nsa_attn_kernel(q_ref, ck_ref, cv_ref, ks_ref, vs_ref, kw_ref, vw_ref, g_ref, bg_ref, o_ref,
                     ksb_ref, vst_ref, kwb_ref, vwt_ref, *states, tq, hpg, nbc, nbs, nbs_pad):
    qi = pl.program_id(2)
    t0 = qi * tq
    cols = hpg * tq
    tk = tq
    n_kt = ks_ref.shape[0] // tk
    nbc_pad = ck_ref.shape[2]

    @pl.when(qi == 0)
    def _():
        ksb_ref[...] = ks_ref[...].astype(BF16)
        kwb_ref[...] = kw_ref[...].astype(BF16)
        for kt in range(n_kt):
            vst_ref[kt] = vs_ref[kt * tk:(kt + 1) * tk, :].T.astype(BF16)
            vwt_ref[kt] = vw_ref[kt * tk:(kt + 1) * tk, :].T.astype(BF16)

    qb = (jnp.concatenate([q_ref[:, h * NSA_HD:(h + 1) * NSA_HD] for h in range(hpg)], axis=0)
          * (NSA_HD ** -0.5 * math.log2(math.e))).astype(BF16)

    n = _iota((nbc_pad, cols), 0)
    t_col = t0 + jnp.concatenate([_iota((nbc_pad, tq), 1)] * hpg, axis=1)
    cmask = jnp.where(n < nbc, jnp.where(n * CMP_STRIDE + (CMP_BLOCK - 1) <= t_col, 1.0, 0.0), 0.0)
    sc = jnp.where(cmask > 0.5, _dot_nt(ck_ref[0, 0].astype(BF16), qb), NEG)
    pe = jnp.exp2(sc - jnp.max(sc, axis=0, keepdims=True))
    p_cmp = pe / jnp.sum(pe, axis=0, keepdims=True) * cmask
    o_cmp = _dot(cv_ref[0, 0].T.astype(BF16), p_cmp.astype(BF16))

    p_sum = p_cmp[:, 0:tq]
    for h in range(1, hpg):
        p_sum = p_sum + p_cmp[:, h * tq:(h + 1) * tq]
    ov_t = _overlap_t(nbs_pad, nbc_pad).astype(BF16)
    p_hi, p_lo = _split_bf16(p_sum)
    imp_t = _dot(ov_t, p_hi) + _dot(ov_t, p_lo)
    j = _iota((nbs_pad, tq), 0)
    t = t0 + _iota((nbs_pad, tq), 1)
    t_blk = t // SEL_BLOCK
    forced = jnp.where(j == 0, 1.0, jnp.where(j == t_blk, 1.0, jnp.where(j == t_blk - 1, 1.0, 0.0)))
    visible = jnp.where(j < nbs, jnp.where(j * SEL_BLOCK <= t, 1.0, 0.0), 0.0)
    score = jnp.where(visible > 0.5, jnp.where(forced > 0.5, BIG, imp_t), NEG)
    sel_t = _rank_select(score, min(N_SELECT, nbs), axis=0).astype(BF16)

    k_ofs = _iota((tk, tq), 0)
    q_ofs = _iota((tk, tq), 1)
    causal = k_ofs <= q_ofs
    per_head = lambda bias: jnp.concatenate([bias] * hpg, axis=1)

    def scores(kb_ref, kt):
        k_b = kb_ref[pl.ds(pl.multiple_of(kt * tk, tk), tk), :]
        return _dot_nt(k_b, qb)

    half = tk // 2
    early = [(h * tq, h * tq + half) for h in range(hpg)]
    late = [(h * tq + half, (h + 1) * tq) for h in range(hpg)]
    q_of = lambda ranges: jnp.concatenate([qb[a:b] for a, b in ranges], axis=0)

    def half_scores(kb_ref, kt, key_half, q_part):
        k_b = kb_ref[pl.ds(pl.multiple_of(kt * tk + key_half * half, half), half), :]
        return _dot_nt(k_b, q_part)

    def slc_bias(kt):
        blk = (kt * tk + _iota((tk, nbs_pad), 0)) // SEL_BLOCK
        expand = jnp.where(blk == _iota((tk, nbs_pad), 1), 1.0, 0.0).astype(BF16)
        return (_dot(expand, sel_t) - 1.0) * BIG

    slc_a, slc_b, win_a, win_b = states[0:3], states[3:6], states[6:9], states[9:12]

    def slc_tile(kt):
        return scores(ksb_ref, kt) + per_head(slc_bias(kt)), vst_ref[kt]

    n_back = WINDOW // tk

    def win_more(chain, other, back):
        kt = qi - back
        if back < n_back:
            _attn_more(chain, scores(kwb_ref, kt), vwt_ref[kt])
            return
        seen = jnp.where(k_ofs >= q_ofs, 0.0, NEG)
        _attn_more(chain, half_scores(kwb_ref, kt, 1, qb) + per_head(seen[half:, :]), vwt_ref[kt][:, half:])
        _attn_more(other, half_scores(kwb_ref, kt, 0, q_of(early)) + per_head(seen[:half, :half]),
                   vwt_ref[kt][:, :half], early)

    slc_diag = jnp.where(causal, slc_bias(qi), NEG)
    win_diag = jnp.where(causal, 0.0, NEG)
    _attn_first(slc_a, half_scores(ksb_ref, qi, 0, qb) + per_head(slc_diag[:half, :]), vst_ref[qi][:, :half])
    _attn_first(win_a, half_scores(kwb_ref, qi, 0, qb) + per_head(win_diag[:half, :]), vwt_ref[qi][:, :half])
    for st in (slc_b, win_b):
        _attn_empty(st)
    q_late = q_of(late)
    _attn_more(slc_b, half_scores(ksb_ref, qi, 1, q_late) + per_head(slc_diag[half:, half:]),
               vst_ref[qi][:, half:], late)
    _attn_more(win_b, half_scores(kwb_ref, qi, 1, q_late) + per_head(win_diag[half:, half:]),
               vwt_ref[qi][:, half:], late)

    for back in range(1, n_back + 1, 2):
        both = back + 1 <= n_back

        @pl.when(qi >= back + 1 if both else qi >= back)
        def _():
            win_more(win_b, win_a, back)
            if both:
                win_more(win_a, win_b, back + 1)

        if both:
            @pl.when(qi == back)
            def _():
                win_more(win_b, win_a, back)

    def slc_pair(pair, _):
        _attn_more(slc_a, *slc_tile(2 * pair))
        _attn_more(slc_b, *slc_tile(2 * pair + 1))
        return 0

    lax.fori_loop(0, lax.shift_right_logical(qi, 1), slc_pair, 0)

    @pl.when(lax.bitwise_and(qi, 1) == 1)
    def _():
        _attn_more(slc_a, *slc_tile(qi - 1))

    o_slc = _attn_merged(slc_a, slc_b)
    o_win = _attn_merged(win_a, win_b)
    gate_t = _sigmoid(g_ref[...] + bg_ref[...]).T
    for h in range(hpg):
        sl = slice(h * tq, (h + 1) * tq)
        o_t = (gate_t[3 * h:3 * h + 1, :] * o_cmp[:, sl] + gate_t[3 * h + 1:3 * h + 2, :] * o_slc[:, sl]
               + gate_t[3 * h + 2:3 * h + 3, :] * o_win[:, sl])
        o_ref[:, h * NSA_HD:(h + 1) * NSA_HD] = o_t.T.astype(BF16)


def nsa_attn_prompt(q, ck, cv, ks, vs, kw, vw, gates, b_gate, bsz, T):
    G, HPG = NSA_KV, NSA_HPG
    tq = math.gcd(T, ATT_TILE)
    assert WINDOW % tq == 0
    nq = T // tq
    nch = T // CMP_STRIDE
    nbc = nch - CMP_BLOCK // CMP_STRIDE + 1
    nbs = -(-T // SEL_BLOCK)
    nbs_pad = -(-nbs // SUBLANES) * SUBLANES
    cols = HPG * tq
    kv_scratch = [pltpu.VMEM((T, NSA_HD), BF16), pltpu.VMEM((nq, NSA_HD, tq), BF16)] * 2
    stats = [pltpu.VMEM((1, cols), F32), pltpu.VMEM((1, cols), F32), pltpu.VMEM((NSA_HD, cols), F32)] * 4
    seq = pl.BlockSpec((T, NSA_HD), lambda b, g, i: (b, g))
    cspec = pl.BlockSpec((1, 1, nch, NSA_HD), lambda b, g, i: (b, g, 0, 0))
    qspec = pl.BlockSpec((tq, HPG * NSA_HD), lambda b, g, i: (b * nq + i, g))
    return pl.pallas_call(
        functools.partial(_nsa_attn_kernel, tq=tq, hpg=HPG, nbc=nbc, nbs=nbs, nbs_pad=nbs_pad),
        grid=(bsz, G, nq),
        in_specs=[qspec, cspec, cspec, seq, seq, seq, seq,
                  pl.BlockSpec((tq, LANES), lambda b, g, i: (b * nq + i, g)),
                  pl.BlockSpec((1, LANES), lambda b, g, i: (0, g))],
        out_specs=qspec,
        out_shape=jax.ShapeDtypeStruct((bsz * T, G * HPG * NSA_HD), BF16),
        scratch_shapes=kv_scratch + stats,
        compiler_params=_cparams("arbitrary", "arbitrary", "arbitrary"),
    )(q, ck, cv, ks, vs, kw, vw, gates, b_gate)


CHUNK_ROWS = CMP_STRIDE * NSA_KV
CHUNK_PITCH = CHUNK_ROWS + SUBLANES


def _dec_cmp_kernel(tbl_ref, *refs, P, nbc, nbs, nbs_pad, p0):
    k_pages, v_pages = refs[:P], refs[P:2 * P]
    w1_ref, q_ref, pos_ref, b1_ref, w2_ref, b2_ref, o_ref, sel_ref, pad_ref, ak_ref, av_ref = refs[2 * P:]
    cpp = LANES // CMP_STRIDE
    step = pl.program_id(1)
    for idx, (pages, a_ref) in enumerate(((k_pages, ak_ref), (v_pages, av_ref))):
        for i, pg in enumerate(pages):
            for c in range(cpp):
                pad_ref[idx * P + i, c * CHUNK_PITCH:c * CHUNK_PITCH + CHUNK_ROWS, :] = (
                    pg[0, c * CHUNK_ROWS:(c + 1) * CHUNK_ROWS, :])
        blocks = []
        for g in range(NSA_KV):
            for i in range(P):
                blocks.append(jnp.concatenate(
                    [pad_ref[idx * P + i, pl.ds(p * NSA_KV + g, cpp, stride=CHUNK_PITCH), :]
                     for p in range(CMP_STRIDE)], axis=1))
        a = _dot(jnp.concatenate(blocks, axis=0).astype(BF16), w1_ref[idx])
        per_g = P * cpp
        for g in range(NSA_KV):
            a_ref[g, pl.ds(pl.multiple_of(step * per_g, per_g), per_g), :] = a[g * per_g:(g + 1) * per_g]

    @pl.when(step == pl.num_programs(1) - 1)
    def _():
        _dec_cmp_finish(ak_ref, av_ref, q_ref, pos_ref, w1_ref, b1_ref, w2_ref, b2_ref, o_ref, sel_ref,
                        nbc=nbc, nbs=nbs, nbs_pad=nbs_pad, p0=p0)


def _dec_cmp_finish(ak_ref, av_ref, q_ref, pos_ref, w1_ref, b1_ref, w2_ref, b2_ref, o_ref, sel_ref,
                    *, nbc, nbs, nbs_pad, p0):
    nch = ak_ref.shape[1]
    for g in range(NSA_KV):
        cks = []
        for idx, a_ref in enumerate((ak_ref, av_ref)):
            posw = _dot(pos_ref[idx], w1_ref[idx])
            cks.append(_cmp_finish(a_ref[g], posw, b1_ref[idx], w2_ref[idx], b2_ref[idx]))
        ck, cv = cks
        qb = (q_ref[0, g] * (NSA_HD ** -0.5)).astype(BF16)
        n = _iota((SUBLANES, nch), 1)
        cmask = jnp.where(n < nbc, jnp.where(n * CMP_STRIDE + (CMP_BLOCK - 1) <= p0, 1.0, 0.0), 0.0)
        sc = jnp.where(cmask > 0.5, _dot_nt(qb, ck.astype(BF16)), NEG)
        pe = jnp.exp(sc - jnp.max(sc, axis=-1, keepdims=True))
        p_cmp = pe / jnp.sum(pe, axis=-1, keepdims=True) * cmask
        o_ref[0, g] = _dot(p_cmp.astype(BF16), cv.astype(BF16))

        head = jnp.where(_iota((SUBLANES, nch), 0) < NSA_HPG, p_cmp, 0.0)
        p_sum = jnp.broadcast_to(jnp.sum(head, axis=0, keepdims=True), (SUBLANES, nch))
        ov_t = _overlap_t(nbs_pad, nch).astype(BF16)
        p_hi, p_lo = _split_bf16(p_sum)
        imp = _dot_nt(p_hi, ov_t) + _dot_nt(p_lo, ov_t)
        j = _iota((SUBLANES, nbs_pad), 1)
        t_blk = p0 // SEL_BLOCK
        forced = jnp.where(j == 0, 1.0, jnp.where(j == t_blk, 1.0, jnp.where(j == t_blk - 1, 1.0, 0.0)))
        visible = jnp.where(j < nbs, jnp.where(j * SEL_BLOCK <= p0, 1.0, 0.0), 0.0)
        score = jnp.where(visible > 0.5, jnp.where(forced > 0.5, BIG, imp), NEG)
        s_row = jnp.broadcast_to(score[0:1], (nbs_pad, nbs_pad))
        s_col = jnp.concatenate([jnp.broadcast_to(score[0:1], (LANES, nbs_pad)).T] * (nbs_pad // LANES), axis=1)
        jr = _iota((nbs_pad, nbs_pad), 0)
        jc = _iota((nbs_pad, nbs_pad), 1)
        tie = jnp.where(jc < jr, 1.0, 0.0)
        beats = jnp.where(s_row > s_col, 1.0, jnp.where(s_row == s_col, tie, 0.0))
        rank = jnp.sum(beats, axis=-1, keepdims=True)
        slot = _iota((nbs_pad, LANES), 1).astype(F32)
        blk = _iota((nbs_pad, LANES), 0).astype(F32)
        picked = jnp.sum(jnp.where(rank == slot, blk, 0.0), axis=0, keepdims=True)
        sel_ref[0, g:g + 1, :] = picked.astype(jnp.int32)


def dec_cmp(pool_k, pool_v, table, q, cw, p0):
    w1, pos, b1, w2, b2 = cw
    n_pool, page, G, hd = pool_k.shape
    assert page == LANES and G == NSA_KV and hd == NSA_HD
    bsz, n_pages = table.shape
    assert p0 == n_pages * page
    P = math.gcd(n_pages, PAGES_PER_STEP)
    cpp = page // CMP_STRIDE
    nch = n_pages * cpp
    hid2 = w1.shape[2]
    nbc = nch - CMP_BLOCK // CMP_STRIDE + 1
    nbs = -(-(p0 + 1) // SEL_BLOCK)
    nbs_pad = -(-nbs // LANES) * LANES
    pk = pool_k.reshape(n_pool, page * G, hd)
    pv = pool_v.reshape(n_pool, page * G, hd)
    pspec = lambda i: pl.BlockSpec((1, page * G, hd),
                                   functools.partial(lambda b, s, tbl, i: (tbl[b * n_pages + s * P + i], 0, 0), i=i))
    whole = lambda a: pl.BlockSpec(a.shape, lambda b, s, tbl: (0,) * a.ndim)
    per_seq = lambda shape: pl.BlockSpec((1,) + shape, lambda b, s, tbl: (b,) + (0,) * len(shape))
    grid_spec = pltpu.PrefetchScalarGridSpec(
        num_scalar_prefetch=1,
        grid=(bsz, n_pages // P),
        in_specs=[pspec(i) for i in range(P)] * 2
        + [whole(w1), per_seq((G, SUBLANES, hd)), whole(pos), whole(b1), whole(w2), whole(b2)],
        out_specs=[per_seq((G, SUBLANES, hd)), per_seq((G, LANES))],
        scratch_shapes=[pltpu.VMEM((2 * P, cpp * CHUNK_PITCH, hd), F32),
                        pltpu.VMEM((G, nch, hid2), F32), pltpu.VMEM((G, nch, hid2), F32)],
    )
    return pl.pallas_call(
        functools.partial(_dec_cmp_kernel, P=P, nbc=nbc, nbs=nbs, nbs_pad=nbs_pad, p0=p0),
        grid_spec=grid_spec,
        out_shape=[jax.ShapeDtypeStruct((bsz, G, SUBLANES, hd), F32),
                   jax.ShapeDtypeStruct((bsz, G, LANES), jnp.int32)],
        compiler_params=_cparams("arbitrary", "arbitrary"),
    )(table.reshape(-1), *([pk] * P), *([pv] * P), w1, q, pos, b1, w2, b2)


def _softmax_with_new_key(qb, k_b, v_b, bias, s_new, v_new):
    s = _dot_nt(qb, k_b)
    if bias is not None:
        s = s + bias
    mx = jnp.maximum(jnp.max(s, axis=-1, keepdims=True), s_new)
    p = jnp.exp(s - mx)
    p_new = jnp.exp(s_new - mx)
    return (_dot(p.astype(BF16), v_b) + p_new * v_new) / (jnp.sum(p, axis=-1, keepdims=True) + p_new)


def _dec_attn_kernel(sel_ref, tbl_ref, *refs, n_cache_blocks, n_sel):
    ks_refs, vs_refs = refs[:n_sel], refs[n_sel:2 * n_sel]
    q_ref, new_ref, kw_ref, vw_ref, oc_ref, g_ref, bg_ref, o_ref = refs[2 * n_sel:]
    b, g = pl.program_id(0), pl.program_id(1)
    q = q_ref[0, 0] * (NSA_HD ** -0.5)
    qb = q.astype(BF16)
    new = new_ref[0, 0]

    mine = lambda r, n: r[pl.ds(g, n, stride=NSA_KV), :]
    k_all = jnp.concatenate([mine(r.at[0, 0], SEL_BLOCK) for r in ks_refs], axis=0).astype(BF16)
    v_all = jnp.concatenate([mine(r.at[0, 0], SEL_BLOCK) for r in vs_refs], axis=0).astype(BF16)
    slot = _iota((SUBLANES, n_sel * SEL_BLOCK), 1) // SEL_BLOCK
    bias = jnp.zeros((SUBLANES, n_sel * SEL_BLOCK), F32)
    for s in range(n_sel):
        blk = sel_ref[(b * NSA_KV + g) * n_sel + s]
        bias = jnp.where(slot == s, jnp.where(blk < n_cache_blocks, 0.0, NEG), bias)
    o_slc = _softmax_with_new_key(qb, k_all, v_all, bias, jnp.sum(q * new[0:1], axis=-1, keepdims=True), new[1:2])

    n_win = kw_ref.shape[1] // NSA_KV
    o_win = _softmax_with_new_key(qb, mine(kw_ref.at[0], n_win).astype(BF16), mine(vw_ref.at[0], n_win).astype(BF16),
                                  None,
                                  jnp.sum(q * new[2:3], axis=-1, keepdims=True), new[3:4])

    gate = jnp.broadcast_to(_sigmoid(g_ref[0, 0] + bg_ref[...]), (SUBLANES, LANES))
    head = _iota((SUBLANES, LANES), 0)
    lane = _iota((SUBLANES, LANES), 1)
    o = jnp.zeros((SUBLANES, NSA_HD), F32)
    for br, ob in enumerate((oc_ref[0, 0], o_slc, o_win)):
        o = o + jnp.sum(jnp.where(lane == 3 * head + br, gate, 0.0), axis=-1, keepdims=True) * ob
    o_ref[0, 0] = o


def dec_attn(sel, table, pool_ks, pool_vs, q, new, kw_past, vw_past, o_cmp, gates, b_gate):
    n_pool, page, G, hd = pool_ks.shape
    bsz, n_pages = table.shape
    halves = page // SEL_BLOCK
    wb = kw_past.shape[1]
    assert wb <= WINDOW
    n_sel = sel.shape[-1]
    pk = pool_ks.reshape(n_pool, halves, SEL_BLOCK * G, hd)
    pv = pool_vs.reshape(n_pool, halves, SEL_BLOCK * G, hd)
    n_cache_blocks = n_pages * halves

    def page_map(b, g, sel_ref, tbl_ref, s):
        blk = jnp.minimum(sel_ref[(b * G + g) * n_sel + s], n_cache_blocks - 1)
        return (tbl_ref[b * n_pages + blk // halves], blk % halves, 0, 0)

    bg_map = lambda b, g, sel_ref, tbl_ref: (b, g, 0, 0)
    small = pl.BlockSpec((1, 1, SUBLANES, hd), bg_map)
    wspec = pl.BlockSpec((1, wb * G, hd), lambda b, g, sel_ref, tbl_ref: (b, 0, 0))
    blocks = [pl.BlockSpec((1, 1, SEL_BLOCK * G, hd), functools.partial(page_map, s=s)) for s in range(n_sel)]
    grid_spec = pltpu.PrefetchScalarGridSpec(
        num_scalar_prefetch=2,
        grid=(bsz, G),
        in_specs=blocks + blocks + [small, small, wspec, wspec, small,
                                    pl.BlockSpec((1, 1, 1, LANES), bg_map),
                                    pl.BlockSpec((1, LANES), lambda b, g, sel_ref, tbl_ref: (0, g))],
        out_specs=small,
    )
    return pl.pallas_call(
        functools.partial(_dec_attn_kernel, n_cache_blocks=n_cache_blocks, n_sel=n_sel),
        grid_spec=grid_spec,
        out_shape=jax.ShapeDtypeStruct((bsz, G, SUBLANES, hd), F32),
        compiler_params=_cparams("parallel", "parallel"),
    )(sel.reshape(-1), table.reshape(-1), *([pk] * n_sel), *([pv] * n_sel), q, new,
      kw_past.reshape(bsz, wb * G, hd), vw_past.reshape(bsz, wb * G, hd), o_cmp,
      gates.reshape(bsz, G, 1, LANES), b_gate)


def _gate_weights(wt, layer, row0, n):
    rows = lax.slice(wt, (layer, row0, 0), (layer + 1, row0 + n, wt.shape[2]))
    return jnp.pad(rows, ((0, 0), (0, LANES - n), (0, 0))), 0


def _nsa_gate_weights(wt, layer, b_gate, row0):
    per = 3 * NSA_HPG
    k = wt.shape[2]
    rows = lax.slice(wt, (layer, row0, 0), (layer + 1, row0 + NSA_KV * per, k)).reshape(NSA_KV, per, k)
    wg = jnp.pad(rows, ((0, 0), (0, LANES - per), (0, 0))).reshape(1, NSA_KV * LANES, k)
    bg = jnp.pad(b_gate.reshape(NSA_KV, per), ((0, 0), (0, LANES - per))).reshape(1, NSA_KV * LANES)
    return (wg, 0), bg


def kernel(x_prompt, x_sample, state_a_C, state_a_n, state_a_m, state_b_conv, cache_c_k_cmp, cache_c_v_cmp, cache_c_k_slc, cache_c_v_slc, cache_c_k_win, cache_c_v_win, page_table, norm_w, ffn_w_in, ffn_w_out, a_w_in, a_b_gates, a_head_norm, a_w_out, b_w_in, b_conv_w, b_w_out, c_w_in, c_b_gate, c_cmp_pos, c_cmp_w1, c_cmp_b1, c_cmp_w2, c_cmp_b2, c_w_out):
    bp, T, D = x_prompt.shape
    bs = x_sample.shape[0]
    assert x_sample.shape[1] == 1
    depth = norm_w.shape[0]
    G, hd = NSA_KV, NSA_HD
    past_len = page_table.shape[1] * cache_c_k_cmp.shape[2]
    xp = x_prompt.reshape(bp * T, D)
    xs = x_sample.reshape(bs, D)
    hp_in = norm_cast(xp, norm_w[0, 0])
    hs_in = norm_cast(xs, norm_w[0, 0])
    a_wt = jnp.swapaxes(a_w_in, 1, 2)
    c_wt = jnp.swapaxes(c_w_in, 1, 2)
    w_outs = {0: a_w_out, 1: b_w_out, 2: c_w_out}
    pa, sa, pb, sb, pc, sc = [], [], [], [], [], []
    for i in range(depth):
        kind, j = i % 3, i // 3
        nw = norm_w[i]
        w_out_f32 = (w_outs[kind], j)
        conv = None
        if kind == 0:
            hdv = a_head_norm.shape[1]
            n_main = 3 * hdv
            wg = _gate_weights(a_wt, j, n_main, 2 * MLSTM_HEADS)
            qkv_p, qkv_s, w_out = proj_plain(hp_in, hs_in, (a_wt, j), 0, 2 * hdv, MXU_TILE, transposed=True,
                                             cast=w_out_f32, prompt_dtype=BF16)
            o_p, o_s = proj_plain(hp_in, hs_in, (a_wt, j), 2 * hdv, hdv, MXU_TILE, transposed=True)
            qkvo_s = jnp.concatenate([qkv_s, o_s], axis=1)
            g_p, g_s = proj_plain(hp_in, hs_in, wg, 0, LANES, LANES, transposed=True)
            hp, c_p, n_p, m_p = mlstm_prompt(qkv_p, o_p, g_p, a_b_gates[j], a_head_norm[j], bp, T)
            hs, c_s, n_s, m_s = mlstm_step(qkvo_s, g_s, a_b_gates[j], a_head_norm[j],
                                           state_a_C[j], state_a_n[j], state_a_m[j])
            pa.append((c_p, n_p, m_p[:, :, 0]))
            sa.append((c_s, n_s, m_s))
        elif kind == 1:
            (bg_p, z_p), (bg_s, z_s), w_out = proj_conv_in(hp_in, hs_in, (b_w_in, j), D, w_out_f32)
            hp, conv = None, (bg_p, z_p, b_conv_w[j], T)
            buf = state_b_conv[j]
            hs = conv_gate_step(bg_s, z_s, buf.transpose(1, 0, 2), b_conv_w[j]).astype(BF16)
            pb.append(z_p.reshape(bp, T, D)[:, T - (CONV_W - 1):])
            sb.append(jnp.concatenate([buf, z_s[:, None]], axis=1)[:, -(CONV_W - 1):])
        else:
            nq_cols = NSA_HPG * G * hd
            nheads = G * NSA_HPG
            wg, bgate = _nsa_gate_weights(c_wt, j, c_b_gate[j], nq_cols + 6 * G * hd)
            cw = _cmp_weights(c_cmp_pos[j], c_cmp_w1[j], c_cmp_b1[j], c_cmp_w2[j], c_cmp_b2[j])
            q_p, q_s, w_out = proj_plain(hp_in, hs_in, (c_wt, j), 0, nq_cols, MXU_TILE, transposed=True, cast=w_out_f32)
            kv_p, kv_rows_p, kv_s = [], [], []
            for half in range(2):
                (tok_p, rows_p), (tok_s, _) = proj_groups(hp_in, hs_in, (c_wt, j), nq_cols + half * 3 * G * hd,
                                                         3, G * hd, transposed=True)
                kv_p, kv_rows_p, kv_s = kv_p + list(tok_p), kv_rows_p + list(rows_p), kv_s + list(tok_s)
            gt_p, gt_s = proj_plain(hp_in, hs_in, wg, 0, G * LANES, G * LANES, transposed=True)
            ck, cv = cmp_prompt(kv_p[0], kv_p[1], bp, T, cw)
            hp = nsa_attn_prompt(q_p, ck, cv, kv_p[2], kv_p[3], kv_p[4], kv_p[5], gt_p, bgate, bp, T)
            kv_rows_p = [a.reshape(bp, T, G, hd) for a in kv_rows_p]
            keep = min(WINDOW, T)
            pc.append(tuple(kv_rows_p[:4]) + tuple(a[:, T - keep:] for a in kv_rows_p[4:]))
            q_s = jnp.pad(q_s.reshape(bs, G, NSA_HPG, hd), ((0, 0), (0, 0), (0, SUBLANES - NSA_HPG), (0, 0)))
            kv_s = jnp.stack([a.reshape(bs, G, hd) for a in kv_s], axis=1)
            new = jnp.pad(kv_s[:, 2:6].transpose(0, 2, 1, 3), ((0, 0), (0, 0), (0, SUBLANES - 4), (0, 0)))
            o_cmp, ranked = dec_cmp(cache_c_k_cmp[j], cache_c_v_cmp[j], page_table, q_s, cw, past_len)
            n_sel = min(N_SELECT, -(-(past_len + 1) // SEL_BLOCK))
            o_s = dec_attn(ranked[:, :, :n_sel], page_table, cache_c_k_slc[j], cache_c_v_slc[j], q_s, new,
                           cache_c_k_win[j], cache_c_v_win[j], o_cmp, gt_s, bgate)
            hs = o_s[:, :, :NSA_HPG].reshape(bs, nheads * hd).astype(BF16)
            keep_s = min(WINDOW, cache_c_k_win.shape[2] + 1)
            win = [jnp.concatenate([c[j], kv_s[:, a][:, None]], axis=1)[:, -keep_s:]
                   for c, a in ((cache_c_k_win, 4), (cache_c_v_win, 5))]
            sc.append(tuple(kv_s[:, a][:, None] for a in range(4)) + tuple(win))
        last = i == depth - 1
        nw_next = norm_w[i + 1, 0] if not last else nw[0]
        (xp, xs), (hp_mid, hs_mid) = proj_out(hp, hs, w_out, xp, xs, nw[1], nw[2], conv=conv)
        act_p, act_s, f_out = proj_swiglu(hp_mid, hs_mid, (ffn_w_in, i), ffn_w_out.shape[1], (ffn_w_out, i))
        (xp, xs), (hp_in, hs_in) = proj_out(act_p, act_s, f_out, xp, xs, nw[3], nw_next, not last)

    stack = lambda items: [jnp.stack(a) for a in zip(*items)]
    p_a, s_a = stack(pa), stack(sa)
    p_c, s_c = stack(pc), stack(sc)
    return (xp.reshape(bp, T, D), xs.reshape(bs, 1, D), *p_a, *s_a, jnp.stack(pb), jnp.stack(sb), *p_c, *s_c)
```
